```python
import math
import jax
import jax.numpy as jnp
from jax import lax
import numpy as np

D_MODEL = 1024
BATCH = 16
SEQ = 2048
DEPTH = 4

N_MIXERS = 3
EPS = 1e-6

S5_GROUP = 16
S5_GROUPS = D_MODEL // S5_GROUP
S5_STATE = 64
S5_DT_MIN = 1e-3
S5_DT_MAX = 1e-1

GLA_HEADS = 4
GLA_DK = D_MODEL // 2
GLA_DV = D_MODEL
GLA_HDK = GLA_DK // GLA_HEADS
GLA_HDV = GLA_DV // GLA_HEADS
GLA_GATE_RANK = 16
GLA_GATE_NORM = 16.0
GLA_CHUNK = 64
GLA_IN = 2 * GLA_DK + 2 * GLA_DV + GLA_GATE_RANK

SWA_HEAD_DIM = 64
SWA_Q_HEADS = D_MODEL // SWA_HEAD_DIM
SWA_KV_HEADS = 2
SWA_GROUP = SWA_Q_HEADS // SWA_KV_HEADS
SWA_WINDOW = 128
SWA_BLOCK = 128
SWA_QKV = (SWA_Q_HEADS + 2 * SWA_KV_HEADS) * SWA_HEAD_DIM
MASK_VALUE = -1e30

FFN_HIDDEN = 2816
N_EXPERTS = 8
TOP_K = 2
EXPERT_HIDDEN = 3584

kernel_name = 'hybrid_s5_gla_swa_moe_trunk'


def rmsnorm(x, g):
    xf = x.astype(jnp.float32)
    y = xf * lax.rsqrt(jnp.mean(xf * xf, axis=-1, keepdims=True) + EPS)
    return (y * g.astype(jnp.float32)).astype(x.dtype)


def alibi_slopes(n_heads):
    return jnp.exp2(-8.0 * jnp.arange(1, n_heads + 1, dtype=jnp.float32) / n_heads)


def s5_mixer(h, lam_re, lam_im, log_dt, b_re, b_im, c_re, c_im, d_skip, w_glu, b_glu):
    f32 = jnp.float32
    bsz, seqlen, _ = h.shape
    hf = h.astype(f32)
    u = hf.reshape(bsz, seqlen, S5_GROUPS, S5_GROUP)
    lam = lax.complex(lam_re.astype(f32), lam_im.astype(f32))
    dt = jnp.exp(log_dt.astype(f32))[:, None]
    lam_bar = jnp.exp(lam * dt)
    b = lax.complex(b_re.astype(f32), b_im.astype(f32))
    b_bar = ((lam_bar - 1.0) / lam)[..., None] * b
    bu = lax.complex(jnp.einsum('blgh,gph->blgp', u, jnp.real(b_bar)),
                     jnp.einsum('blgh,gph->blgp', u, jnp.imag(b_bar)))
    a = jnp.broadcast_to(lam_bar, (1, seqlen) + lam_bar.shape)

    def combine(e1, e2):
        a1, x1 = e1
        a2, x2 = e2
        return a1 * a2, a2 * x1 + x2

    _, states = lax.associative_scan(combine, (a, bu), axis=1)
    y = (jnp.einsum('blgp,ghp->blgh', jnp.real(states), c_re.astype(f32))
         - jnp.einsum('blgp,ghp->blgh', jnp.imag(states), c_im.astype(f32)))
    y = y.reshape(bsz, seqlen, D_MODEL) + d_skip.astype(f32) * hf
    y = jax.nn.gelu(y).astype(h.dtype)
    return y * jax.nn.sigmoid(y @ w_glu + b_glu)


def gla_mixer(h, w_in, w_g2, b_g2, g_norm, w_out):
    f32 = jnp.float32
    bsz, seqlen, _ = h.shape
    n_chunks = seqlen // GLA_CHUNK
    proj = h @ w_in
    q, k, v, r, g_low = jnp.split(
        proj, [GLA_DK, 2 * GLA_DK, 2 * GLA_DK + GLA_DV, 2 * GLA_DK + 2 * GLA_DV], axis=-1)
    log_alpha = jax.nn.log_sigmoid((g_low @ w_g2 + b_g2).astype(f32)) / GLA_GATE_NORM

    def chunked(t, hd):
        return t.astype(f32).reshape(bsz, n_chunks, GLA_CHUNK, GLA_HEADS, hd)

    q = chunked(q, GLA_HDK) * GLA_HDK ** -0.5
    k = chunked(k, GLA_HDK)
    v = chunked(v, GLA_HDV)
    gcum = jnp.cumsum(chunked(log_alpha, GLA_HDK), axis=2)
    g_last = gcum[:, :, -1:]
    q_s = q * jnp.exp(gcum)
    k_s = k * jnp.exp(-gcum)
    k_end = k * jnp.exp(g_last - gcum)
    causal = jnp.tril(jnp.ones((GLA_CHUNK, GLA_CHUNK), dtype=bool))
    scores = jnp.where(causal, jnp.einsum('bnihd,bnjhd->bnhij', q_s, k_s), 0.0)
    o_intra = jnp.einsum('bnhij,bnjhe->bnihe', scores, v)
    kv_chunk = jnp.einsum('bnjhd,bnjhe->bnhde', k_end, v)
    decay = jnp.exp(g_last[:, :, 0])

    def step(state, xs):
        q_n, kv_n, d_n = xs
        o_n = jnp.einsum('bihd,bhde->bihe', q_n, state)
        return d_n[..., None] * state + kv_n, o_n

    xs = (jnp.moveaxis(q_s, 1, 0), jnp.moveaxis(kv_chunk, 1, 0), jnp.moveaxis(decay, 1, 0))
    init = jnp.zeros((bsz, GLA_HEADS, GLA_HDK, GLA_HDV), f32)
    _, o_inter = lax.scan(step, init, xs)
    o = (o_intra + jnp.moveaxis(o_inter, 0, 1)).reshape(bsz, seqlen, GLA_HEADS, GLA_HDV)
    o = o * lax.rsqrt(jnp.mean(o * o, axis=-1, keepdims=True) + EPS)
    o = o.reshape(bsz, seqlen, GLA_DV) * g_norm.astype(f32)
    o = (o * jax.nn.silu(r.astype(f32))).astype(h.dtype)
    return o @ w_out


def swa_mixer(h, w_qkv, b_qkv, sinks, w_out, b_out):
    f32 = jnp.float32
    bsz, seqlen, _ = h.shape
    nb = seqlen // SWA_BLOCK
    qkv = h @ w_qkv + b_qkv
    q, k, v = jnp.split(
        qkv, [SWA_Q_HEADS * SWA_HEAD_DIM, (SWA_Q_HEADS + SWA_KV_HEADS) * SWA_HEAD_DIM], axis=-1)
    q = q.reshape(bsz, nb, SWA_BLOCK, SWA_KV_HEADS, SWA_GROUP, SWA_HEAD_DIM)
    k = k.reshape(bsz, nb, SWA_BLOCK, SWA_KV_HEADS, SWA_HEAD_DIM)
    v = v.reshape(bsz, nb, SWA_BLOCK, SWA_KV_HEADS, SWA_HEAD_DIM)

    def band(t):
        prev = jnp.pad(t, ((0, 0), (1, 0), (0, 0), (0, 0), (0, 0)))[:, :-1]
        return jnp.concatenate([prev, t], axis=2)

    kb = band(k)
    vb = band(v)
    scores = jnp.einsum('bnikgd,bnjkd->bnkgij', q, kb).astype(f32) * SWA_HEAD_DIM ** -0.5
    qi = jnp.arange(SWA_BLOCK)[:, None]
    kj = jnp.arange(2 * SWA_BLOCK)[None, :]
    dist = qi + SWA_BLOCK - kj
    in_window = (dist >= 0) & (dist < SWA_WINDOW)
    blk_valid = (jnp.arange(nb)[:, None, None] > 0) | (kj[None] >= SWA_BLOCK)
    mask = in_window[None] & blk_valid
    slopes = alibi_slopes(SWA_Q_HEADS).reshape(SWA_KV_HEADS, SWA_GROUP)
    scores = scores - slopes[:, :, None, None] * dist.astype(f32)
    scores = jnp.where(mask[None, :, None, None], scores, MASK_VALUE)
    sink = jnp.broadcast_to(sinks.astype(f32).reshape(SWA_KV_HEADS, SWA_GROUP, 1, 1),
                            scores.shape[:-1] + (1,))
    probs = jax.nn.softmax(jnp.concatenate([scores, sink], axis=-1), axis=-1)[..., :-1]
    o = jnp.einsum('bnkgij,bnjkd->bnikgd', probs.astype(h.dtype), vb)
    return o.reshape(bsz, seqlen, D_MODEL) @ w_out + b_out


def dense_swiglu(h, w_gate_up, w_down):
    gate, up = jnp.split(h @ w_gate_up, 2, axis=-1)
    return (jax.nn.silu(gate) * up) @ w_down


def moe_swiglu(h, w_router, w_gate_up, w_down):
    logits = (h @ w_router).astype(jnp.float32)
    top_vals, top_idx = lax.top_k(logits, TOP_K)
    top_w = jax.nn.softmax(top_vals, axis=-1)
    gates = jnp.sum(jax.nn.one_hot(top_idx, N_EXPERTS, dtype=jnp.float32) * top_w[..., None],
                    axis=-2).astype(h.dtype)
    out = jnp.zeros_like(h)
    for e in range(N_EXPERTS):
        out = out + gates[..., e:e + 1] * dense_swiglu(h, w_gate_up[e], w_down[e])
    return out


def _normal(key, shape, scale):
    return jax.random.normal(key, shape, jnp.float32) * scale


def _gain(key, n):
    return 1.0 + 0.02 * jax.random.normal(key, (n,), jnp.float32)


def _s5_params(p, keys, prefix):
    n = jnp.arange(S5_STATE, dtype=jnp.float32)
    gp = (S5_GROUPS, S5_STATE)
    p[prefix + 'lam_re'] = -0.5 + _normal(next(keys), gp, 0.01)
    p[prefix + 'lam_im'] = math.pi * n[None, :] + _normal(next(keys), gp, 0.01)
    p[prefix + 'log_dt'] = jax.random.uniform(next(keys), (S5_GROUPS,), jnp.float32,
                                              math.log(S5_DT_MIN), math.log(S5_DT_MAX))
    p[prefix + 'b_re'] = _normal(next(keys), (S5_GROUPS, S5_STATE, S5_GROUP), (2 * S5_GROUP) ** -0.5)
    p[prefix + 'b_im'] = _normal(next(keys), (S5_GROUPS, S5_STATE, S5_GROUP), (2 * S5_GROUP) ** -0.5)
    p[prefix + 'c_re'] = _normal(next(keys), (S5_GROUPS, S5_GROUP, S5_STATE), (2 * S5_STATE) ** -0.5)
    p[prefix + 'c_im'] = _normal(next(keys), (S5_GROUPS, S5_GROUP, S5_STATE), (2 * S5_STATE) ** -0.5)
    p[prefix + 'd'] = _normal(next(keys), (D_MODEL,), 0.5)
    p[prefix + 'w_glu'] = _normal(next(keys), (D_MODEL, D_MODEL), D_MODEL ** -0.5)
    p[prefix + 'b_glu'] = _normal(next(keys), (D_MODEL,), 0.02)


def _dense_params(p, keys, prefix):
    p[prefix + 'w_gate_up'] = _normal(next(keys), (D_MODEL, 2 * FFN_HIDDEN), D_MODEL ** -0.5)
    p[prefix + 'w_down'] = _normal(next(keys), (FFN_HIDDEN, D_MODEL), FFN_HIDDEN ** -0.5)


def _moe_params(p, keys, prefix):
    p[prefix + 'router'] = _normal(next(keys), (D_MODEL, N_EXPERTS), D_MODEL ** -0.5)
    p[prefix + 'w_gate_up'] = _normal(next(keys), (N_EXPERTS, D_MODEL, 2 * EXPERT_HIDDEN), D_MODEL ** -0.5)
    p[prefix + 'w_down'] = _normal(next(keys), (N_EXPERTS, EXPERT_HIDDEN, D_MODEL), EXPERT_HIDDEN ** -0.5)


def setup_inputs(seed: int = 0) -> dict:
    key = jax.random.key(seed)
    keys = iter(jax.random.split(key, 64))
    p = {}
    p['x'] = _normal(next(keys), (BATCH, SEQ, D_MODEL), 1.0)
    p['l0_ln1'] = _gain(next(keys), D_MODEL)
    _s5_params(p, keys, 'l0_s5_')
    p['l0_ln2'] = _gain(next(keys), D_MODEL)
    _dense_params(p, keys, 'l0_ffn_')
    p['l1_ln1'] = _gain(next(keys), D_MODEL)
    p['l1_gla_w_in'] = _normal(next(keys), (D_MODEL, GLA_IN), D_MODEL ** -0.5)
    p['l1_gla_w_g2'] = _normal(next(keys), (GLA_GATE_RANK, GLA_DK), GLA_GATE_RANK ** -0.5)
    p['l1_gla_b_g2'] = _normal(next(keys), (GLA_DK,), 0.02)
    p['l1_gla_norm'] = _gain(next(keys), GLA_DV)
    p['l1_gla_w_out'] = _normal(next(keys), (GLA_DV, D_MODEL), GLA_DV ** -0.5)
    p['l1_ln2'] = _gain(next(keys), D_MODEL)
    _moe_params(p, keys, 'l1_moe_')
    p['l2_ln1'] = _gain(next(keys), D_MODEL)
    p['l2_swa_w_qkv'] = _normal(next(keys), (D_MODEL, SWA_QKV), D_MODEL ** -0.5)
    p['l2_swa_b_qkv'] = _normal(next(keys), (SWA_QKV,), 0.02)
    p['l2_swa_sinks'] = _normal(next(keys), (SWA_Q_HEADS,), 0.5)
    p['l2_swa_w_out'] = _normal(next(keys), (D_MODEL, D_MODEL), D_MODEL ** -0.5)
    p['l2_swa_b_out'] = _normal(next(keys), (D_MODEL,), 0.02)
    p['l2_ln2'] = _gain(next(keys), D_MODEL)
    _dense_params(p, keys, 'l2_ffn_')
    p['l3_ln1'] = _gain(next(keys), D_MODEL)
    _s5_params(p, keys, 'l3_s5_')
    p['l3_ln2'] = _gain(next(keys), D_MODEL)
    _moe_params(p, keys, 'l3_moe_')
    p['ln_f'] = _gain(next(keys), D_MODEL)
    return p


def reference(x,
              l0_ln1, l0_s5_lam_re, l0_s5_lam_im, l0_s5_log_dt, l0_s5_b_re, l0_s5_b_im,
              l0_s5_c_re, l0_s5_c_im, l0_s5_d, l0_s5_w_glu, l0_s5_b_glu,
              l0_ln2, l0_ffn_w_gate_up, l0_ffn_w_down,
              l1_ln1, l1_gla_w_in, l1_gla_w_g2, l1_gla_b_g2, l1_gla_norm, l1_gla_w_out,
              l1_ln2, l1_moe_router, l1_moe_w_gate_up, l1_moe_w_down,
              l2_ln1, l2_swa_w_qkv, l2_swa_b_qkv, l2_swa_sinks, l2_swa_w_out, l2_swa_b_out,
              l2_ln2, l2_ffn_w_gate_up, l2_ffn_w_down,
              l3_ln1, l3_s5_lam_re, l3_s5_lam_im, l3_s5_log_dt, l3_s5_b_re, l3_s5_b_im,
              l3_s5_c_re, l3_s5_c_im, l3_s5_d, l3_s5_w_glu, l3_s5_b_glu,
              l3_ln2, l3_moe_router, l3_moe_w_gate_up, l3_moe_w_down,
              ln_f):
    layer_ln1 = (l0_ln1, l1_ln1, l2_ln1, l3_ln1)
    layer_ln2 = (l0_ln2, l1_ln2, l2_ln2, l3_ln2)
    mixer_params = (
        (l0_s5_lam_re, l0_s5_lam_im, l0_s5_log_dt, l0_s5_b_re, l0_s5_b_im,
         l0_s5_c_re, l0_s5_c_im, l0_s5_d, l0_s5_w_glu, l0_s5_b_glu),
        (l1_gla_w_in, l1_gla_w_g2, l1_gla_b_g2, l1_gla_norm, l1_gla_w_out),
        (l2_swa_w_qkv, l2_swa_b_qkv, l2_swa_sinks, l2_swa_w_out, l2_swa_b_out),
        (l3_s5_lam_re, l3_s5_lam_im, l3_s5_log_dt, l3_s5_b_re, l3_s5_b_im,
         l3_s5_c_re, l3_s5_c_im, l3_s5_d, l3_s5_w_glu, l3_s5_b_glu),
    )
    ffn_params = (
        (l0_ffn_w_gate_up, l0_ffn_w_down),
        (l1_moe_router, l1_moe_w_gate_up, l1_moe_w_down),
        (l2_ffn_w_gate_up, l2_ffn_w_down),
        (l3_moe_router, l3_moe_w_gate_up, l3_moe_w_down),
    )
    mixers = (s5_mixer, gla_mixer, swa_mixer)
    for i in range(DEPTH):
        x = x + mixers[i % N_MIXERS](rmsnorm(x, layer_ln1[i]), *mixer_params[i])
        channel = moe_swiglu if i % 2 == 1 else dense_swiglu
        x = x + channel(rmsnorm(x, layer_ln2[i]), *ffn_params[i])
    return rmsnorm(x, ln_f)
```

```python
import functools
import math

import jax
import jax.numpy as jnp
from jax import lax
from jax.experimental import pallas as pl
from jax.experimental.pallas import tpu as pltpu

F32 = jnp.float32
BF16 = jnp.bfloat16
EPS = 1e-6
LANES = 128
MIB = 1 << 20

S5_GROUP = 16
S5_STATE = 64
S5_CHUNK = 16
S5_SLAB_GROUPS = LANES // S5_GROUP
S5_PITCH_PAD = 8

GLA_HEADS = 4
GLA_GATE_RANK = 16
GLA_GATE_NORM = 16.0
GLA_CHUNK = 64

SWA_HEAD_DIM = 64
SWA_KV_HEADS = 2
SWA_WINDOW = 128
SWA_BLOCK = 128
MASK_VALUE = -1e30

TOP_K = 2


def _params(semantics, vmem_mib):
    return pltpu.CompilerParams(dimension_semantics=semantics, vmem_limit_bytes=vmem_mib * MIB)


def _resident(block_shape, index_map):
    return pl.BlockSpec(block_shape, index_map, pipeline_mode=pl.Buffered(1))


def _rms(xf, gain):
    return xf * lax.rsqrt(jnp.mean(xf * xf, axis=-1, keepdims=True) + EPS) * gain


def _gelu_tanh(x):
    return 0.5 * x * (1.0 + jnp.tanh(math.sqrt(2.0 / math.pi) * (x + 0.044715 * (x * x * x))))


def _silu(x):
    return x * jax.nn.sigmoid(x)


def _bdot(a, b):
    return jnp.dot(a, b, preferred_element_type=F32)


def _s5_norm_kernel(x_ref, g_ref, o_ref, *, ks, d):
    for j in range(ks):
        xf = x_ref[0, :, j * d:(j + 1) * d]
        o_ref[0, j] = _rms(xf, g_ref[...]).astype(o_ref.dtype)


def _s5_norm(x, gain, *, ks=4):
    bsz, seqlen, d = x.shape
    nch = seqlen // S5_CHUNK
    xv = x.reshape(bsz, nch, S5_CHUNK * d)
    return pl.pallas_call(
        functools.partial(_s5_norm_kernel, ks=ks, d=d),
        out_shape=jax.ShapeDtypeStruct((bsz, S5_CHUNK, nch, d), BF16),
        grid=(bsz, S5_CHUNK // ks),
        in_specs=[pl.BlockSpec((1, nch, ks * d), lambda b, i: (b, 0, i)),
                  pl.BlockSpec((1, d), lambda b, i: (0, 0))],
        out_specs=pl.BlockSpec((1, ks, nch, d), lambda b, i: (b, i, 0, 0)),
        compiler_params=_params(("parallel", "parallel"), 32),
        name="s5_norm",
    )(xv, gain.reshape(1, d))


def _s5_conv_kernel(h_ref, toep_ref, win_ref, wout_ref, a_ref, d_ref, o_ref, s_ref, *, nseq, nch):
    pitch = nch + S5_PITCH_PAD
    nl = a_ref.shape[1] // 2
    lhs = jnp.concatenate(
        [jnp.concatenate([h_ref[bl, s] for s in range(S5_CHUNK)], axis=1) for bl in range(nseq)], axis=0)
    bc = _bdot(lhs, win_ref[0])
    for bl in range(nseq):
        for j in range(2 * nl):
            s_ref[j, bl * pitch:bl * pitch + nch, :] = bc[bl * nch:(bl + 1) * nch, j * LANES:(j + 1) * LANES]
    a_re = [a_ref[0, j:j + 1, :] for j in range(nl)]
    a_im = [a_ref[0, nl + j:nl + j + 1, :] for j in range(nl)]

    def step(n, carry):
        p_re, p_im = carry
        rows = pl.ds(n, nseq, stride=pitch)
        n_re, n_im = [], []
        for j in range(nl):
            c_re = s_ref[j, rows, :]
            c_im = s_ref[nl + j, rows, :]
            s_ref[j, rows, :] = p_re[j]
            s_ref[nl + j, rows, :] = p_im[j]
            n_re.append(a_re[j] * p_re[j] - a_im[j] * p_im[j] + c_re)
            n_im.append(a_re[j] * p_im[j] + a_im[j] * p_re[j] + c_im)
        return tuple(n_re), tuple(n_im)

    zeros = tuple(jnp.zeros((nseq, LANES), F32) for _ in range(nl))
    lax.fori_loop(0, nch, step, (zeros, zeros))
    x_prev = jnp.concatenate(
        [jnp.concatenate([s_ref[j, bl * pitch:bl * pitch + nch, :] for j in range(2 * nl)], axis=1)
         for bl in range(nseq)], axis=0).astype(BF16)
    y = _bdot(lhs, toep_ref[0]) + _bdot(x_prev, wout_ref[0])
    dskip = d_ref[0]
    for bl in range(nseq):
        for s in range(S5_CHUNK):
            ys = y[bl * nch:(bl + 1) * nch, s * LANES:(s + 1) * LANES]
            ys = ys + dskip * h_ref[bl, s].astype(F32)
            o_ref[bl, s] = _gelu_tanh(ys).astype(o_ref.dtype)


def _s5_conv(hp, toep, w_in, w_out, a_pack, d_skip, *, nseq):
    bsz, _, nch, d = hp.shape
    nslab = d // LANES
    kdim = S5_CHUNK * LANES
    sdim = w_in.shape[-1]
    return pl.pallas_call(
        functools.partial(_s5_conv_kernel, nseq=nseq, nch=nch),
        out_shape=jax.ShapeDtypeStruct(hp.shape, BF16),
        grid=(nslab, bsz // nseq),
        in_specs=[pl.BlockSpec((nseq, S5_CHUNK, nch, LANES), lambda c, b: (b, 0, 0, c)),
                  _resident((1, kdim, kdim), lambda c, b: (c, 0, 0)),
                  _resident((1, kdim, sdim), lambda c, b: (c, 0, 0)),
                  _resident((1, sdim, kdim), lambda c, b: (c, 0, 0)),
                  pl.BlockSpec((1, sdim // LANES, LANES), lambda c, b: (c, 0, 0)),
                  pl.BlockSpec((1, 1, LANES), lambda c, b: (c, 0, 0))],
        out_specs=pl.BlockSpec((nseq, S5_CHUNK, nch, LANES), lambda c, b: (b, 0, 0, c)),
        scratch_shapes=[pltpu.VMEM((sdim // LANES, nseq * (nch + S5_PITCH_PAD), LANES), F32)],
        compiler_params=_params(("parallel", "parallel"), 56),
        name="s5_conv",
    )(hp, toep, w_in, w_out, a_pack, d_skip.reshape(nslab, 1, LANES))


def _s5_glu_kernel(y_ref, x_ref, w_ref, b_ref, o_ref, *, ks, d):
    nch = y_ref.shape[2]
    y = jnp.concatenate([y_ref[0, j] for j in range(ks)], axis=0)
    z = _bdot(y, w_ref[...]) + b_ref[...]
    out = y.astype(F32) * jax.nn.sigmoid(z)
    for j in range(ks):
        o_ref[0, :, j * d:(j + 1) * d] = x_ref[0, :, j * d:(j + 1) * d] + out[j * nch:(j + 1) * nch]


def _s5_glu(yp, x, w_glu, b_glu, *, ks=4):
    bsz, seqlen, d = x.shape
    nch = seqlen // S5_CHUNK
    xv = x.reshape(bsz, nch, S5_CHUNK * d)
    out = pl.pallas_call(
        functools.partial(_s5_glu_kernel, ks=ks, d=d),
        out_shape=jax.ShapeDtypeStruct(xv.shape, F32),
        grid=(bsz, S5_CHUNK // ks),
        in_specs=[pl.BlockSpec((1, ks, nch, d), lambda b, i: (b, i, 0, 0)),
                  pl.BlockSpec((1, nch, ks * d), lambda b, i: (b, 0, i)),
                  _resident((d, d), lambda b, i: (0, 0)),
                  pl.BlockSpec((1, d), lambda b, i: (0, 0))],
        out_specs=pl.BlockSpec((1, nch, ks * d), lambda b, i: (b, 0, i)),
        compiler_params=_params(("parallel", "parallel"), 40),
        name="s5_glu",
    )(yp, xv, w_glu.astype(BF16), b_glu.reshape(1, d))
    return out.reshape(bsz, seqlen, d)


def _s5_operators(lam_re, lam_im, log_dt, b_re, b_im, c_re, c_im):
    hi = lax.Precision.HIGHEST
    ngroups, nstate = lam_re.shape
    gpc = S5_SLAB_GROUPS
    nslab = ngroups // gpc
    dt = jnp.exp(log_dt)[:, None]
    j = jnp.arange(S5_CHUNK + 1, dtype=F32)[:, None, None]
    mag = jnp.exp(j * (lam_re * dt)[None])
    ang = j * (lam_im * dt)[None]
    pw_re, pw_im = mag * jnp.cos(ang), mag * jnp.sin(ang)
    num_re, num_im = pw_re[1] - 1.0, pw_im[1]
    den = lam_re * lam_re + lam_im * lam_im
    f_re = (num_re * lam_re + num_im * lam_im) / den
    f_im = (num_im * lam_re - num_re * lam_im) / den
    bb_re = f_re[..., None] * b_re - f_im[..., None] * b_im
    bb_im = f_re[..., None] * b_im + f_im[..., None] * b_re
    cp_re = c_re[None] * pw_re[:S5_CHUNK, :, None, :] - c_im[None] * pw_im[:S5_CHUNK, :, None, :]
    cp_im = c_re[None] * pw_im[:S5_CHUNK, :, None, :] + c_im[None] * pw_re[:S5_CHUNK, :, None, :]
    taps = (jnp.einsum('jghp,gpi->jghi', cp_re, bb_re, precision=hi)
            - jnp.einsum('jghp,gpi->jghi', cp_im, bb_im, precision=hi))
    eye = jnp.eye(gpc, dtype=F32)
    pos = jnp.arange(S5_CHUNK)
    lag = pos[None, :] - pos[:, None]
    sel = jnp.where((lag >= 0)[..., None, None, None], taps[jnp.clip(lag, 0, S5_CHUNK - 1)], 0.0)
    sel = sel.reshape(S5_CHUNK, S5_CHUNK, nslab, gpc, S5_GROUP, S5_GROUP)
    toep = jnp.einsum('abcgoi,gk->cagibko', sel, eye)
    kdim = S5_CHUNK * LANES
    toep = toep.reshape(nslab, kdim, kdim).astype(BF16)
    jr = (S5_CHUNK - 1) - jnp.arange(S5_CHUNK, dtype=F32)[:, None, None]
    mag_r = jnp.exp(jr * (lam_re * dt)[None])
    ang_r = jr * (lam_im * dt)[None]
    rev_re, rev_im = mag_r * jnp.cos(ang_r), mag_r * jnp.sin(ang_r)
    v_re = rev_re[..., None] * bb_re[None] - rev_im[..., None] * bb_im[None]
    v_im = rev_re[..., None] * bb_im[None] + rev_im[..., None] * bb_re[None]
    v = jnp.stack([v_re, v_im], axis=0).reshape(2, S5_CHUNK, nslab, gpc, nstate, S5_GROUP)
    w_in = jnp.einsum('rscgpi,gk->csgirkp', v, eye).reshape(nslab, kdim, 2 * gpc * nstate).astype(BF16)
    m_re = c_re[None] * pw_re[1:, :, None, :] - c_im[None] * pw_im[1:, :, None, :]
    m_im = c_re[None] * pw_im[1:, :, None, :] + c_im[None] * pw_re[1:, :, None, :]
    m = jnp.stack([m_re, -m_im], axis=0).reshape(2, S5_CHUNK, nslab, gpc, S5_GROUP, nstate)
    w_out = jnp.einsum('rscgop,gk->crgpsko', m, eye).reshape(nslab, 2 * gpc * nstate, kdim).astype(BF16)
    half = gpc * nstate // LANES
    a_pack = jnp.concatenate([pw_re[S5_CHUNK].reshape(nslab, half, LANES),
                              pw_im[S5_CHUNK].reshape(nslab, half, LANES)], axis=1)
    return toep, w_in, w_out, a_pack


def _s5_layer(x, ln, lam_re, lam_im, log_dt, b_re, b_im, c_re, c_im, d_skip, w_glu, b_glu, *, nseq=4):
    toep, w_in, w_out, a_pack = _s5_operators(lam_re, lam_im, log_dt, b_re, b_im, c_re, c_im)
    hp = _s5_norm(x, ln)
    yp = _s5_conv(hp, toep, w_in, w_out, a_pack, d_skip, nseq=min(nseq, x.shape[0]))
    return _s5_glu(yp, x, w_glu, b_glu)


def _dense_ffn_kernel(x_ref, g_ref, wg_ref, wu_ref, wd_ref, o_ref):
    xf = x_ref[...]
    h = _rms(xf, g_ref[...]).astype(BF16)
    act = (_silu(_bdot(h, wg_ref[...])) * _bdot(h, wu_ref[...])).astype(BF16)
    o_ref[...] = xf + _bdot(act, wd_ref[...])


def _dense_ffn_layer(x, ln, w_gate_up, w_down, *, tm=512):
    bsz, seqlen, d = x.shape
    ntok = bsz * seqlen
    hidden = w_down.shape[0]
    tm = min(tm, ntok)
    wgu = w_gate_up.astype(BF16)
    out = pl.pallas_call(
        _dense_ffn_kernel,
        out_shape=jax.ShapeDtypeStruct((ntok, d), F32),
        grid=(ntok // tm,),
        in_specs=[pl.BlockSpec((tm, d), lambda i: (i, 0)),
                  pl.BlockSpec((1, d), lambda i: (0, 0)),
                  _resident((d, hidden), lambda i: (0, 0)),
                  _resident((d, hidden), lambda i: (0, 1)),
                  _resident((hidden, d), lambda i: (0, 0))],
        out_specs=pl.BlockSpec((tm, d), lambda i: (i, 0)),
        compiler_params=_params(("parallel",), 56),
        name="dense_ffn",
    )(x.reshape(ntok, d), ln.reshape(1, d), wgu, wgu, w_down.astype(BF16))
    return out.reshape(bsz, seqlen, d)


def _log_sigmoid(z):
    return jnp.minimum(z, 0.0) - jnp.log(1.0 + jnp.exp(-jnp.abs(z)))


def _gla_kernel(x_ref, ln_ref, wm_ref, wgl_ref, wg2_ref, bg2_ref, gn_ref, wo_ref, o_ref, st_ref,
                *, tq, dk, dv, heads):
    hdk, hdv = dk // heads, dv // heads
    chunk = GLA_CHUNK
    nt = (((1,), (1,)), ((), ()))
    tn = (((0,), (0,)), ((), ()))

    @pl.when(pl.program_id(1) == 0)
    def _():
        st_ref[...] = jnp.zeros_like(st_ref)

    xf = x_ref[0]
    h = _rms(xf, ln_ref[...]).astype(BF16)
    proj = _bdot(h, wm_ref[...])
    glow = _bdot(h, wgl_ref[...]).astype(BF16)
    la = _log_sigmoid(_bdot(glow, wg2_ref[...]) + bg2_ref[...]) * (1.0 / GLA_GATE_NORM)
    row = lax.broadcasted_iota(jnp.int32, (chunk, chunk), 0)
    col = lax.broadcasted_iota(jnp.int32, (chunk, chunk), 1)
    causal = row >= col
    tri = jnp.where(causal, 1.0, 0.0).astype(BF16)
    scale = hdk ** -0.5
    outs = []
    for c in range(tq // chunk):
        r0 = c * chunk
        la_c = la[r0:r0 + chunk, :]
        la_hi = la_c.astype(BF16)
        la_lo = (la_c - la_hi.astype(F32)).astype(BF16)
        gcum_all = _bdot(tri, la_hi) + _bdot(tri, la_lo)
        head_out = []
        for hd in range(heads):
            gcum = gcum_all[:, hd * hdk:(hd + 1) * hdk]
            g_last = gcum[chunk - 1:chunk, :]
            q_c = proj[r0:r0 + chunk, hd * hdk:(hd + 1) * hdk] * scale
            k_c = proj[r0:r0 + chunk, dk + hd * hdk:dk + (hd + 1) * hdk]
            v_c = proj[r0:r0 + chunk, 2 * dk + hd * hdv:2 * dk + (hd + 1) * hdv].astype(BF16)
            q_s = (q_c * jnp.exp(gcum)).astype(BF16)
            k_s = (k_c * jnp.exp(-gcum)).astype(BF16)
            k_end = (k_c * jnp.exp(g_last - gcum)).astype(BF16)
            scores = lax.dot_general(q_s, k_s, nt, preferred_element_type=F32)
            scores = jnp.where(causal, scores, 0.0).astype(BF16)
            state_t = st_ref[hd]
            o = _bdot(scores, v_c) + lax.dot_general(q_s, state_t.astype(BF16), nt,
                                                     preferred_element_type=F32)
            kv_t = lax.dot_general(v_c, k_end, tn, preferred_element_type=F32)
            st_ref[hd] = state_t * jnp.exp(g_last) + kv_t
            head_out.append(o * lax.rsqrt(jnp.mean(o * o, axis=-1, keepdims=True) + EPS))
        outs.append(jnp.concatenate(head_out, axis=1))
    o_all = jnp.concatenate(outs, axis=0)
    r = proj[:, 2 * dk + dv:]
    o_all = (o_all * gn_ref[...] * _silu(r)).astype(BF16)
    o_ref[0] = xf + _bdot(o_all, wo_ref[...])


def _gla_layer(x, ln, w_in, w_g2, b_g2, g_norm, w_out, *, tq=256):
    bsz, seqlen, d = x.shape
    dk = w_g2.shape[1]
    dv = w_out.shape[0]
    nmain = 2 * dk + 2 * dv
    tq = min(tq, seqlen)
    w_main = w_in[:, :nmain].astype(BF16)
    w_glow = jnp.pad(w_in[:, nmain:], ((0, 0), (0, LANES - GLA_GATE_RANK))).astype(BF16)
    w_g2p = jnp.pad(w_g2, ((0, LANES - GLA_GATE_RANK), (0, 0))).astype(BF16)
    hdk, hdv = dk // GLA_HEADS, dv // GLA_HEADS
    const = lambda b, t: (0, 0)
    return pl.pallas_call(
        functools.partial(_gla_kernel, tq=tq, dk=dk, dv=dv, heads=GLA_HEADS),
        out_shape=jax.ShapeDtypeStruct(x.shape, F32),
        grid=(bsz, seqlen // tq),
        in_specs=[pl.BlockSpec((1, tq, d), lambda b, t: (b, t, 0)),
                  pl.BlockSpec((1, d), const),
                  _resident((d, nmain), const),
                  _resident((d, LANES), const),
                  _resident((LANES, dk), const),
                  pl.BlockSpec((1, dk), const),
                  pl.BlockSpec((1, dv), const),
                  _resident((dv, d), const)],
        out_specs=pl.BlockSpec((1, tq, d), lambda b, t: (b, t, 0)),
        scratch_shapes=[pltpu.VMEM((GLA_HEADS, hdv, hdk), F32)],
        compiler_params=_params(("parallel", "arbitrary"), 48),
        name="gla",
    )(x, ln.reshape(1, d), w_main, w_glow, w_g2p, b_g2.reshape(1, dk), g_norm.reshape(1, dv),
      w_out.astype(BF16))


def _swa_kernel(sink_ref, x_ref, ln_ref, wqkv_ref, bqkv_ref, wo_ref, bo_ref, o_ref, kprev_ref, vprev_ref,
                *, tq, q_heads):
    hd = SWA_HEAD_DIM
    group = q_heads // SWA_KV_HEADS
    blk = SWA_BLOCK
    nt = (((1,), (1,)), ((), ()))
    t = pl.program_id(1)

    @pl.when(t == 0)
    def _():
        kprev_ref[...] = jnp.zeros_like(kprev_ref)
        vprev_ref[...] = jnp.zeros_like(vprev_ref)

    xf = x_ref[0]
    h = _rms(xf, ln_ref[...]).astype(BF16)
    qkv = _bdot(h, wqkv_ref[...]) + bqkv_ref[...]
    nq = q_heads * hd
    nkv = SWA_KV_HEADS * hd
    rows = group * blk
    qi = lax.broadcasted_iota(jnp.int32, (rows, 2 * blk), 0) % blk
    kj = lax.broadcasted_iota(jnp.int32, (rows, 2 * blk), 1)
    dist = qi + blk - kj
    in_window = (dist >= 0) & (dist < SWA_WINDOW)
    dist_f = dist.astype(F32)
    g_of_row = (lax.broadcasted_iota(jnp.int32, (rows, 1), 0) // blk).astype(F32)
    outs = []
    for i in range(tq // blk):
        r0 = i * blk
        k_cur = qkv[r0:r0 + blk, nq:nq + nkv].astype(BF16)
        v_cur = qkv[r0:r0 + blk, nq + nkv:nq + 2 * nkv].astype(BF16)
        if i == 0:
            k_prev, v_prev = kprev_ref[...], vprev_ref[...]
            mask = in_window & ((t > 0) | (kj >= blk))
        else:
            mask = in_window
        kb = jnp.concatenate([k_prev, k_cur], axis=0)
        vb = jnp.concatenate([v_prev, v_cur], axis=0)
        head_cols = []
        for kh in range(SWA_KV_HEADS):
            q_st = jnp.concatenate(
                [qkv[r0:r0 + blk, (kh * group + g) * hd:(kh * group + g + 1) * hd] for g in range(group)],
                axis=0).astype(BF16)
            s = lax.dot_general(q_st, kb[:, kh * hd:(kh + 1) * hd], nt, preferred_element_type=F32)
            slope = jnp.exp2(-8.0 * (g_of_row + (kh * group + 1)) / q_heads)
            s = s * hd ** -0.5 - slope * dist_f
            s = jnp.where(mask, s, MASK_VALUE)
            sink = jnp.concatenate(
                [jnp.full((blk, 1), sink_ref[kh * group + g], F32) for g in range(group)], axis=0)
            m = jnp.maximum(jnp.max(s, axis=-1, keepdims=True), sink)
            p = jnp.exp(s - m)
            denom = jnp.sum(p, axis=-1, keepdims=True) + jnp.exp(sink - m)
            probs = (p / denom).astype(BF16)
            o = _bdot(probs, vb[:, kh * hd:(kh + 1) * hd])
            head_cols.extend(o[g * blk:(g + 1) * blk, :] for g in range(group))
        outs.append(jnp.concatenate(head_cols, axis=1))
        k_prev, v_prev = k_cur, v_cur
    kprev_ref[...] = k_prev
    vprev_ref[...] = v_prev
    o_all = jnp.concatenate(outs, axis=0).astype(BF16)
    o_ref[0] = xf + _bdot(o_all, wo_ref[...]) + bo_ref[...]


def _swa_layer(x, ln, w_qkv, b_qkv, sinks, w_out, b_out, *, tq=256):
    bsz, seqlen, d = x.shape
    nqkv = w_qkv.shape[1]
    q_heads = sinks.shape[0]
    nkv = SWA_KV_HEADS * SWA_HEAD_DIM
    tq = min(tq, seqlen)
    const = lambda b, t, s: (0, 0)
    return pl.pallas_call(
        functools.partial(_swa_kernel, tq=tq, q_heads=q_heads),
        out_shape=jax.ShapeDtypeStruct(x.shape, F32),
        grid_spec=pltpu.PrefetchScalarGridSpec(
            num_scalar_prefetch=1,
            grid=(bsz, seqlen // tq),
            in_specs=[pl.BlockSpec((1, tq, d), lambda b, t, s: (b, t, 0)),
                      pl.BlockSpec((1, d), const),
                      _resident((d, nqkv), const),
                      pl.BlockSpec((1, nqkv), const),
                      _resident((d, d), const),
                      pl.BlockSpec((1, d), const)],
            out_specs=pl.BlockSpec((1, tq, d), lambda b, t, s: (b, t, 0)),
            scratch_shapes=[pltpu.VMEM((SWA_BLOCK, nkv), BF16), pltpu.VMEM((SWA_BLOCK, nkv), BF16)]),
        compiler_params=_params(("parallel", "arbitrary"), 48),
        name="swa",
    )(sinks, x, ln.reshape(1, d), w_qkv.astype(BF16), b_qkv.reshape(1, nqkv), w_out.astype(BF16),
      b_out.reshape(1, d))


def _router_kernel(x_ref, ln_ref, whi_ref, wlo_ref, idx_ref, gate_ref):
    nt = (((1,), (1,)), ((), ()))
    h = _rms(x_ref[...], ln_ref[...])
    h_hi = h.astype(BF16)
    h_lo = (h - h_hi.astype(F32)).astype(BF16)
    w_hi, w_lo = whi_ref[...], wlo_ref[...]
    logits = (lax.dot_general(w_hi, h_hi, nt, preferred_element_type=F32)
              + lax.dot_general(w_hi, h_lo, nt, preferred_element_type=F32)
              + lax.dot_general(w_lo, h_hi, nt, preferred_element_type=F32))
    n_exp = logits.shape[0]
    eid = lax.broadcasted_iota(jnp.int32, logits.shape, 0)
    m1 = jnp.max(logits, axis=0, keepdims=True)
    i1 = jnp.min(jnp.where(logits == m1, eid, n_exp), axis=0, keepdims=True)
    rest = jnp.where(eid == i1, -jnp.inf, logits)
    m2 = jnp.max(rest, axis=0, keepdims=True)
    i2 = jnp.min(jnp.where(rest == m2, eid, n_exp), axis=0, keepdims=True)
    e2 = jnp.exp(m2 - m1)
    g1 = 1.0 / (1.0 + e2)
    idx_ref[...] = jnp.concatenate([i1, i2], axis=0)
    gate_ref[...] = jnp.concatenate([g1, e2 * g1], axis=0)


def _router(x2, ln, w_router, *, tm=512):
    ntok, d = x2.shape
    n_exp = w_router.shape[1]
    tm = min(tm, ntok)
    wt = w_router.T
    w_hi = wt.astype(BF16)
    w_lo = (wt - w_hi.astype(F32)).astype(BF16)
    return pl.pallas_call(
        _router_kernel,
        out_shape=(jax.ShapeDtypeStruct((TOP_K, ntok), jnp.int32), jax.ShapeDtypeStruct((TOP_K, ntok), F32)),
        grid=(ntok // tm,),
        in_specs=[pl.BlockSpec((tm, d), lambda i: (i, 0)),
                  pl.BlockSpec((1, d), lambda i: (0, 0)),
                  pl.BlockSpec((n_exp, d), lambda i: (0, 0)),
                  pl.BlockSpec((n_exp, d), lambda i: (0, 0))],
        out_specs=(pl.BlockSpec((TOP_K, tm), lambda i: (0, i)), pl.BlockSpec((TOP_K, tm), lambda i: (0, i))),
        compiler_params=_params(("parallel",), 32),
        name="moe_router",
    )(x2, ln.reshape(1, d), w_hi, w_lo)


def _moe_plan(idx, n_exp, tile):
    nslots = idx.size
    flat = idx.reshape(-1)
    onehot = (flat[None, :] == jnp.arange(n_exp, dtype=jnp.int32)[:, None]).astype(jnp.int32)
    csum = jnp.cumsum(onehot, axis=1)
    counts = csum[:, -1]
    ends = jnp.cumsum(counts)
    offs = ends - counts
    rank = jnp.sum(onehot * (csum - 1 + offs[:, None]), axis=0).reshape(idx.shape)
    n_tiles = nslots // tile
    n_visits = n_tiles + n_exp - 1
    first_tile = offs // tile
    last_tile = (ends - 1) // tile
    nvis = jnp.where(counts > 0, last_tile - first_tile + 1, 0)
    vend = jnp.cumsum(nvis)
    vstart = vend - nvis
    total = vend[-1]
    v = jnp.arange(n_visits, dtype=jnp.int32)
    vc = jnp.minimum(v, total - 1)
    e = jnp.minimum(jnp.sum((vc[:, None] >= vend[None, :]).astype(jnp.int32), axis=1), n_exp - 1)
    sel = (e[:, None] == jnp.arange(n_exp, dtype=jnp.int32)[None, :]).astype(jnp.int32)
    pick = lambda a: jnp.sum(sel * a[None, :], axis=1)
    tile_id = pick(first_tile) + vc - pick(vstart)
    lo = jnp.maximum(pick(offs), tile_id * tile) - tile_id * tile
    hi = jnp.minimum(pick(ends), (tile_id + 1) * tile) - tile_id * tile
    valid = v < total
    lo = jnp.where(valid, lo, 0)
    hi = jnp.where(valid, hi, 0)
    prev_tile = jnp.concatenate([jnp.full((1,), -1, jnp.int32), tile_id[:-1]])
    first = (valid & (tile_id != prev_tile)).astype(jnp.int32)
    meta = jnp.stack([tile_id, e, lo, hi, first]).astype(jnp.int32)
    return rank.astype(jnp.int32), meta


def _dispatch_kernel(rank_ref, x_ref, ln_ref, xg_ref, h_ref, sem):
    tm = x_ref.shape[0]
    h_ref[...] = _rms(x_ref[...], ln_ref[...])

    def row_copy(k, i):
        return pltpu.make_async_copy(h_ref.at[pl.ds(i, 1)], xg_ref.at[pl.ds(rank_ref[k, i], 1)], sem)

    def issue(i, carry):
        for k in range(TOP_K):
            row_copy(k, i).start()
        return carry

    def drain(i, carry):
        for k in range(TOP_K):
            row_copy(k, i).wait()
        return carry

    lax.fori_loop(0, tm, issue, 0)
    lax.fori_loop(0, tm, drain, 0)


def _dispatch(x2, ln, rank, *, tm=256):
    ntok, d = x2.shape
    tm = min(tm, ntok)
    return pl.pallas_call(
        _dispatch_kernel,
        out_shape=jax.ShapeDtypeStruct((TOP_K * ntok, d), F32),
        grid=(ntok // tm,),
        in_specs=[pl.BlockSpec((TOP_K, tm), lambda i: (0, i), memory_space=pltpu.SMEM),
                  pl.BlockSpec((tm, d), lambda i: (i, 0)),
                  pl.BlockSpec((1, d), lambda i: (0, 0))],
        out_specs=pl.BlockSpec(memory_space=pl.ANY),
        scratch_shapes=[pltpu.VMEM((tm, d), F32), pltpu.SemaphoreType.DMA],
        compiler_params=_params(("arbitrary",), 32),
        name="moe_dispatch",
    )(rank, x2, ln.reshape(1, d))


def _expert_kernel(meta_ref, x_ref, wg_ref, wu_ref, wd_ref, o_ref, *, ts):
    v = pl.program_id(0)
    hc = pl.program_id(1)
    lo, hi, first = meta_ref[2, v], meta_ref[3, v], meta_ref[4, v]

    @pl.when((first == 1) & (hc == 0))
    def _():
        o_ref[...] = jnp.zeros_like(o_ref)

    for sub in range(x_ref.shape[0] // ts):
        r0 = sub * ts

        @pl.when((lo < r0 + ts) & (hi > r0))
        def _():
            xs = x_ref[r0:r0 + ts, :].astype(BF16)
            act = (_silu(_bdot(xs, wg_ref[0])) * _bdot(xs, wu_ref[0])).astype(BF16)
            y = _bdot(act, wd_ref[0])
            rows = r0 + lax.broadcasted_iota(jnp.int32, (ts, 1), 0)
            o_ref[r0:r0 + ts, :] += jnp.where((rows >= lo) & (rows < hi), y, 0.0)


def _experts(xg, meta, w_gate_up, w_down, *, tile, th=512, ts=512):
    nrows, d = xg.shape
    n_exp, hidden, _ = w_down.shape
    n_hc = hidden // th
    ts = min(ts, tile)
    wgu = w_gate_up.astype(BF16)
    return pl.pallas_call(
        functools.partial(_expert_kernel, ts=ts),
        out_shape=jax.ShapeDtypeStruct((nrows, d), F32),
        grid_spec=pltpu.PrefetchScalarGridSpec(
            num_scalar_prefetch=1,
            grid=(meta.shape[1], n_hc),
            in_specs=[pl.BlockSpec((tile, d), lambda v, c, m: (m[0, v], 0)),
                      pl.BlockSpec((1, d, th), lambda v, c, m: (m[1, v], 0, c)),
                      pl.BlockSpec((1, d, th), lambda v, c, m: (m[1, v], 0, c + n_hc)),
                      pl.BlockSpec((1, th, d), lambda v, c, m: (m[1, v], c, 0))],
            out_specs=pl.BlockSpec((tile, d), lambda v, c, m: (m[0, v], 0))),
        compiler_params=_params(("arbitrary", "arbitrary"), 56),
        name="moe_experts",
    )(meta, xg, wgu, wgu, w_down.astype(BF16))


def _combine_kernel(rank_ref, x_ref, gate_ref, fg_ref, y_ref, o_ref, b0_ref, b1_ref, sem, *, final_norm):
    tm = x_ref.shape[0]
    bufs = (b0_ref, b1_ref)

    def row_copy(k, i):
        return pltpu.make_async_copy(y_ref.at[pl.ds(rank_ref[k, i], 1)], bufs[k].at[pl.ds(i, 1)], sem)

    def issue(i, carry):
        for k in range(TOP_K):
            row_copy(k, i).start()
        return carry

    def drain(i, carry):
        for k in range(TOP_K):
            row_copy(k, i).wait()
        return carry

    lax.fori_loop(0, tm, issue, 0)
    lax.fori_loop(0, tm, drain, 0)
    g = gate_ref[...]
    out = x_ref[...] + g[:, 0:1] * b0_ref[...] + g[:, 1:2] * b1_ref[...]
    if final_norm:
        out = _rms(out, fg_ref[...])
    o_ref[...] = out


def _combine(x2, gates_t, rank, y, final_gain, *, tm=256):
    ntok, d = x2.shape
    tm = min(tm, ntok)
    final_norm = final_gain is not None
    fg = (final_gain if final_norm else jnp.ones((d,), F32)).reshape(1, d)
    return pl.pallas_call(
        functools.partial(_combine_kernel, final_norm=final_norm),
        out_shape=jax.ShapeDtypeStruct((ntok, d), F32),
        grid=(ntok // tm,),
        in_specs=[pl.BlockSpec((TOP_K, tm), lambda i: (0, i), memory_space=pltpu.SMEM),
                  pl.BlockSpec((tm, d), lambda i: (i, 0)),
                  pl.BlockSpec((tm, TOP_K), lambda i: (i, 0)),
                  pl.BlockSpec((1, d), lambda i: (0, 0)),
                  pl.BlockSpec(memory_space=pl.ANY)],
        out_specs=pl.BlockSpec((tm, d), lambda i: (i, 0)),
        scratch_shapes=[pltpu.VMEM((tm, d), F32), pltpu.VMEM((tm, d), F32), pltpu.SemaphoreType.DMA],
        compiler_params=_params(("arbitrary",), 32),
        name="moe_combine",
    )(rank, x2, gates_t, fg, y)


def _moe_layer(x, ln, w_router, w_gate_up, w_down, *, final_gain=None, tile=2048):
    bsz, seqlen, d = x.shape
    ntok = bsz * seqlen
    n_exp = w_router.shape[1]
    tile = min(tile, TOP_K * ntok)
    x2 = x.reshape(ntok, d)
    idx, gates = _router(x2, ln, w_router)
    rank, meta = _moe_plan(idx, n_exp, tile)
    xg = _dispatch(x2, ln, rank)
    y = _experts(xg, meta, w_gate_up, w_down, tile=tile)
    out = _combine(x2, gates.T, rank, y, final_gain)
    return out.reshape(bsz, seqlen, d)


def kernel(x, l0_ln1, l0_s5_lam_re, l0_s5_lam_im, l0_s5_log_dt, l0_s5_b_re, l0_s5_b_im, l0_s5_c_re, l0_s5_c_im, l0_s5_d, l0_s5_w_glu, l0_s5_b_glu, l0_ln2, l0_ffn_w_gate_up, l0_ffn_w_down, l1_ln1, l1_gla_w_in, l1_gla_w_g2, l1_gla_b_g2, l1_gla_norm, l1_gla_w_out, l1_ln2, l1_moe_router, l1_moe_w_gate_up, l1_moe_w_down, l2_ln1, l2_swa_w_qkv, l2_swa_b_qkv, l2_swa_sinks, l2_swa_w_out, l2_swa_b_out, l2_ln2, l2_ffn_w_gate_up, l2_ffn_w_down, l3_ln1, l3_s5_lam_re, l3_s5_lam_im, l3_s5_log_dt, l3_s5_b_re, l3_s5_b_im, l3_s5_c_re, l3_s5_c_im, l3_s5_d, l3_s5_w_glu, l3_s5_b_glu, l3_ln2, l3_moe_router, l3_moe_w_gate_up, l3_moe_w_down, ln_f):
    x = _s5_layer(x, l0_ln1, l0_s5_lam_re, l0_s5_lam_im, l0_s5_log_dt, l0_s5_b_re, l0_s5_b_im,
                  l0_s5_c_re, l0_s5_c_im, l0_s5_d, l0_s5_w_glu, l0_s5_b_glu)
    x = _dense_ffn_layer(x, l0_ln2, l0_ffn_w_gate_up, l0_ffn_w_down)
    x = _gla_layer(x, l1_ln1, l1_gla_w_in, l1_gla_w_g2, l1_gla_b_g2, l1_gla_norm, l1_gla_w_out)
    x = _moe_layer(x, l1_ln2, l1_moe_router, l1_moe_w_gate_up, l1_moe_w_down)
    x = _swa_layer(x, l2_ln1, l2_swa_w_qkv, l2_swa_b_qkv, l2_swa_sinks, l2_swa_w_out, l2_swa_b_out)
    x = _dense_ffn_layer(x, l2_ln2, l2_ffn_w_gate_up, l2_ffn_w_down)
    x = _s5_layer(x, l3_ln1, l3_s5_lam_re, l3_s5_lam_im, l3_s5_log_dt, l3_s5_b_re, l3_s5_b_im,
                  l3_s5_c_re, l3_s5_c_im, l3_s5_d, l3_s5_w_glu, l3_s5_b_glu)
    return _moe_layer(x, l3_ln2, l3_moe_router, l3_moe_w_gate_up, l3_moe_w_down, final_gain=ln_f)
```

```python
import functools
import math

import jax
import jax.numpy as jnp
from jax import lax
from jax.experimental import pallas as pl
from jax.experimental.pallas import tpu as pltpu

F32 = jnp.float32
BF16 = jnp.bfloat16
EPS = 1e-6
LANES = 128
MIB = 1 << 20

S5_GROUP = 16
S5_STATE = 64
S5_CHUNK = 16
S5_SLAB_GROUPS = LANES // S5_GROUP
S5_PITCH_PAD = 8

GLA_HEADS = 4
GLA_GATE_RANK = 16
GLA_GATE_NORM = 16.0
GLA_CHUNK = 64

SWA_HEAD_DIM = 64
SWA_KV_HEADS = 2
SWA_WINDOW = 128
SWA_BLOCK = 128
MASK_VALUE = -1e30

TOP_K = 2


def _params(semantics, vmem_mib):
    return pltpu.CompilerParams(dimension_semantics=semantics, vmem_limit_bytes=vmem_mib * MIB)


def _resident(block_shape, index_map):
    return pl.BlockSpec(block_shape, index_map, pipeline_mode=pl.Buffered(1))


def _rms(xf, gain):
    return xf * lax.rsqrt(jnp.mean(xf * xf, axis=-1, keepdims=True) + EPS) * gain


def _gelu_tanh(x):
    return 0.5 * x * (1.0 + jnp.tanh(math.sqrt(2.0 / math.pi) * (x + 0.044715 * (x * x * x))))


def _silu(x):
    return x * jax.nn.sigmoid(x)


def _bdot(a, b):
    return jnp.dot(a, b, preferred_element_type=F32)


S5_ROW_TILE = 512


def _s5_norm_kernel(x_ref, g_ref, o_ref, scr_ref, *, nloc):
    h = _rms(x_ref[0], g_ref[...])
    nslab = scr_ref.shape[0]
    for c in range(nslab):
        scr_ref[c] = h[:, c * LANES:(c + 1) * LANES]
    for s in range(S5_CHUNK):
        rows = pl.ds(s, nloc, stride=S5_CHUNK)
        o_ref[0, s] = jnp.concatenate([scr_ref[c, rows, :] for c in range(nslab)], axis=1).astype(o_ref.dtype)


def _s5_norm(x, gain):
    bsz, seqlen, d = x.shape
    nch = seqlen // S5_CHUNK
    tm = min(S5_ROW_TILE, seqlen)
    nloc = tm // S5_CHUNK
    return pl.pallas_call(
        functools.partial(_s5_norm_kernel, nloc=nloc),
        out_shape=jax.ShapeDtypeStruct((bsz, S5_CHUNK, nch, d), BF16),
        grid=(bsz, seqlen // tm),
        in_specs=[pl.BlockSpec((1, tm, d), lambda b, i: (b, i, 0)),
                  pl.BlockSpec((1, d), lambda b, i: (0, 0))],
        out_specs=pl.BlockSpec((1, S5_CHUNK, nloc, d), lambda b, i: (b, 0, i, 0)),
        scratch_shapes=[pltpu.VMEM((d // LANES, tm, LANES), F32)],
        compiler_params=_params(("parallel", "parallel"), 32),
        name="s5_norm",
    )(x, gain.reshape(1, d))


def _tiling_matrix(rows, cols):
    p = lax.broadcasted_iota(jnp.int32, (rows, cols), 0)
    c = lax.broadcasted_iota(jnp.int32, (rows, cols), 1)
    return jnp.where(c % rows == p, 1.0, 0.0).astype(BF16)


def _same_group(shape, row_group, col_group):
    r = lax.broadcasted_iota(jnp.int32, shape, 0)
    c = lax.broadcasted_iota(jnp.int32, shape, 1)
    return (r // row_group) == (c // col_group)


def _s5_build_operators(tw_ref, vw_ref, mw_ref, toep_ref, win_ref, wout_ref):
    tn = (((0,), (0,)), ((), ()))
    nstate = vw_ref.shape[-1]
    half = S5_SLAB_GROUPS * nstate
    rep_ch = _tiling_matrix(S5_GROUP, LANES)
    rep_st = _tiling_matrix(nstate, half)
    diag = _same_group((LANES, LANES), S5_GROUP, S5_GROUP)
    taps = []
    for j in range(S5_CHUNK):
        e = lax.dot_general(tw_ref[0, j].astype(BF16), rep_ch, tn, preferred_element_type=F32)
        taps.append(jnp.where(diag, e, 0.0).astype(BF16))
    zero = jnp.zeros((LANES, LANES), BF16)
    for a in range(S5_CHUNK):
        for b in range(S5_CHUNK):
            toep_ref[a * LANES:(a + 1) * LANES, b * LANES:(b + 1) * LANES] = taps[b - a] if b >= a else zero
    diag_in = _same_group((LANES, half), S5_GROUP, nstate)
    diag_out = _same_group((half, LANES), nstate, S5_GROUP)
    for a in range(S5_CHUNK):
        for r in range(2):
            e = _bdot(vw_ref[0, 2 * a + r].astype(BF16), rep_st)
            win_ref[a * LANES:(a + 1) * LANES, r * half:(r + 1) * half] = jnp.where(diag_in, e, 0.0).astype(BF16)
            e = lax.dot_general(mw_ref[0, 2 * a + r].astype(BF16), rep_ch, tn, preferred_element_type=F32)
            wout_ref[r * half:(r + 1) * half, a * LANES:(a + 1) * LANES] = jnp.where(diag_out, e, 0.0).astype(BF16)


def _s5_conv_kernel(h_ref, tw_ref, vw_ref, mw_ref, a_ref, d_ref, o_ref, s_ref, toep_ref, win_ref, wout_ref,
                    *, nseq, nch):
    pitch = nch + S5_PITCH_PAD
    nl = a_ref.shape[1] // 2

    @pl.when(pl.program_id(1) == 0)
    def _():
        _s5_build_operators(tw_ref, vw_ref, mw_ref, toep_ref, win_ref, wout_ref)

    lhs = jnp.concatenate(
        [jnp.concatenate([h_ref[bl, s] for s in range(S5_CHUNK)], axis=1) for bl in range(nseq)], axis=0)
    bc = _bdot(lhs, win_ref[...])
    for bl in range(nseq):
        for j in range(2 * nl):
            s_ref[j, bl * pitch:bl * pitch + nch, :] = bc[bl * nch:(bl + 1) * nch, j * LANES:(j + 1) * LANES]
    a_re = [a_ref[0, j:j + 1, :] for j in range(nl)]
    a_im = [a_ref[0, nl + j:nl + j + 1, :] for j in range(nl)]

    def step(n, carry):
        p_re, p_im = carry
        rows = pl.ds(n, nseq, stride=pitch)
        n_re, n_im = [], []
        for j in range(nl):
            c_re = s_ref[j, rows, :]
            c_im = s_ref[nl + j, rows, :]
            s_ref[j, rows, :] = p_re[j]
            s_ref[nl + j, rows, :] = p_im[j]
            n_re.append(a_re[j] * p_re[j] - a_im[j] * p_im[j] + c_re)
            n_im.append(a_re[j] * p_im[j] + a_im[j] * p_re[j] + c_im)
        return tuple(n_re), tuple(n_im)

    zeros = tuple(jnp.zeros((nseq, LANES), F32) for _ in range(nl))
    lax.fori_loop(0, nch, step, (zeros, zeros))
    x_prev = jnp.concatenate(
        [jnp.concatenate([s_ref[j, bl * pitch:bl * pitch + nch, :] for j in range(2 * nl)], axis=1)
         for bl in range(nseq)], axis=0).astype(BF16)
    y = _bdot(lhs, toep_ref[...]) + _bdot(x_prev, wout_ref[...])
    dskip = d_ref[0]
    for bl in range(nseq):
        for s in range(S5_CHUNK):
            ys = y[bl * nch:(bl + 1) * nch, s * LANES:(s + 1) * LANES]
            ys = ys + dskip * h_ref[bl, s].astype(F32)
            o_ref[bl, s] = _gelu_tanh(ys).astype(o_ref.dtype)


def _s5_conv(hp, tw, vw, mw, a_pack, d_skip, *, nseq):
    bsz, _, nch, d = hp.shape
    nslab = d // LANES
    kdim = S5_CHUNK * LANES
    sdim = a_pack.shape[1] * LANES
    blk4 = lambda a: pl.BlockSpec((1,) + a.shape[1:], lambda c, b: (c, 0, 0, 0))
    return pl.pallas_call(
        functools.partial(_s5_conv_kernel, nseq=nseq, nch=nch),
        out_shape=jax.ShapeDtypeStruct(hp.shape, BF16),
        grid=(nslab, bsz // nseq),
        in_specs=[pl.BlockSpec((nseq, S5_CHUNK, nch, LANES), lambda c, b: (b, 0, 0, c)),
                  blk4(tw), blk4(vw), blk4(mw),
                  pl.BlockSpec((1, sdim // LANES, LANES), lambda c, b: (c, 0, 0)),
                  pl.BlockSpec((1, 1, LANES), lambda c, b: (c, 0, 0))],
        out_specs=pl.BlockSpec((nseq, S5_CHUNK, nch, LANES), lambda c, b: (b, 0, 0, c)),
        scratch_shapes=[pltpu.VMEM((sdim // LANES, nseq * (nch + S5_PITCH_PAD), LANES), F32),
                        pltpu.VMEM((kdim, kdim), BF16),
                        pltpu.VMEM((kdim, sdim), BF16),
                        pltpu.VMEM((sdim, kdim), BF16)],
        compiler_params=_params(("arbitrary", "arbitrary"), 56),
        name="s5_conv",
    )(hp, tw, vw, mw, a_pack, d_skip.reshape(nslab, 1, LANES))


def _s5_glu_kernel(y_ref, x_ref, w_ref, b_ref, o_ref, scr_ref, *, nloc):
    nslab = scr_ref.shape[0]
    y = jnp.concatenate([y_ref[0, s] for s in range(S5_CHUNK)], axis=0)
    u = y.astype(F32) * jax.nn.sigmoid(_bdot(y, w_ref[...]) + b_ref[...])
    for s in range(S5_CHUNK):
        rows = pl.ds(s, nloc, stride=S5_CHUNK)
        for c in range(nslab):
            scr_ref[c, rows, :] = u[s * nloc:(s + 1) * nloc, c * LANES:(c + 1) * LANES]
    o_ref[0] = x_ref[0] + jnp.concatenate([scr_ref[c] for c in range(nslab)], axis=1)


def _s5_glu(yp, x, w_glu, b_glu):
    bsz, seqlen, d = x.shape
    tm = min(S5_ROW_TILE, seqlen)
    nloc = tm // S5_CHUNK
    return pl.pallas_call(
        functools.partial(_s5_glu_kernel, nloc=nloc),
        out_shape=jax.ShapeDtypeStruct(x.shape, F32),
        grid=(bsz, seqlen // tm),
        in_specs=[pl.BlockSpec((1, S5_CHUNK, nloc, d), lambda b, i: (b, 0, i, 0)),
                  pl.BlockSpec((1, tm, d), lambda b, i: (b, i, 0)),
                  _resident((d, d), lambda b, i: (0, 0)),
                  pl.BlockSpec((1, d), lambda b, i: (0, 0))],
        out_specs=pl.BlockSpec((1, tm, d), lambda b, i: (b, i, 0)),
        scratch_shapes=[pltpu.VMEM((d // LANES, tm, LANES), F32)],
        compiler_params=_params(("parallel", "parallel"), 40),
        name="s5_glu",
    )(yp, x, w_glu.astype(BF16), b_glu.reshape(1, d))


def _s5_operators(lam_re, lam_im, log_dt, b_re, b_im, c_re, c_im):
    hi = lax.Precision.HIGHEST
    ngroups, nstate = lam_re.shape
    gpc = S5_SLAB_GROUPS
    nslab = ngroups // gpc
    dt = jnp.exp(log_dt)[:, None]
    j = jnp.arange(S5_CHUNK + 1, dtype=F32)[:, None, None]
    mag = jnp.exp(j * (lam_re * dt)[None])
    ang = j * (lam_im * dt)[None]
    pw_re, pw_im = mag * jnp.cos(ang), mag * jnp.sin(ang)
    num_re, num_im = pw_re[1] - 1.0, pw_im[1]
    den = lam_re * lam_re + lam_im * lam_im
    f_re = (num_re * lam_re + num_im * lam_im) / den
    f_im = (num_im * lam_re - num_re * lam_im) / den
    bb_re = f_re[..., None] * b_re - f_im[..., None] * b_im
    bb_im = f_re[..., None] * b_im + f_im[..., None] * b_re
    cp_re = c_re[None] * pw_re[:S5_CHUNK, :, None, :] - c_im[None] * pw_im[:S5_CHUNK, :, None, :]
    cp_im = c_re[None] * pw_im[:S5_CHUNK, :, None, :] + c_im[None] * pw_re[:S5_CHUNK, :, None, :]
    taps = (jnp.einsum('jghp,gpi->jghi', cp_re, bb_re, precision=hi)
            - jnp.einsum('jghp,gpi->jghi', cp_im, bb_im, precision=hi))
    tw = taps.reshape(S5_CHUNK, nslab, gpc, S5_GROUP, S5_GROUP)
    tw = tw.transpose(1, 0, 3, 2, 4).reshape(nslab, S5_CHUNK, S5_GROUP, LANES)
    jr = (S5_CHUNK - 1) - jnp.arange(S5_CHUNK, dtype=F32)[:, None, None]
    mag_r = jnp.exp(jr * (lam_re * dt)[None])
    ang_r = jr * (lam_im * dt)[None]
    rev_re, rev_im = mag_r * jnp.cos(ang_r), mag_r * jnp.sin(ang_r)
    v_re = rev_re[..., None] * bb_re[None] - rev_im[..., None] * bb_im[None]
    v_im = rev_re[..., None] * bb_im[None] + rev_im[..., None] * bb_re[None]
    v = jnp.stack([v_re, v_im], axis=0).reshape(2, S5_CHUNK, nslab, gpc, nstate, S5_GROUP)
    vw = v.transpose(2, 1, 0, 3, 5, 4).reshape(nslab, 2 * S5_CHUNK, LANES, nstate)
    m_re = c_re[None] * pw_re[1:, :, None, :] - c_im[None] * pw_im[1:, :, None, :]
    m_im = c_re[None] * pw_im[1:, :, None, :] + c_im[None] * pw_re[1:, :, None, :]
    m = jnp.stack([m_re, -m_im], axis=0).reshape(2, S5_CHUNK, nslab, gpc, S5_GROUP, nstate)
    mw = m.transpose(2, 1, 0, 4, 3, 5).reshape(nslab, 2 * S5_CHUNK, S5_GROUP, gpc * nstate)
    half = gpc * nstate // LANES
    a_pack = jnp.concatenate([pw_re[S5_CHUNK].reshape(nslab, half, LANES),
                              pw_im[S5_CHUNK].reshape(nslab, half, LANES)], axis=1)
    return tw, vw, mw, a_pack


def _s5_layer(x, ln, lam_re, lam_im, log_dt, b_re, b_im, c_re, c_im, d_skip, w_glu, b_glu, *, nseq=4):
    tw, vw, mw, a_pack = _s5_operators(lam_re, lam_im, log_dt, b_re, b_im, c_re, c_im)
    hp = _s5_norm(x, ln)
    yp = _s5_conv(hp, tw, vw, mw, a_pack, d_skip, nseq=min(nseq, x.shape[0]))
    return _s5_glu(yp, x, w_glu, b_glu)


def _dense_ffn_kernel(x_ref, g_ref, wg_ref, wu_ref, wd_ref, o_ref):
    xf = x_ref[...]
    h = _rms(xf, g_ref[...]).astype(BF16)
    act = (_silu(_bdot(h, wg_ref[...])) * _bdot(h, wu_ref[...])).astype(BF16)
    o_ref[...] = xf + _bdot(act, wd_ref[...])


def _dense_ffn_layer(x, ln, w_gate_up, w_down, *, tm=512):
    bsz, seqlen, d = x.shape
    ntok = bsz * seqlen
    hidden = w_down.shape[0]
    tm = min(tm, ntok)
    wgu = w_gate_up.astype(BF16)
    out = pl.pallas_call(
        _dense_ffn_kernel,
        out_shape=jax.ShapeDtypeStruct((ntok, d), F32),
        grid=(ntok // tm,),
        in_specs=[pl.BlockSpec((tm, d), lambda i: (i, 0)),
                  pl.BlockSpec((1, d), lambda i: (0, 0)),
                  _resident((d, hidden), lambda i: (0, 0)),
                  _resident((d, hidden), lambda i: (0, 1)),
                  _resident((hidden, d), lambda i: (0, 0))],
        out_specs=pl.BlockSpec((tm, d), lambda i: (i, 0)),
        compiler_params=_params(("parallel",), 56),
        name="dense_ffn",
    )(x.reshape(ntok, d), ln.reshape(1, d), wgu, wgu, w_down.astype(BF16))
    return out.reshape(bsz, seqlen, d)


def _log_sigmoid(z):
    return jnp.minimum(z, 0.0) - jnp.log(1.0 + jnp.exp(-jnp.abs(z)))


def _gla_kernel(x_ref, ln_ref, wm_ref, wgl_ref, wg2_ref, bg2_ref, gn_ref, wo_ref, o_ref, st_ref,
                *, tq, dk, dv, heads):
    hdk, hdv = dk // heads, dv // heads
    chunk = GLA_CHUNK
    nt = (((1,), (1,)), ((), ()))
    tn = (((0,), (0,)), ((), ()))

    @pl.when(pl.program_id(1) == 0)
    def _():
        st_ref[...] = jnp.zeros_like(st_ref)

    xf = x_ref[0]
    h = _rms(xf, ln_ref[...]).astype(BF16)
    proj = _bdot(h, wm_ref[...])
    glow = _bdot(h, wgl_ref[...]).astype(BF16)
    la = _log_sigmoid(_bdot(glow, wg2_ref[...]) + bg2_ref[...]) * (1.0 / GLA_GATE_NORM)
    row = lax.broadcasted_iota(jnp.int32, (chunk, chunk), 0)
    col = lax.broadcasted_iota(jnp.int32, (chunk, chunk), 1)
    causal = row >= col
    tri = jnp.where(causal, 1.0, 0.0).astype(BF16)
    scale = hdk ** -0.5
    outs = []
    for c in range(tq // chunk):
        r0 = c * chunk
        la_c = la[r0:r0 + chunk, :]
        la_hi = la_c.astype(BF16)
        la_lo = (la_c - la_hi.astype(F32)).astype(BF16)
        gcum_all = _bdot(tri, la_hi) + _bdot(tri, la_lo)
        head_out = []
        for hd in range(heads):
            gcum = gcum_all[:, hd * hdk:(hd + 1) * hdk]
            g_last = gcum[chunk - 1:chunk, :]
            q_c = proj[r0:r0 + chunk, hd * hdk:(hd + 1) * hdk] * scale
            k_c = proj[r0:r0 + chunk, dk + hd * hdk:dk + (hd + 1) * hdk]
            v_c = proj[r0:r0 + chunk, 2 * dk + hd * hdv:2 * dk + (hd + 1) * hdv].astype(BF16)
            q_s = (q_c * jnp.exp(gcum)).astype(BF16)
            k_s = (k_c * jnp.exp(-gcum)).astype(BF16)
            k_end = (k_c * jnp.exp(g_last - gcum)).astype(BF16)
            scores = lax.dot_general(q_s, k_s, nt, preferred_element_type=F32)
            scores = jnp.where(causal, scores, 0.0).astype(BF16)
            state_t = st_ref[hd]
            o = _bdot(scores, v_c) + lax.dot_general(q_s, state_t.astype(BF16), nt,
                                                     preferred_element_type=F32)
            kv_t = lax.dot_general(v_c, k_end, tn, preferred_element_type=F32)
            st_ref[hd] = state_t * jnp.exp(g_last) + kv_t
            head_out.append(o * lax.rsqrt(jnp.mean(o * o, axis=-1, keepdims=True) + EPS))
        outs.append(jnp.concatenate(head_out, axis=1))
    o_all = jnp.concatenate(outs, axis=0)
    r = proj[:, 2 * dk + dv:]
    o_all = (o_all * gn_ref[...] * _silu(r)).astype(BF16)
    o_ref[0] = xf + _bdot(o_all, wo_ref[...])


def _gla_layer(x, ln, w_in, w_g2, b_g2, g_norm, w_out, *, tq=256):
    bsz, seqlen, d = x.shape
    dk = w_g2.shape[1]
    dv = w_out.shape[0]
    nmain = 2 * dk + 2 * dv
    tq = min(tq, seqlen)
    w_main = w_in[:, :nmain].astype(BF16)
    w_glow = jnp.pad(w_in[:, nmain:], ((0, 0), (0, LANES - GLA_GATE_RANK))).astype(BF16)
    w_g2p = jnp.pad(w_g2, ((0, LANES - GLA_GATE_RANK), (0, 0))).astype(BF16)
    hdk, hdv = dk // GLA_HEADS, dv // GLA_HEADS
    const = lambda b, t: (0, 0)
    return pl.pallas_call(
        functools.partial(_gla_kernel, tq=tq, dk=dk, dv=dv, heads=GLA_HEADS),
        out_shape=jax.ShapeDtypeStruct(x.shape, F32),
        grid=(bsz, seqlen // tq),
        in_specs=[pl.BlockSpec((1, tq, d), lambda b, t: (b, t, 0)),
                  pl.BlockSpec((1, d), const),
                  _resident((d, nmain), const),
                  _resident((d, LANES), const),
                  _resident((LANES, dk), const),
                  pl.BlockSpec((1, dk), const),
                  pl.BlockSpec((1, dv), const),
                  _resident((dv, d), const)],
        out_specs=pl.BlockSpec((1, tq, d), lambda b, t: (b, t, 0)),
        scratch_shapes=[pltpu.VMEM((GLA_HEADS, hdv, hdk), F32)],
        compiler_params=_params(("parallel", "arbitrary"), 48),
        name="gla",
    )(x, ln.reshape(1, d), w_main, w_glow, w_g2p, b_g2.reshape(1, dk), g_norm.reshape(1, dv),
      w_out.astype(BF16))


LOG2E = math.log2(math.e)
SWA_HEAD_UNROLL = 4


def _swa_kernel(sink_ref, x_ref, ln_ref, wqkv_ref, bqkv_ref, wo_ref, bo_ref, o_ref, k_ref, v_ref,
                bias_ref, q_ref, a_ref, *, tq, q_heads):
    group = q_heads // SWA_KV_HEADS
    blk = SWA_BLOCK
    nt = (((1,), (1,)), ((), ()))
    b = pl.program_id(0)
    t = pl.program_id(1)
    nq = q_heads * LANES

    @pl.when((b == 0) & (t == 0))
    def _():
        qi = lax.broadcasted_iota(jnp.int32, (blk, 2 * blk), 0)
        kj = lax.broadcasted_iota(jnp.int32, (blk, 2 * blk), 1)
        dist = qi + blk - kj
        in_window = (dist >= 0) & (dist < SWA_WINDOW)
        for hq in range(q_heads):
            slope = 2.0 ** (-8.0 * (hq + 1) / q_heads)
            bias_ref[hq] = jnp.where(in_window, -(slope * LOG2E) * dist.astype(F32), MASK_VALUE)

    @pl.when(t == 0)
    def _():
        k_ref[0:blk, :] = jnp.zeros((blk, LANES), BF16)
        v_ref[0:blk, :] = jnp.zeros((blk, LANES), BF16)

    xf = x_ref[0]
    h = _rms(xf, ln_ref[...]).astype(BF16)
    qkv = _bdot(h, wqkv_ref[...]) + bqkv_ref[...]
    for hq in range(q_heads):
        q_ref[hq] = qkv[:, hq * LANES:(hq + 1) * LANES].astype(BF16)
    k_ref[blk:blk + tq, :] = qkv[:, nq:nq + LANES].astype(BF16)
    v_ref[blk:blk + tq, :] = qkv[:, nq + LANES:nq + 2 * LANES].astype(BF16)
    kj_row = lax.broadcasted_iota(jnp.int32, (1, 2 * blk), 1)
    no_prev = jnp.where(kj_row < blk, jnp.where(t == 0, MASK_VALUE, 0.0), 0.0)

    def heads(j, carry):
        for u in range(SWA_HEAD_UNROLL):
            hq = j * SWA_HEAD_UNROLL + u
            sink = sink_ref[hq] * LOG2E
            for i in range(tq // blk):
                r0 = i * blk
                s = lax.dot_general(q_ref[hq, r0:r0 + blk, :], k_ref[r0:r0 + 2 * blk, :], nt,
                                    preferred_element_type=F32) + bias_ref[hq]
                if i == 0:
                    s = s + no_prev
                m = jnp.maximum(jnp.max(s, axis=-1, keepdims=True), sink)
                p = jnp.exp2(s - m)
                denom = jnp.sum(p, axis=-1, keepdims=True) + jnp.exp2(sink - m)
                o = _bdot(p.astype(BF16), v_ref[r0:r0 + 2 * blk, :]) * (1.0 / denom)
                a_ref[hq, r0:r0 + blk, :] = o.astype(BF16)
        return carry

    lax.fori_loop(0, q_heads // SWA_HEAD_UNROLL, heads, 0)
    k_ref[0:blk, :] = k_ref[tq:tq + blk, :]
    v_ref[0:blk, :] = v_ref[tq:tq + blk, :]
    o_all = jnp.concatenate([a_ref[hq] for hq in range(q_heads)], axis=1)
    o_ref[0] = xf + _bdot(o_all, wo_ref[...]) + bo_ref[...]


def _swa_layer(x, ln, w_qkv, b_qkv, sinks, w_out, b_out, *, tq=512):
    bsz, seqlen, d = x.shape
    hd = SWA_HEAD_DIM
    q_heads = sinks.shape[0]
    group = q_heads // SWA_KV_HEADS
    nq = q_heads * hd
    tq = min(tq, seqlen)
    kv_of_head = jnp.arange(q_heads) // group
    place = (jnp.arange(LANES // hd)[None, :] == kv_of_head[:, None]).astype(F32)
    q_scale = hd ** -0.5 * LOG2E
    wq = (w_qkv[:, :nq] * q_scale).reshape(d, q_heads, 1, hd) * place[None, :, :, None]
    bq = (b_qkv[:nq] * q_scale).reshape(q_heads, 1, hd) * place[:, :, None]
    w_all = jnp.concatenate([wq.reshape(d, q_heads * LANES), w_qkv[:, nq:]], axis=1).astype(BF16)
    b_all = jnp.concatenate([bq.reshape(q_heads * LANES), b_qkv[nq:]]).reshape(1, -1)
    wo = (w_out.reshape(q_heads, 1, hd, d) * place[:, :, None, None]).reshape(q_heads * LANES, d).astype(BF16)
    nall = w_all.shape[1]
    const = lambda b, t, s: (0, 0)
    return pl.pallas_call(
        functools.partial(_swa_kernel, tq=tq, q_heads=q_heads),
        out_shape=jax.ShapeDtypeStruct(x.shape, F32),
        grid_spec=pltpu.PrefetchScalarGridSpec(
            num_scalar_prefetch=1,
            grid=(bsz, seqlen // tq),
            in_specs=[pl.BlockSpec((1, tq, d), lambda b, t, s: (b, t, 0)),
                      pl.BlockSpec((1, d), const),
                      _resident((d, nall), const),
                      pl.BlockSpec((1, nall), const),
                      _resident((q_heads * LANES, d), const),
                      pl.BlockSpec((1, d), const)],
            out_specs=pl.BlockSpec((1, tq, d), lambda b, t, s: (b, t, 0)),
            scratch_shapes=[pltpu.VMEM((SWA_BLOCK + tq, LANES), BF16), pltpu.VMEM((SWA_BLOCK + tq, LANES), BF16),
                            pltpu.VMEM((q_heads, SWA_BLOCK, 2 * SWA_BLOCK), F32),
                            pltpu.VMEM((q_heads, tq, LANES), BF16), pltpu.VMEM((q_heads, tq, LANES), BF16)]),
        compiler_params=_params(("arbitrary", "arbitrary"), 48),
        name="swa",
    )(sinks, x, ln.reshape(1, d), w_all, b_all, wo, b_out.reshape(1, d))


def _router_kernel(x_ref, ln_ref, whi_ref, wlo_ref, idx_ref, gate_ref):
    nt = (((1,), (1,)), ((), ()))
    h = _rms(x_ref[...], ln_ref[...])
    h_hi = h.astype(BF16)
    h_lo = (h - h_hi.astype(F32)).astype(BF16)
    w_hi, w_lo = whi_ref[...], wlo_ref[...]
    logits = (lax.dot_general(w_hi, h_hi, nt, preferred_element_type=F32)
              + lax.dot_general(w_hi, h_lo, nt, preferred_element_type=F32)
              + lax.dot_general(w_lo, h_hi, nt, preferred_element_type=F32))
    n_exp = logits.shape[0]
    eid = lax.broadcasted_iota(jnp.int32, logits.shape, 0)
    m1 = jnp.max(logits, axis=0, keepdims=True)
    i1 = jnp.min(jnp.where(logits == m1, eid, n_exp), axis=0, keepdims=True)
    rest = jnp.where(eid == i1, -jnp.inf, logits)
    m2 = jnp.max(rest, axis=0, keepdims=True)
    i2 = jnp.min(jnp.where(rest == m2, eid, n_exp), axis=0, keepdims=True)
    e2 = jnp.exp(m2 - m1)
    g1 = 1.0 / (1.0 + e2)
    idx_ref[...] = jnp.concatenate([i1, i2], axis=0)
    gate_ref[...] = jnp.concatenate([g1, e2 * g1], axis=0)


def _router(x2, ln, w_router, *, tm=512):
    ntok, d = x2.shape
    n_exp = w_router.shape[1]
    tm = min(tm, ntok)
    wt = w_router.T
    w_hi = wt.astype(BF16)
    w_lo = (wt - w_hi.astype(F32)).astype(BF16)
    return pl.pallas_call(
        _router_kernel,
        out_shape=(jax.ShapeDtypeStruct((TOP_K, ntok), jnp.int32), jax.ShapeDtypeStruct((TOP_K, ntok), F32)),
        grid=(ntok // tm,),
        in_specs=[pl.BlockSpec((tm, d), lambda i: (i, 0)),
                  pl.BlockSpec((1, d), lambda i: (0, 0)),
                  pl.BlockSpec((n_exp, d), lambda i: (0, 0)),
                  pl.BlockSpec((n_exp, d), lambda i: (0, 0))],
        out_specs=(pl.BlockSpec((TOP_K, tm), lambda i: (0, i)), pl.BlockSpec((TOP_K, tm), lambda i: (0, i))),
        compiler_params=_params(("parallel",), 32),
        name="moe_router",
    )(x2, ln.reshape(1, d), w_hi, w_lo)


def _moe_plan(idx, n_exp, tile):
    nslots = idx.size
    flat = idx.reshape(-1)
    onehot = (flat[None, :] == jnp.arange(n_exp, dtype=jnp.int32)[:, None]).astype(jnp.int32)
    csum = jnp.cumsum(onehot, axis=1)
    counts = csum[:, -1]
    ends = jnp.cumsum(counts)
    offs = ends - counts
    rank = jnp.sum(onehot * (csum - 1 + offs[:, None]), axis=0).reshape(idx.shape)
    n_tiles = nslots // tile
    n_visits = n_tiles + n_exp - 1
    first_tile = offs // tile
    last_tile = (ends - 1) // tile
    nvis = jnp.where(counts > 0, last_tile - first_tile + 1, 0)
    vend = jnp.cumsum(nvis)
    vstart = vend - nvis
    total = vend[-1]
    v = jnp.arange(n_visits, dtype=jnp.int32)
    vc = jnp.minimum(v, total - 1)
    e = jnp.minimum(jnp.sum((vc[:, None] >= vend[None, :]).astype(jnp.int32), axis=1), n_exp - 1)
    sel = (e[:, None] == jnp.arange(n_exp, dtype=jnp.int32)[None, :]).astype(jnp.int32)
    pick = lambda a: jnp.sum(sel * a[None, :], axis=1)
    tile_id = pick(first_tile) + vc - pick(vstart)
    lo = jnp.maximum(pick(offs), tile_id * tile) - tile_id * tile
    hi = jnp.minimum(pick(ends), (tile_id + 1) * tile) - tile_id * tile
    valid = v < total
    lo = jnp.where(valid, lo, 0)
    hi = jnp.where(valid, hi, 0)
    prev_tile = jnp.concatenate([jnp.full((1,), -1, jnp.int32), tile_id[:-1]])
    first = (valid & (tile_id != prev_tile)).astype(jnp.int32)
    meta = jnp.stack([tile_id, e, lo, hi, first]).astype(jnp.int32)
    return rank.astype(jnp.int32), meta


def _pack_bf16_pairs(h):
    half = h.shape[1] // 2
    bits = lax.bitcast_convert_type(h.astype(BF16).astype(F32), jnp.uint32)
    return (bits[:, half:] & jnp.uint32(0xFFFF0000)) | (bits[:, :half] >> 16)


def _unpack_bf16_pairs(u):
    lo = lax.bitcast_convert_type(u << 16, F32)
    hi = lax.bitcast_convert_type(u & jnp.uint32(0xFFFF0000), F32)
    return jnp.concatenate([lo, hi], axis=1).astype(BF16)


DMA_UNROLL = 8


def _start_and_wait_rows(tm, row_copy):
    def issue(j, carry):
        for u in range(DMA_UNROLL):
            for k in range(TOP_K):
                row_copy(k, j * DMA_UNROLL + u).start()
        return carry

    def drain(j, carry):
        for u in range(DMA_UNROLL):
            for k in range(TOP_K):
                row_copy(k, j * DMA_UNROLL + u).wait()
        return carry

    lax.fori_loop(0, tm // DMA_UNROLL, issue, 0)
    lax.fori_loop(0, tm // DMA_UNROLL, drain, 0)


def _dispatch_kernel(rank_ref, x_ref, ln_ref, xg_ref, h_ref, sem):
    h_ref[...] = _pack_bf16_pairs(_rms(x_ref[...], ln_ref[...]))

    def row_copy(k, i):
        return pltpu.make_async_copy(h_ref.at[pl.ds(i, 1)], xg_ref.at[pl.ds(rank_ref[k, i], 1)], sem)

    _start_and_wait_rows(x_ref.shape[0], row_copy)


def _dispatch(x2, ln, rank, *, tm=256):
    ntok, d = x2.shape
    tm = min(tm, ntok)
    return pl.pallas_call(
        _dispatch_kernel,
        out_shape=jax.ShapeDtypeStruct((TOP_K * ntok, d // 2), jnp.uint32),
        grid=(ntok // tm,),
        in_specs=[pl.BlockSpec((TOP_K, tm), lambda i: (0, i), memory_space=pltpu.SMEM),
                  pl.BlockSpec((tm, d), lambda i: (i, 0)),
                  pl.BlockSpec((1, d), lambda i: (0, 0))],
        out_specs=pl.BlockSpec(memory_space=pl.ANY),
        scratch_shapes=[pltpu.VMEM((tm, d // 2), jnp.uint32), pltpu.SemaphoreType.DMA],
        compiler_params=_params(("arbitrary",), 32),
        name="moe_dispatch",
    )(rank, x2, ln.reshape(1, d))


def _expert_kernel(meta_ref, x_ref, wg_ref, wu_ref, wd_ref, o_ref, *, ts):
    v = pl.program_id(0)
    hc = pl.program_id(1)
    lo, hi, first = meta_ref[2, v], meta_ref[3, v], meta_ref[4, v]

    @pl.when((first == 1) & (hc == 0))
    def _():
        o_ref[...] = jnp.zeros_like(o_ref)

    for sub in range(x_ref.shape[0] // ts):
        r0 = sub * ts

        @pl.when((lo < r0 + ts) & (hi > r0))
        def _():
            xs = _unpack_bf16_pairs(x_ref[r0:r0 + ts, :])
            gate = _bdot(xs, wg_ref[0].astype(BF16))
            up = _bdot(xs, wu_ref[0].astype(BF16))
            y = _bdot((_silu(gate) * up).astype(BF16), wd_ref[0].astype(BF16))
            rows = r0 + lax.broadcasted_iota(jnp.int32, (ts, 1), 0)
            o_ref[r0:r0 + ts, :] += jnp.where((rows >= lo) & (rows < hi), y, 0.0)


def _experts(xg, meta, w_gate_up, w_down, *, tile, th=512, ts=512):
    nrows = xg.shape[0]
    n_exp, hidden, d = w_down.shape
    n_hc = hidden // th
    ts = min(ts, tile)
    wgu = w_gate_up
    return pl.pallas_call(
        functools.partial(_expert_kernel, ts=ts),
        out_shape=jax.ShapeDtypeStruct((nrows, d), F32),
        grid_spec=pltpu.PrefetchScalarGridSpec(
            num_scalar_prefetch=1,
            grid=(meta.shape[1], n_hc),
            in_specs=[pl.BlockSpec((tile, d // 2), lambda v, c, m: (m[0, v], 0)),
                      pl.BlockSpec((1, d, th), lambda v, c, m: (m[1, v], 0, c)),
                      pl.BlockSpec((1, d, th), lambda v, c, m: (m[1, v], 0, c + n_hc)),
                      pl.BlockSpec((1, th, d), lambda v, c, m: (m[1, v], c, 0))],
            out_specs=pl.BlockSpec((tile, d), lambda v, c, m: (m[0, v], 0))),
        compiler_params=_params(("arbitrary", "arbitrary"), 56),
        name="moe_experts",
    )(meta, xg, wgu, wgu, w_down)


def _combine_kernel(rank_ref, x_ref, gate_ref, fg_ref, y_ref, o_ref, b0_ref, b1_ref, sem, *, final_norm):
    tm = x_ref.shape[0]
    bufs = (b0_ref, b1_ref)

    def row_copy(k, i):
        return pltpu.make_async_copy(y_ref.at[pl.ds(rank_ref[k, i], 1)], bufs[k].at[pl.ds(i, 1)], sem)

    _start_and_wait_rows(tm, row_copy)
    g = gate_ref[...]
    out = x_ref[...] + g[:, 0:1] * b0_ref[...] + g[:, 1:2] * b1_ref[...]
    if final_norm:
        out = _rms(out, fg_ref[...])
    o_ref[...] = out


def _combine(x2, gates_t, rank, y, final_gain, *, tm=256):
    ntok, d = x2.shape
    tm = min(tm, ntok)
    final_norm = final_gain is not None
    fg = (final_gain if final_norm else jnp.ones((d,), F32)).reshape(1, d)
    return pl.pallas_call(
        functools.partial(_combine_kernel, final_norm=final_norm),
        out_shape=jax.ShapeDtypeStruct((ntok, d), F32),
        grid=(ntok // tm,),
        in_specs=[pl.BlockSpec((TOP_K, tm), lambda i: (0, i), memory_space=pltpu.SMEM),
                  pl.BlockSpec((tm, d), lambda i: (i, 0)),
                  pl.BlockSpec((tm, TOP_K), lambda i: (i, 0)),
                  pl.BlockSpec((1, d), lambda i: (0, 0)),
                  pl.BlockSpec(memory_space=pl.ANY)],
        out_specs=pl.BlockSpec((tm, d), lambda i: (i, 0)),
        scratch_shapes=[pltpu.VMEM((tm, d), F32), pltpu.VMEM((tm, d), F32), pltpu.SemaphoreType.DMA],
        compiler_params=_params(("arbitrary",), 32),
        name="moe_combine",
    )(rank, x2, gates_t, fg, y)


def _moe_layer(x, ln, w_router, w_gate_up, w_down, *, final_gain=None, tile=2048):
    bsz, seqlen, d = x.shape
    ntok = bsz * seqlen
    n_exp = w_router.shape[1]
    tile = min(tile, TOP_K * ntok)
    x2 = x.reshape(ntok, d)
    idx, gates = _router(x2, ln, w_router)
    rank, meta = _moe_plan(idx, n_exp, tile)
    xg = _dispatch(x2, ln, rank)
    y = _experts(xg, meta, w_gate_up, w_down, tile=tile)
    out = _combine(x2, gates.T, rank, y, final_gain)
    return out.reshape(bsz, seqlen, d)


def kernel(x, l0_ln1, l0_s5_lam_re, l0_s5_lam_im, l0_s5_log_dt, l0_s5_b_re, l0_s5_b_im, l0_s5_c_re, l0_s5_c_im, l0_s5_d, l0_s5_w_glu, l0_s5_b_glu, l0_ln2, l0_ffn_w_gate_up, l0_ffn_w_down, l1_ln1, l1_gla_w_in, l1_gla_w_g2, l1_gla_b_g2, l1_gla_norm, l1_gla_w_out, l1_ln2, l1_moe_router, l1_moe_w_gate_up, l1_moe_w_down, l2_ln1, l2_swa_w_qkv, l2_swa_b_qkv, l2_swa_sinks, l2_swa_w_out, l2_swa_b_out, l2_ln2, l2_ffn_w_gate_up, l2_ffn_w_down, l3_ln1, l3_s5_lam_re, l3_s5_lam_im, l3_s5_log_dt, l3_s5_b_re, l3_s5_b_im, l3_s5_c_re, l3_s5_c_im, l3_s5_d, l3_s5_w_glu, l3_s5_b_glu, l3_ln2, l3_moe_router, l3_moe_w_gate_up, l3_moe_w_down, ln_f):
    x = _s5_layer(x, l0_ln1, l0_s5_lam_re, l0_s5_lam_im, l0_s5_log_dt, l0_s5_b_re, l0_s5_b_im,
                  l0_s5_c_re, l0_s5_c_im, l0_s5_d, l0_s5_w_glu, l0_s5_b_glu)
    x = _dense_ffn_layer(x, l0_ln2, l0_ffn_w_gate_up, l0_ffn_w_down)
    x = _gla_layer(x, l1_ln1, l1_gla_w_in, l1_gla_w_g2, l1_gla_b_g2, l1_gla_norm, l1_gla_w_out)
    x = _moe_layer(x, l1_ln2, l1_moe_router, l1_moe_w_gate_up, l1_moe_w_down)
    x = _swa_layer(x, l2_ln1, l2_swa_w_qkv, l2_swa_b_qkv, l2_swa_sinks, l2_swa_w_out, l2_swa_b_out)
    x = _dense_ffn_layer(x, l2_ln2, l2_ffn_w_gate_up, l2_ffn_w_down)
    x = _s5_layer(x, l3_ln1, l3_s5_lam_re, l3_s5_lam_im, l3_s5_log_dt, l3_s5_b_re, l3_s5_b_im,
                  l3_s5_c_re, l3_s5_c_im, l3_s5_d, l3_s5_w_glu, l3_s5_b_glu)
    return _moe_layer(x, l3_ln2, l3_moe_router, l3_moe_w_gate_up, l3_moe_w_down, final_gain=ln_f)
```

```python
import functools
import math

import jax
import jax.numpy as jnp
from jax import lax
from jax.experimental import pallas as pl
from jax.experimental.pallas import tpu as pltpu

F32 = jnp.float32
BF16 = jnp.bfloat16
EPS = 1e-6
LANES = 128
MIB = 1 << 20

S5_GROUP = 16
S5_STATE = 64
S5_CHUNK = 16
S5_SLAB_GROUPS = LANES // S5_GROUP
S5_PITCH_PAD = 8

GLA_HEADS = 4
GLA_GATE_RANK = 16
GLA_GATE_NORM = 16.0
GLA_CHUNK = 64

SWA_HEAD_DIM = 64
SWA_KV_HEADS = 2
SWA_WINDOW = 128
SWA_BLOCK = 128
MASK_VALUE = -1e30

TOP_K = 2


def _params(semantics, vmem_mib):
    return pltpu.CompilerParams(dimension_semantics=semantics, vmem_limit_bytes=vmem_mib * MIB)


def _resident(block_shape, index_map):
    return pl.BlockSpec(block_shape, index_map, pipeline_mode=pl.Buffered(1))


def _rms(xf, gain):
    return xf * lax.rsqrt(jnp.mean(xf * xf, axis=-1, keepdims=True) + EPS) * gain


def _gelu_tanh(x):
    return 0.5 * x * (1.0 + jnp.tanh(math.sqrt(2.0 / math.pi) * (x + 0.044715 * (x * x * x))))


def _silu(x):
    return x * jax.nn.sigmoid(x)


def _bdot(a, b):
    return jnp.dot(a, b, preferred_element_type=F32)


S5_ROW_TILE = 512


def _s5_norm_kernel(x_ref, g_ref, o_ref, scr_ref, *, nloc):
    h = _rms(x_ref[0], g_ref[...])
    nslab = scr_ref.shape[0]
    for c in range(nslab):
        scr_ref[c] = h[:, c * LANES:(c + 1) * LANES]
    for s in range(S5_CHUNK):
        rows = pl.ds(s, nloc, stride=S5_CHUNK)
        o_ref[0, s] = jnp.concatenate([scr_ref[c, rows, :] for c in range(nslab)], axis=1).astype(o_ref.dtype)


def _s5_norm(x, gain):
    bsz, seqlen, d = x.shape
    nch = seqlen // S5_CHUNK
    tm = min(S5_ROW_TILE, seqlen)
    nloc = tm // S5_CHUNK
    return pl.pallas_call(
        functools.partial(_s5_norm_kernel, nloc=nloc),
        out_shape=jax.ShapeDtypeStruct((bsz, S5_CHUNK, nch, d), BF16),
        grid=(bsz, seqlen // tm),
        in_specs=[pl.BlockSpec((1, tm, d), lambda b, i: (b, i, 0)),
                  pl.BlockSpec((1, d), lambda b, i: (0, 0))],
        out_specs=pl.BlockSpec((1, S5_CHUNK, nloc, d), lambda b, i: (b, 0, i, 0)),
        scratch_shapes=[pltpu.VMEM((d // LANES, tm, LANES), F32)],
        compiler_params=_params(("parallel", "parallel"), 32),
        name="s5_norm",
    )(x, gain.reshape(1, d))


def _tiling_matrix(rows, cols):
    p = lax.broadcasted_iota(jnp.int32, (rows, cols), 0)
    c = lax.broadcasted_iota(jnp.int32, (rows, cols), 1)
    return jnp.where(c % rows == p, 1.0, 0.0).astype(BF16)


def _same_group(shape, row_group, col_group):
    r = lax.broadcasted_iota(jnp.int32, shape, 0)
    c = lax.broadcasted_iota(jnp.int32, shape, 1)
    return (r // row_group) == (c // col_group)


def _s5_build_operators(tw_ref, vw_ref, mw_ref, toep_ref, win_ref, wout_ref):
    tn = (((0,), (0,)), ((), ()))
    nstate = vw_ref.shape[-1]
    half = S5_SLAB_GROUPS * nstate
    rep_ch = _tiling_matrix(S5_GROUP, LANES)
    rep_st = _tiling_matrix(nstate, half)
    diag = _same_group((LANES, LANES), S5_GROUP, S5_GROUP)
    taps = []
    for j in range(S5_CHUNK):
        e = lax.dot_general(tw_ref[0, j].astype(BF16), rep_ch, tn, preferred_element_type=F32)
        taps.append(jnp.where(diag, e, 0.0).astype(BF16))
    zero = jnp.zeros((LANES, LANES), BF16)
    for a in range(S5_CHUNK):
        for b in range(S5_CHUNK):
            toep_ref[a * LANES:(a + 1) * LANES, b * LANES:(b + 1) * LANES] = taps[b - a] if b >= a else zero
    diag_in = _same_group((LANES, half), S5_GROUP, nstate)
    diag_out = _same_group((half, LANES), nstate, S5_GROUP)
    for a in range(S5_CHUNK):
        for r in range(2):
            e = _bdot(vw_ref[0, 2 * a + r].astype(BF16), rep_st)
            win_ref[a * LANES:(a + 1) * LANES, r * half:(r + 1) * half] = jnp.where(diag_in, e, 0.0).astype(BF16)
            e = lax.dot_general(mw_ref[0, 2 * a + r].astype(BF16), rep_ch, tn, preferred_element_type=F32)
            wout_ref[r * half:(r + 1) * half, a * LANES:(a + 1) * LANES] = jnp.where(diag_out, e, 0.0).astype(BF16)


def _s5_conv_kernel(h_ref, tw_ref, vw_ref, mw_ref, a_ref, d_ref, o_ref, s_ref, toep_ref, win_ref, wout_ref,
                    *, nseq, nch):
    pitch = nch + S5_PITCH_PAD
    nl = a_ref.shape[1] // 2

    @pl.when(pl.program_id(1) == 0)
    def _():
        _s5_build_operators(tw_ref, vw_ref, mw_ref, toep_ref, win_ref, wout_ref)

    lhs = jnp.concatenate(
        [jnp.concatenate([h_ref[bl, s] for s in range(S5_CHUNK)], axis=1) for bl in range(nseq)], axis=0)
    bc = _bdot(lhs, win_ref[...])
    for bl in range(nseq):
        for j in range(2 * nl):
            s_ref[j, bl * pitch:bl * pitch + nch, :] = bc[bl * nch:(bl + 1) * nch, j * LANES:(j + 1) * LANES]
    a_re = [a_ref[0, j:j + 1, :] for j in range(nl)]
    a_im = [a_ref[0, nl + j:nl + j + 1, :] for j in range(nl)]

    def step(n, carry):
        p_re, p_im = carry
        rows = pl.ds(n, nseq, stride=pitch)
        n_re, n_im = [], []
        for j in range(nl):
            c_re = s_ref[j, rows, :]
            c_im = s_ref[nl + j, rows, :]
            s_ref[j, rows, :] = p_re[j]
            s_ref[nl + j, rows, :] = p_im[j]
            n_re.append(a_re[j] * p_re[j] - a_im[j] * p_im[j] + c_re)
            n_im.append(a_re[j] * p_im[j] + a_im[j] * p_re[j] + c_im)
        return tuple(n_re), tuple(n_im)

    zeros = tuple(jnp.zeros((nseq, LANES), F32) for _ in range(nl))
    lax.fori_loop(0, nch, step, (zeros, zeros))
    x_prev = jnp.concatenate(
        [jnp.concatenate([s_ref[j, bl * pitch:bl * pitch + nch, :] for j in range(2 * nl)], axis=1)
         for bl in range(nseq)], axis=0).astype(BF16)
    y = _bdot(lhs, toep_ref[...]) + _bdot(x_prev, wout_ref[...])
    dskip = d_ref[0]
    for bl in range(nseq):
        for s in range(S5_CHUNK):
            ys = y[bl * nch:(bl + 1) * nch, s * LANES:(s + 1) * LANES]
            ys = ys + dskip * h_ref[bl, s].astype(F32)
            o_ref[bl, s] = _gelu_tanh(ys).astype(o_ref.dtype)


def _s5_conv(hp, tw, vw, mw, a_pack, d_skip, *, nseq):
    bsz, _, nch, d = hp.shape
    nslab = d // LANES
    kdim = S5_CHUNK * LANES
    sdim = a_pack.shape[1] * LANES
    blk4 = lambda a: pl.BlockSpec((1,) + a.shape[1:], lambda c, b: (c, 0, 0, 0))
    return pl.pallas_call(
        functools.partial(_s5_conv_kernel, nseq=nseq, nch=nch),
        out_shape=jax.ShapeDtypeStruct(hp.shape, BF16),
        grid=(nslab, bsz // nseq),
        in_specs=[pl.BlockSpec((nseq, S5_CHUNK, nch, LANES), lambda c, b: (b, 0, 0, c)),
                  blk4(tw), blk4(vw), blk4(mw),
                  pl.BlockSpec((1, sdim // LANES, LANES), lambda c, b: (c, 0, 0)),
                  pl.BlockSpec((1, 1, LANES), lambda c, b: (c, 0, 0))],
        out_specs=pl.BlockSpec((nseq, S5_CHUNK, nch, LANES), lambda c, b: (b, 0, 0, c)),
        scratch_shapes=[pltpu.VMEM((sdim // LANES, nseq * (nch + S5_PITCH_PAD), LANES), F32),
                        pltpu.VMEM((kdim, kdim), BF16),
                        pltpu.VMEM((kdim, sdim), BF16),
                        pltpu.VMEM((sdim, kdim), BF16)],
        compiler_params=_params(("arbitrary", "arbitrary"), 56),
        name="s5_conv",
    )(hp, tw, vw, mw, a_pack, d_skip.reshape(nslab, 1, LANES))


def _s5_glu_kernel(y_ref, x_ref, w_ref, b_ref, o_ref, scr_ref, *, nloc):
    nslab = scr_ref.shape[0]
    y = jnp.concatenate([y_ref[0, s] for s in range(S5_CHUNK)], axis=0)
    u = y.astype(F32) * jax.nn.sigmoid(_bdot(y, w_ref[...]) + b_ref[...])
    for s in range(S5_CHUNK):
        rows = pl.ds(s, nloc, stride=S5_CHUNK)
        for c in range(nslab):
            scr_ref[c, rows, :] = u[s * nloc:(s + 1) * nloc, c * LANES:(c + 1) * LANES]
    o_ref[0] = x_ref[0] + jnp.concatenate([scr_ref[c] for c in range(nslab)], axis=1)


def _s5_glu(yp, x, w_glu, b_glu):
    bsz, seqlen, d = x.shape
    tm = min(S5_ROW_TILE, seqlen)
    nloc = tm // S5_CHUNK
    return pl.pallas_call(
        functools.partial(_s5_glu_kernel, nloc=nloc),
        out_shape=jax.ShapeDtypeStruct(x.shape, F32),
        grid=(bsz, seqlen // tm),
        in_specs=[pl.BlockSpec((1, S5_CHUNK, nloc, d), lambda b, i: (b, 0, i, 0)),
                  pl.BlockSpec((1, tm, d), lambda b, i: (b, i, 0)),
                  _resident((d, d), lambda b, i: (0, 0)),
                  pl.BlockSpec((1, d), lambda b, i: (0, 0))],
        out_specs=pl.BlockSpec((1, tm, d), lambda b, i: (b, i, 0)),
        scratch_shapes=[pltpu.VMEM((d // LANES, tm, LANES), F32)],
        compiler_params=_params(("parallel", "parallel"), 40),
        name="s5_glu",
    )(yp, x, w_glu.astype(BF16), b_glu.reshape(1, d))


def _s5_operators(lam_re, lam_im, log_dt, b_re, b_im, c_re, c_im):
    hi = lax.Precision.HIGHEST
    ngroups, nstate = lam_re.shape
    gpc = S5_SLAB_GROUPS
    nslab = ngroups // gpc
    dt = jnp.exp(log_dt)[:, None]
    j = jnp.arange(S5_CHUNK + 1, dtype=F32)[:, None, None]
    mag = jnp.exp(j * (lam_re * dt)[None])
    ang = j * (lam_im * dt)[None]
    pw_re, pw_im = mag * jnp.cos(ang), mag * jnp.sin(ang)
    num_re, num_im = pw_re[1] - 1.0, pw_im[1]
    den = lam_re * lam_re + lam_im * lam_im
    f_re = (num_re * lam_re + num_im * lam_im) / den
    f_im = (num_im * lam_re - num_re * lam_im) / den
    bb_re = f_re[..., None] * b_re - f_im[..., None] * b_im
    bb_im = f_re[..., None] * b_im + f_im[..., None] * b_re
    cp_re = c_re[None] * pw_re[:S5_CHUNK, :, None, :] - c_im[None] * pw_im[:S5_CHUNK, :, None, :]
    cp_im = c_re[None] * pw_im[:S5_CHUNK, :, None, :] + c_im[None] * pw_re[:S5_CHUNK, :, None, :]
    taps = (jnp.einsum('jghp,gpi->jghi', cp_re, bb_re, precision=hi)
            - jnp.einsum('jghp,gpi->jghi', cp_im, bb_im, precision=hi))
    tw = taps.reshape(S5_CHUNK, nslab, gpc, S5_GROUP, S5_GROUP)
    tw = tw.transpose(1, 0, 3, 2, 4).reshape(nslab, S5_CHUNK, S5_GROUP, LANES)
    jr = (S5_CHUNK - 1) - jnp.arange(S5_CHUNK, dtype=F32)[:, None, None]
    mag_r = jnp.exp(jr * (lam_re * dt)[None])
    ang_r = jr * (lam_im * dt)[None]
    rev_re, rev_im = mag_r * jnp.cos(ang_r), mag_r * jnp.sin(ang_r)
    v_re = rev_re[..., None] * bb_re[None] - rev_im[..., None] * bb_im[None]
    v_im = rev_re[..., None] * bb_im[None] + rev_im[..., None] * bb_re[None]
    v = jnp.stack([v_re, v_im], axis=0).reshape(2, S5_CHUNK, nslab, gpc, nstate, S5_GROUP)
    vw = v.transpose(2, 1, 0, 3, 5, 4).reshape(nslab, 2 * S5_CHUNK, LANES, nstate)
    m_re = c_re[None] * pw_re[1:, :, None, :] - c_im[None] * pw_im[1:, :, None, :]
    m_im = c_re[None] * pw_im[1:, :, None, :] + c_im[None] * pw_re[1:, :, None, :]
    m = jnp.stack([m_re, -m_im], axis=0).reshape(2, S5_CHUNK, nslab, gpc, S5_GROUP, nstate)
    mw = m.transpose(2, 1, 0, 4, 3, 5).reshape(nslab, 2 * S5_CHUNK, S5_GROUP, gpc * nstate)
    half = gpc * nstate // LANES
    a_pack = jnp.concatenate([pw_re[S5_CHUNK].reshape(nslab, half, LANES),
                              pw_im[S5_CHUNK].reshape(nslab, half, LANES)], axis=1)
    return tw, vw, mw, a_pack


def _s5_layer(x, ln, lam_re, lam_im, log_dt, b_re, b_im, c_re, c_im, d_skip, w_glu, b_glu, *, nseq=4):
    tw, vw, mw, a_pack = _s5_operators(lam_re, lam_im, log_dt, b_re, b_im, c_re, c_im)
    hp = _s5_norm(x, ln)
    yp = _s5_conv(hp, tw, vw, mw, a_pack, d_skip, nseq=min(nseq, x.shape[0]))
    return _s5_glu(yp, x, w_glu, b_glu)


def _dense_ffn_kernel(x_ref, g_ref, wg_ref, wu_ref, wd_ref, o_ref):
    xf = x_ref[...]
    h = _rms(xf, g_ref[...]).astype(BF16)
    act = (_silu(_bdot(h, wg_ref[...])) * _bdot(h, wu_ref[...])).astype(BF16)
    o_ref[...] = xf + _bdot(act, wd_ref[...])


def _dense_ffn_layer(x, ln, w_gate_up, w_down, *, tm=512):
    bsz, seqlen, d = x.shape
    ntok = bsz * seqlen
    hidden = w_down.shape[0]
    tm = min(tm, ntok)
    wgu = w_gate_up.astype(BF16)
    out = pl.pallas_call(
        _dense_ffn_kernel,
        out_shape=jax.ShapeDtypeStruct((ntok, d), F32),
        grid=(ntok // tm,),
        in_specs=[pl.BlockSpec((tm, d), lambda i: (i, 0)),
                  pl.BlockSpec((1, d), lambda i: (0, 0)),
                  _resident((d, hidden), lambda i: (0, 0)),
                  _resident((d, hidden), lambda i: (0, 1)),
                  _resident((hidden, d), lambda i: (0, 0))],
        out_specs=pl.BlockSpec((tm, d), lambda i: (i, 0)),
        compiler_params=_params(("parallel",), 56),
        name="dense_ffn",
    )(x.reshape(ntok, d), ln.reshape(1, d), wgu, wgu, w_down.astype(BF16))
    return out.reshape(bsz, seqlen, d)


def _log_sigmoid(z):
    return jnp.minimum(z, 0.0) - jnp.log(1.0 + jnp.exp(-jnp.abs(z)))


def _gla_kernel(x_ref, ln_ref, wm_ref, wgl_ref, wg2_ref, bg2_ref, gn_ref, wo_ref, o_ref, st_ref,
                *, tq, dk, dv, heads):
    hdk, hdv = dk // heads, dv // heads
    chunk = GLA_CHUNK
    nt = (((1,), (1,)), ((), ()))
    tn = (((0,), (0,)), ((), ()))

    @pl.when(pl.program_id(1) == 0)
    def _():
        st_ref[...] = jnp.zeros_like(st_ref)

    xf = x_ref[0]
    h = _rms(xf, ln_ref[...]).astype(BF16)
    proj = _bdot(h, wm_ref[...])
    glow = _bdot(h, wgl_ref[...]).astype(BF16)
    la = _log_sigmoid(_bdot(glow, wg2_ref[...]) + bg2_ref[...]) * (1.0 / GLA_GATE_NORM)
    row = lax.broadcasted_iota(jnp.int32, (chunk, chunk), 0)
    col = lax.broadcasted_iota(jnp.int32, (chunk, chunk), 1)
    causal = row >= col
    tri = jnp.where(causal, 1.0, 0.0).astype(BF16)
    scale = hdk ** -0.5
    outs = []
    for c in range(tq // chunk):
        r0 = c * chunk
        la_c = la[r0:r0 + chunk, :]
        la_hi = la_c.astype(BF16)
        la_lo = (la_c - la_hi.astype(F32)).astype(BF16)
        gcum_all = _bdot(tri, la_hi) + _bdot(tri, la_lo)
        head_out = []
        for hd in range(heads):
            gcum = gcum_all[:, hd * hdk:(hd + 1) * hdk]
            g_last = gcum[chunk - 1:chunk, :]
            q_c = proj[r0:r0 + chunk, hd * hdk:(hd + 1) * hdk] * scale
            k_c = proj[r0:r0 + chunk, dk + hd * hdk:dk + (hd + 1) * hdk]
            v_c = proj[r0:r0 + chunk, 2 * dk + hd * hdv:2 * dk + (hd + 1) * hdv].astype(BF16)
            q_s = (q_c * jnp.exp(gcum)).astype(BF16)
            k_s = (k_c * jnp.exp(-gcum)).astype(BF16)
            k_end = (k_c * jnp.exp(g_last - gcum)).astype(BF16)
            scores = lax.dot_general(q_s, k_s, nt, preferred_element_type=F32)
            scores = jnp.where(causal, scores, 0.0).astype(BF16)
            state_t = st_ref[hd]
            o = _bdot(scores, v_c) + lax.dot_general(q_s, state_t.astype(BF16), nt,
                                                     preferred_element_type=F32)
            kv_t = lax.dot_general(v_c, k_end, tn, preferred_element_type=F32)
            st_ref[hd] = state_t * jnp.exp(g_last) + kv_t
            head_out.append(o * lax.rsqrt(jnp.mean(o * o, axis=-1, keepdims=True) + EPS))
        outs.append(jnp.concatenate(head_out, axis=1))
    o_all = jnp.concatenate(outs, axis=0)
    r = proj[:, 2 * dk + dv:]
    o_all = (o_all * gn_ref[...] * _silu(r)).astype(BF16)
    o_ref[0] = xf + _bdot(o_all, wo_ref[...])


def _gla_layer(x, ln, w_in, w_g2, b_g2, g_norm, w_out, *, tq=256):
    bsz, seqlen, d = x.shape
    dk = w_g2.shape[1]
    dv = w_out.shape[0]
    nmain = 2 * dk + 2 * dv
    tq = min(tq, seqlen)
    w_main = w_in[:, :nmain].astype(BF16)
    w_glow = jnp.pad(w_in[:, nmain:], ((0, 0), (0, LANES - GLA_GATE_RANK))).astype(BF16)
    w_g2p = jnp.pad(w_g2, ((0, LANES - GLA_GATE_RANK), (0, 0))).astype(BF16)
    hdk, hdv = dk // GLA_HEADS, dv // GLA_HEADS
    const = lambda b, t: (0, 0)
    return pl.pallas_call(
        functools.partial(_gla_kernel, tq=tq, dk=dk, dv=dv, heads=GLA_HEADS),
        out_shape=jax.ShapeDtypeStruct(x.shape, F32),
        grid=(bsz, seqlen // tq),
        in_specs=[pl.BlockSpec((1, tq, d), lambda b, t: (b, t, 0)),
                  pl.BlockSpec((1, d), const),
                  _resident((d, nmain), const),
                  _resident((d, LANES), const),
                  _resident((LANES, dk), const),
                  pl.BlockSpec((1, dk), const),
                  pl.BlockSpec((1, dv), const),
                  _resident((dv, d), const)],
        out_specs=pl.BlockSpec((1, tq, d), lambda b, t: (b, t, 0)),
        scratch_shapes=[pltpu.VMEM((GLA_HEADS, hdv, hdk), F32)],
        compiler_params=_params(("parallel", "arbitrary"), 48),
        name="gla",
    )(x, ln.reshape(1, d), w_main, w_glow, w_g2p, b_g2.reshape(1, dk), g_norm.reshape(1, dv),
      w_out.astype(BF16))


LOG2E = math.log2(math.e)
SWA_HEAD_UNROLL = 4


def _swa_kernel(sink_ref, x_ref, ln_ref, wqkv_ref, bqkv_ref, wo_ref, bo_ref, o_ref, k_ref, v_ref,
                bias_ref, q_ref, a_ref, *, tq, q_heads):
    group = q_heads // SWA_KV_HEADS
    blk = SWA_BLOCK
    nt = (((1,), (1,)), ((), ()))
    b = pl.program_id(0)
    t = pl.program_id(1)
    nq = q_heads * LANES

    @pl.when((b == 0) & (t == 0))
    def _():
        qi = lax.broadcasted_iota(jnp.int32, (blk, 2 * blk), 0)
        kj = lax.broadcasted_iota(jnp.int32, (blk, 2 * blk), 1)
        dist = qi + blk - kj
        in_window = (dist >= 0) & (dist < SWA_WINDOW)
        for hq in range(q_heads):
            slope = 2.0 ** (-8.0 * (hq + 1) / q_heads)
            bias_ref[hq] = jnp.where(in_window, -(slope * LOG2E) * dist.astype(F32), MASK_VALUE)

    @pl.when(t == 0)
    def _():
        k_ref[0:blk, :] = jnp.zeros((blk, LANES), BF16)
        v_ref[0:blk, :] = jnp.zeros((blk, LANES), BF16)

    xf = x_ref[0]
    h = _rms(xf, ln_ref[...]).astype(BF16)
    qkv = _bdot(h, wqkv_ref[...]) + bqkv_ref[...]
    for hq in range(q_heads):
        q_ref[hq] = qkv[:, hq * LANES:(hq + 1) * LANES].astype(BF16)
    k_ref[blk:blk + tq, :] = qkv[:, nq:nq + LANES].astype(BF16)
    v_ref[blk:blk + tq, :] = qkv[:, nq + LANES:nq + 2 * LANES].astype(BF16)
    kj_row = lax.broadcasted_iota(jnp.int32, (1, 2 * blk), 1)
    no_prev = jnp.where(kj_row < blk, jnp.where(t == 0, MASK_VALUE, 0.0), 0.0)

    def heads(j, carry):
        for u in range(SWA_HEAD_UNROLL):
            hq = j * SWA_HEAD_UNROLL + u
            sink = sink_ref[hq] * LOG2E
            for i in range(tq // blk):
                r0 = i * blk
                s = lax.dot_general(q_ref[hq, r0:r0 + blk, :], k_ref[r0:r0 + 2 * blk, :], nt,
                                    preferred_element_type=F32) + bias_ref[hq]
                if i == 0:
                    s = s + no_prev
                m = jnp.maximum(jnp.max(s, axis=-1, keepdims=True), sink)
                p = jnp.exp2(s - m)
                denom = jnp.sum(p, axis=-1, keepdims=True) + jnp.exp2(sink - m)
                o = _bdot(p.astype(BF16), v_ref[r0:r0 + 2 * blk, :]) * (1.0 / denom)
                a_ref[hq, r0:r0 + blk, :] = o.astype(BF16)
        return carry

    lax.fori_loop(0, q_heads // SWA_HEAD_UNROLL, heads, 0)
    k_ref[0:blk, :] = k_ref[tq:tq + blk, :]
    v_ref[0:blk, :] = v_ref[tq:tq + blk, :]
    o_all = jnp.concatenate([a_ref[hq] for hq in range(q_heads)], axis=1)
    o_ref[0] = xf + _bdot(o_all, wo_ref[...]) + bo_ref[...]


def _swa_layer(x, ln, w_qkv, b_qkv, sinks, w_out, b_out, *, tq=512):
    bsz, seqlen, d = x.shape
    hd = SWA_HEAD_DIM
    q_heads = sinks.shape[0]
    group = q_heads // SWA_KV_HEADS
    nq = q_heads * hd
    tq = min(tq, seqlen)
    kv_of_head = jnp.arange(q_heads) // group
    place = (jnp.arange(LANES // hd)[None, :] == kv_of_head[:, None]).astype(F32)
    q_scale = hd ** -0.5 * LOG2E
    wq = (w_qkv[:, :nq] * q_scale).reshape(d, q_heads, 1, hd) * place[None, :, :, None]
    bq = (b_qkv[:nq] * q_scale).reshape(q_heads, 1, hd) * place[:, :, None]
    w_all = jnp.concatenate([wq.reshape(d, q_heads * LANES), w_qkv[:, nq:]], axis=1).astype(BF16)
    b_all = jnp.concatenate([bq.reshape(q_heads * LANES), b_qkv[nq:]]).reshape(1, -1)
    wo = (w_out.reshape(q_heads, 1, hd, d) * place[:, :, None, None]).reshape(q_heads * LANES, d).astype(BF16)
    nall = w_all.shape[1]
    const = lambda b, t, s: (0, 0)
    return pl.pallas_call(
        functools.partial(_swa_kernel, tq=tq, q_heads=q_heads),
        out_shape=jax.ShapeDtypeStruct(x.shape, F32),
        grid_spec=pltpu.PrefetchScalarGridSpec(
            num_scalar_prefetch=1,
            grid=(bsz, seqlen // tq),
            in_specs=[pl.BlockSpec((1, tq, d), lambda b, t, s: (b, t, 0)),
                      pl.BlockSpec((1, d), const),
                      _resident((d, nall), const),
                      pl.BlockSpec((1, nall), const),
                      _resident((q_heads * LANES, d), const),
                      pl.BlockSpec((1, d), const)],
            out_specs=pl.BlockSpec((1, tq, d), lambda b, t, s: (b, t, 0)),
            scratch_shapes=[pltpu.VMEM((SWA_BLOCK + tq, LANES), BF16), pltpu.VMEM((SWA_BLOCK + tq, LANES), BF16),
                            pltpu.VMEM((q_heads, SWA_BLOCK, 2 * SWA_BLOCK), F32),
                            pltpu.VMEM((q_heads, tq, LANES), BF16), pltpu.VMEM((q_heads, tq, LANES), BF16)]),
        compiler_params=_params(("arbitrary", "arbitrary"), 48),
        name="swa",
    )(sinks, x, ln.reshape(1, d), w_all, b_all, wo, b_out.reshape(1, d))


def _router_kernel(x_ref, ln_ref, whi_ref, wlo_ref, idx_ref, gate_ref):
    nt = (((1,), (1,)), ((), ()))
    h = _rms(x_ref[...], ln_ref[...])
    h_hi = h.astype(BF16)
    h_lo = (h - h_hi.astype(F32)).astype(BF16)
    w_hi, w_lo = whi_ref[...], wlo_ref[...]
    logits = (lax.dot_general(w_hi, h_hi, nt, preferred_element_type=F32)
              + lax.dot_general(w_hi, h_lo, nt, preferred_element_type=F32)
              + lax.dot_general(w_lo, h_hi, nt, preferred_element_type=F32))
    n_exp = logits.shape[0]
    eid = lax.broadcasted_iota(jnp.int32, logits.shape, 0)
    m1 = jnp.max(logits, axis=0, keepdims=True)
    i1 = jnp.min(jnp.where(logits == m1, eid, n_exp), axis=0, keepdims=True)
    rest = jnp.where(eid == i1, -jnp.inf, logits)
    m2 = jnp.max(rest, axis=0, keepdims=True)
    i2 = jnp.min(jnp.where(rest == m2, eid, n_exp), axis=0, keepdims=True)
    e2 = jnp.exp(m2 - m1)
    g1 = 1.0 / (1.0 + e2)
    idx_ref[...] = jnp.concatenate([i1, i2], axis=0)
    gate_ref[...] = jnp.concatenate([g1, e2 * g1], axis=0)


def _router(x2, ln, w_router, *, tm=512):
    ntok, d = x2.shape
    n_exp = w_router.shape[1]
    tm = min(tm, ntok)
    wt = w_router.T
    w_hi = wt.astype(BF16)
    w_lo = (wt - w_hi.astype(F32)).astype(BF16)
    return pl.pallas_call(
        _router_kernel,
        out_shape=(jax.ShapeDtypeStruct((TOP_K, ntok), jnp.int32), jax.ShapeDtypeStruct((TOP_K, ntok), F32)),
        grid=(ntok // tm,),
        in_specs=[pl.BlockSpec((tm, d), lambda i: (i, 0)),
                  pl.BlockSpec((1, d), lambda i: (0, 0)),
                  pl.BlockSpec((n_exp, d), lambda i: (0, 0)),
                  pl.BlockSpec((n_exp, d), lambda i: (0, 0))],
        out_specs=(pl.BlockSpec((TOP_K, tm), lambda i: (0, i)), pl.BlockSpec((TOP_K, tm), lambda i: (0, i))),
        compiler_params=_params(("parallel",), 32),
        name="moe_router",
    )(x2, ln.reshape(1, d), w_hi, w_lo)


def _moe_plan(idx, n_exp, tile):
    nslots = idx.size
    flat = idx.reshape(-1)
    onehot = (flat[None, :] == jnp.arange(n_exp, dtype=jnp.int32)[:, None]).astype(jnp.int32)
    csum = jnp.cumsum(onehot, axis=1)
    counts = csum[:, -1]
    ends = jnp.cumsum(counts)
    offs = ends - counts
    rank = jnp.sum(onehot * (csum - 1 + offs[:, None]), axis=0).reshape(idx.shape)
    n_tiles = nslots // tile
    n_visits = n_tiles + n_exp - 1
    first_tile = offs // tile
    last_tile = (ends - 1) // tile
    nvis = jnp.where(counts > 0, last_tile - first_tile + 1, 0)
    vend = jnp.cumsum(nvis)
    vstart = vend - nvis
    total = vend[-1]
    v = jnp.arange(n_visits, dtype=jnp.int32)
    vc = jnp.minimum(v, total - 1)
    e = jnp.minimum(jnp.sum((vc[:, None] >= vend[None, :]).astype(jnp.int32), axis=1), n_exp - 1)
    sel = (e[:, None] == jnp.arange(n_exp, dtype=jnp.int32)[None, :]).astype(jnp.int32)
    pick = lambda a: jnp.sum(sel * a[None, :], axis=1)
    tile_id = pick(first_tile) + vc - pick(vstart)
    lo = jnp.maximum(pick(offs), tile_id * tile) - tile_id * tile
    hi = jnp.minimum(pick(ends), (tile_id + 1) * tile) - tile_id * tile
    valid = v < total
    lo = jnp.where(valid, lo, 0)
    hi = jnp.where(valid, hi, 0)
    prev_tile = jnp.concatenate([jnp.full((1,), -1, jnp.int32), tile_id[:-1]])
    first = (valid & (tile_id != prev_tile)).astype(jnp.int32)
    meta = jnp.stack([tile_id, e, lo, hi, first]).astype(jnp.int32)
    return rank.astype(jnp.int32), meta


def _pack_bf16_pairs(h):
    half = h.shape[1] // 2
    bits = lax.bitcast_convert_type(h.astype(BF16).astype(F32), jnp.uint32)
    return (bits[:, half:] & jnp.uint32(0xFFFF0000)) | (bits[:, :half] >> 16)


def _unpack_bf16_pairs(u):
    lo = lax.bitcast_convert_type(u << 16, F32)
    hi = lax.bitcast_convert_type(u & jnp.uint32(0xFFFF0000), F32)
    return jnp.concatenate([lo, hi], axis=1).astype(BF16)


DMA_UNROLL = 8


def _start_and_wait_rows(tm, row_copy):
    def issue(j, carry):
        for u in range(DMA_UNROLL):
            for k in range(TOP_K):
                row_copy(k, j * DMA_UNROLL + u).start()
        return carry

    def drain(j, carry):
        for u in range(DMA_UNROLL):
            for k in range(TOP_K):
                row_copy(k, j * DMA_UNROLL + u).wait()
        return carry

    lax.fori_loop(0, tm // DMA_UNROLL, issue, 0)
    lax.fori_loop(0, tm // DMA_UNROLL, drain, 0)


def _dispatch_kernel(rank_ref, x_ref, ln_ref, xg_ref, h_ref, sem):
    h_ref[...] = _pack_bf16_pairs(_rms(x_ref[...], ln_ref[...]))

    def row_copy(k, i):
        return pltpu.make_async_copy(h_ref.at[pl.ds(i, 1)], xg_ref.at[pl.ds(rank_ref[k, i], 1)], sem)

    _start_and_wait_rows(x_ref.shape[0], row_copy)


def _dispatch(x2, ln, rank, *, tm=256):
    ntok, d = x2.shape
    tm = min(tm, ntok)
    return pl.pallas_call(
        _dispatch_kernel,
        out_shape=jax.ShapeDtypeStruct((TOP_K * ntok, d // 2), jnp.uint32),
        grid=(ntok // tm,),
        in_specs=[pl.BlockSpec((TOP_K, tm), lambda i: (0, i), memory_space=pltpu.SMEM),
                  pl.BlockSpec((tm, d), lambda i: (i, 0)),
                  pl.BlockSpec((1, d), lambda i: (0, 0))],
        out_specs=pl.BlockSpec(memory_space=pl.ANY),
        scratch_shapes=[pltpu.VMEM((tm, d // 2), jnp.uint32), pltpu.SemaphoreType.DMA],
        compiler_params=_params(("arbitrary",), 32),
        name="moe_dispatch",
    )(rank, x2, ln.reshape(1, d))


MXU_N = 256


def _expert_kernel(meta_ref, x_ref, wg_ref, wu_ref, wd_ref, o_ref, xb_ref, act_ref, wgb_ref, wub_ref, wdb_ref,
                   *, ts):
    v = pl.program_id(0)
    hc = pl.program_id(1)
    lo, hi, first = meta_ref[2, v], meta_ref[3, v], meta_ref[4, v]
    tile, d = o_ref.shape
    nsub = tile // ts
    th = wgb_ref.shape[1]
    full = (lo == 0) & (hi == tile)

    @pl.when(hc == 0)
    def _():
        for sub in range(nsub):
            xb_ref[sub * ts:(sub + 1) * ts, :] = _unpack_bf16_pairs(x_ref[sub * ts:(sub + 1) * ts, :])

    @pl.when((first == 1) & (hc == 0))
    def _():
        o_ref[...] = jnp.zeros_like(o_ref)

    @pl.when(full)
    def _():
        for n in range(th // MXU_N):
            cols = slice(n * MXU_N, (n + 1) * MXU_N)
            gate = _bdot(xb_ref[...], wg_ref[0, :, cols].astype(BF16))
            up = _bdot(xb_ref[...], wu_ref[0, :, cols].astype(BF16))
            act_ref[:, cols] = (_silu(gate) * up).astype(BF16)
        for n in range(d // MXU_N):
            cols = slice(n * MXU_N, (n + 1) * MXU_N)
            o_ref[:, cols] += _bdot(act_ref[...], wd_ref[0, :, cols].astype(BF16))

    @pl.when(jnp.logical_not(full) & (hi > lo))
    def _():
        wgb_ref[...] = wg_ref[0].astype(BF16)
        wub_ref[...] = wu_ref[0].astype(BF16)
        wdb_ref[...] = wd_ref[0].astype(BF16)
        for sub in range(nsub):
            r0 = sub * ts

            @pl.when((lo < r0 + ts) & (hi > r0))
            def _():
                xs = xb_ref[r0:r0 + ts, :]
                act = (_silu(_bdot(xs, wgb_ref[...])) * _bdot(xs, wub_ref[...])).astype(BF16)
                y = _bdot(act, wdb_ref[...])
                rows = r0 + lax.broadcasted_iota(jnp.int32, (ts, 1), 0)
                o_ref[r0:r0 + ts, :] += jnp.where((rows >= lo) & (rows < hi), y, 0.0)


def _experts(xg, meta, w_gate_up, w_down, *, tile, th=512, ts=512):
    nrows = xg.shape[0]
    n_exp, hidden, d = w_down.shape
    n_hc = hidden // th
    ts = min(ts, tile)
    wgu = w_gate_up
    return pl.pallas_call(
        functools.partial(_expert_kernel, ts=ts),
        out_shape=jax.ShapeDtypeStruct((nrows, d), F32),
        grid_spec=pltpu.PrefetchScalarGridSpec(
            num_scalar_prefetch=1,
            grid=(meta.shape[1], n_hc),
            in_specs=[pl.BlockSpec((tile, d // 2), lambda v, c, m: (m[0, v], 0)),
                      pl.BlockSpec((1, d, th), lambda v, c, m: (m[1, v], 0, c)),
                      pl.BlockSpec((1, d, th), lambda v, c, m: (m[1, v], 0, c + n_hc)),
                      pl.BlockSpec((1, th, d), lambda v, c, m: (m[1, v], c, 0))],
            out_specs=pl.BlockSpec((tile, d), lambda v, c, m: (m[0, v], 0)),
            scratch_shapes=[pltpu.VMEM((tile, d), BF16), pltpu.VMEM((tile, th), BF16), pltpu.VMEM((d, th), BF16),
                            pltpu.VMEM((d, th), BF16), pltpu.VMEM((th, d), BF16)]),
        compiler_params=_params(("arbitrary", "arbitrary"), 56),
        name="moe_experts",
    )(meta, xg, wgu, wgu, w_down)


def _combine_kernel(rank_ref, x_ref, gate_ref, fg_ref, y_ref, o_ref, b0_ref, b1_ref, sem, *, final_norm):
    tm = x_ref.shape[0]
    bufs = (b0_ref, b1_ref)

    def row_copy(k, i):
        return pltpu.make_async_copy(y_ref.at[pl.ds(rank_ref[k, i], 1)], bufs[k].at[pl.ds(i, 1)], sem)

    _start_and_wait_rows(tm, row_copy)
    g = gate_ref[...]
    out = x_ref[...] + g[:, 0:1] * b0_ref[...] + g[:, 1:2] * b1_ref[...]
    if final_norm:
        out = _rms(out, fg_ref[...])
    o_ref[...] = out


def _combine(x2, gates_t, rank, y, final_gain, *, tm=256):
    ntok, d = x2.shape
    tm = min(tm, ntok)
    final_norm = final_gain is not None
    fg = (final_gain if final_norm else jnp.ones((d,), F32)).reshape(1, d)
    return pl.pallas_call(
        functools.partial(_combine_kernel, final_norm=final_norm),
        out_shape=jax.ShapeDtypeStruct((ntok, d), F32),
        grid=(ntok // tm,),
        in_specs=[pl.BlockSpec((TOP_K, tm), lambda i: (0, i), memory_space=pltpu.SMEM),
                  pl.BlockSpec((tm, d), lambda i: (i, 0)),
                  pl.BlockSpec((tm, TOP_K), lambda i: (i, 0)),
                  pl.BlockSpec((1, d), lambda i: (0, 0)),
                  pl.BlockSpec(memory_space=pl.ANY)],
        out_specs=pl.BlockSpec((tm, d), lambda i: (i, 0)),
        scratch_shapes=[pltpu.VMEM((tm, d), F32), pltpu.VMEM((tm, d), F32), pltpu.SemaphoreType.DMA],
        compiler_params=_params(("arbitrary",), 32),
        name="moe_combine",
    )(rank, x2, gates_t, fg, y)


def _moe_layer(x, ln, w_router, w_gate_up, w_down, *, final_gain=None, tile=2048):
    bsz, seqlen, d = x.shape
    ntok = bsz * seqlen
    n_exp = w_router.shape[1]
    tile = min(tile, TOP_K * ntok)
    x2 = x.reshape(ntok, d)
    idx, gates = _router(x2, ln, w_router)
    rank, meta = _moe_plan(idx, n_exp, tile)
    xg = _dispatch(x2, ln, rank)
    y = _experts(xg, meta, w_gate_up, w_down, tile=tile)
    out = _combine(x2, gates.T, rank, y, final_gain)
    return out.reshape(bsz, seqlen, d)


def kernel(x, l0_ln1, l0_s5_lam_re, l0_s5_lam_im, l0_s5_log_dt, l0_s5_b_re, l0_s5_b_im, l0_s5_c_re, l0_s5_c_im, l0_s5_d, l0_s5_w_glu, l0_s5_b_glu, l0_ln2, l0_ffn_w_gate_up, l0_ffn_w_down, l1_ln1, l1_gla_w_in, l1_gla_w_g2, l1_gla_b_g2, l1_gla_norm, l1_gla_w_out, l1_ln2, l1_moe_router, l1_moe_w_gate_up, l1_moe_w_down, l2_ln1, l2_swa_w_qkv, l2_swa_b_qkv, l2_swa_sinks, l2_swa_w_out, l2_swa_b_out, l2_ln2, l2_ffn_w_gate_up, l2_ffn_w_down, l3_ln1, l3_s5_lam_re, l3_s5_lam_im, l3_s5_log_dt, l3_s5_b_re, l3_s5_b_im, l3_s5_c_re, l3_s5_c_im, l3_s5_d, l3_s5_w_glu, l3_s5_b_glu, l3_ln2, l3_moe_router, l3_moe_w_gate_up, l3_moe_w_down, ln_f):
    x = _s5_layer(x, l0_ln1, l0_s5_lam_re, l0_s5_lam_im, l0_s5_log_dt, l0_s5_b_re, l0_s5_b_im,
                  l0_s5_c_re, l0_s5_c_im, l0_s5_d, l0_s5_w_glu, l0_s5_b_glu)
    x = _dense_ffn_layer(x, l0_ln2, l0_ffn_w_gate_up, l0_ffn_w_down)
    x = _gla_layer(x, l1_ln1, l1_gla_w_in, l1_gla_w_g2, l1_gla_b_g2, l1_gla_norm, l1_gla_w_out)
    x = _moe_layer(x, l1_ln2, l1_moe_router, l1_moe_w_gate_up, l1_moe_w_down)
    x = _swa_layer(x, l2_ln1, l2_swa_w_qkv, l2_swa_b_qkv, l2_swa_sinks, l2_swa_w_out, l2_swa_b_out)
    x = _dense_ffn_layer(x, l2_ln2, l2_ffn_w_gate_up, l2_ffn_w_down)
    x = _s5_layer(x, l3_ln1, l3_s5_lam_re, l3_s5_lam_im, l3_s5_log_dt, l3_s5_b_re, l3_s5_b_im,
                  l3_s5_c_re, l3_s5_c_im, l3_s5_d, l3_s5_w_glu, l3_s5_b_glu)
    return _moe_layer(x, l3_ln2, l3_moe_router, l3_moe_w_gate_up, l3_moe_w_down, final_gain=ln_f)
```

```python
import functools
import math

import jax
import jax.numpy as jnp
from jax import lax
from jax.experimental import pallas as pl
from jax.experimental.pallas import tpu as pltpu
from jax.experimental.pallas import tpu_sc as plsc

F32 = jnp.float32
BF16 = jnp.bfloat16
EPS = 1e-6
LANES = 128
MIB = 1 << 20

S5_GROUP = 16
S5_STATE = 64
S5_CHUNK = 16
S5_SLAB_GROUPS = LANES // S5_GROUP
S5_PITCH_PAD = 8

GLA_HEADS = 4
GLA_GATE_RANK = 16
GLA_GATE_NORM = 16.0
GLA_CHUNK = 64

SWA_HEAD_DIM = 64
SWA_KV_HEADS = 2
SWA_WINDOW = 128
SWA_BLOCK = 128
MASK_VALUE = -1e30

TOP_K = 2


def _params(semantics, vmem_mib):
    return pltpu.CompilerParams(dimension_semantics=semantics, vmem_limit_bytes=vmem_mib * MIB)


def _resident(block_shape, index_map):
    return pl.BlockSpec(block_shape, index_map, pipeline_mode=pl.Buffered(1))


def _rms(xf, gain):
    return xf * lax.rsqrt(jnp.mean(xf * xf, axis=-1, keepdims=True) + EPS) * gain


def _gelu_tanh(x):
    return 0.5 * x * (1.0 + jnp.tanh(math.sqrt(2.0 / math.pi) * (x + 0.044715 * (x * x * x))))


def _silu(x):
    return x * jax.nn.sigmoid(x)


def _bdot(a, b):
    return jnp.dot(a, b, preferred_element_type=F32)


S5_ROW_TILE = 512


def _s5_norm_kernel(x_ref, g_ref, o_ref, scr_ref, *, nloc):
    h = _rms(x_ref[0], g_ref[...])
    nslab = scr_ref.shape[0]
    for c in range(nslab):
        scr_ref[c] = h[:, c * LANES:(c + 1) * LANES]
    for s in range(S5_CHUNK):
        rows = pl.ds(s, nloc, stride=S5_CHUNK)
        o_ref[0, s] = jnp.concatenate([scr_ref[c, rows, :] for c in range(nslab)], axis=1).astype(o_ref.dtype)


def _s5_norm(x, gain):
    bsz, seqlen, d = x.shape
    nch = seqlen // S5_CHUNK
    tm = min(S5_ROW_TILE, seqlen)
    nloc = tm // S5_CHUNK
    return pl.pallas_call(
        functools.partial(_s5_norm_kernel, nloc=nloc),
        out_shape=jax.ShapeDtypeStruct((bsz, S5_CHUNK, nch, d), BF16),
        grid=(bsz, seqlen // tm),
        in_specs=[pl.BlockSpec((1, tm, d), lambda b, i: (b, i, 0)),
                  pl.BlockSpec((1, d), lambda b, i: (0, 0))],
        out_specs=pl.BlockSpec((1, S5_CHUNK, nloc, d), lambda b, i: (b, 0, i, 0)),
        scratch_shapes=[pltpu.VMEM((d // LANES, tm, LANES), F32)],
        compiler_params=_params(("parallel", "parallel"), 32),
        name="s5_norm",
    )(x, gain.reshape(1, d))


def _tiling_matrix(rows, cols):
    p = lax.broadcasted_iota(jnp.int32, (rows, cols), 0)
    c = lax.broadcasted_iota(jnp.int32, (rows, cols), 1)
    return jnp.where(c % rows == p, 1.0, 0.0).astype(BF16)


def _same_group(shape, row_group, col_group):
    r = lax.broadcasted_iota(jnp.int32, shape, 0)
    c = lax.broadcasted_iota(jnp.int32, shape, 1)
    return (r // row_group) == (c // col_group)


def _s5_build_operators(tw_ref, vw_ref, mw_ref, toep_ref, win_ref, wout_ref):
    tn = (((0,), (0,)), ((), ()))
    nstate = vw_ref.shape[-1]
    half = S5_SLAB_GROUPS * nstate
    rep_ch = _tiling_matrix(S5_GROUP, LANES)
    rep_st = _tiling_matrix(nstate, half)
    diag = _same_group((LANES, LANES), S5_GROUP, S5_GROUP)
    taps = []
    for j in range(S5_CHUNK):
        e = lax.dot_general(tw_ref[0, j].astype(BF16), rep_ch, tn, preferred_element_type=F32)
        taps.append(jnp.where(diag, e, 0.0).astype(BF16))
    zero = jnp.zeros((LANES, LANES), BF16)
    for a in range(S5_CHUNK):
        for b in range(S5_CHUNK):
            toep_ref[a * LANES:(a + 1) * LANES, b * LANES:(b + 1) * LANES] = taps[b - a] if b >= a else zero
    diag_in = _same_group((LANES, half), S5_GROUP, nstate)
    diag_out = _same_group((half, LANES), nstate, S5_GROUP)
    for a in range(S5_CHUNK):
        for r in range(2):
            e = _bdot(vw_ref[0, 2 * a + r].astype(BF16), rep_st)
            win_ref[a * LANES:(a + 1) * LANES, r * half:(r + 1) * half] = jnp.where(diag_in, e, 0.0).astype(BF16)
            e = lax.dot_general(mw_ref[0, 2 * a + r].astype(BF16), rep_ch, tn, preferred_element_type=F32)
            wout_ref[r * half:(r + 1) * half, a * LANES:(a + 1) * LANES] = jnp.where(diag_out, e, 0.0).astype(BF16)


def _s5_conv_kernel(h_ref, tw_ref, vw_ref, mw_ref, a_ref, d_ref, o_ref, s_ref, toep_ref, win_ref, wout_ref,
                    *, nseq, nch):
    pitch = nch + S5_PITCH_PAD
    nl = a_ref.shape[1] // 2

    @pl.when(pl.program_id(1) == 0)
    def _():
        _s5_build_operators(tw_ref, vw_ref, mw_ref, toep_ref, win_ref, wout_ref)

    lhs = jnp.concatenate(
        [jnp.concatenate([h_ref[bl, s] for s in range(S5_CHUNK)], axis=1) for bl in range(nseq)], axis=0)
    bc = _bdot(lhs, win_ref[...])
    for bl in range(nseq):
        for j in range(2 * nl):
            s_ref[j, bl * pitch:bl * pitch + nch, :] = bc[bl * nch:(bl + 1) * nch, j * LANES:(j + 1) * LANES]
    a_re = [a_ref[0, j:j + 1, :] for j in range(nl)]
    a_im = [a_ref[0, nl + j:nl + j + 1, :] for j in range(nl)]

    def step(n, carry):
        p_re, p_im = carry
        rows = pl.ds(n, nseq, stride=pitch)
        n_re, n_im = [], []
        for j in range(nl):
            c_re = s_ref[j, rows, :]
            c_im = s_ref[nl + j, rows, :]
            s_ref[j, rows, :] = p_re[j]
            s_ref[nl + j, rows, :] = p_im[j]
            n_re.append(a_re[j] * p_re[j] - a_im[j] * p_im[j] + c_re)
            n_im.append(a_re[j] * p_im[j] + a_im[j] * p_re[j] + c_im)
        return tuple(n_re), tuple(n_im)

    zeros = tuple(jnp.zeros((nseq, LANES), F32) for _ in range(nl))
    lax.fori_loop(0, nch, step, (zeros, zeros))
    x_prev = jnp.concatenate(
        [jnp.concatenate([s_ref[j, bl * pitch:bl * pitch + nch, :] for j in range(2 * nl)], axis=1)
         for bl in range(nseq)], axis=0).astype(BF16)
    y = _bdot(lhs, toep_ref[...]) + _bdot(x_prev, wout_ref[...])
    dskip = d_ref[0]
    for bl in range(nseq):
        for s in range(S5_CHUNK):
            ys = y[bl * nch:(bl + 1) * nch, s * LANES:(s + 1) * LANES]
            ys = ys + dskip * h_ref[bl, s].astype(F32)
            o_ref[bl, s] = _gelu_tanh(ys).astype(o_ref.dtype)


def _s5_conv(hp, tw, vw, mw, a_pack, d_skip, *, nseq):
    bsz, _, nch, d = hp.shape
    nslab = d // LANES
    kdim = S5_CHUNK * LANES
    sdim = a_pack.shape[1] * LANES
    blk4 = lambda a: pl.BlockSpec((1,) + a.shape[1:], lambda c, b: (c, 0, 0, 0))
    return pl.pallas_call(
        functools.partial(_s5_conv_kernel, nseq=nseq, nch=nch),
        out_shape=jax.ShapeDtypeStruct(hp.shape, BF16),
        grid=(nslab, bsz // nseq),
        in_specs=[pl.BlockSpec((nseq, S5_CHUNK, nch, LANES), lambda c, b: (b, 0, 0, c)),
                  blk4(tw), blk4(vw), blk4(mw),
                  pl.BlockSpec((1, sdim // LANES, LANES), lambda c, b: (c, 0, 0)),
                  pl.BlockSpec((1, 1, LANES), lambda c, b: (c, 0, 0))],
        out_specs=pl.BlockSpec((nseq, S5_CHUNK, nch, LANES), lambda c, b: (b, 0, 0, c)),
        scratch_shapes=[pltpu.VMEM((sdim // LANES, nseq * (nch + S5_PITCH_PAD), LANES), F32),
                        pltpu.VMEM((kdim, kdim), BF16),
                        pltpu.VMEM((kdim, sdim), BF16),
                        pltpu.VMEM((sdim, kdim), BF16)],
        compiler_params=_params(("arbitrary", "arbitrary"), 56),
        name="s5_conv",
    )(hp, tw, vw, mw, a_pack, d_skip.reshape(nslab, 1, LANES))


def _s5_glu_kernel(y_ref, x_ref, w_ref, b_ref, o_ref, scr_ref, *, nloc):
    nslab = scr_ref.shape[0]
    y = jnp.concatenate([y_ref[0, s] for s in range(S5_CHUNK)], axis=0)
    u = y.astype(F32) * jax.nn.sigmoid(_bdot(y, w_ref[...]) + b_ref[...])
    for s in range(S5_CHUNK):
        rows = pl.ds(s, nloc, stride=S5_CHUNK)
        for c in range(nslab):
            scr_ref[c, rows, :] = u[s * nloc:(s + 1) * nloc, c * LANES:(c + 1) * LANES]
    o_ref[0] = x_ref[0] + jnp.concatenate([scr_ref[c] for c in range(nslab)], axis=1)


def _s5_glu(yp, x, w_glu, b_glu):
    bsz, seqlen, d = x.shape
    tm = min(S5_ROW_TILE, seqlen)
    nloc = tm // S5_CHUNK
    return pl.pallas_call(
        functools.partial(_s5_glu_kernel, nloc=nloc),
        out_shape=jax.ShapeDtypeStruct(x.shape, F32),
        grid=(bsz, seqlen // tm),
        in_specs=[pl.BlockSpec((1, S5_CHUNK, nloc, d), lambda b, i: (b, 0, i, 0)),
                  pl.BlockSpec((1, tm, d), lambda b, i: (b, i, 0)),
                  _resident((d, d), lambda b, i: (0, 0)),
                  pl.BlockSpec((1, d), lambda b, i: (0, 0))],
        out_specs=pl.BlockSpec((1, tm, d), lambda b, i: (b, i, 0)),
        scratch_shapes=[pltpu.VMEM((d // LANES, tm, LANES), F32)],
        compiler_params=_params(("parallel", "parallel"), 40),
        name="s5_glu",
    )(yp, x, w_glu.astype(BF16), b_glu.reshape(1, d))


def _s5_operators(lam_re, lam_im, log_dt, b_re, b_im, c_re, c_im):
    hi = lax.Precision.HIGHEST
    ngroups, nstate = lam_re.shape
    gpc = S5_SLAB_GROUPS
    nslab = ngroups // gpc
    dt = jnp.exp(log_dt)[:, None]
    j = jnp.arange(S5_CHUNK + 1, dtype=F32)[:, None, None]
    mag = jnp.exp(j * (lam_re * dt)[None])
    ang = j * (lam_im * dt)[None]
    pw_re, pw_im = mag * jnp.cos(ang), mag * jnp.sin(ang)
    num_re, num_im = pw_re[1] - 1.0, pw_im[1]
    den = lam_re * lam_re + lam_im * lam_im
    f_re = (num_re * lam_re + num_im * lam_im) / den
    f_im = (num_im * lam_re - num_re * lam_im) / den
    bb_re = f_re[..., None] * b_re - f_im[..., None] * b_im
    bb_im = f_re[..., None] * b_im + f_im[..., None] * b_re
    cp_re = c_re[None] * pw_re[:S5_CHUNK, :, None, :] - c_im[None] * pw_im[:S5_CHUNK, :, None, :]
    cp_im = c_re[None] * pw_im[:S5_CHUNK, :, None, :] + c_im[None] * pw_re[:S5_CHUNK, :, None, :]
    taps = (jnp.einsum('jghp,gpi->jghi', cp_re, bb_re, precision=hi)
            - jnp.einsum('jghp,gpi->jghi', cp_im, bb_im, precision=hi))
    tw = taps.reshape(S5_CHUNK, nslab, gpc, S5_GROUP, S5_GROUP)
    tw = tw.transpose(1, 0, 3, 2, 4).reshape(nslab, S5_CHUNK, S5_GROUP, LANES)
    jr = (S5_CHUNK - 1) - jnp.arange(S5_CHUNK, dtype=F32)[:, None, None]
    mag_r = jnp.exp(jr * (lam_re * dt)[None])
    ang_r = jr * (lam_im * dt)[None]
    rev_re, rev_im = mag_r * jnp.cos(ang_r), mag_r * jnp.sin(ang_r)
    v_re = rev_re[..., None] * bb_re[None] - rev_im[..., None] * bb_im[None]
    v_im = rev_re[..., None] * bb_im[None] + rev_im[..., None] * bb_re[None]
    v = jnp.stack([v_re, v_im], axis=0).reshape(2, S5_CHUNK, nslab, gpc, nstate, S5_GROUP)
    vw = v.transpose(2, 1, 0, 3, 5, 4).reshape(nslab, 2 * S5_CHUNK, LANES, nstate)
    m_re = c_re[None] * pw_re[1:, :, None, :] - c_im[None] * pw_im[1:, :, None, :]
    m_im = c_re[None] * pw_im[1:, :, None, :] + c_im[None] * pw_re[1:, :, None, :]
    m = jnp.stack([m_re, -m_im], axis=0).reshape(2, S5_CHUNK, nslab, gpc, S5_GROUP, nstate)
    mw = m.transpose(2, 1, 0, 4, 3, 5).reshape(nslab, 2 * S5_CHUNK, S5_GROUP, gpc * nstate)
    half = gpc * nstate // LANES
    a_pack = jnp.concatenate([pw_re[S5_CHUNK].reshape(nslab, half, LANES),
                              pw_im[S5_CHUNK].reshape(nslab, half, LANES)], axis=1)
    return tw, vw, mw, a_pack


def _s5_layer(x, ln, lam_re, lam_im, log_dt, b_re, b_im, c_re, c_im, d_skip, w_glu, b_glu, *, nseq=4):
    tw, vw, mw, a_pack = _s5_operators(lam_re, lam_im, log_dt, b_re, b_im, c_re, c_im)
    hp = _s5_norm(x, ln)
    yp = _s5_conv(hp, tw, vw, mw, a_pack, d_skip, nseq=min(nseq, x.shape[0]))
    return _s5_glu(yp, x, w_glu, b_glu)


def _dense_ffn_kernel(x_ref, g_ref, wg_ref, wu_ref, wd_ref, o_ref):
    xf = x_ref[...]
    h = _rms(xf, g_ref[...]).astype(BF16)
    act = (_silu(_bdot(h, wg_ref[...])) * _bdot(h, wu_ref[...])).astype(BF16)
    o_ref[...] = xf + _bdot(act, wd_ref[...])


def _dense_ffn_layer(x, ln, w_gate_up, w_down, *, tm=512):
    bsz, seqlen, d = x.shape
    ntok = bsz * seqlen
    hidden = w_down.shape[0]
    tm = min(tm, ntok)
    wgu = w_gate_up.astype(BF16)
    out = pl.pallas_call(
        _dense_ffn_kernel,
        out_shape=jax.ShapeDtypeStruct((ntok, d), F32),
        grid=(ntok // tm,),
        in_specs=[pl.BlockSpec((tm, d), lambda i: (i, 0)),
                  pl.BlockSpec((1, d), lambda i: (0, 0)),
                  _resident((d, hidden), lambda i: (0, 0)),
                  _resident((d, hidden), lambda i: (0, 1)),
                  _resident((hidden, d), lambda i: (0, 0))],
        out_specs=pl.BlockSpec((tm, d), lambda i: (i, 0)),
        compiler_params=_params(("parallel",), 56),
        name="dense_ffn",
    )(x.reshape(ntok, d), ln.reshape(1, d), wgu, wgu, w_down.astype(BF16))
    return out.reshape(bsz, seqlen, d)


def _log_sigmoid(z):
    return jnp.minimum(z, 0.0) - jnp.log(1.0 + jnp.exp(-jnp.abs(z)))


def _gla_kernel(x_ref, ln_ref, wm_ref, wgl_ref, wg2_ref, bg2_ref, gn_ref, wo_ref, o_ref, st_ref,
                *, tq, dk, dv, heads):
    hdk, hdv = dk // heads, dv // heads
    chunk = GLA_CHUNK
    nt = (((1,), (1,)), ((), ()))
    tn = (((0,), (0,)), ((), ()))

    @pl.when(pl.program_id(1) == 0)
    def _():
        st_ref[...] = jnp.zeros_like(st_ref)

    xf = x_ref[0]
    h = _rms(xf, ln_ref[...]).astype(BF16)
    proj = _bdot(h, wm_ref[...])
    glow = _bdot(h, wgl_ref[...]).astype(BF16)
    la = _log_sigmoid(_bdot(glow, wg2_ref[...]) + bg2_ref[...]) * (1.0 / GLA_GATE_NORM)
    row = lax.broadcasted_iota(jnp.int32, (chunk, chunk), 0)
    col = lax.broadcasted_iota(jnp.int32, (chunk, chunk), 1)
    causal = row >= col
    tri = jnp.where(causal, 1.0, 0.0).astype(BF16)
    scale = hdk ** -0.5
    outs = []
    for c in range(tq // chunk):
        r0 = c * chunk
        la_c = la[r0:r0 + chunk, :]
        la_hi = la_c.astype(BF16)
        la_lo = (la_c - la_hi.astype(F32)).astype(BF16)
        gcum_all = _bdot(tri, la_hi) + _bdot(tri, la_lo)
        head_out = []
        for hd in range(heads):
            gcum = gcum_all[:, hd * hdk:(hd + 1) * hdk]
            g_last = gcum[chunk - 1:chunk, :]
            q_c = proj[r0:r0 + chunk, hd * hdk:(hd + 1) * hdk] * scale
            k_c = proj[r0:r0 + chunk, dk + hd * hdk:dk + (hd + 1) * hdk]
            v_c = proj[r0:r0 + chunk, 2 * dk + hd * hdv:2 * dk + (hd + 1) * hdv].astype(BF16)
            q_s = (q_c * jnp.exp(gcum)).astype(BF16)
            k_s = (k_c * jnp.exp(-gcum)).astype(BF16)
            k_end = (k_c * jnp.exp(g_last - gcum)).astype(BF16)
            scores = lax.dot_general(q_s, k_s, nt, preferred_element_type=F32)
            scores = jnp.where(causal, scores, 0.0).astype(BF16)
            state_t = st_ref[hd]
            o = _bdot(scores, v_c) + lax.dot_general(q_s, state_t.astype(BF16), nt,
                                                     preferred_element_type=F32)
            kv_t = lax.dot_general(v_c, k_end, tn, preferred_element_type=F32)
            st_ref[hd] = state_t * jnp.exp(g_last) + kv_t
            head_out.append(o * lax.rsqrt(jnp.mean(o * o, axis=-1, keepdims=True) + EPS))
        outs.append(jnp.concatenate(head_out, axis=1))
    o_all = jnp.concatenate(outs, axis=0)
    r = proj[:, 2 * dk + dv:]
    o_all = (o_all * gn_ref[...] * _silu(r)).astype(BF16)
    o_ref[0] = xf + _bdot(o_all, wo_ref[...])


def _gla_layer(x, ln, w_in, w_g2, b_g2, g_norm, w_out, *, tq=256):
    bsz, seqlen, d = x.shape
    dk = w_g2.shape[1]
    dv = w_out.shape[0]
    nmain = 2 * dk + 2 * dv
    tq = min(tq, seqlen)
    w_main = w_in[:, :nmain].astype(BF16)
    w_glow = jnp.pad(w_in[:, nmain:], ((0, 0), (0, LANES - GLA_GATE_RANK))).astype(BF16)
    w_g2p = jnp.pad(w_g2, ((0, LANES - GLA_GATE_RANK), (0, 0))).astype(BF16)
    hdk, hdv = dk // GLA_HEADS, dv // GLA_HEADS
    const = lambda b, t: (0, 0)
    return pl.pallas_call(
        functools.partial(_gla_kernel, tq=tq, dk=dk, dv=dv, heads=GLA_HEADS),
        out_shape=jax.ShapeDtypeStruct(x.shape, F32),
        grid=(bsz, seqlen // tq),
        in_specs=[pl.BlockSpec((1, tq, d), lambda b, t: (b, t, 0)),
                  pl.BlockSpec((1, d), const),
                  _resident((d, nmain), const),
                  _resident((d, LANES), const),
                  _resident((LANES, dk), const),
                  pl.BlockSpec((1, dk), const),
                  pl.BlockSpec((1, dv), const),
                  _resident((dv, d), const)],
        out_specs=pl.BlockSpec((1, tq, d), lambda b, t: (b, t, 0)),
        scratch_shapes=[pltpu.VMEM((GLA_HEADS, hdv, hdk), F32)],
        compiler_params=_params(("parallel", "arbitrary"), 48),
        name="gla",
    )(x, ln.reshape(1, d), w_main, w_glow, w_g2p, b_g2.reshape(1, dk), g_norm.reshape(1, dv),
      w_out.astype(BF16))


LOG2E = math.log2(math.e)
SWA_HEAD_UNROLL = 4


def _swa_kernel(sink_ref, x_ref, ln_ref, wqkv_ref, bqkv_ref, wo_ref, bo_ref, o_ref, k_ref, v_ref,
                bias_ref, q_ref, a_ref, *, tq, q_heads):
    group = q_heads // SWA_KV_HEADS
    blk = SWA_BLOCK
    nt = (((1,), (1,)), ((), ()))
    b = pl.program_id(0)
    t = pl.program_id(1)
    nq = q_heads * LANES

    @pl.when((b == 0) & (t == 0))
    def _():
        qi = lax.broadcasted_iota(jnp.int32, (blk, 2 * blk), 0)
        kj = lax.broadcasted_iota(jnp.int32, (blk, 2 * blk), 1)
        dist = qi + blk - kj
        in_window = (dist >= 0) & (dist < SWA_WINDOW)
        for hq in range(q_heads):
            slope = 2.0 ** (-8.0 * (hq + 1) / q_heads)
            bias_ref[hq] = jnp.where(in_window, -(slope * LOG2E) * dist.astype(F32), MASK_VALUE)

    @pl.when(t == 0)
    def _():
        k_ref[0:blk, :] = jnp.zeros((blk, LANES), BF16)
        v_ref[0:blk, :] = jnp.zeros((blk, LANES), BF16)

    xf = x_ref[0]
    h = _rms(xf, ln_ref[...]).astype(BF16)
    qkv = _bdot(h, wqkv_ref[...]) + bqkv_ref[...]
    for hq in range(q_heads):
        q_ref[hq] = qkv[:, hq * LANES:(hq + 1) * LANES].astype(BF16)
    k_ref[blk:blk + tq, :] = qkv[:, nq:nq + LANES].astype(BF16)
    v_ref[blk:blk + tq, :] = qkv[:, nq + LANES:nq + 2 * LANES].astype(BF16)
    kj_row = lax.broadcasted_iota(jnp.int32, (1, 2 * blk), 1)
    no_prev = jnp.where(kj_row < blk, jnp.where(t == 0, MASK_VALUE, 0.0), 0.0)

    def heads(j, carry):
        for u in range(SWA_HEAD_UNROLL):
            hq = j * SWA_HEAD_UNROLL + u
            sink = sink_ref[hq] * LOG2E
            for i in range(tq // blk):
                r0 = i * blk
                s = lax.dot_general(q_ref[hq, r0:r0 + blk, :], k_ref[r0:r0 + 2 * blk, :], nt,
                                    preferred_element_type=F32) + bias_ref[hq]
                if i == 0:
                    s = s + no_prev
                m = jnp.maximum(jnp.max(s, axis=-1, keepdims=True), sink)
                p = jnp.exp2(s - m)
                denom = jnp.sum(p, axis=-1, keepdims=True) + jnp.exp2(sink - m)
                o = _bdot(p.astype(BF16), v_ref[r0:r0 + 2 * blk, :]) * (1.0 / denom)
                a_ref[hq, r0:r0 + blk, :] = o.astype(BF16)
        return carry

    lax.fori_loop(0, q_heads // SWA_HEAD_UNROLL, heads, 0)
    k_ref[0:blk, :] = k_ref[tq:tq + blk, :]
    v_ref[0:blk, :] = v_ref[tq:tq + blk, :]
    o_all = jnp.concatenate([a_ref[hq] for hq in range(q_heads)], axis=1)
    o_ref[0] = xf + _bdot(o_all, wo_ref[...]) + bo_ref[...]


def _swa_layer(x, ln, w_qkv, b_qkv, sinks, w_out, b_out, *, tq=512):
    bsz, seqlen, d = x.shape
    hd = SWA_HEAD_DIM
    q_heads = sinks.shape[0]
    group = q_heads // SWA_KV_HEADS
    nq = q_heads * hd
    tq = min(tq, seqlen)
    kv_of_head = jnp.arange(q_heads) // group
    place = (jnp.arange(LANES // hd)[None, :] == kv_of_head[:, None]).astype(F32)
    q_scale = hd ** -0.5 * LOG2E
    wq = (w_qkv[:, :nq] * q_scale).reshape(d, q_heads, 1, hd) * place[None, :, :, None]
    bq = (b_qkv[:nq] * q_scale).reshape(q_heads, 1, hd) * place[:, :, None]
    w_all = jnp.concatenate([wq.reshape(d, q_heads * LANES), w_qkv[:, nq:]], axis=1).astype(BF16)
    b_all = jnp.concatenate([bq.reshape(q_heads * LANES), b_qkv[nq:]]).reshape(1, -1)
    wo = (w_out.reshape(q_heads, 1, hd, d) * place[:, :, None, None]).reshape(q_heads * LANES, d).astype(BF16)
    nall = w_all.shape[1]
    const = lambda b, t, s: (0, 0)
    return pl.pallas_call(
        functools.partial(_swa_kernel, tq=tq, q_heads=q_heads),
        out_shape=jax.ShapeDtypeStruct(x.shape, F32),
        grid_spec=pltpu.PrefetchScalarGridSpec(
            num_scalar_prefetch=1,
            grid=(bsz, seqlen // tq),
            in_specs=[pl.BlockSpec((1, tq, d), lambda b, t, s: (b, t, 0)),
                      pl.BlockSpec((1, d), const),
                      _resident((d, nall), const),
                      pl.BlockSpec((1, nall), const),
                      _resident((q_heads * LANES, d), const),
                      pl.BlockSpec((1, d), const)],
            out_specs=pl.BlockSpec((1, tq, d), lambda b, t, s: (b, t, 0)),
            scratch_shapes=[pltpu.VMEM((SWA_BLOCK + tq, LANES), BF16), pltpu.VMEM((SWA_BLOCK + tq, LANES), BF16),
                            pltpu.VMEM((q_heads, SWA_BLOCK, 2 * SWA_BLOCK), F32),
                            pltpu.VMEM((q_heads, tq, LANES), BF16), pltpu.VMEM((q_heads, tq, LANES), BF16)]),
        compiler_params=_params(("arbitrary", "arbitrary"), 48),
        name="swa",
    )(sinks, x, ln.reshape(1, d), w_all, b_all, wo, b_out.reshape(1, d))


def _router_kernel(x_ref, ln_ref, whi_ref, wlo_ref, idx_ref, gate_ref, hp_ref):
    nt = (((1,), (1,)), ((), ()))
    h = _rms(x_ref[...], ln_ref[...])
    h_hi = h.astype(BF16)
    h_lo = (h - h_hi.astype(F32)).astype(BF16)
    w_hi, w_lo = whi_ref[...], wlo_ref[...]
    logits = (lax.dot_general(w_hi, h_hi, nt, preferred_element_type=F32)
              + lax.dot_general(w_hi, h_lo, nt, preferred_element_type=F32)
              + lax.dot_general(w_lo, h_hi, nt, preferred_element_type=F32))
    n_exp = logits.shape[0]
    eid = lax.broadcasted_iota(jnp.int32, logits.shape, 0)
    m1 = jnp.max(logits, axis=0, keepdims=True)
    i1 = jnp.min(jnp.where(logits == m1, eid, n_exp), axis=0, keepdims=True)
    rest = jnp.where(eid == i1, -jnp.inf, logits)
    m2 = jnp.max(rest, axis=0, keepdims=True)
    i2 = jnp.min(jnp.where(rest == m2, eid, n_exp), axis=0, keepdims=True)
    e2 = jnp.exp(m2 - m1)
    g1 = 1.0 / (1.0 + e2)
    idx_ref[...] = jnp.concatenate([i1, i2], axis=0)
    gate_ref[...] = jnp.concatenate([g1, e2 * g1], axis=0)
    hp_ref[...] = _pack_bf16_pairs(h)


def _router(x2, ln, w_router, *, tm=512):
    ntok, d = x2.shape
    n_exp = w_router.shape[1]
    tm = min(tm, ntok)
    wt = w_router.T
    w_hi = wt.astype(BF16)
    w_lo = (wt - w_hi.astype(F32)).astype(BF16)
    return pl.pallas_call(
        _router_kernel,
        out_shape=(jax.ShapeDtypeStruct((TOP_K, ntok), jnp.int32), jax.ShapeDtypeStruct((TOP_K, ntok), F32),
                   jax.ShapeDtypeStruct((ntok, d // 2), jnp.uint32)),
        grid=(ntok // tm,),
        in_specs=[pl.BlockSpec((tm, d), lambda i: (i, 0)),
                  pl.BlockSpec((1, d), lambda i: (0, 0)),
                  pl.BlockSpec((n_exp, d), lambda i: (0, 0)),
                  pl.BlockSpec((n_exp, d), lambda i: (0, 0))],
        out_specs=(pl.BlockSpec((TOP_K, tm), lambda i: (0, i)), pl.BlockSpec((TOP_K, tm), lambda i: (0, i)),
                   pl.BlockSpec((tm, d // 2), lambda i: (i, 0))),
        compiler_params=_params(("parallel",), 32),
        name="moe_router",
    )(x2, ln.reshape(1, d), w_hi, w_lo)


def _moe_plan(idx, n_exp, tile):
    nslots = idx.size
    flat = idx.reshape(-1)
    onehot = (flat[None, :] == jnp.arange(n_exp, dtype=jnp.int32)[:, None]).astype(jnp.int32)
    csum = jnp.cumsum(onehot, axis=1)
    counts = csum[:, -1]
    ends = jnp.cumsum(counts)
    offs = ends - counts
    rank = jnp.sum(onehot * (csum - 1 + offs[:, None]), axis=0).reshape(idx.shape)
    n_tiles = nslots // tile
    n_visits = n_tiles + n_exp - 1
    first_tile = offs // tile
    last_tile = (ends - 1) // tile
    nvis = jnp.where(counts > 0, last_tile - first_tile + 1, 0)
    vend = jnp.cumsum(nvis)
    vstart = vend - nvis
    total = vend[-1]
    v = jnp.arange(n_visits, dtype=jnp.int32)
    vc = jnp.minimum(v, total - 1)
    e = jnp.minimum(jnp.sum((vc[:, None] >= vend[None, :]).astype(jnp.int32), axis=1), n_exp - 1)
    sel = (e[:, None] == jnp.arange(n_exp, dtype=jnp.int32)[None, :]).astype(jnp.int32)
    pick = lambda a: jnp.sum(sel * a[None, :], axis=1)
    tile_id = pick(first_tile) + vc - pick(vstart)
    lo = jnp.maximum(pick(offs), tile_id * tile) - tile_id * tile
    hi = jnp.minimum(pick(ends), (tile_id + 1) * tile) - tile_id * tile
    valid = v < total
    lo = jnp.where(valid, lo, 0)
    hi = jnp.where(valid, hi, 0)
    prev_tile = jnp.concatenate([jnp.full((1,), -1, jnp.int32), tile_id[:-1]])
    first = (valid & (tile_id != prev_tile)).astype(jnp.int32)
    meta = jnp.stack([tile_id, e, lo, hi, first]).astype(jnp.int32)
    return rank.astype(jnp.int32), meta


def _pack_bf16_pairs(h):
    half = h.shape[1] // 2
    bits = lax.bitcast_convert_type(h.astype(BF16).astype(F32), jnp.uint32)
    return (bits[:, half:] & jnp.uint32(0xFFFF0000)) | (bits[:, :half] >> 16)


def _unpack_bf16_pairs(u):
    lo = lax.bitcast_convert_type(u << 16, F32)
    hi = lax.bitcast_convert_type(u & jnp.uint32(0xFFFF0000), F32)
    return jnp.concatenate([lo, hi], axis=1).astype(BF16)


SC_CORES = 2
SC_SUBCORES = 16
SC_INDEX_WINDOW = 128


def _sc_mesh():
    return plsc.VectorSubcoreMesh(core_axis_name="c", subcore_axis_name="s")


def _sc_worker_id():
    return lax.axis_index("c") * SC_SUBCORES + lax.axis_index("s")


def _sc_scatter_rows(src, rank, nrows):
    ntok, width = src.shape
    win = SC_INDEX_WINDOW
    per = ntok // (SC_CORES * SC_SUBCORES)

    @pl.kernel(out_type=jax.ShapeDtypeStruct((nrows, width), src.dtype), mesh=_sc_mesh(),
               scratch_types=[pltpu.VMEM((1, win), jnp.int32)] * TOP_K + [pltpu.VMEM((win, width), src.dtype)],
               name="moe_dispatch_sc")
    def scatter(src_hbm, rank_hbm, o_hbm, *scratch):
        idx_vmem, buf = scratch[:TOP_K], scratch[TOP_K]
        wid = _sc_worker_id()

        @pl.loop(0, per // win)
        def _(blk):
            base = wid * per + blk * win
            for k in range(TOP_K):
                pltpu.sync_copy(rank_hbm.at[pl.ds(k, 1), pl.ds(base, win)], idx_vmem[k])
            pltpu.sync_copy(src_hbm.at[pl.ds(base, win)], buf)
            for k in range(TOP_K):
                pltpu.sync_copy(buf, o_hbm.at[idx_vmem[k].at[0]])

    return scatter(src, rank)


def _sc_gather_rows(src, idx, *, sub=32):
    n = idx.shape[0]
    width = src.shape[1]
    win = SC_INDEX_WINDOW
    per = n // (SC_CORES * SC_SUBCORES)
    nsub = win // sub

    @pl.kernel(out_type=jax.ShapeDtypeStruct((n, width), src.dtype), mesh=_sc_mesh(),
               scratch_types=[pltpu.VMEM((1, win), jnp.int32)] + [pltpu.VMEM((sub, width), src.dtype)] * 2
               + [pltpu.SemaphoreType.DMA] * 4,
               name="moe_gather_sc")
    def gather(src_hbm, idx_hbm, o_hbm, i_vmem, buf0, buf1, g0, g1, w0, w1):
        bufs, gsem, wsem = (buf0, buf1), (g0, g1), (w0, w1)
        wid = _sc_worker_id()

        @pl.loop(0, per // win)
        def _(blk):
            base = wid * per + blk * win
            pltpu.sync_copy(idx_hbm.at[:, pl.ds(base, win)], i_vmem)
            gathers = [pltpu.make_async_copy(src_hbm.at[i_vmem.at[0, pl.ds(sub * j, sub)]], bufs[j % 2], gsem[j % 2])
                       for j in range(nsub)]
            writes = [pltpu.make_async_copy(bufs[j % 2], o_hbm.at[pl.ds(base + sub * j, sub)], wsem[j % 2])
                      for j in range(nsub)]
            gathers[0].start()
            for j in range(nsub):
                if j + 1 < nsub:
                    if j >= 1:
                        writes[j - 1].wait()
                    gathers[j + 1].start()
                gathers[j].wait()
                writes[j].start()
            writes[nsub - 2].wait()
            writes[nsub - 1].wait()

    return gather(src, idx.reshape(1, n))


MXU_N = 256


def _expert_kernel(meta_ref, x_ref, wg_ref, wu_ref, wd_ref, o_ref, xb_ref, act_ref, wgb_ref, wub_ref, wdb_ref,
                   *, ts):
    v = pl.program_id(0)
    hc = pl.program_id(1)
    lo, hi, first = meta_ref[2, v], meta_ref[3, v], meta_ref[4, v]
    tile, d = o_ref.shape
    nsub = tile // ts
    th = wgb_ref.shape[1]
    full = (lo == 0) & (hi == tile)

    @pl.when(hc == 0)
    def _():
        for sub in range(nsub):
            xb_ref[sub * ts:(sub + 1) * ts, :] = _unpack_bf16_pairs(x_ref[sub * ts:(sub + 1) * ts, :])

    @pl.when((first == 1) & (hc == 0))
    def _():
        o_ref[...] = jnp.zeros_like(o_ref)

    @pl.when(full)
    def _():
        for n in range(th // MXU_N):
            cols = slice(n * MXU_N, (n + 1) * MXU_N)
            gate = _bdot(xb_ref[...], wg_ref[0, :, cols].astype(BF16))
            up = _bdot(xb_ref[...], wu_ref[0, :, cols].astype(BF16))
            act_ref[:, cols] = (_silu(gate) * up).astype(BF16)
        for n in range(d // MXU_N):
            cols = slice(n * MXU_N, (n + 1) * MXU_N)
            o_ref[:, cols] += _bdot(act_ref[...], wd_ref[0, :, cols].astype(BF16))

    @pl.when(jnp.logical_not(full) & (hi > lo))
    def _():
        wgb_ref[...] = wg_ref[0].astype(BF16)
        wub_ref[...] = wu_ref[0].astype(BF16)
        wdb_ref[...] = wd_ref[0].astype(BF16)
        for sub in range(nsub):
            r0 = sub * ts

            @pl.when((lo < r0 + ts) & (hi > r0))
            def _():
                xs = xb_ref[r0:r0 + ts, :]
                act = (_silu(_bdot(xs, wgb_ref[...])) * _bdot(xs, wub_ref[...])).astype(BF16)
                y = _bdot(act, wdb_ref[...])
                rows = r0 + lax.broadcasted_iota(jnp.int32, (ts, 1), 0)
                o_ref[r0:r0 + ts, :] += jnp.where((rows >= lo) & (rows < hi), y, 0.0)


def _experts(xg, meta, w_gate_up, w_down, *, tile, th=512, ts=512):
    nrows = xg.shape[0]
    n_exp, hidden, d = w_down.shape
    n_hc = hidden // th
    ts = min(ts, tile)
    wgu = w_gate_up
    return pl.pallas_call(
        functools.partial(_expert_kernel, ts=ts),
        out_shape=jax.ShapeDtypeStruct((nrows, d), F32),
        grid_spec=pltpu.PrefetchScalarGridSpec(
            num_scalar_prefetch=1,
            grid=(meta.shape[1], n_hc),
            in_specs=[pl.BlockSpec((tile, d // 2), lambda v, c, m: (m[0, v], 0)),
                      pl.BlockSpec((1, d, th), lambda v, c, m: (m[1, v], 0, c)),
                      pl.BlockSpec((1, d, th), lambda v, c, m: (m[1, v], 0, c + n_hc)),
                      pl.BlockSpec((1, th, d), lambda v, c, m: (m[1, v], c, 0))],
            out_specs=pl.BlockSpec((tile, d), lambda v, c, m: (m[0, v], 0)),
            scratch_shapes=[pltpu.VMEM((tile, d), BF16), pltpu.VMEM((tile, th), BF16), pltpu.VMEM((d, th), BF16),
                            pltpu.VMEM((d, th), BF16), pltpu.VMEM((th, d), BF16)]),
        compiler_params=_params(("arbitrary", "arbitrary"), 56),
        name="moe_experts",
    )(meta, xg, wgu, wgu, w_down)


def _combine_kernel(x_ref, gate_ref, fg_ref, y0_ref, y1_ref, o_ref, *, final_norm):
    g = gate_ref[...]
    out = x_ref[...] + g[:, 0:1] * y0_ref[0] + g[:, 1:2] * y1_ref[0]
    if final_norm:
        out = _rms(out, fg_ref[...])
    o_ref[...] = out


def _combine(x2, gates_t, yk, final_gain, *, tm=512):
    ntok, d = x2.shape
    tm = min(tm, ntok)
    final_norm = final_gain is not None
    fg = (final_gain if final_norm else jnp.ones((d,), F32)).reshape(1, d)
    return pl.pallas_call(
        functools.partial(_combine_kernel, final_norm=final_norm),
        out_shape=jax.ShapeDtypeStruct((ntok, d), F32),
        grid=(ntok // tm,),
        in_specs=[pl.BlockSpec((tm, d), lambda i: (i, 0)),
                  pl.BlockSpec((tm, TOP_K), lambda i: (i, 0)),
                  pl.BlockSpec((1, d), lambda i: (0, 0)),
                  pl.BlockSpec((1, tm, d), lambda i: (0, i, 0)),
                  pl.BlockSpec((1, tm, d), lambda i: (1, i, 0))],
        out_specs=pl.BlockSpec((tm, d), lambda i: (i, 0)),
        compiler_params=_params(("parallel",), 40),
        name="moe_combine",
    )(x2, gates_t, fg, yk, yk)


def _moe_layer(x, ln, w_router, w_gate_up, w_down, *, final_gain=None, tile=2048):
    bsz, seqlen, d = x.shape
    ntok = bsz * seqlen
    n_exp = w_router.shape[1]
    tile = min(tile, TOP_K * ntok)
    x2 = x.reshape(ntok, d)
    idx, gates, hp = _router(x2, ln, w_router)
    rank, meta = _moe_plan(idx, n_exp, tile)
    xg = _sc_scatter_rows(hp, rank, TOP_K * ntok)
    y = _experts(xg, meta, w_gate_up, w_down, tile=tile)
    yk = _sc_gather_rows(y, rank.reshape(-1)).reshape(TOP_K, ntok, d)
    out = _combine(x2, gates.T, yk, final_gain)
    return out.reshape(bsz, seqlen, d)


def kernel(x, l0_ln1, l0_s5_lam_re, l0_s5_lam_im, l0_s5_log_dt, l0_s5_b_re, l0_s5_b_im, l0_s5_c_re, l0_s5_c_im, l0_s5_d, l0_s5_w_glu, l0_s5_b_glu, l0_ln2, l0_ffn_w_gate_up, l0_ffn_w_down, l1_ln1, l1_gla_w_in, l1_gla_w_g2, l1_gla_b_g2, l1_gla_norm, l1_gla_w_out, l1_ln2, l1_moe_router, l1_moe_w_gate_up, l1_moe_w_down, l2_ln1, l2_swa_w_qkv, l2_swa_b_qkv, l2_swa_sinks, l2_swa_w_out, l2_swa_b_out, l2_ln2, l2_ffn_w_gate_up, l2_ffn_w_down, l3_ln1, l3_s5_lam_re, l3_s5_lam_im, l3_s5_log_dt, l3_s5_b_re, l3_s5_b_im, l3_s5_c_re, l3_s5_c_im, l3_s5_d, l3_s5_w_glu, l3_s5_b_glu, l3_ln2, l3_moe_router, l3_moe_w_gate_up, l3_moe_w_down, ln_f):
    x = _s5_layer(x, l0_ln1, l0_s5_lam_re, l0_s5_lam_im, l0_s5_log_dt, l0_s5_b_re, l0_s5_b_im,
                  l0_s5_c_re, l0_s5_c_im, l0_s5_d, l0_s5_w_glu, l0_s5_b_glu)
    x = _dense_ffn_layer(x, l0_ln2, l0_ffn_w_gate_up, l0_ffn_w_down)
    x = _gla_layer(x, l1_ln1, l1_gla_w_in, l1_gla_w_g2, l1_gla_b_g2, l1_gla_norm, l1_gla_w_out)
    x = _moe_layer(x, l1_ln2, l1_moe_router, l1_moe_w_gate_up, l1_moe_w_down)
    x = _swa_layer(x, l2_ln1, l2_swa_w_qkv, l2_swa_b_qkv, l2_swa_sinks, l2_swa_w_out, l2_swa_b_out)
    x = _dense_ffn_layer(x, l2_ln2, l2_ffn_w_gate_up, l2_ffn_w_down)
    x = _s5_layer(x, l3_ln1, l3_s5_lam_re, l3_s5_lam_im, l3_s5_log_dt, l3_s5_b_re, l3_s5_b_im,
                  l3_s5_c_re, l3_s5_c_im, l3_s5_d, l3_s5_w_glu, l3_s5_b_glu)
    return _moe_layer(x, l3_ln2, l3_moe_router, l3_moe_w_gate_up, l3_moe_w_down, final_gain=ln_f)
```

```python
import functools
import math

import jax
import jax.numpy as jnp
from jax import lax
from jax.experimental import pallas as pl
from jax.experimental.pallas import tpu as pltpu
from jax.experimental.pallas import tpu_sc as plsc

F32 = jnp.float32
BF16 = jnp.bfloat16
EPS = 1e-6
LANES = 128
MIB = 1 << 20

S5_GROUP = 16
S5_STATE = 64
S5_CHUNK = 16
S5_SLAB_GROUPS = LANES // S5_GROUP
S5_PITCH_PAD = 8

GLA_HEADS = 4
GLA_GATE_RANK = 16
GLA_GATE_NORM = 16.0
GLA_CHUNK = 64

SWA_HEAD_DIM = 64
SWA_KV_HEADS = 2
SWA_WINDOW = 128
SWA_BLOCK = 128
MASK_VALUE = -1e30

TOP_K = 2


def _params(semantics, vmem_mib):
    return pltpu.CompilerParams(dimension_semantics=semantics, vmem_limit_bytes=vmem_mib * MIB)


def _resident(block_shape, index_map):
    return pl.BlockSpec(block_shape, index_map, pipeline_mode=pl.Buffered(1))


def _rms(xf, gain):
    return xf * lax.rsqrt(jnp.mean(xf * xf, axis=-1, keepdims=True) + EPS) * gain


def _gelu_tanh(x):
    return 0.5 * x * (1.0 + jnp.tanh(math.sqrt(2.0 / math.pi) * (x + 0.044715 * (x * x * x))))


def _silu(x):
    return x * jax.nn.sigmoid(x)


def _bdot(a, b):
    return jnp.dot(a, b, preferred_element_type=F32)


S5_ROW_TILE = 512


def _s5_norm_kernel(x_ref, g_ref, o_ref, scr_ref, *, nloc):
    h = _rms(x_ref[0], g_ref[...])
    nslab = scr_ref.shape[0]
    for c in range(nslab):
        scr_ref[c] = h[:, c * LANES:(c + 1) * LANES]
    for s in range(S5_CHUNK):
        rows = pl.ds(s, nloc, stride=S5_CHUNK)
        o_ref[0, s] = jnp.concatenate([scr_ref[c, rows, :] for c in range(nslab)], axis=1).astype(o_ref.dtype)


def _s5_norm(x, gain):
    bsz, seqlen, d = x.shape
    nch = seqlen // S5_CHUNK
    tm = min(S5_ROW_TILE, seqlen)
    nloc = tm // S5_CHUNK
    return pl.pallas_call(
        functools.partial(_s5_norm_kernel, nloc=nloc),
        out_shape=jax.ShapeDtypeStruct((bsz, S5_CHUNK, nch, d), BF16),
        grid=(bsz, seqlen // tm),
        in_specs=[pl.BlockSpec((1, tm, d), lambda b, i: (b, i, 0)),
                  pl.BlockSpec((1, d), lambda b, i: (0, 0))],
        out_specs=pl.BlockSpec((1, S5_CHUNK, nloc, d), lambda b, i: (b, 0, i, 0)),
        scratch_shapes=[pltpu.VMEM((d // LANES, tm, LANES), F32)],
        compiler_params=_params(("parallel", "parallel"), 32),
        name="s5_norm",
    )(x, gain.reshape(1, d))


def _tiling_matrix(rows, cols):
    p = lax.broadcasted_iota(jnp.int32, (rows, cols), 0)
    c = lax.broadcasted_iota(jnp.int32, (rows, cols), 1)
    return jnp.where(c % rows == p, 1.0, 0.0).astype(BF16)


def _same_group(shape, row_group, col_group):
    r = lax.broadcasted_iota(jnp.int32, shape, 0)
    c = lax.broadcasted_iota(jnp.int32, shape, 1)
    return (r // row_group) == (c // col_group)


def _s5_build_operators(tw_ref, vw_ref, mw_ref, toep_ref, win_ref, wout_ref):
    tn = (((0,), (0,)), ((), ()))
    nstate = vw_ref.shape[-1]
    half = S5_SLAB_GROUPS * nstate
    rep_ch = _tiling_matrix(S5_GROUP, LANES)
    rep_st = _tiling_matrix(nstate, half)
    diag = _same_group((LANES, LANES), S5_GROUP, S5_GROUP)
    taps = []
    for j in range(S5_CHUNK):
        e = lax.dot_general(tw_ref[0, j].astype(BF16), rep_ch, tn, preferred_element_type=F32)
        taps.append(jnp.where(diag, e, 0.0).astype(BF16))
    zero = jnp.zeros((LANES, LANES), BF16)
    for a in range(S5_CHUNK):
        for b in range(S5_CHUNK):
            toep_ref[a * LANES:(a + 1) * LANES, b * LANES:(b + 1) * LANES] = taps[b - a] if b >= a else zero
    diag_in = _same_group((LANES, half), S5_GROUP, nstate)
    diag_out = _same_group((half, LANES), nstate, S5_GROUP)
    for a in range(S5_CHUNK):
        for r in range(2):
            e = _bdot(vw_ref[0, 2 * a + r].astype(BF16), rep_st)
            win_ref[a * LANES:(a + 1) * LANES, r * half:(r + 1) * half] = jnp.where(diag_in, e, 0.0).astype(BF16)
            e = lax.dot_general(mw_ref[0, 2 * a + r].astype(BF16), rep_ch, tn, preferred_element_type=F32)
            wout_ref[r * half:(r + 1) * half, a * LANES:(a + 1) * LANES] = jnp.where(diag_out, e, 0.0).astype(BF16)


def _s5_conv_kernel(h_ref, tw_ref, vw_ref, mw_ref, a_ref, d_ref, o_ref, s_ref, toep_ref, win_ref, wout_ref,
                    *, nseq, nch):
    pitch = nch + S5_PITCH_PAD
    nl = a_ref.shape[1] // 2

    @pl.when(pl.program_id(1) == 0)
    def _():
        _s5_build_operators(tw_ref, vw_ref, mw_ref, toep_ref, win_ref, wout_ref)

    lhs = jnp.concatenate(
        [jnp.concatenate([h_ref[bl, s] for s in range(S5_CHUNK)], axis=1) for bl in range(nseq)], axis=0)
    bc = _bdot(lhs, win_ref[...])
    for bl in range(nseq):
        for j in range(2 * nl):
            s_ref[j, bl * pitch:bl * pitch + nch, :] = bc[bl * nch:(bl + 1) * nch, j * LANES:(j + 1) * LANES]
    a_re = [a_ref[0, j:j + 1, :] for j in range(nl)]
    a_im = [a_ref[0, nl + j:nl + j + 1, :] for j in range(nl)]

    def step(n, carry):
        p_re, p_im = carry
        rows = pl.ds(n, nseq, stride=pitch)
        n_re, n_im = [], []
        for j in range(nl):
            c_re = s_ref[j, rows, :]
            c_im = s_ref[nl + j, rows, :]
            s_ref[j, rows, :] = p_re[j]
            s_ref[nl + j, rows, :] = p_im[j]
            n_re.append(a_re[j] * p_re[j] - a_im[j] * p_im[j] + c_re)
            n_im.append(a_re[j] * p_im[j] + a_im[j] * p_re[j] + c_im)
        return tuple(n_re), tuple(n_im)

    zeros = tuple(jnp.zeros((nseq, LANES), F32) for _ in range(nl))
    lax.fori_loop(0, nch, step, (zeros, zeros))
    x_prev = jnp.concatenate(
        [jnp.concatenate([s_ref[j, bl * pitch:bl * pitch + nch, :] for j in range(2 * nl)], axis=1)
         for bl in range(nseq)], axis=0).astype(BF16)
    y = _bdot(lhs, toep_ref[...]) + _bdot(x_prev, wout_ref[...])
    dskip = d_ref[0]
    for bl in range(nseq):
        for s in range(S5_CHUNK):
            ys = y[bl * nch:(bl + 1) * nch, s * LANES:(s + 1) * LANES]
            ys = ys + dskip * h_ref[bl, s].astype(F32)
            o_ref[bl, s] = _gelu_tanh(ys).astype(o_ref.dtype)


def _s5_conv(hp, tw, vw, mw, a_pack, d_skip, *, nseq):
    bsz, _, nch, d = hp.shape
    nslab = d // LANES
    kdim = S5_CHUNK * LANES
    sdim = a_pack.shape[1] * LANES
    blk4 = lambda a: pl.BlockSpec((1,) + a.shape[1:], lambda c, b: (c, 0, 0, 0))
    return pl.pallas_call(
        functools.partial(_s5_conv_kernel, nseq=nseq, nch=nch),
        out_shape=jax.ShapeDtypeStruct(hp.shape, BF16),
        grid=(nslab, bsz // nseq),
        in_specs=[pl.BlockSpec((nseq, S5_CHUNK, nch, LANES), lambda c, b: (b, 0, 0, c)),
                  blk4(tw), blk4(vw), blk4(mw),
                  pl.BlockSpec((1, sdim // LANES, LANES), lambda c, b: (c, 0, 0)),
                  pl.BlockSpec((1, 1, LANES), lambda c, b: (c, 0, 0))],
        out_specs=pl.BlockSpec((nseq, S5_CHUNK, nch, LANES), lambda c, b: (b, 0, 0, c)),
        scratch_shapes=[pltpu.VMEM((sdim // LANES, nseq * (nch + S5_PITCH_PAD), LANES), F32),
                        pltpu.VMEM((kdim, kdim), BF16),
                        pltpu.VMEM((kdim, sdim), BF16),
                        pltpu.VMEM((sdim, kdim), BF16)],
        compiler_params=_params(("arbitrary", "arbitrary"), 56),
        name="s5_conv",
    )(hp, tw, vw, mw, a_pack, d_skip.reshape(nslab, 1, LANES))


def _s5_glu_kernel(y_ref, x_ref, w_ref, b_ref, o_ref, scr_ref, *, nloc):
    nslab = scr_ref.shape[0]
    y = jnp.concatenate([y_ref[0, s] for s in range(S5_CHUNK)], axis=0)
    u = y.astype(F32) * jax.nn.sigmoid(_bdot(y, w_ref[...]) + b_ref[...])
    for s in range(S5_CHUNK):
        rows = pl.ds(s, nloc, stride=S5_CHUNK)
        for c in range(nslab):
            scr_ref[c, rows, :] = u[s * nloc:(s + 1) * nloc, c * LANES:(c + 1) * LANES]
    o_ref[0] = x_ref[0] + jnp.concatenate([scr_ref[c] for c in range(nslab)], axis=1)


def _s5_glu(yp, x, w_glu, b_glu):
    bsz, seqlen, d = x.shape
    tm = min(S5_ROW_TILE, seqlen)
    nloc = tm // S5_CHUNK
    return pl.pallas_call(
        functools.partial(_s5_glu_kernel, nloc=nloc),
        out_shape=jax.ShapeDtypeStruct(x.shape, F32),
        grid=(bsz, seqlen // tm),
        in_specs=[pl.BlockSpec((1, S5_CHUNK, nloc, d), lambda b, i: (b, 0, i, 0)),
                  pl.BlockSpec((1, tm, d), lambda b, i: (b, i, 0)),
                  _resident((d, d), lambda b, i: (0, 0)),
                  pl.BlockSpec((1, d), lambda b, i: (0, 0))],
        out_specs=pl.BlockSpec((1, tm, d), lambda b, i: (b, i, 0)),
        scratch_shapes=[pltpu.VMEM((d // LANES, tm, LANES), F32)],
        compiler_params=_params(("parallel", "parallel"), 40),
        name="s5_glu",
    )(yp, x, w_glu.astype(BF16), b_glu.reshape(1, d))


def _s5_operators(lam_re, lam_im, log_dt, b_re, b_im, c_re, c_im):
    hi = lax.Precision.HIGHEST
    ngroups, nstate = lam_re.shape
    gpc = S5_SLAB_GROUPS
    nslab = ngroups // gpc
    dt = jnp.exp(log_dt)[:, None]
    j = jnp.arange(S5_CHUNK + 1, dtype=F32)[:, None, None]
    mag = jnp.exp(j * (lam_re * dt)[None])
    ang = j * (lam_im * dt)[None]
    pw_re, pw_im = mag * jnp.cos(ang), mag * jnp.sin(ang)
    num_re, num_im = pw_re[1] - 1.0, pw_im[1]
    den = lam_re * lam_re + lam_im * lam_im
    f_re = (num_re * lam_re + num_im * lam_im) / den
    f_im = (num_im * lam_re - num_re * lam_im) / den
    bb_re = f_re[..., None] * b_re - f_im[..., None] * b_im
    bb_im = f_re[..., None] * b_im + f_im[..., None] * b_re
    cp_re = c_re[None] * pw_re[:S5_CHUNK, :, None, :] - c_im[None] * pw_im[:S5_CHUNK, :, None, :]
    cp_im = c_re[None] * pw_im[:S5_CHUNK, :, None, :] + c_im[None] * pw_re[:S5_CHUNK, :, None, :]
    taps = (jnp.einsum('jghp,gpi->jghi', cp_re, bb_re, precision=hi)
            - jnp.einsum('jghp,gpi->jghi', cp_im, bb_im, precision=hi))
    tw = taps.reshape(S5_CHUNK, nslab, gpc, S5_GROUP, S5_GROUP)
    tw = tw.transpose(1, 0, 3, 2, 4).reshape(nslab, S5_CHUNK, S5_GROUP, LANES)
    jr = (S5_CHUNK - 1) - jnp.arange(S5_CHUNK, dtype=F32)[:, None, None]
    mag_r = jnp.exp(jr * (lam_re * dt)[None])
    ang_r = jr * (lam_im * dt)[None]
    rev_re, rev_im = mag_r * jnp.cos(ang_r), mag_r * jnp.sin(ang_r)
    v_re = rev_re[..., None] * bb_re[None] - rev_im[..., None] * bb_im[None]
    v_im = rev_re[..., None] * bb_im[None] + rev_im[..., None] * bb_re[None]
    v = jnp.stack([v_re, v_im], axis=0).reshape(2, S5_CHUNK, nslab, gpc, nstate, S5_GROUP)
    vw = v.transpose(2, 1, 0, 3, 5, 4).reshape(nslab, 2 * S5_CHUNK, LANES, nstate)
    m_re = c_re[None] * pw_re[1:, :, None, :] - c_im[None] * pw_im[1:, :, None, :]
    m_im = c_re[None] * pw_im[1:, :, None, :] + c_im[None] * pw_re[1:, :, None, :]
    m = jnp.stack([m_re, -m_im], axis=0).reshape(2, S5_CHUNK, nslab, gpc, S5_GROUP, nstate)
    mw = m.transpose(2, 1, 0, 4, 3, 5).reshape(nslab, 2 * S5_CHUNK, S5_GROUP, gpc * nstate)
    half = gpc * nstate // LANES
    a_pack = jnp.concatenate([pw_re[S5_CHUNK].reshape(nslab, half, LANES),
                              pw_im[S5_CHUNK].reshape(nslab, half, LANES)], axis=1)
    return tw, vw, mw, a_pack


def _s5_layer(x, ln, lam_re, lam_im, log_dt, b_re, b_im, c_re, c_im, d_skip, w_glu, b_glu, *, nseq=4):
    tw, vw, mw, a_pack = _s5_operators(lam_re, lam_im, log_dt, b_re, b_im, c_re, c_im)
    hp = _s5_norm(x, ln)
    yp = _s5_conv(hp, tw, vw, mw, a_pack, d_skip, nseq=min(nseq, x.shape[0]))
    return _s5_glu(yp, x, w_glu, b_glu)


def _dense_ffn_kernel(x_ref, g_ref, wg_ref, wu_ref, wd_ref, o_ref):
    xf = x_ref[...]
    h = _rms(xf, g_ref[...]).astype(BF16)
    act = (_silu(_bdot(h, wg_ref[...])) * _bdot(h, wu_ref[...])).astype(BF16)
    o_ref[...] = xf + _bdot(act, wd_ref[...])


def _dense_ffn_layer(x, ln, w_gate_up, w_down, *, tm=512):
    bsz, seqlen, d = x.shape
    ntok = bsz * seqlen
    hidden = w_down.shape[0]
    tm = min(tm, ntok)
    wgu = w_gate_up.astype(BF16)
    out = pl.pallas_call(
        _dense_ffn_kernel,
        out_shape=jax.ShapeDtypeStruct((ntok, d), F32),
        grid=(ntok // tm,),
        in_specs=[pl.BlockSpec((tm, d), lambda i: (i, 0)),
                  pl.BlockSpec((1, d), lambda i: (0, 0)),
                  _resident((d, hidden), lambda i: (0, 0)),
                  _resident((d, hidden), lambda i: (0, 1)),
                  _resident((hidden, d), lambda i: (0, 0))],
        out_specs=pl.BlockSpec((tm, d), lambda i: (i, 0)),
        compiler_params=_params(("parallel",), 56),
        name="dense_ffn",
    )(x.reshape(ntok, d), ln.reshape(1, d), wgu, wgu, w_down.astype(BF16))
    return out.reshape(bsz, seqlen, d)


def _log_sigmoid(z):
    return jnp.minimum(z, 0.0) - jnp.log(1.0 + jnp.exp(-jnp.abs(z)))


def _gla_kernel(x_ref, ln_ref, wm_ref, wgl_ref, wg2_ref, bg2_ref, gn_ref, wo_ref, o_ref, st_ref,
                *, tq, dk, dv, heads):
    hdk, hdv = dk // heads, dv // heads
    chunk = GLA_CHUNK
    nt = (((1,), (1,)), ((), ()))
    tn = (((0,), (0,)), ((), ()))

    @pl.when(pl.program_id(1) == 0)
    def _():
        st_ref[...] = jnp.zeros_like(st_ref)

    xf = x_ref[0]
    h = _rms(xf, ln_ref[...]).astype(BF16)
    proj = _bdot(h, wm_ref[...])
    glow = _bdot(h, wgl_ref[...]).astype(BF16)
    la = _log_sigmoid(_bdot(glow, wg2_ref[...]) + bg2_ref[...]) * (1.0 / GLA_GATE_NORM)
    row = lax.broadcasted_iota(jnp.int32, (chunk, chunk), 0)
    col = lax.broadcasted_iota(jnp.int32, (chunk, chunk), 1)
    causal = row >= col
    tri = jnp.where(causal, 1.0, 0.0).astype(BF16)
    scale = hdk ** -0.5
    outs = []
    for c in range(tq // chunk):
        r0 = c * chunk
        la_c = la[r0:r0 + chunk, :]
        la_hi = la_c.astype(BF16)
        la_lo = (la_c - la_hi.astype(F32)).astype(BF16)
        gcum_all = _bdot(tri, la_hi) + _bdot(tri, la_lo)
        head_out = []
        for hd in range(heads):
            gcum = gcum_all[:, hd * hdk:(hd + 1) * hdk]
            g_last = gcum[chunk - 1:chunk, :]
            q_c = proj[r0:r0 + chunk, hd * hdk:(hd + 1) * hdk] * scale
            k_c = proj[r0:r0 + chunk, dk + hd * hdk:dk + (hd + 1) * hdk]
            v_c = proj[r0:r0 + chunk, 2 * dk + hd * hdv:2 * dk + (hd + 1) * hdv].astype(BF16)
            q_s = (q_c * jnp.exp(gcum)).astype(BF16)
            k_s = (k_c * jnp.exp(-gcum)).astype(BF16)
            k_end = (k_c * jnp.exp(g_last - gcum)).astype(BF16)
            scores = lax.dot_general(q_s, k_s, nt, preferred_element_type=F32)
            scores = jnp.where(causal, scores, 0.0).astype(BF16)
            state_t = st_ref[hd]
            o = _bdot(scores, v_c) + lax.dot_general(q_s, state_t.astype(BF16), nt,
                                                     preferred_element_type=F32)
            kv_t = lax.dot_general(v_c, k_end, tn, preferred_element_type=F32)
            st_ref[hd] = state_t * jnp.exp(g_last) + kv_t
            head_out.append(o * lax.rsqrt(jnp.mean(o * o, axis=-1, keepdims=True) + EPS))
        outs.append(jnp.concatenate(head_out, axis=1))
    o_all = jnp.concatenate(outs, axis=0)
    r = proj[:, 2 * dk + dv:]
    o_all = (o_all * gn_ref[...] * _silu(r)).astype(BF16)
    o_ref[0] = xf + _bdot(o_all, wo_ref[...])


def _gla_layer(x, ln, w_in, w_g2, b_g2, g_norm, w_out, *, tq=256):
    bsz, seqlen, d = x.shape
    dk = w_g2.shape[1]
    dv = w_out.shape[0]
    nmain = 2 * dk + 2 * dv
    tq = min(tq, seqlen)
    w_main = w_in[:, :nmain].astype(BF16)
    w_glow = jnp.pad(w_in[:, nmain:], ((0, 0), (0, LANES - GLA_GATE_RANK))).astype(BF16)
    w_g2p = jnp.pad(w_g2, ((0, LANES - GLA_GATE_RANK), (0, 0))).astype(BF16)
    hdk, hdv = dk // GLA_HEADS, dv // GLA_HEADS
    const = lambda b, t: (0, 0)
    return pl.pallas_call(
        functools.partial(_gla_kernel, tq=tq, dk=dk, dv=dv, heads=GLA_HEADS),
        out_shape=jax.ShapeDtypeStruct(x.shape, F32),
        grid=(bsz, seqlen // tq),
        in_specs=[pl.BlockSpec((1, tq, d), lambda b, t: (b, t, 0)),
                  pl.BlockSpec((1, d), const),
                  _resident((d, nmain), const),
                  _resident((d, LANES), const),
                  _resident((LANES, dk), const),
                  pl.BlockSpec((1, dk), const),
                  pl.BlockSpec((1, dv), const),
                  _resident((dv, d), const)],
        out_specs=pl.BlockSpec((1, tq, d), lambda b, t: (b, t, 0)),
        scratch_shapes=[pltpu.VMEM((GLA_HEADS, hdv, hdk), F32)],
        compiler_params=_params(("parallel", "arbitrary"), 48),
        name="gla",
    )(x, ln.reshape(1, d), w_main, w_glow, w_g2p, b_g2.reshape(1, dk), g_norm.reshape(1, dv),
      w_out.astype(BF16))


LOG2E = math.log2(math.e)
SWA_SLOT_UNROLL = 2


def _swa_kernel(sink_ref, x_ref, ln_ref, wqkv_ref, bqkv_ref, wo_ref, bo_ref, o_ref, k_ref, v_ref,
                bias_ref, q_ref, a_ref, *, tq, q_heads):
    group = q_heads // SWA_KV_HEADS
    blk = SWA_BLOCK
    nt = (((1,), (1,)), ((), ()))
    b = pl.program_id(0)
    t = pl.program_id(1)
    nq = group * LANES

    @pl.when((b == 0) & (t == 0))
    def _():
        qi = lax.broadcasted_iota(jnp.int32, (blk, 2 * blk), 0)
        kj = lax.broadcasted_iota(jnp.int32, (blk, 2 * blk), 1)
        dist = qi + blk - kj
        in_window = (dist >= 0) & (dist < SWA_WINDOW)
        for hq in range(q_heads):
            slope = 2.0 ** (-8.0 * (hq + 1) / q_heads)
            bias_ref[hq] = jnp.where(in_window, -(slope * LOG2E) * dist.astype(F32), MASK_VALUE)

    @pl.when(t == 0)
    def _():
        k_ref[0:blk, :] = jnp.zeros((blk, LANES), BF16)
        v_ref[0:blk, :] = jnp.zeros((blk, LANES), BF16)

    xf = x_ref[0]
    h = _rms(xf, ln_ref[...]).astype(BF16)
    qkv = _bdot(h, wqkv_ref[...]) + bqkv_ref[...]
    for j in range(group):
        q_ref[j] = qkv[:, j * LANES:(j + 1) * LANES].astype(BF16)
    k_ref[blk:blk + tq, :] = qkv[:, nq:nq + LANES].astype(BF16)
    v_ref[blk:blk + tq, :] = qkv[:, nq + LANES:nq + 2 * LANES].astype(BF16)
    kj_row = lax.broadcasted_iota(jnp.int32, (1, 2 * blk), 1)
    no_prev = jnp.where(kj_row < blk, jnp.where(t == 0, MASK_VALUE, 0.0), 0.0)
    low_half = lax.broadcasted_iota(jnp.int32, (1, LANES), 1) < SWA_HEAD_DIM
    halves = (low_half, jnp.logical_not(low_half))

    def slots(jj, carry):
        for u in range(SWA_SLOT_UNROLL):
            j = jj * SWA_SLOT_UNROLL + u
            for i in range(tq // blk):
                r0 = i * blk
                q_slot = q_ref[j, r0:r0 + blk, :]
                outs = []
                for kh in range(SWA_KV_HEADS):
                    hq = kh * group + j
                    sink = sink_ref[hq] * LOG2E
                    q_h = jnp.where(halves[kh], q_slot, jnp.zeros_like(q_slot))
                    s = lax.dot_general(q_h, k_ref[r0:r0 + 2 * blk, :], nt, preferred_element_type=F32) + bias_ref[hq]
                    if i == 0:
                        s = s + no_prev
                    m = jnp.maximum(jnp.max(s, axis=-1, keepdims=True), sink)
                    p = jnp.exp2(s - m)
                    denom = jnp.sum(p, axis=-1, keepdims=True) + jnp.exp2(sink - m)
                    outs.append(_bdot(p.astype(BF16), v_ref[r0:r0 + 2 * blk, :]) * (1.0 / denom))
                a_ref[j, r0:r0 + blk, :] = jnp.where(low_half, outs[0], outs[1]).astype(BF16)
        return carry

    lax.fori_loop(0, group // SWA_SLOT_UNROLL, slots, 0)
    k_ref[0:blk, :] = k_ref[tq:tq + blk, :]
    v_ref[0:blk, :] = v_ref[tq:tq + blk, :]
    o_all = jnp.concatenate([a_ref[j] for j in range(group)], axis=1)
    o_ref[0] = xf + _bdot(o_all, wo_ref[...]) + bo_ref[...]


def _swa_layer(x, ln, w_qkv, b_qkv, sinks, w_out, b_out, *, tq=512):
    bsz, seqlen, d = x.shape
    hd = SWA_HEAD_DIM
    q_heads = sinks.shape[0]
    group = q_heads // SWA_KV_HEADS
    nq = q_heads * hd
    tq = min(tq, seqlen)
    q_scale = hd ** -0.5 * LOG2E
    wq = (w_qkv[:, :nq] * q_scale).reshape(d, SWA_KV_HEADS, group, hd).transpose(0, 2, 1, 3).reshape(d, nq)
    bq = (b_qkv[:nq] * q_scale).reshape(SWA_KV_HEADS, group, hd).transpose(1, 0, 2).reshape(nq)
    w_all = jnp.concatenate([wq, w_qkv[:, nq:]], axis=1).astype(BF16)
    b_all = jnp.concatenate([bq, b_qkv[nq:]]).reshape(1, -1)
    wo = w_out.reshape(SWA_KV_HEADS, group, hd, d).transpose(1, 0, 2, 3).reshape(nq, d).astype(BF16)
    nall = w_all.shape[1]
    const = lambda b, t, s: (0, 0)
    return pl.pallas_call(
        functools.partial(_swa_kernel, tq=tq, q_heads=q_heads),
        out_shape=jax.ShapeDtypeStruct(x.shape, F32),
        grid_spec=pltpu.PrefetchScalarGridSpec(
            num_scalar_prefetch=1,
            grid=(bsz, seqlen // tq),
            in_specs=[pl.BlockSpec((1, tq, d), lambda b, t, s: (b, t, 0)),
                      pl.BlockSpec((1, d), const),
                      _resident((d, nall), const),
                      pl.BlockSpec((1, nall), const),
                      _resident((nq, d), const),
                      pl.BlockSpec((1, d), const)],
            out_specs=pl.BlockSpec((1, tq, d), lambda b, t, s: (b, t, 0)),
            scratch_shapes=[pltpu.VMEM((SWA_BLOCK + tq, LANES), BF16), pltpu.VMEM((SWA_BLOCK + tq, LANES), BF16),
                            pltpu.VMEM((q_heads, SWA_BLOCK, 2 * SWA_BLOCK), F32),
                            pltpu.VMEM((group, tq, LANES), BF16), pltpu.VMEM((group, tq, LANES), BF16)]),
        compiler_params=_params(("arbitrary", "arbitrary"), 48),
        name="swa",
    )(sinks, x, ln.reshape(1, d), w_all, b_all, wo, b_out.reshape(1, d))


def _router_kernel(x_ref, ln_ref, whi_ref, wlo_ref, idx_ref, gate_ref, hp_ref, pos_ref, count_ref, tri_ref):
    nt = (((1,), (1,)), ((), ()))
    h = _rms(x_ref[...], ln_ref[...])
    h_hi = h.astype(BF16)
    h_lo = (h - h_hi.astype(F32)).astype(BF16)
    w_hi, w_lo = whi_ref[...], wlo_ref[...]
    logits = (lax.dot_general(w_hi, h_hi, nt, preferred_element_type=F32)
              + lax.dot_general(w_hi, h_lo, nt, preferred_element_type=F32)
              + lax.dot_general(w_lo, h_hi, nt, preferred_element_type=F32))
    n_exp = logits.shape[0]
    eid = lax.broadcasted_iota(jnp.int32, logits.shape, 0)
    m1 = jnp.max(logits, axis=0, keepdims=True)
    i1 = jnp.min(jnp.where(logits == m1, eid, n_exp), axis=0, keepdims=True)
    rest = jnp.where(eid == i1, -jnp.inf, logits)
    m2 = jnp.max(rest, axis=0, keepdims=True)
    i2 = jnp.min(jnp.where(rest == m2, eid, n_exp), axis=0, keepdims=True)
    e2 = jnp.exp(m2 - m1)
    g1 = 1.0 / (1.0 + e2)
    idx_ref[...] = jnp.concatenate([i1, i2], axis=0)
    gate_ref[...] = jnp.concatenate([g1, e2 * g1], axis=0)
    hp_ref[...] = _pack_bf16_pairs(h)
    tm = logits.shape[1]

    @pl.when(pl.program_id(0) == 0)
    def _():
        count_ref[...] = jnp.zeros_like(count_ref)
        r = lax.broadcasted_iota(jnp.int32, (tm, tm), 0)
        c = lax.broadcasted_iota(jnp.int32, (tm, tm), 1)
        tri_ref[...] = jnp.where(r < c, 1.0, 0.0).astype(BF16)

    pick1 = jnp.where(eid == i1, 1.0, 0.0)
    pick2 = jnp.where(eid == i2, 1.0, 0.0)
    picks = pick1 + pick2
    before = _bdot(picks.astype(BF16), tri_ref[...]) + count_ref[:, 0:1]
    pos_ref[...] = jnp.concatenate([jnp.sum(pick1 * before, axis=0, keepdims=True),
                                    jnp.sum(pick2 * before, axis=0, keepdims=True)], axis=0).astype(jnp.int32)
    count_ref[...] = count_ref[...] + jnp.sum(picks, axis=1, keepdims=True)


def _router(x2, ln, w_router, *, tm=512):
    ntok, d = x2.shape
    n_exp = w_router.shape[1]
    tm = min(tm, ntok)
    wt = w_router.T
    w_hi = wt.astype(BF16)
    w_lo = (wt - w_hi.astype(F32)).astype(BF16)
    return pl.pallas_call(
        _router_kernel,
        out_shape=(jax.ShapeDtypeStruct((TOP_K, ntok), jnp.int32), jax.ShapeDtypeStruct((TOP_K, ntok), F32),
                   jax.ShapeDtypeStruct((ntok, d // 2), jnp.uint32),
                   jax.ShapeDtypeStruct((TOP_K, ntok), jnp.int32), jax.ShapeDtypeStruct((n_exp, LANES), F32)),
        grid=(ntok // tm,),
        in_specs=[pl.BlockSpec((tm, d), lambda i: (i, 0)),
                  pl.BlockSpec((1, d), lambda i: (0, 0)),
                  pl.BlockSpec((n_exp, d), lambda i: (0, 0)),
                  pl.BlockSpec((n_exp, d), lambda i: (0, 0))],
        out_specs=(pl.BlockSpec((TOP_K, tm), lambda i: (0, i)), pl.BlockSpec((TOP_K, tm), lambda i: (0, i)),
                   pl.BlockSpec((tm, d // 2), lambda i: (i, 0)),
                   pl.BlockSpec((TOP_K, tm), lambda i: (0, i)), pl.BlockSpec((n_exp, LANES), lambda i: (0, 0))),
        scratch_shapes=[pltpu.VMEM((tm, tm), BF16)],
        compiler_params=_params(("arbitrary",), 32),
        name="moe_router",
    )(x2, ln.reshape(1, d), w_hi, w_lo)


def _moe_plan(idx, pos, counts, tile):
    n_exp = counts.shape[0]
    nslots = idx.size
    counts = counts[:, 0].astype(jnp.int32)
    ends = jnp.cumsum(counts)
    offs = ends - counts
    experts = jnp.arange(n_exp, dtype=jnp.int32).reshape(n_exp, 1, 1)
    rank = pos + jnp.sum(jnp.where(idx[None] == experts, offs.reshape(n_exp, 1, 1), 0), axis=0)
    n_tiles = nslots // tile
    n_visits = n_tiles + n_exp - 1
    first_tile = offs // tile
    last_tile = (ends - 1) // tile
    nvis = jnp.where(counts > 0, last_tile - first_tile + 1, 0)
    vend = jnp.cumsum(nvis)
    vstart = vend - nvis
    total = vend[-1]
    v = jnp.arange(n_visits, dtype=jnp.int32)
    vc = jnp.minimum(v, total - 1)
    e = jnp.minimum(jnp.sum((vc[:, None] >= vend[None, :]).astype(jnp.int32), axis=1), n_exp - 1)
    sel = (e[:, None] == jnp.arange(n_exp, dtype=jnp.int32)[None, :]).astype(jnp.int32)
    pick = lambda a: jnp.sum(sel * a[None, :], axis=1)
    tile_id = pick(first_tile) + vc - pick(vstart)
    lo = jnp.maximum(pick(offs), tile_id * tile) - tile_id * tile
    hi = jnp.minimum(pick(ends), (tile_id + 1) * tile) - tile_id * tile
    valid = v < total
    lo = jnp.where(valid, lo, 0)
    hi = jnp.where(valid, hi, 0)
    prev_tile = jnp.concatenate([jnp.full((1,), -1, jnp.int32), tile_id[:-1]])
    first = (valid & (tile_id != prev_tile)).astype(jnp.int32)
    next_tile = jnp.concatenate([tile_id[1:], jnp.full((1,), -1, jnp.int32)])
    last = (valid & ((tile_id != next_tile) | (v == total - 1))).astype(jnp.int32)
    meta = jnp.stack([tile_id, e, lo, hi, first, last]).astype(jnp.int32)
    return rank.astype(jnp.int32), meta


def _pack_bf16_pairs(h):
    half = h.shape[1] // 2
    bits = lax.bitcast_convert_type(h.astype(BF16).astype(F32), jnp.uint32)
    return (bits[:, half:] & jnp.uint32(0xFFFF0000)) | (bits[:, :half] >> 16)


def _unpack_pairs_f32(u):
    lo = lax.bitcast_convert_type(u << 16, F32)
    hi = lax.bitcast_convert_type(u & jnp.uint32(0xFFFF0000), F32)
    return jnp.concatenate([lo, hi], axis=1)


def _unpack_bf16_pairs(u):
    return _unpack_pairs_f32(u).astype(BF16)


SC_CORES = 2
SC_SUBCORES = 16
SC_INDEX_WINDOW = 128


def _sc_mesh():
    return plsc.VectorSubcoreMesh(core_axis_name="c", subcore_axis_name="s")


def _sc_worker_id():
    return lax.axis_index("c") * SC_SUBCORES + lax.axis_index("s")


def _sc_scatter_rows(src, rank, nrows):
    ntok, width = src.shape
    win = SC_INDEX_WINDOW
    per = ntok // (SC_CORES * SC_SUBCORES)

    @pl.kernel(out_type=jax.ShapeDtypeStruct((nrows, width), src.dtype), mesh=_sc_mesh(),
               scratch_types=[pltpu.VMEM((1, win), jnp.int32)] * TOP_K + [pltpu.VMEM((win, width), src.dtype)],
               name="moe_dispatch_sc")
    def scatter(src_hbm, rank_hbm, o_hbm, *scratch):
        idx_vmem, buf = scratch[:TOP_K], scratch[TOP_K]
        wid = _sc_worker_id()

        @pl.loop(0, per // win)
        def _(blk):
            base = wid * per + blk * win
            for k in range(TOP_K):
                pltpu.sync_copy(rank_hbm.at[pl.ds(k, 1), pl.ds(base, win)], idx_vmem[k])
            pltpu.sync_copy(src_hbm.at[pl.ds(base, win)], buf)
            for k in range(TOP_K):
                pltpu.sync_copy(buf, o_hbm.at[idx_vmem[k].at[0]])

    return scatter(src, rank)


def _sc_gather_rows(src, idx, *, sub=32):
    n = idx.shape[0]
    width = src.shape[1]
    win = SC_INDEX_WINDOW
    per = n // (SC_CORES * SC_SUBCORES)
    nsub = win // sub

    @pl.kernel(out_type=jax.ShapeDtypeStruct((n, width), src.dtype), mesh=_sc_mesh(),
               scratch_types=[pltpu.VMEM((1, win), jnp.int32)] + [pltpu.VMEM((sub, width), src.dtype)] * 2
               + [pltpu.SemaphoreType.DMA] * 4,
               name="moe_gather_sc")
    def gather(src_hbm, idx_hbm, o_hbm, i_vmem, buf0, buf1, g0, g1, w0, w1):
        bufs, gsem, wsem = (buf0, buf1), (g0, g1), (w0, w1)
        wid = _sc_worker_id()

        @pl.loop(0, per // win)
        def _(blk):
            base = wid * per + blk * win
            pltpu.sync_copy(idx_hbm.at[:, pl.ds(base, win)], i_vmem)
            gathers = [pltpu.make_async_copy(src_hbm.at[i_vmem.at[0, pl.ds(sub * j, sub)]], bufs[j % 2], gsem[j % 2])
                       for j in range(nsub)]
            writes = [pltpu.make_async_copy(bufs[j % 2], o_hbm.at[pl.ds(base + sub * j, sub)], wsem[j % 2])
                      for j in range(nsub)]
            gathers[0].start()
            for j in range(nsub):
                if j + 1 < nsub:
                    if j >= 1:
                        writes[j - 1].wait()
                    gathers[j + 1].start()
                gathers[j].wait()
                writes[j].start()
            writes[nsub - 2].wait()
            writes[nsub - 1].wait()

    return gather(src, idx.reshape(1, n))


MXU_N = 256


def _expert_kernel(meta_ref, x_ref, wg_ref, wu_ref, wd_ref, o_ref, acc_ref, xb_ref, act_ref, wgb_ref, wub_ref,
                   wdb_ref, *, ts):
    v = pl.program_id(0)
    hc = pl.program_id(1)
    lo, hi, first, last = meta_ref[2, v], meta_ref[3, v], meta_ref[4, v], meta_ref[5, v]
    tile, d = acc_ref.shape
    nsub = tile // ts
    th = wgb_ref.shape[1]
    full = (lo == 0) & (hi == tile)

    @pl.when(hc == 0)
    def _():
        for sub in range(nsub):
            xb_ref[sub * ts:(sub + 1) * ts, :] = _unpack_bf16_pairs(x_ref[sub * ts:(sub + 1) * ts, :])

    @pl.when((first == 1) & (hc == 0))
    def _():
        acc_ref[...] = jnp.zeros_like(acc_ref)

    @pl.when(full)
    def _():
        for n in range(th // MXU_N):
            cols = slice(n * MXU_N, (n + 1) * MXU_N)
            gate = _bdot(xb_ref[...], wg_ref[0, :, cols].astype(BF16))
            up = _bdot(xb_ref[...], wu_ref[0, :, cols].astype(BF16))
            act_ref[:, cols] = (_silu(gate) * up).astype(BF16)
        for n in range(d // MXU_N):
            cols = slice(n * MXU_N, (n + 1) * MXU_N)
            acc_ref[:, cols] += _bdot(act_ref[...], wd_ref[0, :, cols].astype(BF16))

    @pl.when(jnp.logical_not(full) & (hi > lo))
    def _():
        wgb_ref[...] = wg_ref[0].astype(BF16)
        wub_ref[...] = wu_ref[0].astype(BF16)
        wdb_ref[...] = wd_ref[0].astype(BF16)
        for sub in range(nsub):
            r0 = sub * ts

            @pl.when((lo < r0 + ts) & (hi > r0))
            def _():
                xs = xb_ref[r0:r0 + ts, :]
                act = (_silu(_bdot(xs, wgb_ref[...])) * _bdot(xs, wub_ref[...])).astype(BF16)
                y = _bdot(act, wdb_ref[...])
                rows = r0 + lax.broadcasted_iota(jnp.int32, (ts, 1), 0)
                acc_ref[r0:r0 + ts, :] += jnp.where((rows >= lo) & (rows < hi), y, 0.0)

    @pl.when((last == 1) & (hc == pl.num_programs(1) - 1))
    def _():
        for sub in range(nsub):
            o_ref[sub * ts:(sub + 1) * ts, :] = _pack_bf16_pairs(acc_ref[sub * ts:(sub + 1) * ts, :])


def _experts(xg, meta, w_gate_up, w_down, *, tile, th=512, ts=512):
    nrows = xg.shape[0]
    n_exp, hidden, d = w_down.shape
    n_hc = hidden // th
    ts = min(ts, tile)
    wgu = w_gate_up
    return pl.pallas_call(
        functools.partial(_expert_kernel, ts=ts),
        out_shape=jax.ShapeDtypeStruct((nrows, d // 2), jnp.uint32),
        grid_spec=pltpu.PrefetchScalarGridSpec(
            num_scalar_prefetch=1,
            grid=(meta.shape[1], n_hc),
            in_specs=[pl.BlockSpec((tile, d // 2), lambda v, c, m: (m[0, v], 0)),
                      pl.BlockSpec((1, d, th), lambda v, c, m: (m[1, v], 0, c)),
                      pl.BlockSpec((1, d, th), lambda v, c, m: (m[1, v], 0, c + n_hc)),
                      pl.BlockSpec((1, th, d), lambda v, c, m: (m[1, v], c, 0))],
            out_specs=pl.BlockSpec((tile, d // 2), lambda v, c, m: (m[0, v], 0)),
            scratch_shapes=[pltpu.VMEM((tile, d), F32), pltpu.VMEM((tile, d), BF16), pltpu.VMEM((tile, th), BF16),
                            pltpu.VMEM((d, th), BF16), pltpu.VMEM((d, th), BF16), pltpu.VMEM((th, d), BF16)]),
        compiler_params=_params(("arbitrary", "arbitrary"), 56),
        name="moe_experts",
    )(meta, xg, wgu, wgu, w_down)


def _combine_kernel(x_ref, gate_ref, fg_ref, y0_ref, y1_ref, o_ref, *, final_norm):
    g = gate_ref[...]
    out = x_ref[...] + g[:, 0:1] * _unpack_pairs_f32(y0_ref[0]) + g[:, 1:2] * _unpack_pairs_f32(y1_ref[0])
    if final_norm:
        out = _rms(out, fg_ref[...])
    o_ref[...] = out


def _combine(x2, gates_t, yk, final_gain, *, tm=512):
    ntok, d = x2.shape
    tm = min(tm, ntok)
    final_norm = final_gain is not None
    fg = (final_gain if final_norm else jnp.ones((d,), F32)).reshape(1, d)
    return pl.pallas_call(
        functools.partial(_combine_kernel, final_norm=final_norm),
        out_shape=jax.ShapeDtypeStruct((ntok, d), F32),
        grid=(ntok // tm,),
        in_specs=[pl.BlockSpec((tm, d), lambda i: (i, 0)),
                  pl.BlockSpec((tm, TOP_K), lambda i: (i, 0)),
                  pl.BlockSpec((1, d), lambda i: (0, 0)),
                  pl.BlockSpec((1, tm, d // 2), lambda i: (0, i, 0)),
                  pl.BlockSpec((1, tm, d // 2), lambda i: (1, i, 0))],
        out_specs=pl.BlockSpec((tm, d), lambda i: (i, 0)),
        compiler_params=_params(("parallel",), 40),
        name="moe_combine",
    )(x2, gates_t, fg, yk, yk)


def _moe_layer(x, ln, w_router, w_gate_up, w_down, *, final_gain=None, tile=2048):
    bsz, seqlen, d = x.shape
    ntok = bsz * seqlen
    tile = min(tile, TOP_K * ntok)
    x2 = x.reshape(ntok, d)
    idx, gates, hp, pos, counts = _router(x2, ln, w_router)
    rank, meta = _moe_plan(idx, pos, counts, tile)
    xg = _sc_scatter_rows(hp, rank, TOP_K * ntok)
    y = _experts(xg, meta, w_gate_up, w_down, tile=tile)
    yk = _sc_gather_rows(y, rank.reshape(-1), sub=64).reshape(TOP_K, ntok, d // 2)
    out = _combine(x2, gates.T, yk, final_gain)
    return out.reshape(bsz, seqlen, d)


def kernel(x, l0_ln1, l0_s5_lam_re, l0_s5_lam_im, l0_s5_log_dt, l0_s5_b_re, l0_s5_b_im, l0_s5_c_re, l0_s5_c_im, l0_s5_d, l0_s5_w_glu, l0_s5_b_glu, l0_ln2, l0_ffn_w_gate_up, l0_ffn_w_down, l1_ln1, l1_gla_w_in, l1_gla_w_g2, l1_gla_b_g2, l1_gla_norm, l1_gla_w_out, l1_ln2, l1_moe_router, l1_moe_w_gate_up, l1_moe_w_down, l2_ln1, l2_swa_w_qkv, l2_swa_b_qkv, l2_swa_sinks, l2_swa_w_out, l2_swa_b_out, l2_ln2, l2_ffn_w_gate_up, l2_ffn_w_down, l3_ln1, l3_s5_lam_re, l3_s5_lam_im, l3_s5_log_dt, l3_s5_b_re, l3_s5_b_im, l3_s5_c_re, l3_s5_c_im, l3_s5_d, l3_s5_w_glu, l3_s5_b_glu, l3_ln2, l3_moe_router, l3_moe_w_gate_up, l3_moe_w_down, ln_f):
    x = _s5_layer(x, l0_ln1, l0_s5_lam_re, l0_s5_lam_im, l0_s5_log_dt, l0_s5_b_re, l0_s5_b_im,
                  l0_s5_c_re, l0_s5_c_im, l0_s5_d, l0_s5_w_glu, l0_s5_b_glu)
    x = _dense_ffn_layer(x, l0_ln2, l0_ffn_w_gate_up, l0_ffn_w_down)
    x = _gla_layer(x, l1_ln1, l1_gla_w_in, l1_gla_w_g2, l1_gla_b_g2, l1_gla_norm, l1_gla_w_out)
    x = _moe_layer(x, l1_ln2, l1_moe_router, l1_moe_w_gate_up, l1_moe_w_down)
    x = _swa_layer(x, l2_ln1, l2_swa_w_qkv, l2_swa_b_qkv, l2_swa_sinks, l2_swa_w_out, l2_swa_b_out)
    x = _dense_ffn_layer(x, l2_ln2, l2_ffn_w_gate_up, l2_ffn_w_down)
    x = _s5_layer(x, l3_ln1, l3_s5_lam_re, l3_s5_lam_im, l3_s5_log_dt, l3_s5_b_re, l3_s5_b_im,
                  l3_s5_c_re, l3_s5_c_im, l3_s5_d, l3_s5_w_glu, l3_s5_b_glu)
    return _moe_layer(x, l3_ln2, l3_moe_router, l3_moe_w_gate_up, l3_moe_w_down, final_gain=ln_f)
```

```python
import functools
import math

import jax
import jax.numpy as jnp
from jax import lax
from jax.experimental import pallas as pl
from jax.experimental.pallas import tpu as pltpu
from jax.experimental.pallas import tpu_sc as plsc

F32 = jnp.float32
BF16 = jnp.bfloat16
EPS = 1e-6
LANES = 128
MIB = 1 << 20

S5_GROUP = 16
S5_STATE = 64
S5_CHUNK = 16
S5_SLAB_GROUPS = LANES // S5_GROUP
S5_PITCH_PAD = 8

GLA_HEADS = 4
GLA_GATE_RANK = 16
GLA_GATE_NORM = 16.0
GLA_CHUNK = 64

SWA_HEAD_DIM = 64
SWA_KV_HEADS = 2
SWA_WINDOW = 128
SWA_BLOCK = 128
MASK_VALUE = -1e30

TOP_K = 2


def _params(semantics, vmem_mib):
    return pltpu.CompilerParams(dimension_semantics=semantics, vmem_limit_bytes=vmem_mib * MIB)


def _resident(block_shape, index_map):
    return pl.BlockSpec(block_shape, index_map, pipeline_mode=pl.Buffered(1))


def _rms(xf, gain):
    return xf * lax.rsqrt(jnp.mean(xf * xf, axis=-1, keepdims=True) + EPS) * gain


def _gelu_tanh(x):
    return 0.5 * x * (1.0 + jnp.tanh(math.sqrt(2.0 / math.pi) * (x + 0.044715 * (x * x * x))))


def _silu(x):
    return x * jax.nn.sigmoid(x)


def _bdot(a, b):
    return jnp.dot(a, b, preferred_element_type=F32)


S5_ROW_TILE = 512


def _s5_norm_kernel(x_ref, g_ref, o_ref, scr_ref, *, nloc):
    h = _rms(x_ref[0], g_ref[...])
    nslab = scr_ref.shape[0]
    for c in range(nslab):
        scr_ref[c] = h[:, c * LANES:(c + 1) * LANES]
    for s in range(S5_CHUNK):
        rows = pl.ds(s, nloc, stride=S5_CHUNK)
        o_ref[0, s] = jnp.concatenate([scr_ref[c, rows, :] for c in range(nslab)], axis=1).astype(o_ref.dtype)


def _s5_norm(x, gain):
    bsz, seqlen, d = x.shape
    nch = seqlen // S5_CHUNK
    tm = min(S5_ROW_TILE, seqlen)
    nloc = tm // S5_CHUNK
    return pl.pallas_call(
        functools.partial(_s5_norm_kernel, nloc=nloc),
        out_shape=jax.ShapeDtypeStruct((bsz, S5_CHUNK, nch, d), BF16),
        grid=(bsz, seqlen // tm),
        in_specs=[pl.BlockSpec((1, tm, d), lambda b, i: (b, i, 0)),
                  pl.BlockSpec((1, d), lambda b, i: (0, 0))],
        out_specs=pl.BlockSpec((1, S5_CHUNK, nloc, d), lambda b, i: (b, 0, i, 0)),
        scratch_shapes=[pltpu.VMEM((d // LANES, tm, LANES), F32)],
        compiler_params=_params(("parallel", "parallel"), 32),
        name="s5_norm",
    )(x, gain.reshape(1, d))


def _tiling_matrix(rows, cols):
    p = lax.broadcasted_iota(jnp.int32, (rows, cols), 0)
    c = lax.broadcasted_iota(jnp.int32, (rows, cols), 1)
    return jnp.where(c % rows == p, 1.0, 0.0).astype(BF16)


def _same_group(shape, row_group, col_group):
    r = lax.broadcasted_iota(jnp.int32, shape, 0)
    c = lax.broadcasted_iota(jnp.int32, shape, 1)
    return (r // row_group) == (c // col_group)


def _s5_build_operators(tw_ref, vw_ref, mw_ref, toep_ref, win_ref, wout_ref):
    tn = (((0,), (0,)), ((), ()))
    nstate = vw_ref.shape[-1]
    half = S5_SLAB_GROUPS * nstate
    rep_ch = _tiling_matrix(S5_GROUP, LANES)
    rep_st = _tiling_matrix(nstate, half)
    diag = _same_group((LANES, LANES), S5_GROUP, S5_GROUP)
    taps = []
    for j in range(S5_CHUNK):
        e = lax.dot_general(tw_ref[0, j].astype(BF16), rep_ch, tn, preferred_element_type=F32)
        taps.append(jnp.where(diag, e, 0.0).astype(BF16))
    zero = jnp.zeros((LANES, LANES), BF16)
    for a in range(S5_CHUNK):
        for b in range(S5_CHUNK):
            toep_ref[a * LANES:(a + 1) * LANES, b * LANES:(b + 1) * LANES] = taps[b - a] if b >= a else zero
    diag_in = _same_group((LANES, half), S5_GROUP, nstate)
    diag_out = _same_group((half, LANES), nstate, S5_GROUP)
    for a in range(S5_CHUNK):
        for r in range(2):
            e = _bdot(vw_ref[0, 2 * a + r].astype(BF16), rep_st)
            win_ref[a * LANES:(a + 1) * LANES, r * half:(r + 1) * half] = jnp.where(diag_in, e, 0.0).astype(BF16)
            e = lax.dot_general(mw_ref[0, 2 * a + r].astype(BF16), rep_ch, tn, preferred_element_type=F32)
            wout_ref[r * half:(r + 1) * half, a * LANES:(a + 1) * LANES] = jnp.where(diag_out, e, 0.0).astype(BF16)


def _s5_conv_kernel(h_ref, tw_ref, vw_ref, mw_ref, a_ref, d_ref, o_ref, s_ref, toep_ref, win_ref, wout_ref,
                    *, nseq, nch):
    pitch = nch + S5_PITCH_PAD
    nl = a_ref.shape[1] // 2

    @pl.when(pl.program_id(1) == 0)
    def _():
        _s5_build_operators(tw_ref, vw_ref, mw_ref, toep_ref, win_ref, wout_ref)

    lhs = jnp.concatenate(
        [jnp.concatenate([h_ref[bl, s] for s in range(S5_CHUNK)], axis=1) for bl in range(nseq)], axis=0)
    bc = _bdot(lhs, win_ref[...])
    for bl in range(nseq):
        for j in range(2 * nl):
            s_ref[j, bl * pitch:bl * pitch + nch, :] = bc[bl * nch:(bl + 1) * nch, j * LANES:(j + 1) * LANES]
    a_re = [a_ref[0, j:j + 1, :] for j in range(nl)]
    a_im = [a_ref[0, nl + j:nl + j + 1, :] for j in range(nl)]

    def step(n, carry):
        p_re, p_im = carry
        rows = pl.ds(n, nseq, stride=pitch)
        n_re, n_im = [], []
        for j in range(nl):
            c_re = s_ref[j, rows, :]
            c_im = s_ref[nl + j, rows, :]
            s_ref[j, rows, :] = p_re[j]
            s_ref[nl + j, rows, :] = p_im[j]
            n_re.append(a_re[j] * p_re[j] - a_im[j] * p_im[j] + c_re)
            n_im.append(a_re[j] * p_im[j] + a_im[j] * p_re[j] + c_im)
        return tuple(n_re), tuple(n_im)

    zeros = tuple(jnp.zeros((nseq, LANES), F32) for _ in range(nl))
    lax.fori_loop(0, nch, step, (zeros, zeros))
    x_prev = jnp.concatenate(
        [jnp.concatenate([s_ref[j, bl * pitch:bl * pitch + nch, :] for j in range(2 * nl)], axis=1)
         for bl in range(nseq)], axis=0).astype(BF16)
    y = _bdot(lhs, toep_ref[...]) + _bdot(x_prev, wout_ref[...])
    dskip = d_ref[0]
    for bl in range(nseq):
        for s in range(S5_CHUNK):
            ys = y[bl * nch:(bl + 1) * nch, s * LANES:(s + 1) * LANES]
            ys = ys + dskip * h_ref[bl, s].astype(F32)
            o_ref[bl, s] = _gelu_tanh(ys).astype(o_ref.dtype)


def _s5_conv(hp, tw, vw, mw, a_pack, d_skip, *, nseq):
    bsz, _, nch, d = hp.shape
    nslab = d // LANES
    kdim = S5_CHUNK * LANES
    sdim = a_pack.shape[1] * LANES
    blk4 = lambda a: pl.BlockSpec((1,) + a.shape[1:], lambda c, b: (c, 0, 0, 0))
    return pl.pallas_call(
        functools.partial(_s5_conv_kernel, nseq=nseq, nch=nch),
        out_shape=jax.ShapeDtypeStruct(hp.shape, BF16),
        grid=(nslab, bsz // nseq),
        in_specs=[pl.BlockSpec((nseq, S5_CHUNK, nch, LANES), lambda c, b: (b, 0, 0, c)),
                  blk4(tw), blk4(vw), blk4(mw),
                  pl.BlockSpec((1, sdim // LANES, LANES), lambda c, b: (c, 0, 0)),
                  pl.BlockSpec((1, 1, LANES), lambda c, b: (c, 0, 0))],
        out_specs=pl.BlockSpec((nseq, S5_CHUNK, nch, LANES), lambda c, b: (b, 0, 0, c)),
        scratch_shapes=[pltpu.VMEM((sdim // LANES, nseq * (nch + S5_PITCH_PAD), LANES), F32),
                        pltpu.VMEM((kdim, kdim), BF16),
                        pltpu.VMEM((kdim, sdim), BF16),
                        pltpu.VMEM((sdim, kdim), BF16)],
        compiler_params=_params(("arbitrary", "arbitrary"), 56),
        name="s5_conv",
    )(hp, tw, vw, mw, a_pack, d_skip.reshape(nslab, 1, LANES))


def _s5_glu_kernel(y_ref, x_ref, w_ref, b_ref, o_ref, scr_ref, *, nloc):
    nslab = scr_ref.shape[0]
    y = jnp.concatenate([y_ref[0, s] for s in range(S5_CHUNK)], axis=0)
    u = y.astype(F32) * jax.nn.sigmoid(_bdot(y, w_ref[...]) + b_ref[...])
    for s in range(S5_CHUNK):
        rows = pl.ds(s, nloc, stride=S5_CHUNK)
        for c in range(nslab):
            scr_ref[c, rows, :] = u[s * nloc:(s + 1) * nloc, c * LANES:(c + 1) * LANES]
    o_ref[0] = x_ref[0] + jnp.concatenate([scr_ref[c] for c in range(nslab)], axis=1)


def _s5_glu(yp, x, w_glu, b_glu):
    bsz, seqlen, d = x.shape
    tm = min(S5_ROW_TILE, seqlen)
    nloc = tm // S5_CHUNK
    return pl.pallas_call(
        functools.partial(_s5_glu_kernel, nloc=nloc),
        out_shape=jax.ShapeDtypeStruct(x.shape, F32),
        grid=(bsz, seqlen // tm),
        in_specs=[pl.BlockSpec((1, S5_CHUNK, nloc, d), lambda b, i: (b, 0, i, 0)),
                  pl.BlockSpec((1, tm, d), lambda b, i: (b, i, 0)),
                  _resident((d, d), lambda b, i: (0, 0)),
                  pl.BlockSpec((1, d), lambda b, i: (0, 0))],
        out_specs=pl.BlockSpec((1, tm, d), lambda b, i: (b, i, 0)),
        scratch_shapes=[pltpu.VMEM((d // LANES, tm, LANES), F32)],
        compiler_params=_params(("parallel", "parallel"), 40),
        name="s5_glu",
    )(yp, x, w_glu.astype(BF16), b_glu.reshape(1, d))


def _s5_operators(lam_re, lam_im, log_dt, b_re, b_im, c_re, c_im):
    hi = lax.Precision.HIGHEST
    ngroups, nstate = lam_re.shape
    gpc = S5_SLAB_GROUPS
    nslab = ngroups // gpc
    dt = jnp.exp(log_dt)[:, None]
    j = jnp.arange(S5_CHUNK + 1, dtype=F32)[:, None, None]
    mag = jnp.exp(j * (lam_re * dt)[None])
    ang = j * (lam_im * dt)[None]
    pw_re, pw_im = mag * jnp.cos(ang), mag * jnp.sin(ang)
    num_re, num_im = pw_re[1] - 1.0, pw_im[1]
    den = lam_re * lam_re + lam_im * lam_im
    f_re = (num_re * lam_re + num_im * lam_im) / den
    f_im = (num_im * lam_re - num_re * lam_im) / den
    bb_re = f_re[..., None] * b_re - f_im[..., None] * b_im
    bb_im = f_re[..., None] * b_im + f_im[..., None] * b_re
    cp_re = c_re[None] * pw_re[:S5_CHUNK, :, None, :] - c_im[None] * pw_im[:S5_CHUNK, :, None, :]
    cp_im = c_re[None] * pw_im[:S5_CHUNK, :, None, :] + c_im[None] * pw_re[:S5_CHUNK, :, None, :]
    taps = (jnp.einsum('jghp,gpi->jghi', cp_re, bb_re, precision=hi)
            - jnp.einsum('jghp,gpi->jghi', cp_im, bb_im, precision=hi))
    tw = taps.reshape(S5_CHUNK, nslab, gpc, S5_GROUP, S5_GROUP)
    tw = tw.transpose(1, 0, 3, 2, 4).reshape(nslab, S5_CHUNK, S5_GROUP, LANES)
    jr = (S5_CHUNK - 1) - jnp.arange(S5_CHUNK, dtype=F32)[:, None, None]
    mag_r = jnp.exp(jr * (lam_re * dt)[None])
    ang_r = jr * (lam_im * dt)[None]
    rev_re, rev_im = mag_r * jnp.cos(ang_r), mag_r * jnp.sin(ang_r)
    slabbed = lambda a: a.reshape(a.shape[0], nslab, gpc, nstate).transpose(1, 0, 2, 3)
    rv_re, rv_im = slabbed(rev_re)[:, :, :, None, :], slabbed(rev_im)[:, :, :, None, :]
    bt_re = bb_re.transpose(0, 2, 1).reshape(nslab, 1, gpc, S5_GROUP, nstate)
    bt_im = bb_im.transpose(0, 2, 1).reshape(nslab, 1, gpc, S5_GROUP, nstate)
    vw = jnp.stack([rv_re * bt_re - rv_im * bt_im, rv_re * bt_im + rv_im * bt_re], axis=2)
    vw = vw.reshape(nslab, 2 * S5_CHUNK, LANES, nstate)
    pc_re, pc_im = slabbed(pw_re[1:])[:, :, None, :, :], slabbed(pw_im[1:])[:, :, None, :, :]
    ct_re = c_re.reshape(nslab, gpc, S5_GROUP, nstate).transpose(0, 2, 1, 3)[:, None]
    ct_im = c_im.reshape(nslab, gpc, S5_GROUP, nstate).transpose(0, 2, 1, 3)[:, None]
    mw = jnp.stack([ct_re * pc_re - ct_im * pc_im, -(ct_re * pc_im + ct_im * pc_re)], axis=2)
    mw = mw.reshape(nslab, 2 * S5_CHUNK, S5_GROUP, gpc * nstate)
    half = gpc * nstate // LANES
    a_pack = jnp.concatenate([pw_re[S5_CHUNK].reshape(nslab, half, LANES),
                              pw_im[S5_CHUNK].reshape(nslab, half, LANES)], axis=1)
    return tw, vw, mw, a_pack


def _s5_layer(x, ln, operators, d_skip, w_glu, b_glu, *, nseq=4):
    tw, vw, mw, a_pack = operators
    hp = _s5_norm(x, ln)
    yp = _s5_conv(hp, tw, vw, mw, a_pack, d_skip, nseq=min(nseq, x.shape[0]))
    return _s5_glu(yp, x, w_glu, b_glu)


def _dense_ffn_kernel(x_ref, g_ref, wg_ref, wu_ref, wd_ref, o_ref):
    xf = x_ref[...]
    h = _rms(xf, g_ref[...]).astype(BF16)
    act = (_silu(_bdot(h, wg_ref[...])) * _bdot(h, wu_ref[...])).astype(BF16)
    o_ref[...] = xf + _bdot(act, wd_ref[...])


def _dense_ffn_layer(x, ln, w_gate_up, w_down, *, tm=512):
    bsz, seqlen, d = x.shape
    ntok = bsz * seqlen
    hidden = w_down.shape[0]
    tm = min(tm, ntok)
    wgu = w_gate_up.astype(BF16)
    out = pl.pallas_call(
        _dense_ffn_kernel,
        out_shape=jax.ShapeDtypeStruct((ntok, d), F32),
        grid=(ntok // tm,),
        in_specs=[pl.BlockSpec((tm, d), lambda i: (i, 0)),
                  pl.BlockSpec((1, d), lambda i: (0, 0)),
                  _resident((d, hidden), lambda i: (0, 0)),
                  _resident((d, hidden), lambda i: (0, 1)),
                  _resident((hidden, d), lambda i: (0, 0))],
        out_specs=pl.BlockSpec((tm, d), lambda i: (i, 0)),
        compiler_params=_params(("parallel",), 56),
        name="dense_ffn",
    )(x.reshape(ntok, d), ln.reshape(1, d), wgu, wgu, w_down.astype(BF16))
    return out.reshape(bsz, seqlen, d)


def _log_sigmoid(z):
    return jnp.minimum(z, 0.0) - jnp.log(1.0 + jnp.exp(-jnp.abs(z)))


def _gla_kernel(x_ref, ln_ref, wm_ref, wgl_ref, wg2_ref, bg2_ref, gn_ref, wo_ref, o_ref, st_ref,
                *, tq, dk, dv, heads):
    hdk, hdv = dk // heads, dv // heads
    chunk = GLA_CHUNK
    nt = (((1,), (1,)), ((), ()))
    tn = (((0,), (0,)), ((), ()))

    @pl.when(pl.program_id(1) == 0)
    def _():
        st_ref[...] = jnp.zeros_like(st_ref)

    xf = x_ref[0]
    h = _rms(xf, ln_ref[...]).astype(BF16)
    proj = _bdot(h, wm_ref[...])
    glow = _bdot(h, wgl_ref[...]).astype(BF16)
    la = _log_sigmoid(_bdot(glow, wg2_ref[...]) + bg2_ref[...]) * (1.0 / GLA_GATE_NORM)
    row = lax.broadcasted_iota(jnp.int32, (chunk, chunk), 0)
    col = lax.broadcasted_iota(jnp.int32, (chunk, chunk), 1)
    causal = row >= col
    tri = jnp.where(causal, 1.0, 0.0).astype(BF16)
    scale = hdk ** -0.5
    outs = []
    for c in range(tq // chunk):
        r0 = c * chunk
        la_c = la[r0:r0 + chunk, :]
        la_hi = la_c.astype(BF16)
        la_lo = (la_c - la_hi.astype(F32)).astype(BF16)
        gcum_all = _bdot(tri, la_hi) + _bdot(tri, la_lo)
        head_out = []
        for hd in range(heads):
            gcum = gcum_all[:, hd * hdk:(hd + 1) * hdk]
            g_last = gcum[chunk - 1:chunk, :]
            q_c = proj[r0:r0 + chunk, hd * hdk:(hd + 1) * hdk] * scale
            k_c = proj[r0:r0 + chunk, dk + hd * hdk:dk + (hd + 1) * hdk]
            v_c = proj[r0:r0 + chunk, 2 * dk + hd * hdv:2 * dk + (hd + 1) * hdv].astype(BF16)
            q_s = (q_c * jnp.exp(gcum)).astype(BF16)
            k_s = (k_c * jnp.exp(-gcum)).astype(BF16)
            k_end = (k_c * jnp.exp(g_last - gcum)).astype(BF16)
            scores = lax.dot_general(q_s, k_s, nt, preferred_element_type=F32)
            scores = jnp.where(causal, scores, 0.0).astype(BF16)
            state_t = st_ref[hd]
            o = _bdot(scores, v_c) + lax.dot_general(q_s, state_t.astype(BF16), nt,
                                                     preferred_element_type=F32)
            kv_t = lax.dot_general(v_c, k_end, tn, preferred_element_type=F32)
            st_ref[hd] = state_t * jnp.exp(g_last) + kv_t
            head_out.append(o * lax.rsqrt(jnp.mean(o * o, axis=-1, keepdims=True) + EPS))
        outs.append(jnp.concatenate(head_out, axis=1))
    o_all = jnp.concatenate(outs, axis=0)
    r = proj[:, 2 * dk + dv:]
    o_all = (o_all * gn_ref[...] * _silu(r)).astype(BF16)
    o_ref[0] = xf + _bdot(o_all, wo_ref[...])


def _gla_layer(x, ln, w_in, w_g2, b_g2, g_norm, w_out, *, tq=256):
    bsz, seqlen, d = x.shape
    dk = w_g2.shape[1]
    dv = w_out.shape[0]
    nmain = 2 * dk + 2 * dv
    tq = min(tq, seqlen)
    w_main = w_in[:, :nmain].astype(BF16)
    w_glow = jnp.pad(w_in[:, nmain:], ((0, 0), (0, LANES - GLA_GATE_RANK))).astype(BF16)
    w_g2p = jnp.pad(w_g2, ((0, LANES - GLA_GATE_RANK), (0, 0))).astype(BF16)
    hdk, hdv = dk // GLA_HEADS, dv // GLA_HEADS
    const = lambda b, t: (0, 0)
    return pl.pallas_call(
        functools.partial(_gla_kernel, tq=tq, dk=dk, dv=dv, heads=GLA_HEADS),
        out_shape=jax.ShapeDtypeStruct(x.shape, F32),
        grid=(bsz, seqlen // tq),
        in_specs=[pl.BlockSpec((1, tq, d), lambda b, t: (b, t, 0)),
                  pl.BlockSpec((1, d), const),
                  _resident((d, nmain), const),
                  _resident((d, LANES), const),
                  _resident((LANES, dk), const),
                  pl.BlockSpec((1, dk), const),
                  pl.BlockSpec((1, dv), const),
                  _resident((dv, d), const)],
        out_specs=pl.BlockSpec((1, tq, d), lambda b, t: (b, t, 0)),
        scratch_shapes=[pltpu.VMEM((GLA_HEADS, hdv, hdk), F32)],
        compiler_params=_params(("parallel", "arbitrary"), 48),
        name="gla",
    )(x, ln.reshape(1, d), w_main, w_glow, w_g2p, b_g2.reshape(1, dk), g_norm.reshape(1, dv),
      w_out.astype(BF16))


LOG2E = math.log2(math.e)
SWA_SLOT_UNROLL = 2


def _swa_kernel(sink_ref, x_ref, gate_ref, y0_ref, y1_ref, ln_ref, wqkv_ref, bqkv_ref, wo_ref, bo_ref, o_ref,
                k_ref, v_ref, bias_ref, q_ref, a_ref, *, tq, q_heads):
    group = q_heads // SWA_KV_HEADS
    blk = SWA_BLOCK
    nt = (((1,), (1,)), ((), ()))
    b = pl.program_id(0)
    t = pl.program_id(1)
    nq = group * LANES

    @pl.when((b == 0) & (t == 0))
    def _():
        qi = lax.broadcasted_iota(jnp.int32, (blk, 2 * blk), 0)
        kj = lax.broadcasted_iota(jnp.int32, (blk, 2 * blk), 1)
        dist = qi + blk - kj
        in_window = (dist >= 0) & (dist < SWA_WINDOW)
        for hq in range(q_heads):
            slope = 2.0 ** (-8.0 * (hq + 1) / q_heads)
            bias_ref[hq] = jnp.where(in_window, -(slope * LOG2E) * dist.astype(F32), MASK_VALUE)

    @pl.when(t == 0)
    def _():
        k_ref[0:blk, :] = jnp.zeros((blk, LANES), BF16)
        v_ref[0:blk, :] = jnp.zeros((blk, LANES), BF16)

    g = gate_ref[0]
    xf = x_ref[0] + g[:, 0:1] * _unpack_pairs_f32(y0_ref[0, 0]) + g[:, 1:2] * _unpack_pairs_f32(y1_ref[0, 0])
    h = _rms(xf, ln_ref[...]).astype(BF16)
    qkv = _bdot(h, wqkv_ref[...]) + bqkv_ref[...]
    for j in range(group):
        q_ref[j] = qkv[:, j * LANES:(j + 1) * LANES].astype(BF16)
    k_ref[blk:blk + tq, :] = qkv[:, nq:nq + LANES].astype(BF16)
    v_ref[blk:blk + tq, :] = qkv[:, nq + LANES:nq + 2 * LANES].astype(BF16)
    kj_row = lax.broadcasted_iota(jnp.int32, (1, 2 * blk), 1)
    no_prev = jnp.where(kj_row < blk, jnp.where(t == 0, MASK_VALUE, 0.0), 0.0)
    low_half = lax.broadcasted_iota(jnp.int32, (1, LANES), 1) < SWA_HEAD_DIM
    halves = (low_half, jnp.logical_not(low_half))

    def slots(jj, carry):
        for u in range(SWA_SLOT_UNROLL):
            j = jj * SWA_SLOT_UNROLL + u
            for i in range(tq // blk):
                r0 = i * blk
                q_slot = q_ref[j, r0:r0 + blk, :]
                outs = []
                for kh in range(SWA_KV_HEADS):
                    hq = kh * group + j
                    sink = sink_ref[hq] * LOG2E
                    q_h = jnp.where(halves[kh], q_slot, jnp.zeros_like(q_slot))
                    s = lax.dot_general(q_h, k_ref[r0:r0 + 2 * blk, :], nt, preferred_element_type=F32) + bias_ref[hq]
                    if i == 0:
                        s = s + no_prev
                    m = jnp.maximum(jnp.max(s, axis=-1, keepdims=True), sink)
                    p = jnp.exp2(s - m)
                    denom = jnp.sum(p, axis=-1, keepdims=True) + jnp.exp2(sink - m)
                    outs.append(_bdot(p.astype(BF16), v_ref[r0:r0 + 2 * blk, :]) * (1.0 / denom))
                a_ref[j, r0:r0 + blk, :] = jnp.where(low_half, outs[0], outs[1]).astype(BF16)
        return carry

    lax.fori_loop(0, group // SWA_SLOT_UNROLL, slots, 0)
    k_ref[0:blk, :] = k_ref[tq:tq + blk, :]
    v_ref[0:blk, :] = v_ref[tq:tq + blk, :]
    o_all = jnp.concatenate([a_ref[j] for j in range(group)], axis=1)
    o_ref[0] = xf + _bdot(o_all, wo_ref[...]) + bo_ref[...]


def _swa_layer(x, gates_t, yk, ln, w_qkv, b_qkv, sinks, w_out, b_out, *, tq=512):
    bsz, seqlen, d = x.shape
    hd = SWA_HEAD_DIM
    q_heads = sinks.shape[0]
    group = q_heads // SWA_KV_HEADS
    nq = q_heads * hd
    tq = min(tq, seqlen)
    q_scale = hd ** -0.5 * LOG2E
    wq = (w_qkv[:, :nq] * q_scale).reshape(d, SWA_KV_HEADS, group, hd).transpose(0, 2, 1, 3).reshape(d, nq)
    bq = (b_qkv[:nq] * q_scale).reshape(SWA_KV_HEADS, group, hd).transpose(1, 0, 2).reshape(nq)
    w_all = jnp.concatenate([wq, w_qkv[:, nq:]], axis=1).astype(BF16)
    b_all = jnp.concatenate([bq, b_qkv[nq:]]).reshape(1, -1)
    wo = w_out.reshape(SWA_KV_HEADS, group, hd, d).transpose(1, 0, 2, 3).reshape(nq, d).astype(BF16)
    nall = w_all.shape[1]
    const = lambda b, t, s: (0, 0)
    return pl.pallas_call(
        functools.partial(_swa_kernel, tq=tq, q_heads=q_heads),
        out_shape=jax.ShapeDtypeStruct(x.shape, F32),
        grid_spec=pltpu.PrefetchScalarGridSpec(
            num_scalar_prefetch=1,
            grid=(bsz, seqlen // tq),
            in_specs=[pl.BlockSpec((1, tq, d), lambda b, t, s: (b, t, 0)),
                      pl.BlockSpec((1, tq, TOP_K), lambda b, t, s: (b, t, 0)),
                      pl.BlockSpec((1, 1, tq, d // 2), lambda b, t, s: (0, b, t, 0)),
                      pl.BlockSpec((1, 1, tq, d // 2), lambda b, t, s: (1, b, t, 0)),
                      pl.BlockSpec((1, d), const),
                      _resident((d, nall), const),
                      pl.BlockSpec((1, nall), const),
                      _resident((nq, d), const),
                      pl.BlockSpec((1, d), const)],
            out_specs=pl.BlockSpec((1, tq, d), lambda b, t, s: (b, t, 0)),
            scratch_shapes=[pltpu.VMEM((SWA_BLOCK + tq, LANES), BF16), pltpu.VMEM((SWA_BLOCK + tq, LANES), BF16),
                            pltpu.VMEM((q_heads, SWA_BLOCK, 2 * SWA_BLOCK), F32),
                            pltpu.VMEM((group, tq, LANES), BF16), pltpu.VMEM((group, tq, LANES), BF16)]),
        compiler_params=_params(("arbitrary", "arbitrary"), 48),
        name="swa",
    )(sinks, x, gates_t.reshape(bsz, seqlen, TOP_K), yk.reshape(TOP_K, bsz, seqlen, d // 2),
      yk.reshape(TOP_K, bsz, seqlen, d // 2), ln.reshape(1, d), w_all, b_all, wo, b_out.reshape(1, d))


def _router_kernel(x_ref, ln_ref, whi_ref, wlo_ref, idx_ref, gate_ref, hp_ref, pos_ref, count_ref, tri_ref):
    nt = (((1,), (1,)), ((), ()))
    h = _rms(x_ref[...], ln_ref[...])
    h_hi = h.astype(BF16)
    h_lo = (h - h_hi.astype(F32)).astype(BF16)
    w_hi, w_lo = whi_ref[...], wlo_ref[...]
    logits = (lax.dot_general(w_hi, h_hi, nt, preferred_element_type=F32)
              + lax.dot_general(w_hi, h_lo, nt, preferred_element_type=F32)
              + lax.dot_general(w_lo, h_hi, nt, preferred_element_type=F32))
    n_exp = logits.shape[0]
    eid = lax.broadcasted_iota(jnp.int32, logits.shape, 0)
    m1 = jnp.max(logits, axis=0, keepdims=True)
    i1 = jnp.min(jnp.where(logits == m1, eid, n_exp), axis=0, keepdims=True)
    rest = jnp.where(eid == i1, -jnp.inf, logits)
    m2 = jnp.max(rest, axis=0, keepdims=True)
    i2 = jnp.min(jnp.where(rest == m2, eid, n_exp), axis=0, keepdims=True)
    e2 = jnp.exp(m2 - m1)
    g1 = 1.0 / (1.0 + e2)
    idx_ref[...] = jnp.concatenate([i1, i2], axis=0)
    gate_ref[...] = jnp.concatenate([g1, e2 * g1], axis=0)
    hp_ref[...] = _pack_bf16_pairs(h)
    tm = logits.shape[1]

    @pl.when(pl.program_id(0) == 0)
    def _():
        count_ref[...] = jnp.zeros_like(count_ref)
        r = lax.broadcasted_iota(jnp.int32, (tm, tm), 0)
        c = lax.broadcasted_iota(jnp.int32, (tm, tm), 1)
        tri_ref[...] = jnp.where(r < c, 1.0, 0.0).astype(BF16)

    pick1 = jnp.where(eid == i1, 1.0, 0.0)
    pick2 = jnp.where(eid == i2, 1.0, 0.0)
    picks = pick1 + pick2
    before = _bdot(picks.astype(BF16), tri_ref[...]) + count_ref[:, 0:1]
    pos_ref[...] = jnp.concatenate([jnp.sum(pick1 * before, axis=0, keepdims=True),
                                    jnp.sum(pick2 * before, axis=0, keepdims=True)], axis=0).astype(jnp.int32)
    count_ref[...] = count_ref[...] + jnp.sum(picks, axis=1, keepdims=True)


def _router(x2, ln, w_router, *, tm=512):
    ntok, d = x2.shape
    n_exp = w_router.shape[1]
    tm = min(tm, ntok)
    wt = w_router.T
    w_hi = wt.astype(BF16)
    w_lo = (wt - w_hi.astype(F32)).astype(BF16)
    return pl.pallas_call(
        _router_kernel,
        out_shape=(jax.ShapeDtypeStruct((TOP_K, ntok), jnp.int32), jax.ShapeDtypeStruct((TOP_K, ntok), F32),
                   jax.ShapeDtypeStruct((ntok, d // 2), jnp.uint32),
                   jax.ShapeDtypeStruct((TOP_K, ntok), jnp.int32), jax.ShapeDtypeStruct((n_exp, LANES), F32)),
        grid=(ntok // tm,),
        in_specs=[pl.BlockSpec((tm, d), lambda i: (i, 0)),
                  pl.BlockSpec((1, d), lambda i: (0, 0)),
                  pl.BlockSpec((n_exp, d), lambda i: (0, 0)),
                  pl.BlockSpec((n_exp, d), lambda i: (0, 0))],
        out_specs=(pl.BlockSpec((TOP_K, tm), lambda i: (0, i)), pl.BlockSpec((TOP_K, tm), lambda i: (0, i)),
                   pl.BlockSpec((tm, d // 2), lambda i: (i, 0)),
                   pl.BlockSpec((TOP_K, tm), lambda i: (0, i)), pl.BlockSpec((n_exp, LANES), lambda i: (0, 0))),
        scratch_shapes=[pltpu.VMEM((tm, tm), BF16)],
        compiler_params=_params(("arbitrary",), 32),
        name="moe_router",
    )(x2, ln.reshape(1, d), w_hi, w_lo)


def _moe_plan(idx, pos, counts, tile):
    n_exp = counts.shape[0]
    nslots = idx.size
    counts = counts[:, 0].astype(jnp.int32)
    ends = jnp.cumsum(counts)
    offs = ends - counts
    experts = jnp.arange(n_exp, dtype=jnp.int32).reshape(n_exp, 1, 1)
    rank = pos + jnp.sum(jnp.where(idx[None] == experts, offs.reshape(n_exp, 1, 1), 0), axis=0)
    n_tiles = nslots // tile
    n_visits = n_tiles + n_exp - 1
    first_tile = offs // tile
    last_tile = (ends - 1) // tile
    nvis = jnp.where(counts > 0, last_tile - first_tile + 1, 0)
    vend = jnp.cumsum(nvis)
    vstart = vend - nvis
    total = vend[-1]
    v = jnp.arange(n_visits, dtype=jnp.int32)
    vc = jnp.minimum(v, total - 1)
    e = jnp.minimum(jnp.sum((vc[:, None] >= vend[None, :]).astype(jnp.int32), axis=1), n_exp - 1)
    sel = (e[:, None] == jnp.arange(n_exp, dtype=jnp.int32)[None, :]).astype(jnp.int32)
    pick = lambda a: jnp.sum(sel * a[None, :], axis=1)
    tile_id = pick(first_tile) + vc - pick(vstart)
    lo = jnp.maximum(pick(offs), tile_id * tile) - tile_id * tile
    hi = jnp.minimum(pick(ends), (tile_id + 1) * tile) - tile_id * tile
    valid = v < total
    lo = jnp.where(valid, lo, 0)
    hi = jnp.where(valid, hi, 0)
    prev_tile = jnp.concatenate([jnp.full((1,), -1, jnp.int32), tile_id[:-1]])
    first = (valid & (tile_id != prev_tile)).astype(jnp.int32)
    next_tile = jnp.concatenate([tile_id[1:], jnp.full((1,), -1, jnp.int32)])
    last = (valid & ((tile_id != next_tile) | (v == total - 1))).astype(jnp.int32)
    meta = jnp.stack([tile_id, e, lo, hi, first, last]).astype(jnp.int32)
    return rank.astype(jnp.int32), meta


def _pack_bf16_pairs(h):
    half = h.shape[1] // 2
    bits = lax.bitcast_convert_type(h.astype(BF16).astype(F32), jnp.uint32)
    return (bits[:, half:] & jnp.uint32(0xFFFF0000)) | (bits[:, :half] >> 16)


def _unpack_pairs_f32(u):
    lo = lax.bitcast_convert_type(u << 16, F32)
    hi = lax.bitcast_convert_type(u & jnp.uint32(0xFFFF0000), F32)
    return jnp.concatenate([lo, hi], axis=1)


def _unpack_bf16_pairs(u):
    return _unpack_pairs_f32(u).astype(BF16)


SC_CORES = 2
SC_SUBCORES = 16
SC_INDEX_WINDOW = 128


def _sc_mesh():
    return plsc.VectorSubcoreMesh(core_axis_name="c", subcore_axis_name="s")


def _sc_worker_id():
    return lax.axis_index("c") * SC_SUBCORES + lax.axis_index("s")


def _sc_scatter_rows(src, rank, nrows):
    ntok, width = src.shape
    win = SC_INDEX_WINDOW
    per = ntok // (SC_CORES * SC_SUBCORES)

    @pl.kernel(out_type=jax.ShapeDtypeStruct((nrows, width), src.dtype), mesh=_sc_mesh(),
               scratch_types=[pltpu.VMEM((1, win), jnp.int32)] * TOP_K + [pltpu.VMEM((win, width), src.dtype)],
               name="moe_dispatch_sc")
    def scatter(src_hbm, rank_hbm, o_hbm, *scratch):
        idx_vmem, buf = scratch[:TOP_K], scratch[TOP_K]
        wid = _sc_worker_id()

        @pl.loop(0, per // win)
        def _(blk):
            base = wid * per + blk * win
            for k in range(TOP_K):
                pltpu.sync_copy(rank_hbm.at[pl.ds(k, 1), pl.ds(base, win)], idx_vmem[k])
            pltpu.sync_copy(src_hbm.at[pl.ds(base, win)], buf)
            for k in range(TOP_K):
                pltpu.sync_copy(buf, o_hbm.at[idx_vmem[k].at[0]])

    return scatter(src, rank)


def _sc_gather_rows(src, idx, *, sub=32):
    n = idx.shape[0]
    width = src.shape[1]
    win = SC_INDEX_WINDOW
    per = n // (SC_CORES * SC_SUBCORES)
    nsub = win // sub

    @pl.kernel(out_type=jax.ShapeDtypeStruct((n, width), src.dtype), mesh=_sc_mesh(),
               scratch_types=[pltpu.VMEM((1, win), jnp.int32)] + [pltpu.VMEM((sub, width), src.dtype)] * 2
               + [pltpu.SemaphoreType.DMA] * 4,
               name="moe_gather_sc")
    def gather(src_hbm, idx_hbm, o_hbm, i_vmem, buf0, buf1, g0, g1, w0, w1):
        bufs, gsem, wsem = (buf0, buf1), (g0, g1), (w0, w1)
        wid = _sc_worker_id()

        @pl.loop(0, per // win)
        def _(blk):
            base = wid * per + blk * win
            pltpu.sync_copy(idx_hbm.at[:, pl.ds(base, win)], i_vmem)
            gathers = [pltpu.make_async_copy(src_hbm.at[i_vmem.at[0, pl.ds(sub * j, sub)]], bufs[j % 2], gsem[j % 2])
                       for j in range(nsub)]
            writes = [pltpu.make_async_copy(bufs[j % 2], o_hbm.at[pl.ds(base + sub * j, sub)], wsem[j % 2])
                      for j in range(nsub)]
            gathers[0].start()
            for j in range(nsub):
                if j + 1 < nsub:
                    if j >= 1:
                        writes[j - 1].wait()
                    gathers[j + 1].start()
                gathers[j].wait()
                writes[j].start()
            writes[nsub - 2].wait()
            writes[nsub - 1].wait()

    return gather(src, idx.reshape(1, n))


MXU_N = 256


def _expert_kernel(meta_ref, x_ref, wg_ref, wu_ref, wd_ref, o_ref, acc_ref, xb_ref, act_ref, wgb_ref, wub_ref,
                   wdb_ref, *, ts):
    v = pl.program_id(0)
    hc = pl.program_id(1)
    lo, hi, first, last = meta_ref[2, v], meta_ref[3, v], meta_ref[4, v], meta_ref[5, v]
    tile, d = acc_ref.shape
    nsub = tile // ts
    th = wgb_ref.shape[1]
    full = (lo == 0) & (hi == tile)

    @pl.when(hc == 0)
    def _():
        for sub in range(nsub):
            xb_ref[sub * ts:(sub + 1) * ts, :] = _unpack_bf16_pairs(x_ref[sub * ts:(sub + 1) * ts, :])

    @pl.when((first == 1) & (hc == 0))
    def _():
        acc_ref[...] = jnp.zeros_like(acc_ref)

    @pl.when(full)
    def _():
        for n in range(th // MXU_N):
            cols = slice(n * MXU_N, (n + 1) * MXU_N)
            gate = _bdot(xb_ref[...], wg_ref[0, :, cols].astype(BF16))
            up = _bdot(xb_ref[...], wu_ref[0, :, cols].astype(BF16))
            act_ref[:, cols] = (_silu(gate) * up).astype(BF16)
        for n in range(d // MXU_N):
            cols = slice(n * MXU_N, (n + 1) * MXU_N)
            acc_ref[:, cols] += _bdot(act_ref[...], wd_ref[0, :, cols].astype(BF16))

    @pl.when(jnp.logical_not(full) & (hi > lo))
    def _():
        wgb_ref[...] = wg_ref[0].astype(BF16)
        wub_ref[...] = wu_ref[0].astype(BF16)
        wdb_ref[...] = wd_ref[0].astype(BF16)
        for sub in range(nsub):
            r0 = sub * ts

            @pl.when((lo < r0 + ts) & (hi > r0))
            def _():
                xs = xb_ref[r0:r0 + ts, :]
                act = (_silu(_bdot(xs, wgb_ref[...])) * _bdot(xs, wub_ref[...])).astype(BF16)
                y = _bdot(act, wdb_ref[...])
                rows = r0 + lax.broadcasted_iota(jnp.int32, (ts, 1), 0)
                acc_ref[r0:r0 + ts, :] += jnp.where((rows >= lo) & (rows < hi), y, 0.0)

    @pl.when((last == 1) & (hc == pl.num_programs(1) - 1))
    def _():
        for sub in range(nsub):
            o_ref[sub * ts:(sub + 1) * ts, :] = _pack_bf16_pairs(acc_ref[sub * ts:(sub + 1) * ts, :])


def _experts(xg, meta, w_gate_up, w_down, *, tile, th=512, ts=512):
    nrows = xg.shape[0]
    n_exp, hidden, d = w_down.shape
    n_hc = hidden // th
    ts = min(ts, tile)
    wgu = w_gate_up
    return pl.pallas_call(
        functools.partial(_expert_kernel, ts=ts),
        out_shape=jax.ShapeDtypeStruct((nrows, d // 2), jnp.uint32),
        grid_spec=pltpu.PrefetchScalarGridSpec(
            num_scalar_prefetch=1,
            grid=(meta.shape[1], n_hc),
            in_specs=[pl.BlockSpec((tile, d // 2), lambda v, c, m: (m[0, v], 0)),
                      pl.BlockSpec((1, d, th), lambda v, c, m: (m[1, v], 0, c)),
                      pl.BlockSpec((1, d, th), lambda v, c, m: (m[1, v], 0, c + n_hc)),
                      pl.BlockSpec((1, th, d), lambda v, c, m: (m[1, v], c, 0))],
            out_specs=pl.BlockSpec((tile, d // 2), lambda v, c, m: (m[0, v], 0)),
            scratch_shapes=[pltpu.VMEM((tile, d), F32), pltpu.VMEM((tile, d), BF16), pltpu.VMEM((tile, th), BF16),
                            pltpu.VMEM((d, th), BF16), pltpu.VMEM((d, th), BF16), pltpu.VMEM((th, d), BF16)]),
        compiler_params=_params(("arbitrary", "arbitrary"), 56),
        name="moe_experts",
    )(meta, xg, wgu, wgu, w_down)


def _combine_kernel(x_ref, gate_ref, fg_ref, y0_ref, y1_ref, o_ref, *, final_norm):
    g = gate_ref[...]
    out = x_ref[...] + g[:, 0:1] * _unpack_pairs_f32(y0_ref[0]) + g[:, 1:2] * _unpack_pairs_f32(y1_ref[0])
    if final_norm:
        out = _rms(out, fg_ref[...])
    o_ref[...] = out


def _combine(x2, gates_t, yk, final_gain, *, tm=512):
    ntok, d = x2.shape
    tm = min(tm, ntok)
    final_norm = final_gain is not None
    fg = (final_gain if final_norm else jnp.ones((d,), F32)).reshape(1, d)
    return pl.pallas_call(
        functools.partial(_combine_kernel, final_norm=final_norm),
        out_shape=jax.ShapeDtypeStruct((ntok, d), F32),
        grid=(ntok // tm,),
        in_specs=[pl.BlockSpec((tm, d), lambda i: (i, 0)),
                  pl.BlockSpec((tm, TOP_K), lambda i: (i, 0)),
                  pl.BlockSpec((1, d), lambda i: (0, 0)),
                  pl.BlockSpec((1, tm, d // 2), lambda i: (0, i, 0)),
                  pl.BlockSpec((1, tm, d // 2), lambda i: (1, i, 0))],
        out_specs=pl.BlockSpec((tm, d), lambda i: (i, 0)),
        compiler_params=_params(("parallel",), 40),
        name="moe_combine",
    )(x2, gates_t, fg, yk, yk)


def _moe_routed(x, ln, w_router, w_gate_up, w_down, *, tile=2048):
    bsz, seqlen, d = x.shape
    ntok = bsz * seqlen
    tile = min(tile, TOP_K * ntok)
    x2 = x.reshape(ntok, d)
    idx, gates, hp, pos, counts = _router(x2, ln, w_router)
    rank, meta = _moe_plan(idx, pos, counts, tile)
    xg = _sc_scatter_rows(hp, rank, TOP_K * ntok)
    y = _experts(xg, meta, w_gate_up, w_down, tile=tile)
    yk = _sc_gather_rows(y, rank.reshape(-1), sub=64).reshape(TOP_K, ntok, d // 2)
    return gates.T, yk


def _moe_layer(x, ln, w_router, w_gate_up, w_down, *, final_gain=None, tile=2048):
    bsz, seqlen, d = x.shape
    gates_t, yk = _moe_routed(x, ln, w_router, w_gate_up, w_down, tile=tile)
    out = _combine(x.reshape(bsz * seqlen, d), gates_t, yk, final_gain)
    return out.reshape(bsz, seqlen, d)


def kernel(x, l0_ln1, l0_s5_lam_re, l0_s5_lam_im, l0_s5_log_dt, l0_s5_b_re, l0_s5_b_im, l0_s5_c_re, l0_s5_c_im, l0_s5_d, l0_s5_w_glu, l0_s5_b_glu, l0_ln2, l0_ffn_w_gate_up, l0_ffn_w_down, l1_ln1, l1_gla_w_in, l1_gla_w_g2, l1_gla_b_g2, l1_gla_norm, l1_gla_w_out, l1_ln2, l1_moe_router, l1_moe_w_gate_up, l1_moe_w_down, l2_ln1, l2_swa_w_qkv, l2_swa_b_qkv, l2_swa_sinks, l2_swa_w_out, l2_swa_b_out, l2_ln2, l2_ffn_w_gate_up, l2_ffn_w_down, l3_ln1, l3_s5_lam_re, l3_s5_lam_im, l3_s5_log_dt, l3_s5_b_re, l3_s5_b_im, l3_s5_c_re, l3_s5_c_im, l3_s5_d, l3_s5_w_glu, l3_s5_b_glu, l3_ln2, l3_moe_router, l3_moe_w_gate_up, l3_moe_w_down, ln_f):
    s5_params = ((l0_s5_lam_re, l0_s5_lam_im, l0_s5_log_dt, l0_s5_b_re, l0_s5_b_im, l0_s5_c_re, l0_s5_c_im),
                 (l3_s5_lam_re, l3_s5_lam_im, l3_s5_log_dt, l3_s5_b_re, l3_s5_b_im, l3_s5_c_re, l3_s5_c_im))
    s5_ops = jax.vmap(_s5_operators)(*(jnp.stack(pair) for pair in zip(*s5_params)))
    x = _s5_layer(x, l0_ln1, tuple(a[0] for a in s5_ops), l0_s5_d, l0_s5_w_glu, l0_s5_b_glu)
    x = _dense_ffn_layer(x, l0_ln2, l0_ffn_w_gate_up, l0_ffn_w_down)
    x = _gla_layer(x, l1_ln1, l1_gla_w_in, l1_gla_w_g2, l1_gla_b_g2, l1_gla_norm, l1_gla_w_out)
    gates_t, yk = _moe_routed(x, l1_ln2, l1_moe_router, l1_moe_w_gate_up, l1_moe_w_down)
    x = _swa_layer(x, gates_t, yk, l2_ln1, l2_swa_w_qkv, l2_swa_b_qkv, l2_swa_sinks, l2_swa_w_out, l2_swa_b_out)
    x = _dense_ffn_layer(x, l2_ln2, l2_ffn_w_gate_up, l2_ffn_w_down)
    x = _s5_layer(x, l3_ln1, tuple(a[1] for a in s5_ops), l3_s5_d, l3_s5_w_glu, l3_s5_b_glu)
    return _moe_layer(x, l3_ln2, l3_moe_router, l3_moe_w_gate_up, l3_moe_w_down, final_gain=ln_f)
```

```python
import functools
import math

import jax
import jax.numpy as jnp
from jax import lax
from jax.experimental import pallas as pl
from jax.experimental.pallas import tpu as pltpu
from jax.experimental.pallas import tpu_sc as plsc

F32 = jnp.float32
BF16 = jnp.bfloat16
EPS = 1e-6
LANES = 128
MIB = 1 << 20

S5_GROUP = 16
S5_STATE = 64
S5_CHUNK = 16
S5_SLAB_GROUPS = LANES // S5_GROUP
S5_PITCH_PAD = 8

GLA_HEADS = 4
GLA_GATE_RANK = 16
GLA_GATE_NORM = 16.0
GLA_CHUNK = 64

SWA_HEAD_DIM = 64
SWA_KV_HEADS = 2
SWA_WINDOW = 128
SWA_BLOCK = 128
MASK_VALUE = -1e30

TOP_K = 2


def _params(semantics, vmem_mib):
    return pltpu.CompilerParams(dimension_semantics=semantics, vmem_limit_bytes=vmem_mib * MIB)


def _resident(block_shape, index_map):
    return pl.BlockSpec(block_shape, index_map, pipeline_mode=pl.Buffered(1))


def _rms(xf, gain):
    return xf * lax.rsqrt(jnp.mean(xf * xf, axis=-1, keepdims=True) + EPS) * gain


def _gelu_tanh(x):
    return 0.5 * x * (1.0 + jnp.tanh(math.sqrt(2.0 / math.pi) * (x + 0.044715 * (x * x * x))))


def _silu(x):
    return x * jax.nn.sigmoid(x)


def _bdot(a, b):
    return jnp.dot(a, b, preferred_element_type=F32)


S5_ROW_TILE = 512


def _s5_norm_kernel(x_ref, g_ref, o_ref, scr_ref, *, nloc):
    h = _rms(x_ref[0], g_ref[...])
    nslab = scr_ref.shape[0]
    for c in range(nslab):
        scr_ref[c] = h[:, c * LANES:(c + 1) * LANES]
    for s in range(S5_CHUNK):
        rows = pl.ds(s, nloc, stride=S5_CHUNK)
        o_ref[0, s] = jnp.concatenate([scr_ref[c, rows, :] for c in range(nslab)], axis=1).astype(o_ref.dtype)


def _s5_norm(x, gain):
    bsz, seqlen, d = x.shape
    nch = seqlen // S5_CHUNK
    tm = min(S5_ROW_TILE, seqlen)
    nloc = tm // S5_CHUNK
    return pl.pallas_call(
        functools.partial(_s5_norm_kernel, nloc=nloc),
        out_shape=jax.ShapeDtypeStruct((bsz, S5_CHUNK, nch, d), BF16),
        grid=(bsz, seqlen // tm),
        in_specs=[pl.BlockSpec((1, tm, d), lambda b, i: (b, i, 0)),
                  pl.BlockSpec((1, d), lambda b, i: (0, 0))],
        out_specs=pl.BlockSpec((1, S5_CHUNK, nloc, d), lambda b, i: (b, 0, i, 0)),
        scratch_shapes=[pltpu.VMEM((d // LANES, tm, LANES), F32)],
        compiler_params=_params(("parallel", "parallel"), 32),
        name="s5_norm",
    )(x, gain.reshape(1, d))


def _tiling_matrix(rows, cols):
    p = lax.broadcasted_iota(jnp.int32, (rows, cols), 0)
    c = lax.broadcasted_iota(jnp.int32, (rows, cols), 1)
    return jnp.where(c % rows == p, 1.0, 0.0).astype(BF16)


def _same_group(shape, row_group, col_group):
    r = lax.broadcasted_iota(jnp.int32, shape, 0)
    c = lax.broadcasted_iota(jnp.int32, shape, 1)
    return (r // row_group) == (c // col_group)


def _s5_build_operators(vw_ref, mw_ref, toep_ref, win_ref, wout_ref):
    tn = (((0,), (0,)), ((), ()))
    nstate = vw_ref.shape[-1]
    half = S5_SLAB_GROUPS * nstate
    rep_ch = _tiling_matrix(S5_GROUP, LANES)
    rep_st = _tiling_matrix(nstate, half)
    diag_in = _same_group((LANES, half), S5_GROUP, nstate)
    diag_out = _same_group((half, LANES), nstate, S5_GROUP)

    def out_block(q, r):
        e = lax.dot_general(mw_ref[0, 2 * q + r].astype(BF16), rep_ch, tn, preferred_element_type=F32)
        return jnp.where(diag_out, e, 0.0).astype(BF16)

    for a in range(S5_CHUNK):
        for r in range(2):
            e = _bdot(vw_ref[0, 2 * a + r].astype(BF16), rep_st)
            win_ref[a * LANES:(a + 1) * LANES, r * half:(r + 1) * half] = jnp.where(diag_in, e, 0.0).astype(BF16)
            wout_ref[r * half:(r + 1) * half, a * LANES:(a + 1) * LANES] = out_block(a + 1, r)
    b_bar = win_ref[(S5_CHUNK - 1) * LANES:S5_CHUNK * LANES, :]
    taps = [_bdot(b_bar, jnp.concatenate([out_block(0, 0), out_block(0, 1)], axis=0)).astype(BF16)]
    for j in range(1, S5_CHUNK):
        taps.append(_bdot(b_bar, wout_ref[:, (j - 1) * LANES:j * LANES]).astype(BF16))
    zero = jnp.zeros((LANES, LANES), BF16)
    for a in range(S5_CHUNK):
        for b in range(S5_CHUNK):
            toep_ref[a * LANES:(a + 1) * LANES, b * LANES:(b + 1) * LANES] = taps[b - a] if b >= a else zero


def _s5_conv_kernel(h_ref, vw_ref, mw_ref, a_ref, d_ref, o_ref, s_ref, toep_ref, win_ref, wout_ref,
                    *, nseq, nch):
    pitch = nch + S5_PITCH_PAD
    nl = a_ref.shape[1] // 2

    @pl.when(pl.program_id(1) == 0)
    def _():
        _s5_build_operators(vw_ref, mw_ref, toep_ref, win_ref, wout_ref)

    lhs = jnp.concatenate(
        [jnp.concatenate([h_ref[bl, s] for s in range(S5_CHUNK)], axis=1) for bl in range(nseq)], axis=0)
    bc = _bdot(lhs, win_ref[...])
    for bl in range(nseq):
        for j in range(2 * nl):
            s_ref[j, bl * pitch:bl * pitch + nch, :] = bc[bl * nch:(bl + 1) * nch, j * LANES:(j + 1) * LANES]
    a_re = [a_ref[0, j:j + 1, :] for j in range(nl)]
    a_im = [a_ref[0, nl + j:nl + j + 1, :] for j in range(nl)]

    def step(n, carry):
        p_re, p_im = carry
        rows = pl.ds(n, nseq, stride=pitch)
        n_re, n_im = [], []
        for j in range(nl):
            c_re = s_ref[j, rows, :]
            c_im = s_ref[nl + j, rows, :]
            s_ref[j, rows, :] = p_re[j]
            s_ref[nl + j, rows, :] = p_im[j]
            n_re.append(a_re[j] * p_re[j] - a_im[j] * p_im[j] + c_re)
            n_im.append(a_re[j] * p_im[j] + a_im[j] * p_re[j] + c_im)
        return tuple(n_re), tuple(n_im)

    zeros = tuple(jnp.zeros((nseq, LANES), F32) for _ in range(nl))
    lax.fori_loop(0, nch, step, (zeros, zeros))
    x_prev = jnp.concatenate(
        [jnp.concatenate([s_ref[j, bl * pitch:bl * pitch + nch, :] for j in range(2 * nl)], axis=1)
         for bl in range(nseq)], axis=0).astype(BF16)
    y = _bdot(lhs, toep_ref[...]) + _bdot(x_prev, wout_ref[...])
    dskip = d_ref[0]
    for bl in range(nseq):
        for s in range(S5_CHUNK):
            ys = y[bl * nch:(bl + 1) * nch, s * LANES:(s + 1) * LANES]
            ys = ys + dskip * h_ref[bl, s].astype(F32)
            o_ref[bl, s] = _gelu_tanh(ys).astype(o_ref.dtype)


def _s5_conv(hp, vw, mw, a_pack, d_skip, *, nseq):
    bsz, _, nch, d = hp.shape
    nslab = d // LANES
    kdim = S5_CHUNK * LANES
    sdim = a_pack.shape[1] * LANES
    blk4 = lambda a: pl.BlockSpec((1,) + a.shape[1:], lambda c, b: (c, 0, 0, 0))
    return pl.pallas_call(
        functools.partial(_s5_conv_kernel, nseq=nseq, nch=nch),
        out_shape=jax.ShapeDtypeStruct(hp.shape, BF16),
        grid=(nslab, bsz // nseq),
        in_specs=[pl.BlockSpec((nseq, S5_CHUNK, nch, LANES), lambda c, b: (b, 0, 0, c)),
                  blk4(vw), blk4(mw),
                  pl.BlockSpec((1, sdim // LANES, LANES), lambda c, b: (c, 0, 0)),
                  pl.BlockSpec((1, 1, LANES), lambda c, b: (c, 0, 0))],
        out_specs=pl.BlockSpec((nseq, S5_CHUNK, nch, LANES), lambda c, b: (b, 0, 0, c)),
        scratch_shapes=[pltpu.VMEM((sdim // LANES, nseq * (nch + S5_PITCH_PAD), LANES), F32),
                        pltpu.VMEM((kdim, kdim), BF16),
                        pltpu.VMEM((kdim, sdim), BF16),
                        pltpu.VMEM((sdim, kdim), BF16)],
        compiler_params=_params(("arbitrary", "arbitrary"), 56),
        name="s5_conv",
    )(hp, vw, mw, a_pack, d_skip.reshape(nslab, 1, LANES))


def _s5_glu_kernel(y_ref, x_ref, w_ref, b_ref, o_ref, scr_ref, *, nloc):
    nslab = scr_ref.shape[0]
    y = jnp.concatenate([y_ref[0, s] for s in range(S5_CHUNK)], axis=0)
    u = y.astype(F32) * jax.nn.sigmoid(_bdot(y, w_ref[...]) + b_ref[...])
    for s in range(S5_CHUNK):
        rows = pl.ds(s, nloc, stride=S5_CHUNK)
        for c in range(nslab):
            scr_ref[c, rows, :] = u[s * nloc:(s + 1) * nloc, c * LANES:(c + 1) * LANES]
    o_ref[0] = x_ref[0] + jnp.concatenate([scr_ref[c] for c in range(nslab)], axis=1)


def _s5_glu(yp, x, w_glu, b_glu):
    bsz, seqlen, d = x.shape
    tm = min(S5_ROW_TILE, seqlen)
    nloc = tm // S5_CHUNK
    return pl.pallas_call(
        functools.partial(_s5_glu_kernel, nloc=nloc),
        out_shape=jax.ShapeDtypeStruct(x.shape, F32),
        grid=(bsz, seqlen // tm),
        in_specs=[pl.BlockSpec((1, S5_CHUNK, nloc, d), lambda b, i: (b, 0, i, 0)),
                  pl.BlockSpec((1, tm, d), lambda b, i: (b, i, 0)),
                  _resident((d, d), lambda b, i: (0, 0)),
                  pl.BlockSpec((1, d), lambda b, i: (0, 0))],
        out_specs=pl.BlockSpec((1, tm, d), lambda b, i: (b, i, 0)),
        scratch_shapes=[pltpu.VMEM((d // LANES, tm, LANES), F32)],
        compiler_params=_params(("parallel", "parallel"), 40),
        name="s5_glu",
    )(yp, x, w_glu.astype(BF16), b_glu.reshape(1, d))


def _s5_operators(lam_re, lam_im, log_dt, b_re, b_im, c_re, c_im):
    ngroups, nstate = lam_re.shape
    gpc = S5_SLAB_GROUPS
    nslab = ngroups // gpc
    dt = jnp.exp(log_dt)[:, None]
    j = jnp.arange(S5_CHUNK + 1, dtype=F32)[:, None, None]
    mag = jnp.exp(j * (lam_re * dt)[None])
    ang = j * (lam_im * dt)[None]
    pw_re, pw_im = mag * jnp.cos(ang), mag * jnp.sin(ang)
    num_re, num_im = pw_re[1] - 1.0, pw_im[1]
    den = lam_re * lam_re + lam_im * lam_im
    f_re = (num_re * lam_re + num_im * lam_im) / den
    f_im = (num_im * lam_re - num_re * lam_im) / den
    bb_re = f_re[..., None] * b_re - f_im[..., None] * b_im
    bb_im = f_re[..., None] * b_im + f_im[..., None] * b_re
    jr = (S5_CHUNK - 1) - jnp.arange(S5_CHUNK, dtype=F32)[:, None, None]
    mag_r = jnp.exp(jr * (lam_re * dt)[None])
    ang_r = jr * (lam_im * dt)[None]
    rev_re, rev_im = mag_r * jnp.cos(ang_r), mag_r * jnp.sin(ang_r)
    slabbed = lambda a: a.reshape(a.shape[0], nslab, gpc, nstate).transpose(1, 0, 2, 3)
    rv_re, rv_im = slabbed(rev_re)[:, :, :, None, :], slabbed(rev_im)[:, :, :, None, :]
    bt_re = bb_re.transpose(0, 2, 1).reshape(nslab, 1, gpc, S5_GROUP, nstate)
    bt_im = bb_im.transpose(0, 2, 1).reshape(nslab, 1, gpc, S5_GROUP, nstate)
    vw = jnp.stack([rv_re * bt_re - rv_im * bt_im, rv_re * bt_im + rv_im * bt_re], axis=2)
    vw = vw.reshape(nslab, 2 * S5_CHUNK, LANES, nstate)
    pc_re, pc_im = slabbed(pw_re)[:, :, None, :, :], slabbed(pw_im)[:, :, None, :, :]
    ct_re = c_re.reshape(nslab, gpc, S5_GROUP, nstate).transpose(0, 2, 1, 3)[:, None]
    ct_im = c_im.reshape(nslab, gpc, S5_GROUP, nstate).transpose(0, 2, 1, 3)[:, None]
    mw = jnp.stack([ct_re * pc_re - ct_im * pc_im, -(ct_re * pc_im + ct_im * pc_re)], axis=2)
    mw = mw.reshape(nslab, 2 * (S5_CHUNK + 1), S5_GROUP, gpc * nstate)
    half = gpc * nstate // LANES
    a_pack = jnp.concatenate([pw_re[S5_CHUNK].reshape(nslab, half, LANES),
                              pw_im[S5_CHUNK].reshape(nslab, half, LANES)], axis=1)
    return vw, mw, a_pack


def _s5_layer(x, ln, operators, d_skip, w_glu, b_glu, *, nseq=4):
    vw, mw, a_pack = operators
    hp = _s5_norm(x, ln)
    yp = _s5_conv(hp, vw, mw, a_pack, d_skip, nseq=min(nseq, x.shape[0]))
    return _s5_glu(yp, x, w_glu, b_glu)


def _dense_ffn_kernel(x_ref, g_ref, wg_ref, wu_ref, wd_ref, o_ref):
    xf = x_ref[...]
    h = _rms(xf, g_ref[...]).astype(BF16)
    act = (_silu(_bdot(h, wg_ref[...])) * _bdot(h, wu_ref[...])).astype(BF16)
    o_ref[...] = xf + _bdot(act, wd_ref[...])


def _dense_ffn_layer(x, ln, w_gate_up, w_down, *, tm=512):
    bsz, seqlen, d = x.shape
    ntok = bsz * seqlen
    hidden = w_down.shape[0]
    tm = min(tm, ntok)
    wgu = w_gate_up.astype(BF16)
    out = pl.pallas_call(
        _dense_ffn_kernel,
        out_shape=jax.ShapeDtypeStruct((ntok, d), F32),
        grid=(ntok // tm,),
        in_specs=[pl.BlockSpec((tm, d), lambda i: (i, 0)),
                  pl.BlockSpec((1, d), lambda i: (0, 0)),
                  _resident((d, hidden), lambda i: (0, 0)),
                  _resident((d, hidden), lambda i: (0, 1)),
                  _resident((hidden, d), lambda i: (0, 0))],
        out_specs=pl.BlockSpec((tm, d), lambda i: (i, 0)),
        compiler_params=_params(("parallel",), 56),
        name="dense_ffn",
    )(x.reshape(ntok, d), ln.reshape(1, d), wgu, wgu, w_down.astype(BF16))
    return out.reshape(bsz, seqlen, d)


def _log_sigmoid(z):
    return jnp.minimum(z, 0.0) - jnp.log(1.0 + jnp.exp(-jnp.abs(z)))


def _gla_kernel(x_ref, ln_ref, wm_ref, wgl_ref, wg2_ref, bg2_ref, gn_ref, wo_ref, o_ref, st_ref,
                *, tq, dk, dv, heads):
    hdk, hdv = dk // heads, dv // heads
    chunk = GLA_CHUNK
    nt = (((1,), (1,)), ((), ()))
    tn = (((0,), (0,)), ((), ()))

    @pl.when(pl.program_id(1) == 0)
    def _():
        st_ref[...] = jnp.zeros_like(st_ref)

    xf = x_ref[0]
    h = _rms(xf, ln_ref[...]).astype(BF16)
    proj = _bdot(h, wm_ref[...])
    glow = _bdot(h, wgl_ref[...]).astype(BF16)
    la = _log_sigmoid(_bdot(glow, wg2_ref[...]) + bg2_ref[...]) * (1.0 / GLA_GATE_NORM)
    row = lax.broadcasted_iota(jnp.int32, (chunk, chunk), 0)
    col = lax.broadcasted_iota(jnp.int32, (chunk, chunk), 1)
    causal = row >= col
    tri = jnp.where(causal, 1.0, 0.0).astype(BF16)
    scale = hdk ** -0.5
    outs = []
    for c in range(tq // chunk):
        r0 = c * chunk
        la_c = la[r0:r0 + chunk, :]
        la_hi = la_c.astype(BF16)
        la_lo = (la_c - la_hi.astype(F32)).astype(BF16)
        gcum_all = _bdot(tri, la_hi) + _bdot(tri, la_lo)
        head_out = []
        for hd in range(heads):
            gcum = gcum_all[:, hd * hdk:(hd + 1) * hdk]
            g_last = gcum[chunk - 1:chunk, :]
            q_c = proj[r0:r0 + chunk, hd * hdk:(hd + 1) * hdk] * scale
            k_c = proj[r0:r0 + chunk, dk + hd * hdk:dk + (hd + 1) * hdk]
            v_c = proj[r0:r0 + chunk, 2 * dk + hd * hdv:2 * dk + (hd + 1) * hdv].astype(BF16)
            q_s = (q_c * jnp.exp(gcum)).astype(BF16)
            k_s = (k_c * jnp.exp(-gcum)).astype(BF16)
            k_end = (k_c * jnp.exp(g_last - gcum)).astype(BF16)
            scores = lax.dot_general(q_s, k_s, nt, preferred_element_type=F32)
            scores = jnp.where(causal, scores, 0.0).astype(BF16)
            state_t = st_ref[hd]
            o = _bdot(scores, v_c) + lax.dot_general(q_s, state_t.astype(BF16), nt,
                                                     preferred_element_type=F32)
            kv_t = lax.dot_general(v_c, k_end, tn, preferred_element_type=F32)
            st_ref[hd] = state_t * jnp.exp(g_last) + kv_t
            head_out.append(o * lax.rsqrt(jnp.mean(o * o, axis=-1, keepdims=True) + EPS))
        outs.append(jnp.concatenate(head_out, axis=1))
    o_all = jnp.concatenate(outs, axis=0)
    r = proj[:, 2 * dk + dv:]
    o_all = (o_all * gn_ref[...] * _silu(r)).astype(BF16)
    o_ref[0] = xf + _bdot(o_all, wo_ref[...])


def _gla_layer(x, ln, w_in, w_g2, b_g2, g_norm, w_out, *, tq=256):
    bsz, seqlen, d = x.shape
    dk = w_g2.shape[1]
    dv = w_out.shape[0]
    nmain = 2 * dk + 2 * dv
    tq = min(tq, seqlen)
    w_main = w_in[:, :nmain].astype(BF16)
    w_glow = jnp.pad(w_in[:, nmain:], ((0, 0), (0, LANES - GLA_GATE_RANK))).astype(BF16)
    w_g2p = jnp.pad(w_g2, ((0, LANES - GLA_GATE_RANK), (0, 0))).astype(BF16)
    hdk, hdv = dk // GLA_HEADS, dv // GLA_HEADS
    const = lambda b, t: (0, 0)
    return pl.pallas_call(
        functools.partial(_gla_kernel, tq=tq, dk=dk, dv=dv, heads=GLA_HEADS),
        out_shape=jax.ShapeDtypeStruct(x.shape, F32),
        grid=(bsz, seqlen // tq),
        in_specs=[pl.BlockSpec((1, tq, d), lambda b, t: (b, t, 0)),
                  pl.BlockSpec((1, d), const),
                  _resident((d, nmain), const),
                  _resident((d, LANES), const),
                  _resident((LANES, dk), const),
                  pl.BlockSpec((1, dk), const),
                  pl.BlockSpec((1, dv), const),
                  _resident((dv, d), const)],
        out_specs=pl.BlockSpec((1, tq, d), lambda b, t: (b, t, 0)),
        scratch_shapes=[pltpu.VMEM((GLA_HEADS, hdv, hdk), F32)],
        compiler_params=_params(("parallel", "arbitrary"), 48),
        name="gla",
    )(x, ln.reshape(1, d), w_main, w_glow, w_g2p, b_g2.reshape(1, dk), g_norm.reshape(1, dv),
      w_out.astype(BF16))


LOG2E = math.log2(math.e)
SWA_SLOT_UNROLL = 2


def _swa_kernel(sink_ref, x_ref, ln_ref, wqkv_ref, bqkv_ref, wo_ref, bo_ref, o_ref, k_ref, v_ref,
                bias_ref, q_ref, a_ref, *, tq, q_heads):
    group = q_heads // SWA_KV_HEADS
    blk = SWA_BLOCK
    nt = (((1,), (1,)), ((), ()))
    b = pl.program_id(0)
    t = pl.program_id(1)
    nq = group * LANES

    @pl.when((b == 0) & (t == 0))
    def _():
        qi = lax.broadcasted_iota(jnp.int32, (blk, 2 * blk), 0)
        kj = lax.broadcasted_iota(jnp.int32, (blk, 2 * blk), 1)
        dist = qi + blk - kj
        in_window = (dist >= 0) & (dist < SWA_WINDOW)
        for hq in range(q_heads):
            slope = 2.0 ** (-8.0 * (hq + 1) / q_heads)
            bias_ref[hq] = jnp.where(in_window, -(slope * LOG2E) * dist.astype(F32), MASK_VALUE)

    @pl.when(t == 0)
    def _():
        k_ref[0:blk, :] = jnp.zeros((blk, LANES), BF16)
        v_ref[0:blk, :] = jnp.zeros((blk, LANES), BF16)

    xf = x_ref[0]
    h = _rms(xf, ln_ref[...]).astype(BF16)
    qkv = _bdot(h, wqkv_ref[...]) + bqkv_ref[...]
    for j in range(group):
        q_ref[j] = qkv[:, j * LANES:(j + 1) * LANES].astype(BF16)
    k_ref[blk:blk + tq, :] = qkv[:, nq:nq + LANES].astype(BF16)
    v_ref[blk:blk + tq, :] = qkv[:, nq + LANES:nq + 2 * LANES].astype(BF16)
    kj_row = lax.broadcasted_iota(jnp.int32, (1, 2 * blk), 1)
    no_prev = jnp.where(kj_row < blk, jnp.where(t == 0, MASK_VALUE, 0.0), 0.0)
    low_half = lax.broadcasted_iota(jnp.int32, (1, LANES), 1) < SWA_HEAD_DIM
    halves = (low_half, jnp.logical_not(low_half))

    def slots(jj, carry):
        for u in range(SWA_SLOT_UNROLL):
            j = jj * SWA_SLOT_UNROLL + u
            for i in range(tq // blk):
                r0 = i * blk
                q_slot = q_ref[j, r0:r0 + blk, :]
                outs = []
                for kh in range(SWA_KV_HEADS):
                    hq = kh * group + j
                    sink = sink_ref[hq] * LOG2E
                    q_h = jnp.where(halves[kh], q_slot, jnp.zeros_like(q_slot))
                    s = lax.dot_general(q_h, k_ref[r0:r0 + 2 * blk, :], nt, preferred_element_type=F32) + bias_ref[hq]
                    if i == 0:
                        s = s + no_prev
                    m = jnp.maximum(jnp.max(s, axis=-1, keepdims=True), sink)
                    p = jnp.exp2(s - m)
                    denom = jnp.sum(p, axis=-1, keepdims=True) + jnp.exp2(sink - m)
                    outs.append(_bdot(p.astype(BF16), v_ref[r0:r0 + 2 * blk, :]) * (1.0 / denom))
                a_ref[j, r0:r0 + blk, :] = jnp.where(low_half, outs[0], outs[1]).astype(BF16)
        return carry

    lax.fori_loop(0, group // SWA_SLOT_UNROLL, slots, 0)
    k_ref[0:blk, :] = k_ref[tq:tq + blk, :]
    v_ref[0:blk, :] = v_ref[tq:tq + blk, :]
    o_all = jnp.concatenate([a_ref[j] for j in range(group)], axis=1)
    o_ref[0] = xf + _bdot(o_all, wo_ref[...]) + bo_ref[...]


def _swa_layer(x, ln, w_qkv, b_qkv, sinks, w_out, b_out, *, tq=512):
    bsz, seqlen, d = x.shape
    hd = SWA_HEAD_DIM
    q_heads = sinks.shape[0]
    group = q_heads // SWA_KV_HEADS
    nq = q_heads * hd
    tq = min(tq, seqlen)
    q_scale = hd ** -0.5 * LOG2E
    wq = (w_qkv[:, :nq] * q_scale).reshape(d, SWA_KV_HEADS, group, hd).transpose(0, 2, 1, 3).reshape(d, nq)
    bq = (b_qkv[:nq] * q_scale).reshape(SWA_KV_HEADS, group, hd).transpose(1, 0, 2).reshape(nq)
    w_all = jnp.concatenate([wq, w_qkv[:, nq:]], axis=1).astype(BF16)
    b_all = jnp.concatenate([bq, b_qkv[nq:]]).reshape(1, -1)
    wo = w_out.reshape(SWA_KV_HEADS, group, hd, d).transpose(1, 0, 2, 3).reshape(nq, d).astype(BF16)
    nall = w_all.shape[1]
    const = lambda b, t, s: (0, 0)
    return pl.pallas_call(
        functools.partial(_swa_kernel, tq=tq, q_heads=q_heads),
        out_shape=jax.ShapeDtypeStruct(x.shape, F32),
        grid_spec=pltpu.PrefetchScalarGridSpec(
            num_scalar_prefetch=1,
            grid=(bsz, seqlen // tq),
            in_specs=[pl.BlockSpec((1, tq, d), lambda b, t, s: (b, t, 0)),
                      pl.BlockSpec((1, d), const),
                      _resident((d, nall), const),
                      pl.BlockSpec((1, nall), const),
                      _resident((nq, d), const),
                      pl.BlockSpec((1, d), const)],
            out_specs=pl.BlockSpec((1, tq, d), lambda b, t, s: (b, t, 0)),
            scratch_shapes=[pltpu.VMEM((SWA_BLOCK + tq, LANES), BF16), pltpu.VMEM((SWA_BLOCK + tq, LANES), BF16),
                            pltpu.VMEM((q_heads, SWA_BLOCK, 2 * SWA_BLOCK), F32),
                            pltpu.VMEM((group, tq, LANES), BF16), pltpu.VMEM((group, tq, LANES), BF16)]),
        compiler_params=_params(("arbitrary", "arbitrary"), 48),
        name="swa",
    )(sinks, x, ln.reshape(1, d), w_all, b_all, wo, b_out.reshape(1, d))


def _router_kernel(x_ref, ln_ref, whi_ref, wlo_ref, idx_ref, gate_ref, hp_ref, pos_ref, count_ref, tri_ref):
    nt = (((1,), (1,)), ((), ()))
    h = _rms(x_ref[...], ln_ref[...])
    h_hi = h.astype(BF16)
    h_lo = (h - h_hi.astype(F32)).astype(BF16)
    w_hi, w_lo = whi_ref[...], wlo_ref[...]
    logits = (lax.dot_general(w_hi, h_hi, nt, preferred_element_type=F32)
              + lax.dot_general(w_hi, h_lo, nt, preferred_element_type=F32)
              + lax.dot_general(w_lo, h_hi, nt, preferred_element_type=F32))
    n_exp = logits.shape[0]
    eid = lax.broadcasted_iota(jnp.int32, logits.shape, 0)
    m1 = jnp.max(logits, axis=0, keepdims=True)
    i1 = jnp.min(jnp.where(logits == m1, eid, n_exp), axis=0, keepdims=True)
    rest = jnp.where(eid == i1, -jnp.inf, logits)
    m2 = jnp.max(rest, axis=0, keepdims=True)
    i2 = jnp.min(jnp.where(rest == m2, eid, n_exp), axis=0, keepdims=True)
    e2 = jnp.exp(m2 - m1)
    g1 = 1.0 / (1.0 + e2)
    idx_ref[...] = jnp.concatenate([i1, i2], axis=0)
    gate_ref[...] = jnp.concatenate([g1, e2 * g1], axis=0)
    hp_ref[...] = _pack_bf16_pairs(h)
    tm = logits.shape[1]

    @pl.when(pl.program_id(0) == 0)
    def _():
        count_ref[...] = jnp.zeros_like(count_ref)
        r = lax.broadcasted_iota(jnp.int32, (tm, tm), 0)
        c = lax.broadcasted_iota(jnp.int32, (tm, tm), 1)
        tri_ref[...] = jnp.where(r < c, 1.0, 0.0).astype(BF16)

    pick1 = jnp.where(eid == i1, 1.0, 0.0)
    pick2 = jnp.where(eid == i2, 1.0, 0.0)
    picks = pick1 + pick2
    before = _bdot(picks.astype(BF16), tri_ref[...]) + count_ref[:, 0:1]
    pos_ref[...] = jnp.concatenate([jnp.sum(pick1 * before, axis=0, keepdims=True),
                                    jnp.sum(pick2 * before, axis=0, keepdims=True)], axis=0).astype(jnp.int32)
    count_ref[...] = count_ref[...] + jnp.sum(picks, axis=1, keepdims=True)


def _router(x2, ln, w_router, *, tm=512):
    ntok, d = x2.shape
    n_exp = w_router.shape[1]
    tm = min(tm, ntok)
    wt = w_router.T
    w_hi = wt.astype(BF16)
    w_lo = (wt - w_hi.astype(F32)).astype(BF16)
    return pl.pallas_call(
        _router_kernel,
        out_shape=(jax.ShapeDtypeStruct((TOP_K, ntok), jnp.int32), jax.ShapeDtypeStruct((TOP_K, ntok), F32),
                   jax.ShapeDtypeStruct((ntok, d // 2), jnp.uint32),
                   jax.ShapeDtypeStruct((TOP_K, ntok), jnp.int32), jax.ShapeDtypeStruct((n_exp, LANES), F32)),
        grid=(ntok // tm,),
        in_specs=[pl.BlockSpec((tm, d), lambda i: (i, 0)),
                  pl.BlockSpec((1, d), lambda i: (0, 0)),
                  pl.BlockSpec((n_exp, d), lambda i: (0, 0)),
                  pl.BlockSpec((n_exp, d), lambda i: (0, 0))],
        out_specs=(pl.BlockSpec((TOP_K, tm), lambda i: (0, i)), pl.BlockSpec((TOP_K, tm), lambda i: (0, i)),
                   pl.BlockSpec((tm, d // 2), lambda i: (i, 0)),
                   pl.BlockSpec((TOP_K, tm), lambda i: (0, i)), pl.BlockSpec((n_exp, LANES), lambda i: (0, 0))),
        scratch_shapes=[pltpu.VMEM((tm, tm), BF16)],
        compiler_params=_params(("arbitrary",), 32),
        name="moe_router",
    )(x2, ln.reshape(1, d), w_hi, w_lo)


def _moe_plan(idx, pos, counts, tile):
    n_exp = counts.shape[0]
    nslots = idx.size
    counts = counts[:, 0].astype(jnp.int32)
    ends = jnp.cumsum(counts)
    offs = ends - counts
    experts = jnp.arange(n_exp, dtype=jnp.int32).reshape(n_exp, 1, 1)
    rank = pos + jnp.sum(jnp.where(idx[None] == experts, offs.reshape(n_exp, 1, 1), 0), axis=0)
    n_tiles = nslots // tile
    n_visits = n_tiles + n_exp - 1
    first_tile = offs // tile
    last_tile = (ends - 1) // tile
    nvis = jnp.where(counts > 0, last_tile - first_tile + 1, 0)
    vend = jnp.cumsum(nvis)
    vstart = vend - nvis
    total = vend[-1]
    v = jnp.arange(n_visits, dtype=jnp.int32)
    vc = jnp.minimum(v, total - 1)
    e = jnp.minimum(jnp.sum((vc[:, None] >= vend[None, :]).astype(jnp.int32), axis=1), n_exp - 1)
    sel = (e[:, None] == jnp.arange(n_exp, dtype=jnp.int32)[None, :]).astype(jnp.int32)
    pick = lambda a: jnp.sum(sel * a[None, :], axis=1)
    tile_id = pick(first_tile) + vc - pick(vstart)
    lo = jnp.maximum(pick(offs), tile_id * tile) - tile_id * tile
    hi = jnp.minimum(pick(ends), (tile_id + 1) * tile) - tile_id * tile
    valid = v < total
    lo = jnp.where(valid, lo, 0)
    hi = jnp.where(valid, hi, 0)
    prev_tile = jnp.concatenate([jnp.full((1,), -1, jnp.int32), tile_id[:-1]])
    first = (valid & (tile_id != prev_tile)).astype(jnp.int32)
    next_tile = jnp.concatenate([tile_id[1:], jnp.full((1,), -1, jnp.int32)])
    last = (valid & ((tile_id != next_tile) | (v == total - 1))).astype(jnp.int32)
    meta = jnp.stack([tile_id, e, lo, hi, first, last]).astype(jnp.int32)
    return rank.astype(jnp.int32), meta


def _pack_bf16_pairs(h):
    half = h.shape[1] // 2
    bits = lax.bitcast_convert_type(h.astype(BF16).astype(F32), jnp.uint32)
    return (bits[:, half:] & jnp.uint32(0xFFFF0000)) | (bits[:, :half] >> 16)


def _unpack_pairs_f32(u):
    lo = lax.bitcast_convert_type(u << 16, F32)
    hi = lax.bitcast_convert_type(u & jnp.uint32(0xFFFF0000), F32)
    return jnp.concatenate([lo, hi], axis=1)


def _unpack_bf16_pairs(u):
    return _unpack_pairs_f32(u).astype(BF16)


SC_CORES = 2
SC_SUBCORES = 16
SC_INDEX_WINDOW = 128


def _sc_mesh():
    return plsc.VectorSubcoreMesh(core_axis_name="c", subcore_axis_name="s")


def _sc_worker_id():
    return lax.axis_index("c") * SC_SUBCORES + lax.axis_index("s")


def _sc_scatter_rows(src, rank, nrows):
    ntok, width = src.shape
    win = SC_INDEX_WINDOW
    per = ntok // (SC_CORES * SC_SUBCORES)

    @pl.kernel(out_type=jax.ShapeDtypeStruct((nrows, width), src.dtype), mesh=_sc_mesh(),
               scratch_types=[pltpu.VMEM((1, win), jnp.int32)] * TOP_K + [pltpu.VMEM((win, width), src.dtype)],
               name="moe_dispatch_sc")
    def scatter(src_hbm, rank_hbm, o_hbm, *scratch):
        idx_vmem, buf = scratch[:TOP_K], scratch[TOP_K]
        wid = _sc_worker_id()

        @pl.loop(0, per // win)
        def _(blk):
            base = wid * per + blk * win
            for k in range(TOP_K):
                pltpu.sync_copy(rank_hbm.at[pl.ds(k, 1), pl.ds(base, win)], idx_vmem[k])
            pltpu.sync_copy(src_hbm.at[pl.ds(base, win)], buf)
            for k in range(TOP_K):
                pltpu.sync_copy(buf, o_hbm.at[idx_vmem[k].at[0]])

    return scatter(src, rank)


def _sc_gather_rows(src, idx, *, sub=32):
    n = idx.shape[0]
    width = src.shape[1]
    win = SC_INDEX_WINDOW
    per = n // (SC_CORES * SC_SUBCORES)
    nsub = win // sub

    @pl.kernel(out_type=jax.ShapeDtypeStruct((n, width), src.dtype), mesh=_sc_mesh(),
               scratch_types=[pltpu.VMEM((1, win), jnp.int32)] + [pltpu.VMEM((sub, width), src.dtype)] * 2
               + [pltpu.SemaphoreType.DMA] * 4,
               name="moe_gather_sc")
    def gather(src_hbm, idx_hbm, o_hbm, i_vmem, buf0, buf1, g0, g1, w0, w1):
        bufs, gsem, wsem = (buf0, buf1), (g0, g1), (w0, w1)
        wid = _sc_worker_id()

        @pl.loop(0, per // win)
        def _(blk):
            base = wid * per + blk * win
            pltpu.sync_copy(idx_hbm.at[:, pl.ds(base, win)], i_vmem)
            gathers = [pltpu.make_async_copy(src_hbm.at[i_vmem.at[0, pl.ds(sub * j, sub)]], bufs[j % 2], gsem[j % 2])
                       for j in range(nsub)]
            writes = [pltpu.make_async_copy(bufs[j % 2], o_hbm.at[pl.ds(base + sub * j, sub)], wsem[j % 2])
                      for j in range(nsub)]
            gathers[0].start()
            for j in range(nsub):
                if j + 1 < nsub:
                    if j >= 1:
                        writes[j - 1].wait()
                    gathers[j + 1].start()
                gathers[j].wait()
                writes[j].start()
            writes[nsub - 2].wait()
            writes[nsub - 1].wait()

    return gather(src, idx.reshape(1, n))


MXU_N = 256


def _expert_kernel(meta_ref, x_ref, wg_ref, wu_ref, wd_ref, o_ref, acc_ref, xb_ref, act_ref, wgb_ref, wub_ref,
                   wdb_ref, *, ts):
    v = pl.program_id(0)
    hc = pl.program_id(1)
    lo, hi, first, last = meta_ref[2, v], meta_ref[3, v], meta_ref[4, v], meta_ref[5, v]
    tile, d = acc_ref.shape
    nsub = tile // ts
    th = wgb_ref.shape[1]
    full = (lo == 0) & (hi == tile)

    @pl.when(hc == 0)
    def _():
        for sub in range(nsub):
            xb_ref[sub * ts:(sub + 1) * ts, :] = _unpack_bf16_pairs(x_ref[sub * ts:(sub + 1) * ts, :])

    @pl.when((first == 1) & (hc == 0))
    def _():
        acc_ref[...] = jnp.zeros_like(acc_ref)

    @pl.when(full)
    def _():
        for n in range(th // MXU_N):
            cols = slice(n * MXU_N, (n + 1) * MXU_N)
            gate = _bdot(xb_ref[...], wg_ref[0, :, cols].astype(BF16))
            up = _bdot(xb_ref[...], wu_ref[0, :, cols].astype(BF16))
            act_ref[:, cols] = (_silu(gate) * up).astype(BF16)
        for n in range(d // MXU_N):
            cols = slice(n * MXU_N, (n + 1) * MXU_N)
            acc_ref[:, cols] += _bdot(act_ref[...], wd_ref[0, :, cols].astype(BF16))

    @pl.when(jnp.logical_not(full) & (hi > lo))
    def _():
        wgb_ref[...] = wg_ref[0].astype(BF16)
        wub_ref[...] = wu_ref[0].astype(BF16)
        wdb_ref[...] = wd_ref[0].astype(BF16)
        for sub in range(nsub):
            r0 = sub * ts

            @pl.when((lo < r0 + ts) & (hi > r0))
            def _():
                xs = xb_ref[r0:r0 + ts, :]
                act = (_silu(_bdot(xs, wgb_ref[...])) * _bdot(xs, wub_ref[...])).astype(BF16)
                y = _bdot(act, wdb_ref[...])
                rows = r0 + lax.broadcasted_iota(jnp.int32, (ts, 1), 0)
                acc_ref[r0:r0 + ts, :] += jnp.where((rows >= lo) & (rows < hi), y, 0.0)

    @pl.when((last == 1) & (hc == pl.num_programs(1) - 1))
    def _():
        for sub in range(nsub):
            o_ref[sub * ts:(sub + 1) * ts, :] = _pack_bf16_pairs(acc_ref[sub * ts:(sub + 1) * ts, :])


def _experts(xg, meta, w_gate_up, w_down, *, tile, th=512, ts=512):
    nrows = xg.shape[0]
    n_exp, hidden, d = w_down.shape
    n_hc = hidden // th
    ts = min(ts, tile)
    wgu = w_gate_up
    return pl.pallas_call(
        functools.partial(_expert_kernel, ts=ts),
        out_shape=jax.ShapeDtypeStruct((nrows, d // 2), jnp.uint32),
        grid_spec=pltpu.PrefetchScalarGridSpec(
            num_scalar_prefetch=1,
            grid=(meta.shape[1], n_hc),
            in_specs=[pl.BlockSpec((tile, d // 2), lambda v, c, m: (m[0, v], 0)),
                      pl.BlockSpec((1, d, th), lambda v, c, m: (m[1, v], 0, c)),
                      pl.BlockSpec((1, d, th), lambda v, c, m: (m[1, v], 0, c + n_hc)),
                      pl.BlockSpec((1, th, d), lambda v, c, m: (m[1, v], c, 0))],
            out_specs=pl.BlockSpec((tile, d // 2), lambda v, c, m: (m[0, v], 0)),
            scratch_shapes=[pltpu.VMEM((tile, d), F32), pltpu.VMEM((tile, d), BF16), pltpu.VMEM((tile, th), BF16),
                            pltpu.VMEM((d, th), BF16), pltpu.VMEM((d, th), BF16), pltpu.VMEM((th, d), BF16)]),
        compiler_params=_params(("arbitrary", "arbitrary"), 56),
        name="moe_experts",
    )(meta, xg, wgu, wgu, w_down)


def _combine_kernel(x_ref, gate_ref, fg_ref, y0_ref, y1_ref, o_ref, *, final_norm):
    g = gate_ref[...]
    out = x_ref[...] + g[:, 0:1] * _unpack_pairs_f32(y0_ref[0]) + g[:, 1:2] * _unpack_pairs_f32(y1_ref[0])
    if final_norm:
        out = _rms(out, fg_ref[...])
    o_ref[...] = out


def _combine(x2, gates_t, yk, final_gain, *, tm=512):
    ntok, d = x2.shape
    tm = min(tm, ntok)
    final_norm = final_gain is not None
    fg = (final_gain if final_norm else jnp.ones((d,), F32)).reshape(1, d)
    return pl.pallas_call(
        functools.partial(_combine_kernel, final_norm=final_norm),
        out_shape=jax.ShapeDtypeStruct((ntok, d), F32),
        grid=(ntok // tm,),
        in_specs=[pl.BlockSpec((tm, d), lambda i: (i, 0)),
                  pl.BlockSpec((tm, TOP_K), lambda i: (i, 0)),
                  pl.BlockSpec((1, d), lambda i: (0, 0)),
                  pl.BlockSpec((1, tm, d // 2), lambda i: (0, i, 0)),
                  pl.BlockSpec((1, tm, d // 2), lambda i: (1, i, 0))],
        out_specs=pl.BlockSpec((tm, d), lambda i: (i, 0)),
        compiler_params=_params(("parallel",), 40),
        name="moe_combine",
    )(x2, gates_t, fg, yk, yk)


def _moe_routed(x, ln, w_router, w_gate_up, w_down, *, tile=2048):
    bsz, seqlen, d = x.shape
    ntok = bsz * seqlen
    tile = min(tile, TOP_K * ntok)
    x2 = x.reshape(ntok, d)
    idx, gates, hp, pos, counts = _router(x2, ln, w_router)
    rank, meta = _moe_plan(idx, pos, counts, tile)
    xg = _sc_scatter_rows(hp, rank, TOP_K * ntok)
    y = _experts(xg, meta, w_gate_up, w_down, tile=tile)
    yk = _sc_gather_rows(y, rank.reshape(-1), sub=64).reshape(TOP_K, ntok, d // 2)
    return gates.T, yk


def _moe_layer(x, ln, w_router, w_gate_up, w_down, *, final_gain=None, tile=2048):
    bsz, seqlen, d = x.shape
    gates_t, yk = _moe_routed(x, ln, w_router, w_gate_up, w_down, tile=tile)
    out = _combine(x.reshape(bsz * seqlen, d), gates_t, yk, final_gain)
    return out.reshape(bsz, seqlen, d)


def kernel(x, l0_ln1, l0_s5_lam_re, l0_s5_lam_im, l0_s5_log_dt, l0_s5_b_re, l0_s5_b_im, l0_s5_c_re, l0_s5_c_im, l0_s5_d, l0_s5_w_glu, l0_s5_b_glu, l0_ln2, l0_ffn_w_gate_up, l0_ffn_w_down, l1_ln1, l1_gla_w_in, l1_gla_w_g2, l1_gla_b_g2, l1_gla_norm, l1_gla_w_out, l1_ln2, l1_moe_router, l1_moe_w_gate_up, l1_moe_w_down, l2_ln1, l2_swa_w_qkv, l2_swa_b_qkv, l2_swa_sinks, l2_swa_w_out, l2_swa_b_out, l2_ln2, l2_ffn_w_gate_up, l2_ffn_w_down, l3_ln1, l3_s5_lam_re, l3_s5_lam_im, l3_s5_log_dt, l3_s5_b_re, l3_s5_b_im, l3_s5_c_re, l3_s5_c_im, l3_s5_d, l3_s5_w_glu, l3_s5_b_glu, l3_ln2, l3_moe_router, l3_moe_w_gate_up, l3_moe_w_down, ln_f):
    s5_params = ((l0_s5_lam_re, l0_s5_lam_im, l0_s5_log_dt, l0_s5_b_re, l0_s5_b_im, l0_s5_c_re, l0_s5_c_im),
                 (l3_s5_lam_re, l3_s5_lam_im, l3_s5_log_dt, l3_s5_b_re, l3_s5_b_im, l3_s5_c_re, l3_s5_c_im))
    s5_ops = jax.vmap(_s5_operators)(*(jnp.stack(pair) for pair in zip(*s5_params)))
    x = _s5_layer(x, l0_ln1, tuple(a[0] for a in s5_ops), l0_s5_d, l0_s5_w_glu, l0_s5_b_glu)
    x = _dense_ffn_layer(x, l0_ln2, l0_ffn_w_gate_up, l0_ffn_w_down)
    x = _gla_layer(x, l1_ln1, l1_gla_w_in, l1_gla_w_g2, l1_gla_b_g2, l1_gla_norm, l1_gla_w_out)
    x = _moe_layer(x, l1_ln2, l1_moe_router, l1_moe_w_gate_up, l1_moe_w_down)
    x = _swa_layer(x, l2_ln1, l2_swa_w_qkv, l2_swa_b_qkv, l2_swa_sinks, l2_swa_w_out, l2_swa_b_out)
    x = _dense_ffn_layer(x, l2_ln2, l2_ffn_w_gate_up, l2_ffn_w_down)
    x = _s5_layer(x, l3_ln1, tuple(a[1] for a in s5_ops), l3_s5_d, l3_s5_w_glu, l3_s5_b_glu)
    return _moe_layer(x, l3_ln2, l3_moe_router, l3_moe_w_gate_up, l3_moe_w_down, final_gain=ln_f)
```

```python
import functools
import math

import jax
import jax.numpy as jnp
from jax import lax
from jax.experimental import pallas as pl
from jax.experimental.pallas import tpu as pltpu
from jax.experimental.pallas import tpu_sc as plsc

F32 = jnp.float32
BF16 = jnp.bfloat16
EPS = 1e-6
LANES = 128
MIB = 1 << 20

S5_GROUP = 16
S5_STATE = 64
S5_CHUNK = 16
S5_SLAB_GROUPS = LANES // S5_GROUP
S5_PITCH_PAD = 8

GLA_HEADS = 4
GLA_GATE_RANK = 16
GLA_GATE_NORM = 16.0
GLA_CHUNK = 64

SWA_HEAD_DIM = 64
SWA_KV_HEADS = 2
SWA_WINDOW = 128
SWA_BLOCK = 128
MASK_VALUE = -1e30

TOP_K = 2


def _params(semantics, vmem_mib):
    return pltpu.CompilerParams(dimension_semantics=semantics, vmem_limit_bytes=vmem_mib * MIB)


def _resident(block_shape, index_map):
    return pl.BlockSpec(block_shape, index_map, pipeline_mode=pl.Buffered(1))


def _rms(xf, gain):
    return xf * lax.rsqrt(jnp.mean(xf * xf, axis=-1, keepdims=True) + EPS) * gain


def _gelu_tanh(x):
    return 0.5 * x * (1.0 + jnp.tanh(math.sqrt(2.0 / math.pi) * (x + 0.044715 * (x * x * x))))


def _silu(x):
    return x * jax.nn.sigmoid(x)


def _bdot(a, b):
    return jnp.dot(a, b, preferred_element_type=F32)


S5_ROW_TILE = 512


def _s5_norm_kernel(x_ref, g_ref, o_ref, scr_ref, *, nloc):
    h = _rms(x_ref[0], g_ref[...])
    nslab = scr_ref.shape[0]
    for c in range(nslab):
        scr_ref[c] = h[:, c * LANES:(c + 1) * LANES]
    for s in range(S5_CHUNK):
        rows = pl.ds(s, nloc, stride=S5_CHUNK)
        o_ref[0, s] = jnp.concatenate([scr_ref[c, rows, :] for c in range(nslab)], axis=1).astype(o_ref.dtype)


def _s5_norm(x, gain):
    bsz, seqlen, d = x.shape
    nch = seqlen // S5_CHUNK
    tm = min(S5_ROW_TILE, seqlen)
    nloc = tm // S5_CHUNK
    return pl.pallas_call(
        functools.partial(_s5_norm_kernel, nloc=nloc),
        out_shape=jax.ShapeDtypeStruct((bsz, S5_CHUNK, nch, d), BF16),
        grid=(bsz, seqlen // tm),
        in_specs=[pl.BlockSpec((1, tm, d), lambda b, i: (b, i, 0)),
                  pl.BlockSpec((1, d), lambda b, i: (0, 0))],
        out_specs=pl.BlockSpec((1, S5_CHUNK, nloc, d), lambda b, i: (b, 0, i, 0)),
        scratch_shapes=[pltpu.VMEM((d // LANES, tm, LANES), F32)],
        compiler_params=_params(("parallel", "parallel"), 32),
        name="s5_norm",
    )(x, gain.reshape(1, d))


def _tiling_matrix(rows, cols):
    p = lax.broadcasted_iota(jnp.int32, (rows, cols), 0)
    c = lax.broadcasted_iota(jnp.int32, (rows, cols), 1)
    return jnp.where(c % rows == p, 1.0, 0.0).astype(BF16)


def _same_group(shape, row_group, col_group):
    r = lax.broadcasted_iota(jnp.int32, shape, 0)
    c = lax.broadcasted_iota(jnp.int32, shape, 1)
    return (r // row_group) == (c // col_group)


def _s5_build_operators(vw_ref, mw_ref, toep_ref, win_ref, wout_ref):
    tn = (((0,), (0,)), ((), ()))
    nstate = vw_ref.shape[-1]
    half = S5_SLAB_GROUPS * nstate
    rep_ch = _tiling_matrix(S5_GROUP, LANES)
    rep_st = _tiling_matrix(nstate, half)
    diag_in = _same_group((LANES, half), S5_GROUP, nstate)
    diag_out = _same_group((half, LANES), nstate, S5_GROUP)

    def out_block(q, r):
        e = lax.dot_general(mw_ref[0, 2 * q + r].astype(BF16), rep_ch, tn, preferred_element_type=F32)
        return jnp.where(diag_out, e, 0.0).astype(BF16)

    for a in range(S5_CHUNK):
        for r in range(2):
            e = _bdot(vw_ref[0, 2 * a + r].astype(BF16), rep_st)
            win_ref[a * LANES:(a + 1) * LANES, r * half:(r + 1) * half] = jnp.where(diag_in, e, 0.0).astype(BF16)
            wout_ref[r * half:(r + 1) * half, a * LANES:(a + 1) * LANES] = out_block(a + 1, r)
    b_bar = win_ref[(S5_CHUNK - 1) * LANES:S5_CHUNK * LANES, :]
    taps = [_bdot(b_bar, jnp.concatenate([out_block(0, 0), out_block(0, 1)], axis=0)).astype(BF16)]
    for j in range(1, S5_CHUNK):
        taps.append(_bdot(b_bar, wout_ref[:, (j - 1) * LANES:j * LANES]).astype(BF16))
    zero = jnp.zeros((LANES, LANES), BF16)
    for a in range(S5_CHUNK):
        for b in range(S5_CHUNK):
            toep_ref[a * LANES:(a + 1) * LANES, b * LANES:(b + 1) * LANES] = taps[b - a] if b >= a else zero


def _s5_conv_kernel(h_ref, vw_ref, mw_ref, a_ref, d_ref, o_ref, s_ref, toep_ref, win_ref, wout_ref,
                    *, nseq, nch):
    pitch = nch + S5_PITCH_PAD
    nl = a_ref.shape[1] // 2

    @pl.when(pl.program_id(1) == 0)
    def _():
        _s5_build_operators(vw_ref, mw_ref, toep_ref, win_ref, wout_ref)

    lhs = jnp.concatenate(
        [jnp.concatenate([h_ref[bl, s] for s in range(S5_CHUNK)], axis=1) for bl in range(nseq)], axis=0)
    bc = _bdot(lhs, win_ref[...])
    for bl in range(nseq):
        for j in range(2 * nl):
            s_ref[j, bl * pitch:bl * pitch + nch, :] = bc[bl * nch:(bl + 1) * nch, j * LANES:(j + 1) * LANES]
    a_re = [a_ref[0, j:j + 1, :] for j in range(nl)]
    a_im = [a_ref[0, nl + j:nl + j + 1, :] for j in range(nl)]

    def step(n, carry):
        p_re, p_im = carry
        rows = pl.ds(n, nseq, stride=pitch)
        n_re, n_im = [], []
        for j in range(nl):
            c_re = s_ref[j, rows, :]
            c_im = s_ref[nl + j, rows, :]
            s_ref[j, rows, :] = p_re[j]
            s_ref[nl + j, rows, :] = p_im[j]
            n_re.append(a_re[j] * p_re[j] - a_im[j] * p_im[j] + c_re)
            n_im.append(a_re[j] * p_im[j] + a_im[j] * p_re[j] + c_im)
        return tuple(n_re), tuple(n_im)

    zeros = tuple(jnp.zeros((nseq, LANES), F32) for _ in range(nl))
    lax.fori_loop(0, nch, step, (zeros, zeros))
    x_prev = jnp.concatenate(
        [jnp.concatenate([s_ref[j, bl * pitch:bl * pitch + nch, :] for j in range(2 * nl)], axis=1)
         for bl in range(nseq)], axis=0).astype(BF16)
    y = _bdot(lhs, toep_ref[...]) + _bdot(x_prev, wout_ref[...])
    dskip = d_ref[0]
    for bl in range(nseq):
        for s in range(S5_CHUNK):
            ys = y[bl * nch:(bl + 1) * nch, s * LANES:(s + 1) * LANES]
            ys = ys + dskip * h_ref[bl, s].astype(F32)
            o_ref[bl, s] = _gelu_tanh(ys).astype(o_ref.dtype)


def _s5_conv(hp, vw, mw, a_pack, d_skip, *, nseq):
    bsz, _, nch, d = hp.shape
    nslab = d // LANES
    kdim = S5_CHUNK * LANES
    sdim = a_pack.shape[1] * LANES
    blk4 = lambda a: pl.BlockSpec((1,) + a.shape[1:], lambda c, b: (c, 0, 0, 0))
    return pl.pallas_call(
        functools.partial(_s5_conv_kernel, nseq=nseq, nch=nch),
        out_shape=jax.ShapeDtypeStruct(hp.shape, BF16),
        grid=(nslab, bsz // nseq),
        in_specs=[pl.BlockSpec((nseq, S5_CHUNK, nch, LANES), lambda c, b: (b, 0, 0, c)),
                  blk4(vw), blk4(mw),
                  pl.BlockSpec((1, sdim // LANES, LANES), lambda c, b: (c, 0, 0)),
                  pl.BlockSpec((1, 1, LANES), lambda c, b: (c, 0, 0))],
        out_specs=pl.BlockSpec((nseq, S5_CHUNK, nch, LANES), lambda c, b: (b, 0, 0, c)),
        scratch_shapes=[pltpu.VMEM((sdim // LANES, nseq * (nch + S5_PITCH_PAD), LANES), F32),
                        pltpu.VMEM((kdim, kdim), BF16),
                        pltpu.VMEM((kdim, sdim), BF16),
                        pltpu.VMEM((sdim, kdim), BF16)],
        compiler_params=_params(("arbitrary", "arbitrary"), 56),
        name="s5_conv",
    )(hp, vw, mw, a_pack, d_skip.reshape(nslab, 1, LANES))


def _s5_glu_kernel(y_ref, x_ref, w_ref, b_ref, o_ref, scr_ref, *, nloc):
    nslab = scr_ref.shape[0]
    y = jnp.concatenate([y_ref[0, s] for s in range(S5_CHUNK)], axis=0)
    u = y.astype(F32) * jax.nn.sigmoid(_bdot(y, w_ref[...]) + b_ref[...])
    for s in range(S5_CHUNK):
        rows = pl.ds(s, nloc, stride=S5_CHUNK)
        for c in range(nslab):
            scr_ref[c, rows, :] = u[s * nloc:(s + 1) * nloc, c * LANES:(c + 1) * LANES]
    o_ref[0] = x_ref[0] + jnp.concatenate([scr_ref[c] for c in range(nslab)], axis=1)


def _s5_glu(yp, x, w_glu, b_glu):
    bsz, seqlen, d = x.shape
    tm = min(S5_ROW_TILE, seqlen)
    nloc = tm // S5_CHUNK
    return pl.pallas_call(
        functools.partial(_s5_glu_kernel, nloc=nloc),
        out_shape=jax.ShapeDtypeStruct(x.shape, F32),
        grid=(bsz, seqlen // tm),
        in_specs=[pl.BlockSpec((1, S5_CHUNK, nloc, d), lambda b, i: (b, 0, i, 0)),
                  pl.BlockSpec((1, tm, d), lambda b, i: (b, i, 0)),
                  _resident((d, d), lambda b, i: (0, 0)),
                  pl.BlockSpec((1, d), lambda b, i: (0, 0))],
        out_specs=pl.BlockSpec((1, tm, d), lambda b, i: (b, i, 0)),
        scratch_shapes=[pltpu.VMEM((d // LANES, tm, LANES), F32)],
        compiler_params=_params(("parallel", "parallel"), 40),
        name="s5_glu",
    )(yp, x, w_glu.astype(BF16), b_glu.reshape(1, d))


def _s5_operators(lam_re, lam_im, log_dt, b_re, b_im, c_re, c_im):
    ngroups, nstate = lam_re.shape
    gpc = S5_SLAB_GROUPS
    nslab = ngroups // gpc
    dt = jnp.exp(log_dt)[:, None]
    j = jnp.arange(S5_CHUNK + 1, dtype=F32)[:, None, None]
    mag = jnp.exp(j * (lam_re * dt)[None])
    ang = j * (lam_im * dt)[None]
    pw_re, pw_im = mag * jnp.cos(ang), mag * jnp.sin(ang)
    num_re, num_im = pw_re[1] - 1.0, pw_im[1]
    den = lam_re * lam_re + lam_im * lam_im
    f_re = (num_re * lam_re + num_im * lam_im) / den
    f_im = (num_im * lam_re - num_re * lam_im) / den
    bb_re = f_re[..., None] * b_re - f_im[..., None] * b_im
    bb_im = f_re[..., None] * b_im + f_im[..., None] * b_re
    jr = (S5_CHUNK - 1) - jnp.arange(S5_CHUNK, dtype=F32)[:, None, None]
    mag_r = jnp.exp(jr * (lam_re * dt)[None])
    ang_r = jr * (lam_im * dt)[None]
    rev_re, rev_im = mag_r * jnp.cos(ang_r), mag_r * jnp.sin(ang_r)
    slabbed = lambda a: a.reshape(a.shape[0], nslab, gpc, nstate).transpose(1, 0, 2, 3)
    rv_re, rv_im = slabbed(rev_re)[:, :, :, None, :], slabbed(rev_im)[:, :, :, None, :]
    bt_re = bb_re.transpose(0, 2, 1).reshape(nslab, 1, gpc, S5_GROUP, nstate)
    bt_im = bb_im.transpose(0, 2, 1).reshape(nslab, 1, gpc, S5_GROUP, nstate)
    vw = jnp.stack([rv_re * bt_re - rv_im * bt_im, rv_re * bt_im + rv_im * bt_re], axis=2)
    vw = vw.reshape(nslab, 2 * S5_CHUNK, LANES, nstate)
    pc_re, pc_im = slabbed(pw_re)[:, :, None, :, :], slabbed(pw_im)[:, :, None, :, :]
    ct_re = c_re.reshape(nslab, gpc, S5_GROUP, nstate).transpose(0, 2, 1, 3)[:, None]
    ct_im = c_im.reshape(nslab, gpc, S5_GROUP, nstate).transpose(0, 2, 1, 3)[:, None]
    mw = jnp.stack([ct_re * pc_re - ct_im * pc_im, -(ct_re * pc_im + ct_im * pc_re)], axis=2)
    mw = mw.reshape(nslab, 2 * (S5_CHUNK + 1), S5_GROUP, gpc * nstate)
    half = gpc * nstate // LANES
    a_pack = jnp.concatenate([pw_re[S5_CHUNK].reshape(nslab, half, LANES),
                              pw_im[S5_CHUNK].reshape(nslab, half, LANES)], axis=1)
    return vw, mw, a_pack


def _s5_layer(x, ln, operators, d_skip, w_glu, b_glu, *, nseq=4):
    vw, mw, a_pack = operators
    hp = _s5_norm(x, ln)
    yp = _s5_conv(hp, vw, mw, a_pack, d_skip, nseq=min(nseq, x.shape[0]))
    return _s5_glu(yp, x, w_glu, b_glu)


def _dense_ffn_kernel(x_ref, g_ref, wg_ref, wu_ref, wd_ref, o_ref):
    xf = x_ref[...]
    h = _rms(xf, g_ref[...]).astype(BF16)
    act = (_silu(_bdot(h, wg_ref[...])) * _bdot(h, wu_ref[...])).astype(BF16)
    o_ref[...] = xf + _bdot(act, wd_ref[...])


def _dense_ffn_layer(x, ln, w_gate_up, w_down, *, tm=512):
    bsz, seqlen, d = x.shape
    ntok = bsz * seqlen
    hidden = w_down.shape[0]
    tm = min(tm, ntok)
    wgu = w_gate_up.astype(BF16)
    out = pl.pallas_call(
        _dense_ffn_kernel,
        out_shape=jax.ShapeDtypeStruct((ntok, d), F32),
        grid=(ntok // tm,),
        in_specs=[pl.BlockSpec((tm, d), lambda i: (i, 0)),
                  pl.BlockSpec((1, d), lambda i: (0, 0)),
                  _resident((d, hidden), lambda i: (0, 0)),
                  _resident((d, hidden), lambda i: (0, 1)),
                  _resident((hidden, d), lambda i: (0, 0))],
        out_specs=pl.BlockSpec((tm, d), lambda i: (i, 0)),
        compiler_params=_params(("parallel",), 56),
        name="dense_ffn",
    )(x.reshape(ntok, d), ln.reshape(1, d), wgu, wgu, w_down.astype(BF16))
    return out.reshape(bsz, seqlen, d)


def _log_sigmoid(z):
    return jnp.minimum(z, 0.0) - jnp.log(1.0 + jnp.exp(-jnp.abs(z)))


def _gla_kernel(x_ref, ln_ref, wm_ref, wgl_ref, wg2_ref, bg2_ref, gn_ref, wo_ref, o_ref, st_ref,
                *, tq, dk, dv, heads):
    hdk, hdv = dk // heads, dv // heads
    chunk = GLA_CHUNK
    nt = (((1,), (1,)), ((), ()))
    tn = (((0,), (0,)), ((), ()))

    @pl.when(pl.program_id(1) == 0)
    def _():
        st_ref[...] = jnp.zeros_like(st_ref)

    xf = x_ref[0]
    h = _rms(xf, ln_ref[...]).astype(BF16)
    proj = _bdot(h, wm_ref[...])
    glow = _bdot(h, wgl_ref[...]).astype(BF16)
    la = _log_sigmoid(_bdot(glow, wg2_ref[...]) + bg2_ref[...]) * (1.0 / GLA_GATE_NORM)
    row = lax.broadcasted_iota(jnp.int32, (chunk, chunk), 0)
    col = lax.broadcasted_iota(jnp.int32, (chunk, chunk), 1)
    causal = row >= col
    tri = jnp.where(causal, 1.0, 0.0).astype(BF16)
    scale = hdk ** -0.5
    outs = []
    for c in range(tq // chunk):
        r0 = c * chunk
        la_c = la[r0:r0 + chunk, :]
        la_hi = la_c.astype(BF16)
        la_lo = (la_c - la_hi.astype(F32)).astype(BF16)
        gcum_all = _bdot(tri, la_hi) + _bdot(tri, la_lo)
        head_out = []
        for hd in range(heads):
            gcum = gcum_all[:, hd * hdk:(hd + 1) * hdk]
            g_last = gcum[chunk - 1:chunk, :]
            q_c = proj[r0:r0 + chunk, hd * hdk:(hd + 1) * hdk] * scale
            k_c = proj[r0:r0 + chunk, dk + hd * hdk:dk + (hd + 1) * hdk]
            v_c = proj[r0:r0 + chunk, 2 * dk + hd * hdv:2 * dk + (hd + 1) * hdv].astype(BF16)
            q_s = (q_c * jnp.exp(gcum)).astype(BF16)
            k_s = (k_c * jnp.exp(-gcum)).astype(BF16)
            k_end = (k_c * jnp.exp(g_last - gcum)).astype(BF16)
            scores = lax.dot_general(q_s, k_s, nt, preferred_element_type=F32)
            scores = jnp.where(causal, scores, 0.0).astype(BF16)
            state_t = st_ref[hd]
            o = _bdot(scores, v_c) + lax.dot_general(q_s, state_t.astype(BF16), nt,
                                                     preferred_element_type=F32)
            kv_t = lax.dot_general(v_c, k_end, tn, preferred_element_type=F32)
            st_ref[hd] = state_t * jnp.exp(g_last) + kv_t
            head_out.append(o * lax.rsqrt(jnp.mean(o * o, axis=-1, keepdims=True) + EPS))
        outs.append(jnp.concatenate(head_out, axis=1))
    o_all = jnp.concatenate(outs, axis=0)
    r = proj[:, 2 * dk + dv:]
    o_all = (o_all * gn_ref[...] * _silu(r)).astype(BF16)
    o_ref[0] = xf + _bdot(o_all, wo_ref[...])


def _gla_layer(x, ln, w_in, w_g2, b_g2, g_norm, w_out, *, tq=256):
    bsz, seqlen, d = x.shape
    dk = w_g2.shape[1]
    dv = w_out.shape[0]
    nmain = 2 * dk + 2 * dv
    tq = min(tq, seqlen)
    w_main = w_in[:, :nmain].astype(BF16)
    w_glow = jnp.pad(w_in[:, nmain:], ((0, 0), (0, LANES - GLA_GATE_RANK))).astype(BF16)
    w_g2p = jnp.pad(w_g2, ((0, LANES - GLA_GATE_RANK), (0, 0))).astype(BF16)
    hdk, hdv = dk // GLA_HEADS, dv // GLA_HEADS
    const = lambda b, t: (0, 0)
    return pl.pallas_call(
        functools.partial(_gla_kernel, tq=tq, dk=dk, dv=dv, heads=GLA_HEADS),
        out_shape=jax.ShapeDtypeStruct(x.shape, F32),
        grid=(bsz, seqlen // tq),
        in_specs=[pl.BlockSpec((1, tq, d), lambda b, t: (b, t, 0)),
                  pl.BlockSpec((1, d), const),
                  _resident((d, nmain), const),
                  _resident((d, LANES), const),
                  _resident((LANES, dk), const),
                  pl.BlockSpec((1, dk), const),
                  pl.BlockSpec((1, dv), const),
                  _resident((dv, d), const)],
        out_specs=pl.BlockSpec((1, tq, d), lambda b, t: (b, t, 0)),
        scratch_shapes=[pltpu.VMEM((GLA_HEADS, hdv, hdk), F32)],
        compiler_params=_params(("parallel", "arbitrary"), 48),
        name="gla",
    )(x, ln.reshape(1, d), w_main, w_glow, w_g2p, b_g2.reshape(1, dk), g_norm.reshape(1, dv),
      w_out.astype(BF16))


LOG2E = math.log2(math.e)
SWA_SLOT_UNROLL = 2


def _swa_kernel(sink_ref, x_ref, ln_ref, wqkv_ref, bqkv_ref, wo_ref, bo_ref, o_ref, k_ref, v_ref,
                bias_ref, q_ref, a_ref, *, tq, q_heads):
    group = q_heads // SWA_KV_HEADS
    blk = SWA_BLOCK
    nt = (((1,), (1,)), ((), ()))
    b = pl.program_id(0)
    t = pl.program_id(1)
    nq = group * LANES

    @pl.when((b == 0) & (t == 0))
    def _():
        qi = lax.broadcasted_iota(jnp.int32, (blk, 2 * blk), 0)
        kj = lax.broadcasted_iota(jnp.int32, (blk, 2 * blk), 1)
        dist = qi + blk - kj
        in_window = (dist >= 0) & (dist < SWA_WINDOW)
        for hq in range(q_heads):
            slope = 2.0 ** (-8.0 * (hq + 1) / q_heads)
            bias_ref[hq] = jnp.where(in_window, -(slope * LOG2E) * dist.astype(F32), MASK_VALUE)

    @pl.when(t == 0)
    def _():
        k_ref[0:blk, :] = jnp.zeros((blk, LANES), BF16)
        v_ref[0:blk, :] = jnp.zeros((blk, LANES), BF16)

    xf = x_ref[0]
    h = _rms(xf, ln_ref[...]).astype(BF16)
    qkv = _bdot(h, wqkv_ref[...]) + bqkv_ref[...]
    for j in range(group):
        q_ref[j] = qkv[:, j * LANES:(j + 1) * LANES].astype(BF16)
    k_ref[blk:blk + tq, :] = qkv[:, nq:nq + LANES].astype(BF16)
    v_ref[blk:blk + tq, :] = qkv[:, nq + LANES:nq + 2 * LANES].astype(BF16)
    kj_row = lax.broadcasted_iota(jnp.int32, (1, 2 * blk), 1)
    no_prev = jnp.where(kj_row < blk, jnp.where(t == 0, MASK_VALUE, 0.0), 0.0)
    low_half = lax.broadcasted_iota(jnp.int32, (1, LANES), 1) < SWA_HEAD_DIM
    halves = (low_half, jnp.logical_not(low_half))

    def slots(jj, carry):
        for u in range(SWA_SLOT_UNROLL):
            j = jj * SWA_SLOT_UNROLL + u
            for i in range(tq // blk):
                r0 = i * blk
                q_slot = q_ref[j, r0:r0 + blk, :]
                outs = []
                for kh in range(SWA_KV_HEADS):
                    hq = kh * group + j
                    sink = sink_ref[hq] * LOG2E
                    q_h = jnp.where(halves[kh], q_slot, jnp.zeros_like(q_slot))
                    s = lax.dot_general(q_h, k_ref[r0:r0 + 2 * blk, :], nt, preferred_element_type=F32) + bias_ref[hq]
                    if i == 0:
                        s = s + no_prev
                    m = jnp.maximum(jnp.max(s, axis=-1, keepdims=True), sink)
                    p = jnp.exp2(s - m)
                    denom = jnp.sum(p, axis=-1, keepdims=True) + jnp.exp2(sink - m)
                    outs.append(_bdot(p.astype(BF16), v_ref[r0:r0 + 2 * blk, :]) * (1.0 / denom))
                a_ref[j, r0:r0 + blk, :] = jnp.where(low_half, outs[0], outs[1]).astype(BF16)
        return carry

    lax.fori_loop(0, group // SWA_SLOT_UNROLL, slots, 0)
    k_ref[0:blk, :] = k_ref[tq:tq + blk, :]
    v_ref[0:blk, :] = v_ref[tq:tq + blk, :]
    o_all = jnp.concatenate([a_ref[j] for j in range(group)], axis=1)
    o_ref[0] = xf + _bdot(o_all, wo_ref[...]) + bo_ref[...]


def _swa_layer(x, ln, w_qkv, b_qkv, sinks, w_out, b_out, *, tq=512):
    bsz, seqlen, d = x.shape
    hd = SWA_HEAD_DIM
    q_heads = sinks.shape[0]
    group = q_heads // SWA_KV_HEADS
    nq = q_heads * hd
    tq = min(tq, seqlen)
    q_scale = hd ** -0.5 * LOG2E
    wq = (w_qkv[:, :nq] * q_scale).reshape(d, SWA_KV_HEADS, group, hd).transpose(0, 2, 1, 3).reshape(d, nq)
    bq = (b_qkv[:nq] * q_scale).reshape(SWA_KV_HEADS, group, hd).transpose(1, 0, 2).reshape(nq)
    w_all = jnp.concatenate([wq, w_qkv[:, nq:]], axis=1).astype(BF16)
    b_all = jnp.concatenate([bq, b_qkv[nq:]]).reshape(1, -1)
    wo = w_out.reshape(SWA_KV_HEADS, group, hd, d).transpose(1, 0, 2, 3).reshape(nq, d).astype(BF16)
    nall = w_all.shape[1]
    const = lambda b, t, s: (0, 0)
    return pl.pallas_call(
        functools.partial(_swa_kernel, tq=tq, q_heads=q_heads),
        out_shape=jax.ShapeDtypeStruct(x.shape, F32),
        grid_spec=pltpu.PrefetchScalarGridSpec(
            num_scalar_prefetch=1,
            grid=(bsz, seqlen // tq),
            in_specs=[pl.BlockSpec((1, tq, d), lambda b, t, s: (b, t, 0)),
                      pl.BlockSpec((1, d), const),
                      _resident((d, nall), const),
                      pl.BlockSpec((1, nall), const),
                      _resident((nq, d), const),
                      pl.BlockSpec((1, d), const)],
            out_specs=pl.BlockSpec((1, tq, d), lambda b, t, s: (b, t, 0)),
            scratch_shapes=[pltpu.VMEM((SWA_BLOCK + tq, LANES), BF16), pltpu.VMEM((SWA_BLOCK + tq, LANES), BF16),
                            pltpu.VMEM((q_heads, SWA_BLOCK, 2 * SWA_BLOCK), F32),
                            pltpu.VMEM((group, tq, LANES), BF16), pltpu.VMEM((group, tq, LANES), BF16)]),
        compiler_params=_params(("arbitrary", "arbitrary"), 48),
        name="swa",
    )(sinks, x, ln.reshape(1, d), w_all, b_all, wo, b_out.reshape(1, d))


def _router_kernel(x_ref, ln_ref, whi_ref, wlo_ref, idx_ref, gate_ref, hp_ref, pos_ref, count_ref, tri_ref):
    nt = (((1,), (1,)), ((), ()))
    h = _rms(x_ref[...], ln_ref[...])
    h_hi = h.astype(BF16)
    h_lo = (h - h_hi.astype(F32)).astype(BF16)
    w_hi, w_lo = whi_ref[...], wlo_ref[...]
    logits = (lax.dot_general(w_hi, h_hi, nt, preferred_element_type=F32)
              + lax.dot_general(w_hi, h_lo, nt, preferred_element_type=F32)
              + lax.dot_general(w_lo, h_hi, nt, preferred_element_type=F32))
    n_exp = logits.shape[0]
    eid = lax.broadcasted_iota(jnp.int32, logits.shape, 0)
    m1 = jnp.max(logits, axis=0, keepdims=True)
    i1 = jnp.min(jnp.where(logits == m1, eid, n_exp), axis=0, keepdims=True)
    rest = jnp.where(eid == i1, -jnp.inf, logits)
    m2 = jnp.max(rest, axis=0, keepdims=True)
    i2 = jnp.min(jnp.where(rest == m2, eid, n_exp), axis=0, keepdims=True)
    e2 = jnp.exp(m2 - m1)
    g1 = 1.0 / (1.0 + e2)
    idx_ref[...] = jnp.concatenate([i1, i2], axis=0)
    gate_ref[...] = jnp.concatenate([g1, e2 * g1], axis=0)
    hp_ref[...] = _pack_bf16_pairs(h)
    tm = logits.shape[1]

    @pl.when(pl.program_id(0) == 0)
    def _():
        count_ref[...] = jnp.zeros_like(count_ref)
        r = lax.broadcasted_iota(jnp.int32, (tm, tm), 0)
        c = lax.broadcasted_iota(jnp.int32, (tm, tm), 1)
        tri_ref[...] = jnp.where(r < c, 1.0, 0.0).astype(BF16)

    pick1 = jnp.where(eid == i1, 1.0, 0.0)
    pick2 = jnp.where(eid == i2, 1.0, 0.0)
    picks = pick1 + pick2
    before = _bdot(picks.astype(BF16), tri_ref[...]) + count_ref[:, 0:1]
    pos_ref[...] = jnp.concatenate([jnp.sum(pick1 * before, axis=0, keepdims=True),
                                    jnp.sum(pick2 * before, axis=0, keepdims=True)], axis=0).astype(jnp.int32)
    count_ref[...] = count_ref[...] + jnp.sum(picks, axis=1, keepdims=True)


def _router(x2, ln, w_router, *, tm=512):
    ntok, d = x2.shape
    n_exp = w_router.shape[1]
    tm = min(tm, ntok)
    wt = w_router.T
    w_hi = wt.astype(BF16)
    w_lo = (wt - w_hi.astype(F32)).astype(BF16)
    return pl.pallas_call(
        _router_kernel,
        out_shape=(jax.ShapeDtypeStruct((TOP_K, ntok), jnp.int32), jax.ShapeDtypeStruct((TOP_K, ntok), F32),
                   jax.ShapeDtypeStruct((ntok, d // 2), jnp.uint32),
                   jax.ShapeDtypeStruct((TOP_K, ntok), jnp.int32), jax.ShapeDtypeStruct((n_exp, LANES), F32)),
        grid=(ntok // tm,),
        in_specs=[pl.BlockSpec((tm, d), lambda i: (i, 0)),
                  pl.BlockSpec((1, d), lambda i: (0, 0)),
                  pl.BlockSpec((n_exp, d), lambda i: (0, 0)),
                  pl.BlockSpec((n_exp, d), lambda i: (0, 0))],
        out_specs=(pl.BlockSpec((TOP_K, tm), lambda i: (0, i)), pl.BlockSpec((TOP_K, tm), lambda i: (0, i)),
                   pl.BlockSpec((tm, d // 2), lambda i: (i, 0)),
                   pl.BlockSpec((TOP_K, tm), lambda i: (0, i)), pl.BlockSpec((n_exp, LANES), lambda i: (0, 0))),
        scratch_shapes=[pltpu.VMEM((tm, tm), BF16)],
        compiler_params=_params(("arbitrary",), 32),
        name="moe_router",
    )(x2, ln.reshape(1, d), w_hi, w_lo)


def _moe_plan(idx, pos, counts, tile):
    n_exp = counts.shape[0]
    nslots = idx.size
    counts = counts[:, 0].astype(jnp.int32)
    ends = jnp.cumsum(counts)
    offs = ends - counts
    experts = jnp.arange(n_exp, dtype=jnp.int32).reshape(n_exp, 1, 1)
    rank = pos + jnp.sum(jnp.where(idx[None] == experts, offs.reshape(n_exp, 1, 1), 0), axis=0)
    n_tiles = nslots // tile
    n_visits = n_tiles + n_exp - 1
    first_tile = offs // tile
    last_tile = (ends - 1) // tile
    nvis = jnp.where(counts > 0, last_tile - first_tile + 1, 0)
    vend = jnp.cumsum(nvis)
    vstart = vend - nvis
    total = vend[-1]
    v = jnp.arange(n_visits, dtype=jnp.int32)
    vc = jnp.minimum(v, total - 1)
    e = jnp.minimum(jnp.sum((vc[:, None] >= vend[None, :]).astype(jnp.int32), axis=1), n_exp - 1)
    sel = (e[:, None] == jnp.arange(n_exp, dtype=jnp.int32)[None, :]).astype(jnp.int32)
    pick = lambda a: jnp.sum(sel * a[None, :], axis=1)
    tile_id = pick(first_tile) + vc - pick(vstart)
    lo = jnp.maximum(pick(offs), tile_id * tile) - tile_id * tile
    hi = jnp.minimum(pick(ends), (tile_id + 1) * tile) - tile_id * tile
    valid = v < total
    lo = jnp.where(valid, lo, 0)
    hi = jnp.where(valid, hi, 0)
    prev_tile = jnp.concatenate([jnp.full((1,), -1, jnp.int32), tile_id[:-1]])
    first = (valid & (tile_id != prev_tile)).astype(jnp.int32)
    next_tile = jnp.concatenate([tile_id[1:], jnp.full((1,), -1, jnp.int32)])
    last = (valid & ((tile_id != next_tile) | (v == total - 1))).astype(jnp.int32)
    meta = jnp.stack([tile_id, e, lo, hi, first, last]).astype(jnp.int32)
    return rank.astype(jnp.int32), meta


def _pack_bf16_pairs(h):
    half = h.shape[1] // 2
    bits = lax.bitcast_convert_type(h.astype(BF16).astype(F32), jnp.uint32)
    return (bits[:, half:] & jnp.uint32(0xFFFF0000)) | (bits[:, :half] >> 16)


def _unpack_pairs_f32(u):
    lo = lax.bitcast_convert_type(u << 16, F32)
    hi = lax.bitcast_convert_type(u & jnp.uint32(0xFFFF0000), F32)
    return jnp.concatenate([lo, hi], axis=1)


def _unpack_bf16_pairs(u):
    return _unpack_pairs_f32(u).astype(BF16)


SC_CORES = 2
SC_SUBCORES = 16
SC_INDEX_WINDOW = 128


def _sc_mesh():
    return plsc.VectorSubcoreMesh(core_axis_name="c", subcore_axis_name="s")


def _sc_worker_id():
    return lax.axis_index("c") * SC_SUBCORES + lax.axis_index("s")


def _sc_scatter_rows(src, rank, nrows):
    ntok, width = src.shape
    win = SC_INDEX_WINDOW
    per = ntok // (SC_CORES * SC_SUBCORES)

    @pl.kernel(out_type=jax.ShapeDtypeStruct((nrows, width), src.dtype), mesh=_sc_mesh(),
               scratch_types=[pltpu.VMEM((1, win), jnp.int32)] * TOP_K + [pltpu.VMEM((win, width), src.dtype)],
               name="moe_dispatch_sc")
    def scatter(src_hbm, rank_hbm, o_hbm, *scratch):
        idx_vmem, buf = scratch[:TOP_K], scratch[TOP_K]
        wid = _sc_worker_id()

        @pl.loop(0, per // win)
        def _(blk):
            base = wid * per + blk * win
            for k in range(TOP_K):
                pltpu.sync_copy(rank_hbm.at[pl.ds(k, 1), pl.ds(base, win)], idx_vmem[k])
            pltpu.sync_copy(src_hbm.at[pl.ds(base, win)], buf)
            for k in range(TOP_K):
                pltpu.sync_copy(buf, o_hbm.at[idx_vmem[k].at[0]])

    return scatter(src, rank)


def _sc_gather_rows(src, idx, *, sub=32):
    n = idx.shape[0]
    width = src.shape[1]
    win = SC_INDEX_WINDOW
    per = n // (SC_CORES * SC_SUBCORES)
    nsub = win // sub

    @pl.kernel(out_type=jax.ShapeDtypeStruct((n, width), src.dtype), mesh=_sc_mesh(),
               scratch_types=[pltpu.VMEM((1, win), jnp.int32)] + [pltpu.VMEM((sub, width), src.dtype)] * 2
               + [pltpu.SemaphoreType.DMA] * 4,
               name="moe_gather_sc")
    def gather(src_hbm, idx_hbm, o_hbm, i_vmem, buf0, buf1, g0, g1, w0, w1):
        bufs, gsem, wsem = (buf0, buf1), (g0, g1), (w0, w1)
        wid = _sc_worker_id()

        @pl.loop(0, per // win)
        def _(blk):
            base = wid * per + blk * win
            pltpu.sync_copy(idx_hbm.at[:, pl.ds(base, win)], i_vmem)
            gathers = [pltpu.make_async_copy(src_hbm.at[i_vmem.at[0, pl.ds(sub * j, sub)]], bufs[j % 2], gsem[j % 2])
                       for j in range(nsub)]
            writes = [pltpu.make_async_copy(bufs[j % 2], o_hbm.at[pl.ds(base + sub * j, sub)], wsem[j % 2])
                      for j in range(nsub)]
            gathers[0].start()
            for j in range(nsub):
                if j + 1 < nsub:
                    if j >= 1:
                        writes[j - 1].wait()
                    gathers[j + 1].start()
                gathers[j].wait()
                writes[j].start()
            writes[nsub - 2].wait()
            writes[nsub - 1].wait()

    return gather(src, idx.reshape(1, n))


MXU_N = 256


def _expert_kernel(meta_ref, x_ref, wg_ref, wu_ref, wd_ref, o_ref, acc_ref, xb_ref, act_ref, wgb_ref, wub_ref,
                   wdb_ref, *, ts):
    v = pl.program_id(0)
    hc = pl.program_id(1)
    lo, hi, first, last = meta_ref[2, v], meta_ref[3, v], meta_ref[4, v], meta_ref[5, v]
    tile, d = acc_ref.shape
    nsub = tile // ts
    th = wgb_ref.shape[1]
    wide = (hi - lo) * 2 > tile

    @pl.when(hc == 0)
    def _():
        for sub in range(nsub):
            xb_ref[sub * ts:(sub + 1) * ts, :] = _unpack_bf16_pairs(x_ref[sub * ts:(sub + 1) * ts, :])

    @pl.when((first == 1) & (hc == 0))
    def _():
        acc_ref[...] = jnp.zeros_like(acc_ref)

    @pl.when(wide)
    def _():
        rows = lax.broadcasted_iota(jnp.int32, (tile, 1), 0)
        mine = (rows >= lo) & (rows < hi)
        for n in range(th // MXU_N):
            cols = slice(n * MXU_N, (n + 1) * MXU_N)
            gate = _bdot(xb_ref[...], wg_ref[0, :, cols].astype(BF16))
            up = _bdot(xb_ref[...], wu_ref[0, :, cols].astype(BF16))
            act_ref[:, cols] = (_silu(gate) * up).astype(BF16)
        for n in range(d // MXU_N):
            cols = slice(n * MXU_N, (n + 1) * MXU_N)
            acc_ref[:, cols] += jnp.where(mine, _bdot(act_ref[...], wd_ref[0, :, cols].astype(BF16)), 0.0)

    @pl.when(jnp.logical_not(wide) & (hi > lo))
    def _():
        wgb_ref[...] = wg_ref[0].astype(BF16)
        wub_ref[...] = wu_ref[0].astype(BF16)
        wdb_ref[...] = wd_ref[0].astype(BF16)
        for sub in range(nsub):
            r0 = sub * ts

            @pl.when((lo < r0 + ts) & (hi > r0))
            def _():
                xs = xb_ref[r0:r0 + ts, :]
                act = (_silu(_bdot(xs, wgb_ref[...])) * _bdot(xs, wub_ref[...])).astype(BF16)
                y = _bdot(act, wdb_ref[...])
                rows = r0 + lax.broadcasted_iota(jnp.int32, (ts, 1), 0)
                acc_ref[r0:r0 + ts, :] += jnp.where((rows >= lo) & (rows < hi), y, 0.0)

    @pl.when((last == 1) & (hc == pl.num_programs(1) - 1))
    def _():
        for sub in range(nsub):
            o_ref[sub * ts:(sub + 1) * ts, :] = _pack_bf16_pairs(acc_ref[sub * ts:(sub + 1) * ts, :])


def _experts(xg, meta, w_gate_up, w_down, *, tile, th=512, ts=512):
    nrows = xg.shape[0]
    n_exp, hidden, d = w_down.shape
    n_hc = hidden // th
    ts = min(ts, tile)
    wgu = w_gate_up
    return pl.pallas_call(
        functools.partial(_expert_kernel, ts=ts),
        out_shape=jax.ShapeDtypeStruct((nrows, d // 2), jnp.uint32),
        grid_spec=pltpu.PrefetchScalarGridSpec(
            num_scalar_prefetch=1,
            grid=(meta.shape[1], n_hc),
            in_specs=[pl.BlockSpec((tile, d // 2), lambda v, c, m: (m[0, v], 0)),
                      pl.BlockSpec((1, d, th), lambda v, c, m: (m[1, v], 0, c)),
                      pl.BlockSpec((1, d, th), lambda v, c, m: (m[1, v], 0, c + n_hc)),
                      pl.BlockSpec((1, th, d), lambda v, c, m: (m[1, v], c, 0))],
            out_specs=pl.BlockSpec((tile, d // 2), lambda v, c, m: (m[0, v], 0)),
            scratch_shapes=[pltpu.VMEM((tile, d), F32), pltpu.VMEM((tile, d), BF16), pltpu.VMEM((tile, th), BF16),
                            pltpu.VMEM((d, th), BF16), pltpu.VMEM((d, th), BF16), pltpu.VMEM((th, d), BF16)]),
        compiler_params=_params(("arbitrary", "arbitrary"), 56),
        name="moe_experts",
    )(meta, xg, wgu, wgu, w_down)


def _combine_kernel(x_ref, gate_ref, fg_ref, y0_ref, y1_ref, o_ref, *, final_norm):
    g = gate_ref[...]
    out = x_ref[...] + g[:, 0:1] * _unpack_pairs_f32(y0_ref[0]) + g[:, 1:2] * _unpack_pairs_f32(y1_ref[0])
    if final_norm:
        out = _rms(out, fg_ref[...])
    o_ref[...] = out


def _combine(x2, gates_t, yk, final_gain, *, tm=512):
    ntok, d = x2.shape
    tm = min(tm, ntok)
    final_norm = final_gain is not None
    fg = (final_gain if final_norm else jnp.ones((d,), F32)).reshape(1, d)
    return pl.pallas_call(
        functools.partial(_combine_kernel, final_norm=final_norm),
        out_shape=jax.ShapeDtypeStruct((ntok, d), F32),
        grid=(ntok // tm,),
        in_specs=[pl.BlockSpec((tm, d), lambda i: (i, 0)),
                  pl.BlockSpec((tm, TOP_K), lambda i: (i, 0)),
                  pl.BlockSpec((1, d), lambda i: (0, 0)),
                  pl.BlockSpec((1, tm, d // 2), lambda i: (0, i, 0)),
                  pl.BlockSpec((1, tm, d // 2), lambda i: (1, i, 0))],
        out_specs=pl.BlockSpec((tm, d), lambda i: (i, 0)),
        compiler_params=_params(("parallel",), 40),
        name="moe_combine",
    )(x2, gates_t, fg, yk, yk)


def _moe_routed(x, ln, w_router, w_gate_up, w_down, *, tile=2048):
    bsz, seqlen, d = x.shape
    ntok = bsz * seqlen
    tile = min(tile, TOP_K * ntok)
    x2 = x.reshape(ntok, d)
    idx, gates, hp, pos, counts = _router(x2, ln, w_router)
    rank, meta = _moe_plan(idx, pos, counts, tile)
    xg = _sc_scatter_rows(hp, rank, TOP_K * ntok)
    y = _experts(xg, meta, w_gate_up, w_down, tile=tile)
    yk = _sc_gather_rows(y, rank.reshape(-1), sub=64).reshape(TOP_K, ntok, d // 2)
    return gates.T, yk


def _moe_layer(x, ln, w_router, w_gate_up, w_down, *, final_gain=None, tile=2048):
    bsz, seqlen, d = x.shape
    gates_t, yk = _moe_routed(x, ln, w_router, w_gate_up, w_down, tile=tile)
    out = _combine(x.reshape(bsz * seqlen, d), gates_t, yk, final_gain)
    return out.reshape(bsz, seqlen, d)


def kernel(x, l0_ln1, l0_s5_lam_re, l0_s5_lam_im, l0_s5_log_dt, l0_s5_b_re, l0_s5_b_im, l0_s5_c_re, l0_s5_c_im, l0_s5_d, l0_s5_w_glu, l0_s5_b_glu, l0_ln2, l0_ffn_w_gate_up, l0_ffn_w_down, l1_ln1, l1_gla_w_in, l1_gla_w_g2, l1_gla_b_g2, l1_gla_norm, l1_gla_w_out, l1_ln2, l1_moe_router, l1_moe_w_gate_up, l1_moe_w_down, l2_ln1, l2_swa_w_qkv, l2_swa_b_qkv, l2_swa_sinks, l2_swa_w_out, l2_swa_b_out, l2_ln2, l2_ffn_w_gate_up, l2_ffn_w_down, l3_ln1, l3_s5_lam_re, l3_s5_lam_im, l3_s5_log_dt, l3_s5_b_re, l3_s5_b_im, l3_s5_c_re, l3_s5_c_im, l3_s5_d, l3_s5_w_glu, l3_s5_b_glu, l3_ln2, l3_moe_router, l3_moe_w_gate_up, l3_moe_w_down, ln_f):
    s5_params = ((l0_s5_lam_re, l0_s5_lam_im, l0_s5_log_dt, l0_s5_b_re, l0_s5_b_im, l0_s5_c_re, l0_s5_c_im),
                 (l3_s5_lam_re, l3_s5_lam_im, l3_s5_log_dt, l3_s5_b_re, l3_s5_b_im, l3_s5_c_re, l3_s5_c_im))
    s5_ops = jax.vmap(_s5_operators)(*(jnp.stack(pair) for pair in zip(*s5_params)))
    x = _s5_layer(x, l0_ln1, tuple(a[0] for a in s5_ops), l0_s5_d, l0_s5_w_glu, l0_s5_b_glu)
    x = _dense_ffn_layer(x, l0_ln2, l0_ffn_w_gate_up, l0_ffn_w_down)
    x = _gla_layer(x, l1_ln1, l1_gla_w_in, l1_gla_w_g2, l1_gla_b_g2, l1_gla_norm, l1_gla_w_out)
    x = _moe_layer(x, l1_ln2, l1_moe_router, l1_moe_w_gate_up, l1_moe_w_down)
    x = _swa_layer(x, l2_ln1, l2_swa_w_qkv, l2_swa_b_qkv, l2_swa_sinks, l2_swa_w_out, l2_swa_b_out)
    x = _dense_ffn_layer(x, l2_ln2, l2_ffn_w_gate_up, l2_ffn_w_down)
    x = _s5_layer(x, l3_ln1, tuple(a[1] for a in s5_ops), l3_s5_d, l3_s5_w_glu, l3_s5_b_glu)
    return _moe_layer(x, l3_ln2, l3_moe_router, l3_moe_w_gate_up, l3_moe_w_down, final_gain=ln_f)
```

```python
import functools
import math

import jax
import jax.numpy as jnp
from jax import lax
from jax.experimental import pallas as pl
from jax.experimental.pallas import tpu as pltpu
from jax.experimental.pallas import tpu_sc as plsc

F32 = jnp.float32
BF16 = jnp.bfloat16
EPS = 1e-6
LANES = 128
MIB = 1 << 20

S5_GROUP = 16
S5_STATE = 64
S5_CHUNK = 16
S5_SLAB_GROUPS = LANES // S5_GROUP
S5_PITCH_PAD = 8
S5_SCAN_UNROLL = 8
S5_CAUSAL_BANDS = 4

GLA_HEADS = 4
GLA_GATE_RANK = 16
GLA_GATE_NORM = 16.0
GLA_CHUNK = 64

SWA_HEAD_DIM = 64
SWA_KV_HEADS = 2
SWA_WINDOW = 128
SWA_BLOCK = 128
MASK_VALUE = -1e30

TOP_K = 2


def _params(semantics, vmem_mib):
    return pltpu.CompilerParams(dimension_semantics=semantics, vmem_limit_bytes=vmem_mib * MIB)


def _resident(block_shape, index_map):
    return pl.BlockSpec(block_shape, index_map, pipeline_mode=pl.Buffered(1))


def _rms(xf, gain):
    return xf * lax.rsqrt(jnp.mean(xf * xf, axis=-1, keepdims=True) + EPS) * gain


def _gelu_tanh(x):
    return 0.5 * x * (1.0 + jnp.tanh(math.sqrt(2.0 / math.pi) * (x + 0.044715 * (x * x * x))))


def _silu(x):
    return x * jax.nn.sigmoid(x)


def _bdot(a, b):
    return jnp.dot(a, b, preferred_element_type=F32)


S5_ROW_TILE = 512


def _s5_norm_kernel(x_ref, g_ref, o_ref, scr_ref, *, nloc):
    h = _rms(x_ref[0], g_ref[...])
    nslab = scr_ref.shape[0]
    for c in range(nslab):
        scr_ref[c] = h[:, c * LANES:(c + 1) * LANES]
    for s in range(S5_CHUNK):
        rows = pl.ds(s, nloc, stride=S5_CHUNK)
        o_ref[0, s] = jnp.concatenate([scr_ref[c, rows, :] for c in range(nslab)], axis=1).astype(o_ref.dtype)


def _s5_norm(x, gain):
    bsz, seqlen, d = x.shape
    nch = seqlen // S5_CHUNK
    tm = min(S5_ROW_TILE, seqlen)
    nloc = tm // S5_CHUNK
    return pl.pallas_call(
        functools.partial(_s5_norm_kernel, nloc=nloc),
        out_shape=jax.ShapeDtypeStruct((bsz, S5_CHUNK, nch, d), BF16),
        grid=(bsz, seqlen // tm),
        in_specs=[pl.BlockSpec((1, tm, d), lambda b, i: (b, i, 0)),
                  pl.BlockSpec((1, d), lambda b, i: (0, 0))],
        out_specs=pl.BlockSpec((1, S5_CHUNK, nloc, d), lambda b, i: (b, 0, i, 0)),
        scratch_shapes=[pltpu.VMEM((d // LANES, tm, LANES), F32)],
        compiler_params=_params(("parallel", "parallel"), 32),
        name="s5_norm",
    )(x, gain.reshape(1, d))


def _tiling_matrix(rows, cols):
    p = lax.broadcasted_iota(jnp.int32, (rows, cols), 0)
    c = lax.broadcasted_iota(jnp.int32, (rows, cols), 1)
    return jnp.where(c % rows == p, 1.0, 0.0).astype(BF16)


def _same_group(shape, row_group, col_group):
    r = lax.broadcasted_iota(jnp.int32, shape, 0)
    c = lax.broadcasted_iota(jnp.int32, shape, 1)
    return (r // row_group) == (c // col_group)


def _s5_build_operators(vw_ref, mw_ref, toep_ref, win_ref, wout_ref):
    tn = (((0,), (0,)), ((), ()))
    nstate = vw_ref.shape[-1]
    half = S5_SLAB_GROUPS * nstate
    rep_ch = _tiling_matrix(S5_GROUP, LANES)
    rep_st = _tiling_matrix(nstate, half)
    diag_in = _same_group((LANES, half), S5_GROUP, nstate)
    diag_out = _same_group((half, LANES), nstate, S5_GROUP)

    def out_block(q, r):
        e = lax.dot_general(mw_ref[0, 2 * q + r].astype(BF16), rep_ch, tn, preferred_element_type=F32)
        return jnp.where(diag_out, e, 0.0).astype(BF16)

    for a in range(S5_CHUNK):
        for r in range(2):
            e = _bdot(vw_ref[0, 2 * a + r].astype(BF16), rep_st)
            win_ref[a * LANES:(a + 1) * LANES, r * half:(r + 1) * half] = jnp.where(diag_in, e, 0.0).astype(BF16)
            wout_ref[r * half:(r + 1) * half, a * LANES:(a + 1) * LANES] = out_block(a + 1, r)
    b_bar = win_ref[(S5_CHUNK - 1) * LANES:S5_CHUNK * LANES, :]
    taps = [_bdot(b_bar, jnp.concatenate([out_block(0, 0), out_block(0, 1)], axis=0)).astype(BF16)]
    for j in range(1, S5_CHUNK):
        taps.append(_bdot(b_bar, wout_ref[:, (j - 1) * LANES:j * LANES]).astype(BF16))
    zero = jnp.zeros((LANES, LANES), BF16)
    for a in range(S5_CHUNK):
        for b in range(S5_CHUNK):
            toep_ref[a * LANES:(a + 1) * LANES, b * LANES:(b + 1) * LANES] = taps[b - a] if b >= a else zero


def _s5_conv_kernel(h_ref, vw_ref, mw_ref, a_ref, d_ref, o_ref, s_ref, toep_ref, win_ref, wout_ref,
                    *, nseq, nch):
    pitch = nch + S5_PITCH_PAD
    nl = a_ref.shape[1] // 2

    @pl.when(pl.program_id(1) == 0)
    def _():
        _s5_build_operators(vw_ref, mw_ref, toep_ref, win_ref, wout_ref)

    lhs = jnp.concatenate(
        [jnp.concatenate([h_ref[bl, s] for s in range(S5_CHUNK)], axis=1) for bl in range(nseq)], axis=0)
    bc = _bdot(lhs, win_ref[...])
    for bl in range(nseq):
        for j in range(2 * nl):
            s_ref[j, bl * pitch:bl * pitch + nch, :] = bc[bl * nch:(bl + 1) * nch, j * LANES:(j + 1) * LANES]
    a_re = [a_ref[0, j:j + 1, :] for j in range(nl)]
    a_im = [a_ref[0, nl + j:nl + j + 1, :] for j in range(nl)]

    def step(n, carry):
        p_re, p_im = carry
        rows = pl.ds(n, nseq, stride=pitch)
        n_re, n_im = [], []
        for j in range(nl):
            c_re = s_ref[j, rows, :]
            c_im = s_ref[nl + j, rows, :]
            s_ref[j, rows, :] = p_re[j]
            s_ref[nl + j, rows, :] = p_im[j]
            n_re.append(a_re[j] * p_re[j] - a_im[j] * p_im[j] + c_re)
            n_im.append(a_re[j] * p_im[j] + a_im[j] * p_re[j] + c_im)
        return tuple(n_re), tuple(n_im)

    def steps(m, carry):
        for u in range(S5_SCAN_UNROLL):
            carry = step(m * S5_SCAN_UNROLL + u, carry)
        return carry

    zeros = tuple(jnp.zeros((nseq, LANES), F32) for _ in range(nl))
    lax.fori_loop(0, nch // S5_SCAN_UNROLL, steps, (zeros, zeros))
    x_prev = jnp.concatenate(
        [jnp.concatenate([s_ref[j, bl * pitch:bl * pitch + nch, :] for j in range(2 * nl)], axis=1)
         for bl in range(nseq)], axis=0).astype(BF16)
    band = S5_CHUNK // S5_CAUSAL_BANDS
    y_bands = []
    for q in range(S5_CAUSAL_BANDS):
        kk = (q + 1) * band * LANES
        cols = slice(q * band * LANES, (q + 1) * band * LANES)
        y_bands.append(_bdot(lhs[:, :kk], toep_ref[:kk, cols]) + _bdot(x_prev, wout_ref[:, cols]))
    dskip = d_ref[0]
    for bl in range(nseq):
        for s in range(S5_CHUNK):
            ys = y_bands[s // band][bl * nch:(bl + 1) * nch, (s % band) * LANES:(s % band + 1) * LANES]
            ys = ys + dskip * h_ref[bl, s].astype(F32)
            o_ref[bl, s] = _gelu_tanh(ys).astype(o_ref.dtype)


def _s5_conv(hp, vw, mw, a_pack, d_skip, *, nseq):
    bsz, _, nch, d = hp.shape
    nslab = d // LANES
    kdim = S5_CHUNK * LANES
    sdim = a_pack.shape[1] * LANES
    blk4 = lambda a: pl.BlockSpec((1,) + a.shape[1:], lambda c, b: (c, 0, 0, 0))
    return pl.pallas_call(
        functools.partial(_s5_conv_kernel, nseq=nseq, nch=nch),
        out_shape=jax.ShapeDtypeStruct(hp.shape, BF16),
        grid=(nslab, bsz // nseq),
        in_specs=[pl.BlockSpec((nseq, S5_CHUNK, nch, LANES), lambda c, b: (b, 0, 0, c)),
                  blk4(vw), blk4(mw),
                  pl.BlockSpec((1, sdim // LANES, LANES), lambda c, b: (c, 0, 0)),
                  pl.BlockSpec((1, 1, LANES), lambda c, b: (c, 0, 0))],
        out_specs=pl.BlockSpec((nseq, S5_CHUNK, nch, LANES), lambda c, b: (b, 0, 0, c)),
        scratch_shapes=[pltpu.VMEM((sdim // LANES, nseq * (nch + S5_PITCH_PAD), LANES), F32),
                        pltpu.VMEM((kdim, kdim), BF16),
                        pltpu.VMEM((kdim, sdim), BF16),
                        pltpu.VMEM((sdim, kdim), BF16)],
        compiler_params=_params(("arbitrary", "arbitrary"), 56),
        name="s5_conv",
    )(hp, vw, mw, a_pack, d_skip.reshape(nslab, 1, LANES))


def _s5_glu_kernel(y_ref, x_ref, w_ref, b_ref, o_ref, scr_ref, *, nloc):
    nslab = scr_ref.shape[0]
    y = jnp.concatenate([y_ref[0, s] for s in range(S5_CHUNK)], axis=0)
    u = y.astype(F32) * jax.nn.sigmoid(_bdot(y, w_ref[...]) + b_ref[...])
    for s in range(S5_CHUNK):
        rows = pl.ds(s, nloc, stride=S5_CHUNK)
        for c in range(nslab):
            scr_ref[c, rows, :] = u[s * nloc:(s + 1) * nloc, c * LANES:(c + 1) * LANES]
    o_ref[0] = x_ref[0] + jnp.concatenate([scr_ref[c] for c in range(nslab)], axis=1)


def _s5_glu(yp, x, w_glu, b_glu):
    bsz, seqlen, d = x.shape
    tm = min(S5_ROW_TILE, seqlen)
    nloc = tm // S5_CHUNK
    return pl.pallas_call(
        functools.partial(_s5_glu_kernel, nloc=nloc),
        out_shape=jax.ShapeDtypeStruct(x.shape, F32),
        grid=(bsz, seqlen // tm),
        in_specs=[pl.BlockSpec((1, S5_CHUNK, nloc, d), lambda b, i: (b, 0, i, 0)),
                  pl.BlockSpec((1, tm, d), lambda b, i: (b, i, 0)),
                  _resident((d, d), lambda b, i: (0, 0)),
                  pl.BlockSpec((1, d), lambda b, i: (0, 0))],
        out_specs=pl.BlockSpec((1, tm, d), lambda b, i: (b, i, 0)),
        scratch_shapes=[pltpu.VMEM((d // LANES, tm, LANES), F32)],
        compiler_params=_params(("parallel", "parallel"), 40),
        name="s5_glu",
    )(yp, x, w_glu.astype(BF16), b_glu.reshape(1, d))


def _s5_operators(lam_re, lam_im, log_dt, b_re, b_im, c_re, c_im):
    ngroups, nstate = lam_re.shape
    gpc = S5_SLAB_GROUPS
    nslab = ngroups // gpc
    dt = jnp.exp(log_dt)[:, None]
    j = jnp.arange(S5_CHUNK + 1, dtype=F32)[:, None, None]
    mag = jnp.exp(j * (lam_re * dt)[None])
    ang = j * (lam_im * dt)[None]
    pw_re, pw_im = mag * jnp.cos(ang), mag * jnp.sin(ang)
    num_re, num_im = pw_re[1] - 1.0, pw_im[1]
    den = lam_re * lam_re + lam_im * lam_im
    f_re = (num_re * lam_re + num_im * lam_im) / den
    f_im = (num_im * lam_re - num_re * lam_im) / den
    bb_re = f_re[..., None] * b_re - f_im[..., None] * b_im
    bb_im = f_re[..., None] * b_im + f_im[..., None] * b_re
    jr = (S5_CHUNK - 1) - jnp.arange(S5_CHUNK, dtype=F32)[:, None, None]
    mag_r = jnp.exp(jr * (lam_re * dt)[None])
    ang_r = jr * (lam_im * dt)[None]
    rev_re, rev_im = mag_r * jnp.cos(ang_r), mag_r * jnp.sin(ang_r)
    slabbed = lambda a: a.reshape(a.shape[0], nslab, gpc, nstate).transpose(1, 0, 2, 3)
    rv_re, rv_im = slabbed(rev_re)[:, :, :, None, :], slabbed(rev_im)[:, :, :, None, :]
    bt_re = bb_re.transpose(0, 2, 1).reshape(nslab, 1, gpc, S5_GROUP, nstate)
    bt_im = bb_im.transpose(0, 2, 1).reshape(nslab, 1, gpc, S5_GROUP, nstate)
    vw = jnp.stack([rv_re * bt_re - rv_im * bt_im, rv_re * bt_im + rv_im * bt_re], axis=2)
    vw = vw.reshape(nslab, 2 * S5_CHUNK, LANES, nstate)
    pc_re, pc_im = slabbed(pw_re)[:, :, None, :, :], slabbed(pw_im)[:, :, None, :, :]
    ct_re = c_re.reshape(nslab, gpc, S5_GROUP, nstate).transpose(0, 2, 1, 3)[:, None]
    ct_im = c_im.reshape(nslab, gpc, S5_GROUP, nstate).transpose(0, 2, 1, 3)[:, None]
    mw = jnp.stack([ct_re * pc_re - ct_im * pc_im, -(ct_re * pc_im + ct_im * pc_re)], axis=2)
    mw = mw.reshape(nslab, 2 * (S5_CHUNK + 1), S5_GROUP, gpc * nstate)
    half = gpc * nstate // LANES
    a_pack = jnp.concatenate([pw_re[S5_CHUNK].reshape(nslab, half, LANES),
                              pw_im[S5_CHUNK].reshape(nslab, half, LANES)], axis=1)
    return vw, mw, a_pack


def _s5_layer(x, ln, operators, d_skip, w_glu, b_glu, *, nseq=4):
    vw, mw, a_pack = operators
    hp = _s5_norm(x, ln)
    yp = _s5_conv(hp, vw, mw, a_pack, d_skip, nseq=min(nseq, x.shape[0]))
    return _s5_glu(yp, x, w_glu, b_glu)


def _dense_ffn_kernel(x_ref, g_ref, wg_ref, wu_ref, wd_ref, o_ref):
    xf = x_ref[...]
    h = _rms(xf, g_ref[...]).astype(BF16)
    act = (_silu(_bdot(h, wg_ref[...])) * _bdot(h, wu_ref[...])).astype(BF16)
    o_ref[...] = xf + _bdot(act, wd_ref[...])


def _dense_ffn_layer(x, ln, w_gate_up, w_down, *, tm=512):
    bsz, seqlen, d = x.shape
    ntok = bsz * seqlen
    hidden = w_down.shape[0]
    tm = min(tm, ntok)
    wgu = w_gate_up.astype(BF16)
    out = pl.pallas_call(
        _dense_ffn_kernel,
        out_shape=jax.ShapeDtypeStruct((ntok, d), F32),
        grid=(ntok // tm,),
        in_specs=[pl.BlockSpec((tm, d), lambda i: (i, 0)),
                  pl.BlockSpec((1, d), lambda i: (0, 0)),
                  _resident((d, hidden), lambda i: (0, 0)),
                  _resident((d, hidden), lambda i: (0, 1)),
                  _resident((hidden, d), lambda i: (0, 0))],
        out_specs=pl.BlockSpec((tm, d), lambda i: (i, 0)),
        compiler_params=_params(("parallel",), 56),
        name="dense_ffn",
    )(x.reshape(ntok, d), ln.reshape(1, d), wgu, wgu, w_down.astype(BF16))
    return out.reshape(bsz, seqlen, d)


def _log_sigmoid(z):
    return jnp.minimum(z, 0.0) - jnp.log(1.0 + jnp.exp(-jnp.abs(z)))


def _gla_kernel(x_ref, ln_ref, wm_ref, wgl_ref, wg2_ref, bg2_ref, gn_ref, wo_ref, o_ref, st_ref,
                *, tq, dk, dv, heads):
    hdk, hdv = dk // heads, dv // heads
    chunk = GLA_CHUNK
    nt = (((1,), (1,)), ((), ()))
    tn = (((0,), (0,)), ((), ()))

    @pl.when(pl.program_id(1) == 0)
    def _():
        st_ref[...] = jnp.zeros_like(st_ref)

    xf = x_ref[0]
    h = _rms(xf, ln_ref[...]).astype(BF16)
    proj = _bdot(h, wm_ref[...])
    glow = _bdot(h, wgl_ref[...]).astype(BF16)
    la = _log_sigmoid(_bdot(glow, wg2_ref[...]) + bg2_ref[...]) * (1.0 / GLA_GATE_NORM)
    row = lax.broadcasted_iota(jnp.int32, (chunk, chunk), 0)
    col = lax.broadcasted_iota(jnp.int32, (chunk, chunk), 1)
    causal = row >= col
    tri = jnp.where(causal, 1.0, 0.0).astype(BF16)
    scale = hdk ** -0.5
    outs = []
    for c in range(tq // chunk):
        r0 = c * chunk
        la_c = la[r0:r0 + chunk, :]
        la_hi = la_c.astype(BF16)
        la_lo = (la_c - la_hi.astype(F32)).astype(BF16)
        gcum_all = _bdot(tri, la_hi) + _bdot(tri, la_lo)
        head_out = []
        for hd in range(heads):
            gcum = gcum_all[:, hd * hdk:(hd + 1) * hdk]
            g_last = gcum[chunk - 1:chunk, :]
            q_c = proj[r0:r0 + chunk, hd * hdk:(hd + 1) * hdk] * scale
            k_c = proj[r0:r0 + chunk, dk + hd * hdk:dk + (hd + 1) * hdk]
            v_c = proj[r0:r0 + chunk, 2 * dk + hd * hdv:2 * dk + (hd + 1) * hdv].astype(BF16)
            q_s = (q_c * jnp.exp(gcum)).astype(BF16)
            k_s = (k_c * jnp.exp(-gcum)).astype(BF16)
            k_end = (k_c * jnp.exp(g_last - gcum)).astype(BF16)
            scores = lax.dot_general(q_s, k_s, nt, preferred_element_type=F32)
            scores = jnp.where(causal, scores, 0.0).astype(BF16)
            state_t = st_ref[hd]
            o = _bdot(scores, v_c) + lax.dot_general(q_s, state_t.astype(BF16), nt,
                                                     preferred_element_type=F32)
            kv_t = lax.dot_general(v_c, k_end, tn, preferred_element_type=F32)
            st_ref[hd] = state_t * jnp.exp(g_last) + kv_t
            head_out.append(o * lax.rsqrt(jnp.mean(o * o, axis=-1, keepdims=True) + EPS))
        outs.append(jnp.concatenate(head_out, axis=1))
    o_all = jnp.concatenate(outs, axis=0)
    r = proj[:, 2 * dk + dv:]
    o_all = (o_all * gn_ref[...] * _silu(r)).astype(BF16)
    o_ref[0] = xf + _bdot(o_all, wo_ref[...])


def _gla_layer(x, ln, w_in, w_g2, b_g2, g_norm, w_out, *, tq=256):
    bsz, seqlen, d = x.shape
    dk = w_g2.shape[1]
    dv = w_out.shape[0]
    nmain = 2 * dk + 2 * dv
    tq = min(tq, seqlen)
    w_main = w_in[:, :nmain].astype(BF16)
    w_glow = jnp.pad(w_in[:, nmain:], ((0, 0), (0, LANES - GLA_GATE_RANK))).astype(BF16)
    w_g2p = jnp.pad(w_g2, ((0, LANES - GLA_GATE_RANK), (0, 0))).astype(BF16)
    hdk, hdv = dk // GLA_HEADS, dv // GLA_HEADS
    const = lambda b, t: (0, 0)
    return pl.pallas_call(
        functools.partial(_gla_kernel, tq=tq, dk=dk, dv=dv, heads=GLA_HEADS),
        out_shape=jax.ShapeDtypeStruct(x.shape, F32),
        grid=(bsz, seqlen // tq),
        in_specs=[pl.BlockSpec((1, tq, d), lambda b, t: (b, t, 0)),
                  pl.BlockSpec((1, d), const),
                  _resident((d, nmain), const),
                  _resident((d, LANES), const),
                  _resident((LANES, dk), const),
                  pl.BlockSpec((1, dk), const),
                  pl.BlockSpec((1, dv), const),
                  _resident((dv, d), const)],
        out_specs=pl.BlockSpec((1, tq, d), lambda b, t: (b, t, 0)),
        scratch_shapes=[pltpu.VMEM((GLA_HEADS, hdv, hdk), F32)],
        compiler_params=_params(("parallel", "arbitrary"), 48),
        name="gla",
    )(x, ln.reshape(1, d), w_main, w_glow, w_g2p, b_g2.reshape(1, dk), g_norm.reshape(1, dv),
      w_out.astype(BF16))


LOG2E = math.log2(math.e)
SWA_SLOT_UNROLL = 2


def _swa_kernel(sink_ref, x_ref, ln_ref, wqkv_ref, bqkv_ref, wo_ref, bo_ref, o_ref, k_ref, v_ref,
                bias_ref, q_ref, a_ref, *, tq, q_heads):
    group = q_heads // SWA_KV_HEADS
    blk = SWA_BLOCK
    nt = (((1,), (1,)), ((), ()))
    b = pl.program_id(0)
    t = pl.program_id(1)
    nq = group * LANES

    @pl.when((b == 0) & (t == 0))
    def _():
        qi = lax.broadcasted_iota(jnp.int32, (blk, 2 * blk), 0)
        kj = lax.broadcasted_iota(jnp.int32, (blk, 2 * blk), 1)
        dist = qi + blk - kj
        in_window = (dist >= 0) & (dist < SWA_WINDOW)
        for hq in range(q_heads):
            slope = 2.0 ** (-8.0 * (hq + 1) / q_heads)
            bias_ref[hq] = jnp.where(in_window, -(slope * LOG2E) * dist.astype(F32), MASK_VALUE)

    @pl.when(t == 0)
    def _():
        k_ref[0:blk, :] = jnp.zeros((blk, LANES), BF16)
        v_ref[0:blk, :] = jnp.zeros((blk, LANES), BF16)

    xf = x_ref[0]
    h = _rms(xf, ln_ref[...]).astype(BF16)
    qkv = _bdot(h, wqkv_ref[...]) + bqkv_ref[...]
    for j in range(group):
        q_ref[j] = qkv[:, j * LANES:(j + 1) * LANES].astype(BF16)
    k_ref[blk:blk + tq, :] = qkv[:, nq:nq + LANES].astype(BF16)
    v_ref[blk:blk + tq, :] = qkv[:, nq + LANES:nq + 2 * LANES].astype(BF16)
    kj_row = lax.broadcasted_iota(jnp.int32, (1, 2 * blk), 1)
    no_prev = jnp.where(kj_row < blk, jnp.where(t == 0, MASK_VALUE, 0.0), 0.0)
    low_half = lax.broadcasted_iota(jnp.int32, (1, LANES), 1) < SWA_HEAD_DIM
    halves = (low_half, jnp.logical_not(low_half))

    def slots(jj, carry):
        for u in range(SWA_SLOT_UNROLL):
            j = jj * SWA_SLOT_UNROLL + u
            for i in range(tq // blk):
                r0 = i * blk
                q_slot = q_ref[j, r0:r0 + blk, :]
                outs = []
                for kh in range(SWA_KV_HEADS):
                    hq = kh * group + j
                    sink = sink_ref[hq] * LOG2E
                    q_h = jnp.where(halves[kh], q_slot, jnp.zeros_like(q_slot))
                    s = lax.dot_general(q_h, k_ref[r0:r0 + 2 * blk, :], nt, preferred_element_type=F32) + bias_ref[hq]
                    if i == 0:
                        s = s + no_prev
                    m = jnp.maximum(jnp.max(s, axis=-1, keepdims=True), sink)
                    p = jnp.exp2(s - m)
                    denom = jnp.sum(p, axis=-1, keepdims=True) + jnp.exp2(sink - m)
                    outs.append(_bdot(p.astype(BF16), v_ref[r0:r0 + 2 * blk, :]) * (1.0 / denom))
                a_ref[j, r0:r0 + blk, :] = jnp.where(low_half, outs[0], outs[1]).astype(BF16)
        return carry

    lax.fori_loop(0, group // SWA_SLOT_UNROLL, slots, 0)
    k_ref[0:blk, :] = k_ref[tq:tq + blk, :]
    v_ref[0:blk, :] = v_ref[tq:tq + blk, :]
    o_all = jnp.concatenate([a_ref[j] for j in range(group)], axis=1)
    o_ref[0] = xf + _bdot(o_all, wo_ref[...]) + bo_ref[...]


def _swa_layer(x, ln, w_qkv, b_qkv, sinks, w_out, b_out, *, tq=512):
    bsz, seqlen, d = x.shape
    hd = SWA_HEAD_DIM
    q_heads = sinks.shape[0]
    group = q_heads // SWA_KV_HEADS
    nq = q_heads * hd
    tq = min(tq, seqlen)
    q_scale = hd ** -0.5 * LOG2E
    wq = (w_qkv[:, :nq] * q_scale).reshape(d, SWA_KV_HEADS, group, hd).transpose(0, 2, 1, 3).reshape(d, nq)
    bq = (b_qkv[:nq] * q_scale).reshape(SWA_KV_HEADS, group, hd).transpose(1, 0, 2).reshape(nq)
    w_all = jnp.concatenate([wq, w_qkv[:, nq:]], axis=1).astype(BF16)
    b_all = jnp.concatenate([bq, b_qkv[nq:]]).reshape(1, -1)
    wo = w_out.reshape(SWA_KV_HEADS, group, hd, d).transpose(1, 0, 2, 3).reshape(nq, d).astype(BF16)
    nall = w_all.shape[1]
    const = lambda b, t, s: (0, 0)
    return pl.pallas_call(
        functools.partial(_swa_kernel, tq=tq, q_heads=q_heads),
        out_shape=jax.ShapeDtypeStruct(x.shape, F32),
        grid_spec=pltpu.PrefetchScalarGridSpec(
            num_scalar_prefetch=1,
            grid=(bsz, seqlen // tq),
            in_specs=[pl.BlockSpec((1, tq, d), lambda b, t, s: (b, t, 0)),
                      pl.BlockSpec((1, d), const),
                      _resident((d, nall), const),
                      pl.BlockSpec((1, nall), const),
                      _resident((nq, d), const),
                      pl.BlockSpec((1, d), const)],
            out_specs=pl.BlockSpec((1, tq, d), lambda b, t, s: (b, t, 0)),
            scratch_shapes=[pltpu.VMEM((SWA_BLOCK + tq, LANES), BF16), pltpu.VMEM((SWA_BLOCK + tq, LANES), BF16),
                            pltpu.VMEM((q_heads, SWA_BLOCK, 2 * SWA_BLOCK), F32),
                            pltpu.VMEM((group, tq, LANES), BF16), pltpu.VMEM((group, tq, LANES), BF16)]),
        compiler_params=_params(("arbitrary", "arbitrary"), 48),
        name="swa",
    )(sinks, x, ln.reshape(1, d), w_all, b_all, wo, b_out.reshape(1, d))


def _router_kernel(x_ref, ln_ref, whi_ref, wlo_ref, idx_ref, gate_ref, hp_ref, pos_ref, count_ref, tri_ref):
    nt = (((1,), (1,)), ((), ()))
    h = _rms(x_ref[...], ln_ref[...])
    h_hi = h.astype(BF16)
    h_lo = (h - h_hi.astype(F32)).astype(BF16)
    w_hi, w_lo = whi_ref[...], wlo_ref[...]
    logits = (lax.dot_general(w_hi, h_hi, nt, preferred_element_type=F32)
              + lax.dot_general(w_hi, h_lo, nt, preferred_element_type=F32)
              + lax.dot_general(w_lo, h_hi, nt, preferred_element_type=F32))
    n_exp = logits.shape[0]
    eid = lax.broadcasted_iota(jnp.int32, logits.shape, 0)
    m1 = jnp.max(logits, axis=0, keepdims=True)
    i1 = jnp.min(jnp.where(logits == m1, eid, n_exp), axis=0, keepdims=True)
    rest = jnp.where(eid == i1, -jnp.inf, logits)
    m2 = jnp.max(rest, axis=0, keepdims=True)
    i2 = jnp.min(jnp.where(rest == m2, eid, n_exp), axis=0, keepdims=True)
    e2 = jnp.exp(m2 - m1)
    g1 = 1.0 / (1.0 + e2)
    idx_ref[...] = jnp.concatenate([i1, i2], axis=0)
    gate_ref[...] = jnp.concatenate([g1, e2 * g1], axis=0)
    hp_ref[...] = _pack_bf16_pairs(h)
    tm = logits.shape[1]

    @pl.when(pl.program_id(0) == 0)
    def _():
        count_ref[...] = jnp.zeros_like(count_ref)
        r = lax.broadcasted_iota(jnp.int32, (tm, tm), 0)
        c = lax.broadcasted_iota(jnp.int32, (tm, tm), 1)
        tri_ref[...] = jnp.where(r < c, 1.0, 0.0).astype(BF16)

    pick1 = jnp.where(eid == i1, 1.0, 0.0)
    pick2 = jnp.where(eid == i2, 1.0, 0.0)
    picks = pick1 + pick2
    before = _bdot(picks.astype(BF16), tri_ref[...]) + count_ref[:, 0:1]
    pos_ref[...] = jnp.concatenate([jnp.sum(pick1 * before, axis=0, keepdims=True),
                                    jnp.sum(pick2 * before, axis=0, keepdims=True)], axis=0).astype(jnp.int32)
    count_ref[...] = count_ref[...] + jnp.sum(picks, axis=1, keepdims=True)


def _router(x2, ln, w_router, *, tm=512):
    ntok, d = x2.shape
    n_exp = w_router.shape[1]
    tm = min(tm, ntok)
    wt = w_router.T
    w_hi = wt.astype(BF16)
    w_lo = (wt - w_hi.astype(F32)).astype(BF16)
    return pl.pallas_call(
        _router_kernel,
        out_shape=(jax.ShapeDtypeStruct((TOP_K, ntok), jnp.int32), jax.ShapeDtypeStruct((TOP_K, ntok), F32),
                   jax.ShapeDtypeStruct((ntok, d // 2), jnp.uint32),
                   jax.ShapeDtypeStruct((TOP_K, ntok), jnp.int32), jax.ShapeDtypeStruct((n_exp, LANES), F32)),
        grid=(ntok // tm,),
        in_specs=[pl.BlockSpec((tm, d), lambda i: (i, 0)),
                  pl.BlockSpec((1, d), lambda i: (0, 0)),
                  pl.BlockSpec((n_exp, d), lambda i: (0, 0)),
                  pl.BlockSpec((n_exp, d), lambda i: (0, 0))],
        out_specs=(pl.BlockSpec((TOP_K, tm), lambda i: (0, i)), pl.BlockSpec((TOP_K, tm), lambda i: (0, i)),
                   pl.BlockSpec((tm, d // 2), lambda i: (i, 0)),
                   pl.BlockSpec((TOP_K, tm), lambda i: (0, i)), pl.BlockSpec((n_exp, LANES), lambda i: (0, 0))),
        scratch_shapes=[pltpu.VMEM((tm, tm), BF16)],
        compiler_params=_params(("arbitrary",), 32),
        name="moe_router",
    )(x2, ln.reshape(1, d), w_hi, w_lo)


def _moe_plan(idx, pos, counts, tile):
    n_exp = counts.shape[0]
    nslots = idx.size
    counts = counts[:, 0].astype(jnp.int32)
    ends = jnp.cumsum(counts)
    offs = ends - counts
    experts = jnp.arange(n_exp, dtype=jnp.int32).reshape(n_exp, 1, 1)
    rank = pos + jnp.sum(jnp.where(idx[None] == experts, offs.reshape(n_exp, 1, 1), 0), axis=0)
    n_tiles = nslots // tile
    n_visits = n_tiles + n_exp - 1
    first_tile = offs // tile
    last_tile = (ends - 1) // tile
    nvis = jnp.where(counts > 0, last_tile - first_tile + 1, 0)
    vend = jnp.cumsum(nvis)
    vstart = vend - nvis
    total = vend[-1]
    v = jnp.arange(n_visits, dtype=jnp.int32)
    vc = jnp.minimum(v, total - 1)
    e = jnp.minimum(jnp.sum((vc[:, None] >= vend[None, :]).astype(jnp.int32), axis=1), n_exp - 1)
    sel = (e[:, None] == jnp.arange(n_exp, dtype=jnp.int32)[None, :]).astype(jnp.int32)
    pick = lambda a: jnp.sum(sel * a[None, :], axis=1)
    tile_id = pick(first_tile) + vc - pick(vstart)
    lo = jnp.maximum(pick(offs), tile_id * tile) - tile_id * tile
    hi = jnp.minimum(pick(ends), (tile_id + 1) * tile) - tile_id * tile
    valid = v < total
    lo = jnp.where(valid, lo, 0)
    hi = jnp.where(valid, hi, 0)
    prev_tile = jnp.concatenate([jnp.full((1,), -1, jnp.int32), tile_id[:-1]])
    first = (valid & (tile_id != prev_tile)).astype(jnp.int32)
    next_tile = jnp.concatenate([tile_id[1:], jnp.full((1,), -1, jnp.int32)])
    last = (valid & ((tile_id != next_tile) | (v == total - 1))).astype(jnp.int32)
    meta = jnp.stack([tile_id, e, lo, hi, first, last]).astype(jnp.int32)
    return rank.astype(jnp.int32), meta


def _pack_bf16_pairs(h):
    half = h.shape[1] // 2
    bits = lax.bitcast_convert_type(h.astype(BF16).astype(F32), jnp.uint32)
    return (bits[:, half:] & jnp.uint32(0xFFFF0000)) | (bits[:, :half] >> 16)


def _unpack_pairs_f32(u):
    lo = lax.bitcast_convert_type(u << 16, F32)
    hi = lax.bitcast_convert_type(u & jnp.uint32(0xFFFF0000), F32)
    return jnp.concatenate([lo, hi], axis=1)


def _unpack_bf16_pairs(u):
    return _unpack_pairs_f32(u).astype(BF16)


SC_CORES = 2
SC_SUBCORES = 16
SC_INDEX_WINDOW = 128


def _sc_mesh():
    return plsc.VectorSubcoreMesh(core_axis_name="c", subcore_axis_name="s")


def _sc_worker_id():
    return lax.axis_index("c") * SC_SUBCORES + lax.axis_index("s")


def _sc_scatter_rows(src, rank, nrows):
    ntok, width = src.shape
    win = SC_INDEX_WINDOW
    per = ntok // (SC_CORES * SC_SUBCORES)

    @pl.kernel(out_type=jax.ShapeDtypeStruct((nrows, width), src.dtype), mesh=_sc_mesh(),
               scratch_types=[pltpu.VMEM((1, win), jnp.int32)] * TOP_K + [pltpu.VMEM((win, width), src.dtype)],
               name="moe_dispatch_sc")
    def scatter(src_hbm, rank_hbm, o_hbm, *scratch):
        idx_vmem, buf = scratch[:TOP_K], scratch[TOP_K]
        wid = _sc_worker_id()

        @pl.loop(0, per // win)
        def _(blk):
            base = wid * per + blk * win
            for k in range(TOP_K):
                pltpu.sync_copy(rank_hbm.at[pl.ds(k, 1), pl.ds(base, win)], idx_vmem[k])
            pltpu.sync_copy(src_hbm.at[pl.ds(base, win)], buf)
            for k in range(TOP_K):
                pltpu.sync_copy(buf, o_hbm.at[idx_vmem[k].at[0]])

    return scatter(src, rank)


def _sc_gather_rows(src, idx, *, sub=32):
    n = idx.shape[0]
    width = src.shape[1]
    win = SC_INDEX_WINDOW
    per = n // (SC_CORES * SC_SUBCORES)
    nsub = win // sub

    @pl.kernel(out_type=jax.ShapeDtypeStruct((n, width), src.dtype), mesh=_sc_mesh(),
               scratch_types=[pltpu.VMEM((1, win), jnp.int32)] + [pltpu.VMEM((sub, width), src.dtype)] * 2
               + [pltpu.SemaphoreType.DMA] * 4,
               name="moe_gather_sc")
    def gather(src_hbm, idx_hbm, o_hbm, i_vmem, buf0, buf1, g0, g1, w0, w1):
        bufs, gsem, wsem = (buf0, buf1), (g0, g1), (w0, w1)
        wid = _sc_worker_id()

        @pl.loop(0, per // win)
        def _(blk):
            base = wid * per + blk * win
            pltpu.sync_copy(idx_hbm.at[:, pl.ds(base, win)], i_vmem)
            gathers = [pltpu.make_async_copy(src_hbm.at[i_vmem.at[0, pl.ds(sub * j, sub)]], bufs[j % 2], gsem[j % 2])
                       for j in range(nsub)]
            writes = [pltpu.make_async_copy(bufs[j % 2], o_hbm.at[pl.ds(base + sub * j, sub)], wsem[j % 2])
                      for j in range(nsub)]
            gathers[0].start()
            for j in range(nsub):
                if j + 1 < nsub:
                    if j >= 1:
                        writes[j - 1].wait()
                    gathers[j + 1].start()
                gathers[j].wait()
                writes[j].start()
            writes[nsub - 2].wait()
            writes[nsub - 1].wait()

    return gather(src, idx.reshape(1, n))


MXU_N = 256


def _expert_kernel(meta_ref, x_ref, wg_ref, wu_ref, wd_ref, o_ref, acc_ref, xb_ref, act_ref, wgb_ref, wub_ref,
                   wdb_ref, *, ts):
    v = pl.program_id(0)
    hc = pl.program_id(1)
    lo, hi, first, last = meta_ref[2, v], meta_ref[3, v], meta_ref[4, v], meta_ref[5, v]
    tile, d = acc_ref.shape
    nsub = tile // ts
    th = wgb_ref.shape[1]
    wide = (hi - lo) * 2 > tile

    @pl.when(hc == 0)
    def _():
        for sub in range(nsub):
            xb_ref[sub * ts:(sub + 1) * ts, :] = _unpack_bf16_pairs(x_ref[sub * ts:(sub + 1) * ts, :])

    @pl.when((first == 1) & (hc == 0))
    def _():
        acc_ref[...] = jnp.zeros_like(acc_ref)

    @pl.when(wide)
    def _():
        rows = lax.broadcasted_iota(jnp.int32, (tile, 1), 0)
        mine = (rows >= lo) & (rows < hi)
        for n in range(th // MXU_N):
            cols = slice(n * MXU_N, (n + 1) * MXU_N)
            gate = _bdot(xb_ref[...], wg_ref[0, :, cols].astype(BF16))
            up = _bdot(xb_ref[...], wu_ref[0, :, cols].astype(BF16))
            act_ref[:, cols] = (_silu(gate) * up).astype(BF16)
        for n in range(d // MXU_N):
            cols = slice(n * MXU_N, (n + 1) * MXU_N)
            acc_ref[:, cols] += jnp.where(mine, _bdot(act_ref[...], wd_ref[0, :, cols].astype(BF16)), 0.0)

    @pl.when(jnp.logical_not(wide) & (hi > lo))
    def _():
        wgb_ref[...] = wg_ref[0].astype(BF16)
        wub_ref[...] = wu_ref[0].astype(BF16)
        wdb_ref[...] = wd_ref[0].astype(BF16)
        for sub in range(nsub):
            r0 = sub * ts

            @pl.when((lo < r0 + ts) & (hi > r0))
            def _():
                xs = xb_ref[r0:r0 + ts, :]
                act = (_silu(_bdot(xs, wgb_ref[...])) * _bdot(xs, wub_ref[...])).astype(BF16)
                y = _bdot(act, wdb_ref[...])
                rows = r0 + lax.broadcasted_iota(jnp.int32, (ts, 1), 0)
                acc_ref[r0:r0 + ts, :] += jnp.where((rows >= lo) & (rows < hi), y, 0.0)

    @pl.when((last == 1) & (hc == pl.num_programs(1) - 1))
    def _():
        for sub in range(nsub):
            o_ref[sub * ts:(sub + 1) * ts, :] = _pack_bf16_pairs(acc_ref[sub * ts:(sub + 1) * ts, :])


def _experts(xg, meta, w_gate_up, w_down, *, tile, th=512, ts=512):
    nrows = xg.shape[0]
    n_exp, hidden, d = w_down.shape
    n_hc = hidden // th
    ts = min(ts, tile)
    wgu = w_gate_up
    return pl.pallas_call(
        functools.partial(_expert_kernel, ts=ts),
        out_shape=jax.ShapeDtypeStruct((nrows, d // 2), jnp.uint32),
        grid_spec=pltpu.PrefetchScalarGridSpec(
            num_scalar_prefetch=1,
            grid=(meta.shape[1], n_hc),
            in_specs=[pl.BlockSpec((tile, d // 2), lambda v, c, m: (m[0, v], 0)),
                      pl.BlockSpec((1, d, th), lambda v, c, m: (m[1, v], 0, c)),
                      pl.BlockSpec((1, d, th), lambda v, c, m: (m[1, v], 0, c + n_hc)),
                      pl.BlockSpec((1, th, d), lambda v, c, m: (m[1, v], c, 0))],
            out_specs=pl.BlockSpec((tile, d // 2), lambda v, c, m: (m[0, v], 0)),
            scratch_shapes=[pltpu.VMEM((tile, d), F32), pltpu.VMEM((tile, d), BF16), pltpu.VMEM((tile, th), BF16),
                            pltpu.VMEM((d, th), BF16), pltpu.VMEM((d, th), BF16), pltpu.VMEM((th, d), BF16)]),
        compiler_params=_params(("arbitrary", "arbitrary"), 56),
        name="moe_experts",
    )(meta, xg, wgu, wgu, w_down)


def _combine_kernel(x_ref, gate_ref, fg_ref, y0_ref, y1_ref, o_ref, *, final_norm):
    g = gate_ref[...]
    out = x_ref[...] + g[:, 0:1] * _unpack_pairs_f32(y0_ref[0]) + g[:, 1:2] * _unpack_pairs_f32(y1_ref[0])
    if final_norm:
        out = _rms(out, fg_ref[...])
    o_ref[...] = out


def _combine(x2, gates_t, yk, final_gain, *, tm=512):
    ntok, d = x2.shape
    tm = min(tm, ntok)
    final_norm = final_gain is not None
    fg = (final_gain if final_norm else jnp.ones((d,), F32)).reshape(1, d)
    return pl.pallas_call(
        functools.partial(_combine_kernel, final_norm=final_norm),
        out_shape=jax.ShapeDtypeStruct((ntok, d), F32),
        grid=(ntok // tm,),
        in_specs=[pl.BlockSpec((tm, d), lambda i: (i, 0)),
                  pl.BlockSpec((tm, TOP_K), lambda i: (i, 0)),
                  pl.BlockSpec((1, d), lambda i: (0, 0)),
                  pl.BlockSpec((1, tm, d // 2), lambda i: (0, i, 0)),
                  pl.BlockSpec((1, tm, d // 2), lambda i: (1, i, 0))],
        out_specs=pl.BlockSpec((tm, d), lambda i: (i, 0)),
        compiler_params=_params(("parallel",), 40),
        name="moe_combine",
    )(x2, gates_t, fg, yk, yk)


def _moe_routed(x, ln, w_router, w_gate_up, w_down, *, tile=2048):
    bsz, seqlen, d = x.shape
    ntok = bsz * seqlen
    tile = min(tile, TOP_K * ntok)
    x2 = x.reshape(ntok, d)
    idx, gates, hp, pos, counts = _router(x2, ln, w_router)
    rank, meta = _moe_plan(idx, pos, counts, tile)
    xg = _sc_scatter_rows(hp, rank, TOP_K * ntok)
    y = _experts(xg, meta, w_gate_up, w_down, tile=tile)
    yk = _sc_gather_rows(y, rank.reshape(-1), sub=64).reshape(TOP_K, ntok, d // 2)
    return gates.T, yk


def _moe_layer(x, ln, w_router, w_gate_up, w_down, *, final_gain=None, tile=2048):
    bsz, seqlen, d = x.shape
    gates_t, yk = _moe_routed(x, ln, w_router, w_gate_up, w_down, tile=tile)
    out = _combine(x.reshape(bsz * seqlen, d), gates_t, yk, final_gain)
    return out.reshape(bsz, seqlen, d)


def kernel(x, l0_ln1, l0_s5_lam_re, l0_s5_lam_im, l0_s5_log_dt, l0_s5_b_re, l0_s5_b_im, l0_s5_c_re, l0_s5_c_im, l0_s5_d, l0_s5_w_glu, l0_s5_b_glu, l0_ln2, l0_ffn_w_gate_up, l0_ffn_w_down, l1_ln1, l1_gla_w_in, l1_gla_w_g2, l1_gla_b_g2, l1_gla_norm, l1_gla_w_out, l1_ln2, l1_moe_router, l1_moe_w_gate_up, l1_moe_w_down, l2_ln1, l2_swa_w_qkv, l2_swa_b_qkv, l2_swa_sinks, l2_swa_w_out, l2_swa_b_out, l2_ln2, l2_ffn_w_gate_up, l2_ffn_w_down, l3_ln1, l3_s5_lam_re, l3_s5_lam_im, l3_s5_log_dt, l3_s5_b_re, l3_s5_b_im, l3_s5_c_re, l3_s5_c_im, l3_s5_d, l3_s5_w_glu, l3_s5_b_glu, l3_ln2, l3_moe_router, l3_moe_w_gate_up, l3_moe_w_down, ln_f):
    s5_params = ((l0_s5_lam_re, l0_s5_lam_im, l0_s5_log_dt, l0_s5_b_re, l0_s5_b_im, l0_s5_c_re, l0_s5_c_im),
                 (l3_s5_lam_re, l3_s5_lam_im, l3_s5_log_dt, l3_s5_b_re, l3_s5_b_im, l3_s5_c_re, l3_s5_c_im))
    s5_ops = jax.vmap(_s5_operators)(*(jnp.stack(pair) for pair in zip(*s5_params)))
    x = _s5_layer(x, l0_ln1, tuple(a[0] for a in s5_ops), l0_s5_d, l0_s5_w_glu, l0_s5_b_glu)
    x = _dense_ffn_layer(x, l0_ln2, l0_ffn_w_gate_up, l0_ffn_w_down)
    x = _gla_layer(x, l1_ln1, l1_gla_w_in, l1_gla_w_g2, l1_gla_b_g2, l1_gla_norm, l1_gla_w_out)
    x = _moe_layer(x, l1_ln2, l1_moe_router, l1_moe_w_gate_up, l1_moe_w_down)
    x = _swa_layer(x, l2_ln1, l2_swa_w_qkv, l2_swa_b_qkv, l2_swa_sinks, l2_swa_w_out, l2_swa_b_out)
    x = _dense_ffn_layer(x, l2_ln2, l2_ffn_w_gate_up, l2_ffn_w_down)
    x = _s5_layer(x, l3_ln1, tuple(a[1] for a in s5_ops), l3_s5_d, l3_s5_w_glu, l3_s5_b_glu)
    return _moe_layer(x, l3_ln2, l3_moe_router, l3_moe_w_gate_up, l3_moe_w_down, final_gain=ln_f)
```

```python
import functools
import math

import jax
import jax.numpy as jnp
from jax import lax
from jax.experimental import pallas as pl
from jax.experimental.pallas import tpu as pltpu
from jax.experimental.pallas import tpu_sc as plsc

F32 = jnp.float32
BF16 = jnp.bfloat16
EPS = 1e-6
LANES = 128
MIB = 1 << 20

S5_GROUP = 16
S5_STATE = 64
S5_CHUNK = 16
S5_SLAB_GROUPS = LANES // S5_GROUP
S5_PITCH_PAD = 8
S5_SCAN_UNROLL = 8
S5_CAUSAL_BANDS = 8

GLA_HEADS = 4
GLA_GATE_RANK = 16
GLA_GATE_NORM = 16.0
GLA_CHUNK = 64

SWA_HEAD_DIM = 64
SWA_KV_HEADS = 2
SWA_WINDOW = 128
SWA_BLOCK = 128
MASK_VALUE = -1e30

TOP_K = 2


def _params(semantics, vmem_mib):
    return pltpu.CompilerParams(dimension_semantics=semantics, vmem_limit_bytes=vmem_mib * MIB)


def _resident(block_shape, index_map):
    return pl.BlockSpec(block_shape, index_map, pipeline_mode=pl.Buffered(1))


def _rms(xf, gain):
    return xf * lax.rsqrt(jnp.mean(xf * xf, axis=-1, keepdims=True) + EPS) * gain


def _gelu_tanh(x):
    return 0.5 * x * (1.0 + jnp.tanh(math.sqrt(2.0 / math.pi) * (x + 0.044715 * (x * x * x))))


def _silu(x):
    return x * jax.nn.sigmoid(x)


def _bdot(a, b):
    return jnp.dot(a, b, preferred_element_type=F32)


S5_ROW_TILE = 512


def _s5_norm_kernel(x_ref, g_ref, o_ref, scr_ref, *, nloc):
    h = _rms(x_ref[0], g_ref[...])
    nslab = scr_ref.shape[0]
    for c in range(nslab):
        scr_ref[c] = h[:, c * LANES:(c + 1) * LANES]
    for s in range(S5_CHUNK):
        rows = pl.ds(s, nloc, stride=S5_CHUNK)
        o_ref[0, s] = jnp.concatenate([scr_ref[c, rows, :] for c in range(nslab)], axis=1).astype(o_ref.dtype)


def _s5_norm(x, gain):
    bsz, seqlen, d = x.shape
    nch = seqlen // S5_CHUNK
    tm = min(S5_ROW_TILE, seqlen)
    nloc = tm // S5_CHUNK
    return pl.pallas_call(
        functools.partial(_s5_norm_kernel, nloc=nloc),
        out_shape=jax.ShapeDtypeStruct((bsz, S5_CHUNK, nch, d), BF16),
        grid=(bsz, seqlen // tm),
        in_specs=[pl.BlockSpec((1, tm, d), lambda b, i: (b, i, 0)),
                  pl.BlockSpec((1, d), lambda b, i: (0, 0))],
        out_specs=pl.BlockSpec((1, S5_CHUNK, nloc, d), lambda b, i: (b, 0, i, 0)),
        scratch_shapes=[pltpu.VMEM((d // LANES, tm, LANES), F32)],
        compiler_params=_params(("parallel", "parallel"), 32),
        name="s5_norm",
    )(x, gain.reshape(1, d))


def _tiling_matrix(rows, cols):
    p = lax.broadcasted_iota(jnp.int32, (rows, cols), 0)
    c = lax.broadcasted_iota(jnp.int32, (rows, cols), 1)
    return jnp.where(c % rows == p, 1.0, 0.0).astype(BF16)


def _same_group(shape, row_group, col_group):
    r = lax.broadcasted_iota(jnp.int32, shape, 0)
    c = lax.broadcasted_iota(jnp.int32, shape, 1)
    return (r // row_group) == (c // col_group)


def _s5_build_operators(vw_ref, mw_ref, toep_ref, win_ref, wout_ref):
    tn = (((0,), (0,)), ((), ()))
    nstate = vw_ref.shape[-1]
    half = S5_SLAB_GROUPS * nstate
    rep_ch = _tiling_matrix(S5_GROUP, LANES)
    rep_st = _tiling_matrix(nstate, half)
    diag_in = _same_group((LANES, half), S5_GROUP, nstate)
    diag_out = _same_group((half, LANES), nstate, S5_GROUP)

    def out_block(q, r):
        e = lax.dot_general(mw_ref[0, 2 * q + r].astype(BF16), rep_ch, tn, preferred_element_type=F32)
        return jnp.where(diag_out, e, 0.0).astype(BF16)

    for a in range(S5_CHUNK):
        for r in range(2):
            e = _bdot(vw_ref[0, 2 * a + r].astype(BF16), rep_st)
            win_ref[a * LANES:(a + 1) * LANES, r * half:(r + 1) * half] = jnp.where(diag_in, e, 0.0).astype(BF16)
            wout_ref[r * half:(r + 1) * half, a * LANES:(a + 1) * LANES] = out_block(a + 1, r)
    b_bar = win_ref[(S5_CHUNK - 1) * LANES:S5_CHUNK * LANES, :]
    taps = [_bdot(b_bar, jnp.concatenate([out_block(0, 0), out_block(0, 1)], axis=0)).astype(BF16)]
    for j in range(1, S5_CHUNK):
        taps.append(_bdot(b_bar, wout_ref[:, (j - 1) * LANES:j * LANES]).astype(BF16))
    zero = jnp.zeros((LANES, LANES), BF16)
    for a in range(S5_CHUNK):
        for b in range(S5_CHUNK):
            toep_ref[a * LANES:(a + 1) * LANES, b * LANES:(b + 1) * LANES] = taps[b - a] if b >= a else zero


def _s5_conv_kernel(h_ref, vw_ref, mw_ref, a_ref, d_ref, o_ref, s_ref, toep_ref, win_ref, wout_ref,
                    *, nseq, nch):
    pitch = nch + S5_PITCH_PAD
    nl = a_ref.shape[1] // 2

    @pl.when(pl.program_id(1) == 0)
    def _():
        _s5_build_operators(vw_ref, mw_ref, toep_ref, win_ref, wout_ref)

    lhs = jnp.concatenate(
        [jnp.concatenate([h_ref[bl, s] for s in range(S5_CHUNK)], axis=1) for bl in range(nseq)], axis=0)
    bc = _bdot(lhs, win_ref[...])
    for bl in range(nseq):
        for j in range(2 * nl):
            s_ref[j, bl * pitch:bl * pitch + nch, :] = bc[bl * nch:(bl + 1) * nch, j * LANES:(j + 1) * LANES]
    a_re = [a_ref[0, j:j + 1, :] for j in range(nl)]
    a_im = [a_ref[0, nl + j:nl + j + 1, :] for j in range(nl)]

    def step(n, carry):
        p_re, p_im = carry
        rows = pl.ds(n, nseq, stride=pitch)
        n_re, n_im = [], []
        for j in range(nl):
            c_re = s_ref[j, rows, :]
            c_im = s_ref[nl + j, rows, :]
            s_ref[j, rows, :] = p_re[j]
            s_ref[nl + j, rows, :] = p_im[j]
            n_re.append(a_re[j] * p_re[j] - a_im[j] * p_im[j] + c_re)
            n_im.append(a_re[j] * p_im[j] + a_im[j] * p_re[j] + c_im)
        return tuple(n_re), tuple(n_im)

    def steps(m, carry):
        for u in range(S5_SCAN_UNROLL):
            carry = step(m * S5_SCAN_UNROLL + u, carry)
        return carry

    zeros = tuple(jnp.zeros((nseq, LANES), F32) for _ in range(nl))
    lax.fori_loop(0, nch // S5_SCAN_UNROLL, steps, (zeros, zeros))
    x_prev = jnp.concatenate(
        [jnp.concatenate([s_ref[j, bl * pitch:bl * pitch + nch, :] for j in range(2 * nl)], axis=1)
         for bl in range(nseq)], axis=0).astype(BF16)
    band = S5_CHUNK // S5_CAUSAL_BANDS
    y_bands = []
    for q in range(S5_CAUSAL_BANDS):
        kk = (q + 1) * band * LANES
        cols = slice(q * band * LANES, (q + 1) * band * LANES)
        y_bands.append(_bdot(lhs[:, :kk], toep_ref[:kk, cols]) + _bdot(x_prev, wout_ref[:, cols]))
    dskip = d_ref[0]
    for bl in range(nseq):
        for s in range(S5_CHUNK):
            ys = y_bands[s // band][bl * nch:(bl + 1) * nch, (s % band) * LANES:(s % band + 1) * LANES]
            ys = ys + dskip * h_ref[bl, s].astype(F32)
            o_ref[bl, s] = _gelu_tanh(ys).astype(o_ref.dtype)


def _s5_conv(hp, vw, mw, a_pack, d_skip, *, nseq):
    bsz, _, nch, d = hp.shape
    nslab = d // LANES
    kdim = S5_CHUNK * LANES
    sdim = a_pack.shape[1] * LANES
    blk4 = lambda a: pl.BlockSpec((1,) + a.shape[1:], lambda c, b: (c, 0, 0, 0))
    return pl.pallas_call(
        functools.partial(_s5_conv_kernel, nseq=nseq, nch=nch),
        out_shape=jax.ShapeDtypeStruct(hp.shape, BF16),
        grid=(nslab, bsz // nseq),
        in_specs=[pl.BlockSpec((nseq, S5_CHUNK, nch, LANES), lambda c, b: (b, 0, 0, c)),
                  blk4(vw), blk4(mw),
                  pl.BlockSpec((1, sdim // LANES, LANES), lambda c, b: (c, 0, 0)),
                  pl.BlockSpec((1, 1, LANES), lambda c, b: (c, 0, 0))],
        out_specs=pl.BlockSpec((nseq, S5_CHUNK, nch, LANES), lambda c, b: (b, 0, 0, c)),
        scratch_shapes=[pltpu.VMEM((sdim // LANES, nseq * (nch + S5_PITCH_PAD), LANES), F32),
                        pltpu.VMEM((kdim, kdim), BF16),
                        pltpu.VMEM((kdim, sdim), BF16),
                        pltpu.VMEM((sdim, kdim), BF16)],
        compiler_params=_params(("arbitrary", "arbitrary"), 56),
        name="s5_conv",
    )(hp, vw, mw, a_pack, d_skip.reshape(nslab, 1, LANES))


def _s5_glu_kernel(y_ref, x_ref, w_ref, b_ref, o_ref, scr_ref, *, nloc):
    nslab = scr_ref.shape[0]
    y = jnp.concatenate([y_ref[0, s] for s in range(S5_CHUNK)], axis=0)
    u = y.astype(F32) * jax.nn.sigmoid(_bdot(y, w_ref[...]) + b_ref[...])
    for s in range(S5_CHUNK):
        rows = pl.ds(s, nloc, stride=S5_CHUNK)
        for c in range(nslab):
            scr_ref[c, rows, :] = u[s * nloc:(s + 1) * nloc, c * LANES:(c + 1) * LANES]
    o_ref[0] = x_ref[0] + jnp.concatenate([scr_ref[c] for c in range(nslab)], axis=1)


def _s5_glu(yp, x, w_glu, b_glu):
    bsz, seqlen, d = x.shape
    tm = min(S5_ROW_TILE, seqlen)
    nloc = tm // S5_CHUNK
    return pl.pallas_call(
        functools.partial(_s5_glu_kernel, nloc=nloc),
        out_shape=jax.ShapeDtypeStruct(x.shape, F32),
        grid=(bsz, seqlen // tm),
        in_specs=[pl.BlockSpec((1, S5_CHUNK, nloc, d), lambda b, i: (b, 0, i, 0)),
                  pl.BlockSpec((1, tm, d), lambda b, i: (b, i, 0)),
                  _resident((d, d), lambda b, i: (0, 0)),
                  pl.BlockSpec((1, d), lambda b, i: (0, 0))],
        out_specs=pl.BlockSpec((1, tm, d), lambda b, i: (b, i, 0)),
        scratch_shapes=[pltpu.VMEM((d // LANES, tm, LANES), F32)],
        compiler_params=_params(("parallel", "parallel"), 40),
        name="s5_glu",
    )(yp, x, w_glu.astype(BF16), b_glu.reshape(1, d))


def _s5_operators(lam_re, lam_im, log_dt, b_re, b_im, c_re, c_im):
    ngroups, nstate = lam_re.shape
    gpc = S5_SLAB_GROUPS
    nslab = ngroups // gpc
    dt = jnp.exp(log_dt)[:, None]
    j = jnp.arange(S5_CHUNK + 1, dtype=F32)[:, None, None]
    mag = jnp.exp(j * (lam_re * dt)[None])
    ang = j * (lam_im * dt)[None]
    pw_re, pw_im = mag * jnp.cos(ang), mag * jnp.sin(ang)
    num_re, num_im = pw_re[1] - 1.0, pw_im[1]
    den = lam_re * lam_re + lam_im * lam_im
    f_re = (num_re * lam_re + num_im * lam_im) / den
    f_im = (num_im * lam_re - num_re * lam_im) / den
    bb_re = f_re[..., None] * b_re - f_im[..., None] * b_im
    bb_im = f_re[..., None] * b_im + f_im[..., None] * b_re
    jr = (S5_CHUNK - 1) - jnp.arange(S5_CHUNK, dtype=F32)[:, None, None]
    mag_r = jnp.exp(jr * (lam_re * dt)[None])
    ang_r = jr * (lam_im * dt)[None]
    rev_re, rev_im = mag_r * jnp.cos(ang_r), mag_r * jnp.sin(ang_r)
    slabbed = lambda a: a.reshape(a.shape[0], nslab, gpc, nstate).transpose(1, 0, 2, 3)
    rv_re, rv_im = slabbed(rev_re)[:, :, :, None, :], slabbed(rev_im)[:, :, :, None, :]
    bt_re = bb_re.transpose(0, 2, 1).reshape(nslab, 1, gpc, S5_GROUP, nstate)
    bt_im = bb_im.transpose(0, 2, 1).reshape(nslab, 1, gpc, S5_GROUP, nstate)
    vw = jnp.stack([rv_re * bt_re - rv_im * bt_im, rv_re * bt_im + rv_im * bt_re], axis=2)
    vw = vw.reshape(nslab, 2 * S5_CHUNK, LANES, nstate)
    pc_re, pc_im = slabbed(pw_re)[:, :, None, :, :], slabbed(pw_im)[:, :, None, :, :]
    ct_re = c_re.reshape(nslab, gpc, S5_GROUP, nstate).transpose(0, 2, 1, 3)[:, None]
    ct_im = c_im.reshape(nslab, gpc, S5_GROUP, nstate).transpose(0, 2, 1, 3)[:, None]
    mw = jnp.stack([ct_re * pc_re - ct_im * pc_im, -(ct_re * pc_im + ct_im * pc_re)], axis=2)
    mw = mw.reshape(nslab, 2 * (S5_CHUNK + 1), S5_GROUP, gpc * nstate)
    half = gpc * nstate // LANES
    a_pack = jnp.concatenate([pw_re[S5_CHUNK].reshape(nslab, half, LANES),
                              pw_im[S5_CHUNK].reshape(nslab, half, LANES)], axis=1)
    return vw, mw, a_pack


def _s5_layer(x, ln, operators, d_skip, w_glu, b_glu, *, nseq=4):
    vw, mw, a_pack = operators
    hp = _s5_norm(x, ln)
    yp = _s5_conv(hp, vw, mw, a_pack, d_skip, nseq=min(nseq, x.shape[0]))
    return _s5_glu(yp, x, w_glu, b_glu)


def _dense_ffn_kernel(x_ref, g_ref, wg_ref, wu_ref, wd_ref, o_ref):
    xf = x_ref[...]
    h = _rms(xf, g_ref[...]).astype(BF16)
    act = (_silu(_bdot(h, wg_ref[...])) * _bdot(h, wu_ref[...])).astype(BF16)
    o_ref[...] = xf + _bdot(act, wd_ref[...])


def _dense_ffn_layer(x, ln, w_gate_up, w_down, *, tm=512):
    bsz, seqlen, d = x.shape
    ntok = bsz * seqlen
    hidden = w_down.shape[0]
    tm = min(tm, ntok)
    wgu = w_gate_up.astype(BF16)
    out = pl.pallas_call(
        _dense_ffn_kernel,
        out_shape=jax.ShapeDtypeStruct((ntok, d), F32),
        grid=(ntok // tm,),
        in_specs=[pl.BlockSpec((tm, d), lambda i: (i, 0)),
                  pl.BlockSpec((1, d), lambda i: (0, 0)),
                  _resident((d, hidden), lambda i: (0, 0)),
                  _resident((d, hidden), lambda i: (0, 1)),
                  _resident((hidden, d), lambda i: (0, 0))],
        out_specs=pl.BlockSpec((tm, d), lambda i: (i, 0)),
        compiler_params=_params(("parallel",), 56),
        name="dense_ffn",
    )(x.reshape(ntok, d), ln.reshape(1, d), wgu, wgu, w_down.astype(BF16))
    return out.reshape(bsz, seqlen, d)


def _log_sigmoid(z):
    return jnp.minimum(z, 0.0) - jnp.log(1.0 + jnp.exp(-jnp.abs(z)))


def _gla_kernel(x_ref, ln_ref, wm_ref, wgl_ref, wg2_ref, bg2_ref, gn_ref, wo_ref, o_ref, st_ref,
                *, tq, dk, dv, heads):
    hdk, hdv = dk // heads, dv // heads
    chunk = GLA_CHUNK
    nt = (((1,), (1,)), ((), ()))
    tn = (((0,), (0,)), ((), ()))

    @pl.when(pl.program_id(1) == 0)
    def _():
        st_ref[...] = jnp.zeros_like(st_ref)

    xf = x_ref[0]
    h = _rms(xf, ln_ref[...]).astype(BF16)
    proj = _bdot(h, wm_ref[...])
    glow = _bdot(h, wgl_ref[...]).astype(BF16)
    la = _log_sigmoid(_bdot(glow, wg2_ref[...]) + bg2_ref[...]) * (1.0 / GLA_GATE_NORM)
    row = lax.broadcasted_iota(jnp.int32, (chunk, chunk), 0)
    col = lax.broadcasted_iota(jnp.int32, (chunk, chunk), 1)
    causal = row >= col
    tri = jnp.where(causal, 1.0, 0.0).astype(BF16)
    scale = hdk ** -0.5
    outs = []
    for c in range(tq // chunk):
        r0 = c * chunk
        la_c = la[r0:r0 + chunk, :]
        la_hi = la_c.astype(BF16)
        la_lo = (la_c - la_hi.astype(F32)).astype(BF16)
        gcum_all = _bdot(tri, la_hi) + _bdot(tri, la_lo)
        head_out = []
        for hd in range(heads):
            gcum = gcum_all[:, hd * hdk:(hd + 1) * hdk]
            g_last = gcum[chunk - 1:chunk, :]
            q_c = proj[r0:r0 + chunk, hd * hdk:(hd + 1) * hdk] * scale
            k_c = proj[r0:r0 + chunk, dk + hd * hdk:dk + (hd + 1) * hdk]
            v_c = proj[r0:r0 + chunk, 2 * dk + hd * hdv:2 * dk + (hd + 1) * hdv].astype(BF16)
            q_s = (q_c * jnp.exp(gcum)).astype(BF16)
            k_s = (k_c * jnp.exp(-gcum)).astype(BF16)
            k_end = (k_c * jnp.exp(g_last - gcum)).astype(BF16)
            scores = lax.dot_general(q_s, k_s, nt, preferred_element_type=F32)
            scores = jnp.where(causal, scores, 0.0).astype(BF16)
            state_t = st_ref[hd]
            o = _bdot(scores, v_c) + lax.dot_general(q_s, state_t.astype(BF16), nt,
                                                     preferred_element_type=F32)
            kv_t = lax.dot_general(v_c, k_end, tn, preferred_element_type=F32)
            st_ref[hd] = state_t * jnp.exp(g_last) + kv_t
            head_out.append(o * lax.rsqrt(jnp.mean(o * o, axis=-1, keepdims=True) + EPS))
        outs.append(jnp.concatenate(head_out, axis=1))
    o_all = jnp.concatenate(outs, axis=0)
    r = proj[:, 2 * dk + dv:]
    o_all = (o_all * gn_ref[...] * _silu(r)).astype(BF16)
    o_ref[0] = xf + _bdot(o_all, wo_ref[...])


def _gla_layer(x, ln, w_in, w_g2, b_g2, g_norm, w_out, *, tq=256):
    bsz, seqlen, d = x.shape
    dk = w_g2.shape[1]
    dv = w_out.shape[0]
    nmain = 2 * dk + 2 * dv
    tq = min(tq, seqlen)
    w_main = w_in[:, :nmain].astype(BF16)
    w_glow = jnp.pad(w_in[:, nmain:], ((0, 0), (0, LANES - GLA_GATE_RANK))).astype(BF16)
    w_g2p = jnp.pad(w_g2, ((0, LANES - GLA_GATE_RANK), (0, 0))).astype(BF16)
    hdk, hdv = dk // GLA_HEADS, dv // GLA_HEADS
    const = lambda b, t: (0, 0)
    return pl.pallas_call(
        functools.partial(_gla_kernel, tq=tq, dk=dk, dv=dv, heads=GLA_HEADS),
        out_shape=jax.ShapeDtypeStruct(x.shape, F32),
        grid=(bsz, seqlen // tq),
        in_specs=[pl.BlockSpec((1, tq, d), lambda b, t: (b, t, 0)),
                  pl.BlockSpec((1, d), const),
                  _resident((d, nmain), const),
                  _resident((d, LANES), const),
                  _resident((LANES, dk), const),
                  pl.BlockSpec((1, dk), const),
                  pl.BlockSpec((1, dv), const),
                  _resident((dv, d), const)],
        out_specs=pl.BlockSpec((1, tq, d), lambda b, t: (b, t, 0)),
        scratch_shapes=[pltpu.VMEM((GLA_HEADS, hdv, hdk), F32)],
        compiler_params=_params(("parallel", "arbitrary"), 48),
        name="gla",
    )(x, ln.reshape(1, d), w_main, w_glow, w_g2p, b_g2.reshape(1, dk), g_norm.reshape(1, dv),
      w_out.astype(BF16))


LOG2E = math.log2(math.e)
SWA_SLOT_UNROLL = 4


def _swa_kernel(sink_ref, x_ref, ln_ref, wqkv_ref, bqkv_ref, wo_ref, bo_ref, o_ref, k_ref, v_ref,
                bias_ref, q_ref, a_ref, *, tq, q_heads):
    group = q_heads // SWA_KV_HEADS
    blk = SWA_BLOCK
    nt = (((1,), (1,)), ((), ()))
    b = pl.program_id(0)
    t = pl.program_id(1)
    nq = group * LANES

    @pl.when((b == 0) & (t == 0))
    def _():
        qi = lax.broadcasted_iota(jnp.int32, (blk, 2 * blk), 0)
        kj = lax.broadcasted_iota(jnp.int32, (blk, 2 * blk), 1)
        dist = qi + blk - kj
        in_window = (dist >= 0) & (dist < SWA_WINDOW)
        for hq in range(q_heads):
            slope = 2.0 ** (-8.0 * (hq + 1) / q_heads)
            bias_ref[hq] = jnp.where(in_window, -(slope * LOG2E) * dist.astype(F32), MASK_VALUE)

    @pl.when(t == 0)
    def _():
        k_ref[0:blk, :] = jnp.zeros((blk, LANES), BF16)
        v_ref[0:blk, :] = jnp.zeros((blk, LANES), BF16)

    xf = x_ref[0]
    h = _rms(xf, ln_ref[...]).astype(BF16)
    qkv = _bdot(h, wqkv_ref[...]) + bqkv_ref[...]
    for j in range(group):
        q_ref[j] = qkv[:, j * LANES:(j + 1) * LANES].astype(BF16)
    k_ref[blk:blk + tq, :] = qkv[:, nq:nq + LANES].astype(BF16)
    v_ref[blk:blk + tq, :] = qkv[:, nq + LANES:nq + 2 * LANES].astype(BF16)
    kj_row = lax.broadcasted_iota(jnp.int32, (1, 2 * blk), 1)
    no_prev = jnp.where(kj_row < blk, jnp.where(t == 0, MASK_VALUE, 0.0), 0.0)
    low_half = lax.broadcasted_iota(jnp.int32, (1, LANES), 1) < SWA_HEAD_DIM
    halves = (low_half, jnp.logical_not(low_half))

    def slots(jj, carry):
        for u in range(SWA_SLOT_UNROLL):
            j = jj * SWA_SLOT_UNROLL + u
            for i in range(tq // blk):
                r0 = i * blk
                q_slot = q_ref[j, r0:r0 + blk, :]
                outs = []
                for kh in range(SWA_KV_HEADS):
                    hq = kh * group + j
                    sink = sink_ref[hq] * LOG2E
                    q_h = jnp.where(halves[kh], q_slot, jnp.zeros_like(q_slot))
                    s = lax.dot_general(q_h, k_ref[r0:r0 + 2 * blk, :], nt, preferred_element_type=F32) + bias_ref[hq]
                    if i == 0:
                        s = s + no_prev
                    m = jnp.maximum(jnp.max(s, axis=-1, keepdims=True), sink)
                    p = jnp.exp2(s - m)
                    denom = jnp.sum(p, axis=-1, keepdims=True) + jnp.exp2(sink - m)
                    outs.append(_bdot(p.astype(BF16), v_ref[r0:r0 + 2 * blk, :]) * (1.0 / denom))
                a_ref[j, r0:r0 + blk, :] = jnp.where(low_half, outs[0], outs[1]).astype(BF16)
        return carry

    lax.fori_loop(0, group // SWA_SLOT_UNROLL, slots, 0)
    k_ref[0:blk, :] = k_ref[tq:tq + blk, :]
    v_ref[0:blk, :] = v_ref[tq:tq + blk, :]
    o_all = jnp.concatenate([a_ref[j] for j in range(group)], axis=1)
    o_ref[0] = xf + _bdot(o_all, wo_ref[...]) + bo_ref[...]


def _swa_layer(x, ln, w_qkv, b_qkv, sinks, w_out, b_out, *, tq=512):
    bsz, seqlen, d = x.shape
    hd = SWA_HEAD_DIM
    q_heads = sinks.shape[0]
    group = q_heads // SWA_KV_HEADS
    nq = q_heads * hd
    tq = min(tq, seqlen)
    q_scale = hd ** -0.5 * LOG2E
    wq = (w_qkv[:, :nq] * q_scale).reshape(d, SWA_KV_HEADS, group, hd).transpose(0, 2, 1, 3).reshape(d, nq)
    bq = (b_qkv[:nq] * q_scale).reshape(SWA_KV_HEADS, group, hd).transpose(1, 0, 2).reshape(nq)
    w_all = jnp.concatenate([wq, w_qkv[:, nq:]], axis=1).astype(BF16)
    b_all = jnp.concatenate([bq, b_qkv[nq:]]).reshape(1, -1)
    wo = w_out.reshape(SWA_KV_HEADS, group, hd, d).transpose(1, 0, 2, 3).reshape(nq, d).astype(BF16)
    nall = w_all.shape[1]
    const = lambda b, t, s: (0, 0)
    return pl.pallas_call(
        functools.partial(_swa_kernel, tq=tq, q_heads=q_heads),
        out_shape=jax.ShapeDtypeStruct(x.shape, F32),
        grid_spec=pltpu.PrefetchScalarGridSpec(
            num_scalar_prefetch=1,
            grid=(bsz, seqlen // tq),
            in_specs=[pl.BlockSpec((1, tq, d), lambda b, t, s: (b, t, 0)),
                      pl.BlockSpec((1, d), const),
                      _resident((d, nall), const),
                      pl.BlockSpec((1, nall), const),
                      _resident((nq, d), const),
                      pl.BlockSpec((1, d), const)],
            out_specs=pl.BlockSpec((1, tq, d), lambda b, t, s: (b, t, 0)),
            scratch_shapes=[pltpu.VMEM((SWA_BLOCK + tq, LANES), BF16), pltpu.VMEM((SWA_BLOCK + tq, LANES), BF16),
                            pltpu.VMEM((q_heads, SWA_BLOCK, 2 * SWA_BLOCK), F32),
                            pltpu.VMEM((group, tq, LANES), BF16), pltpu.VMEM((group, tq, LANES), BF16)]),
        compiler_params=_params(("arbitrary", "arbitrary"), 48),
        name="swa",
    )(sinks, x, ln.reshape(1, d), w_all, b_all, wo, b_out.reshape(1, d))


def _router_kernel(x_ref, ln_ref, whi_ref, wlo_ref, idx_ref, gate_ref, hp_ref, pos_ref, count_ref, tri_ref):
    nt = (((1,), (1,)), ((), ()))
    h = _rms(x_ref[...], ln_ref[...])
    h_hi = h.astype(BF16)
    h_lo = (h - h_hi.astype(F32)).astype(BF16)
    w_hi, w_lo = whi_ref[...], wlo_ref[...]
    logits = (lax.dot_general(w_hi, h_hi, nt, preferred_element_type=F32)
              + lax.dot_general(w_hi, h_lo, nt, preferred_element_type=F32)
              + lax.dot_general(w_lo, h_hi, nt, preferred_element_type=F32))
    n_exp = logits.shape[0]
    eid = lax.broadcasted_iota(jnp.int32, logits.shape, 0)
    m1 = jnp.max(logits, axis=0, keepdims=True)
    i1 = jnp.min(jnp.where(logits == m1, eid, n_exp), axis=0, keepdims=True)
    rest = jnp.where(eid == i1, -jnp.inf, logits)
    m2 = jnp.max(rest, axis=0, keepdims=True)
    i2 = jnp.min(jnp.where(rest == m2, eid, n_exp), axis=0, keepdims=True)
    e2 = jnp.exp(m2 - m1)
    g1 = 1.0 / (1.0 + e2)
    idx_ref[...] = jnp.concatenate([i1, i2], axis=0)
    gate_ref[...] = jnp.concatenate([g1, e2 * g1], axis=0)
    hp_ref[...] = _pack_bf16_pairs(h)
    tm = logits.shape[1]

    @pl.when(pl.program_id(0) == 0)
    def _():
        count_ref[...] = jnp.zeros_like(count_ref)
        r = lax.broadcasted_iota(jnp.int32, (tm, tm), 0)
        c = lax.broadcasted_iota(jnp.int32, (tm, tm), 1)
        tri_ref[...] = jnp.where(r < c, 1.0, 0.0).astype(BF16)

    pick1 = jnp.where(eid == i1, 1.0, 0.0)
    pick2 = jnp.where(eid == i2, 1.0, 0.0)
    picks = pick1 + pick2
    before = _bdot(picks.astype(BF16), tri_ref[...]) + count_ref[:, 0:1]
    pos_ref[...] = jnp.concatenate([jnp.sum(pick1 * before, axis=0, keepdims=True),
                                    jnp.sum(pick2 * before, axis=0, keepdims=True)], axis=0).astype(jnp.int32)
    count_ref[...] = count_ref[...] + jnp.sum(picks, axis=1, keepdims=True)


def _router(x2, ln, w_router, *, tm=512):
    ntok, d = x2.shape
    n_exp = w_router.shape[1]
    tm = min(tm, ntok)
    wt = w_router.T
    w_hi = wt.astype(BF16)
    w_lo = (wt - w_hi.astype(F32)).astype(BF16)
    return pl.pallas_call(
        _router_kernel,
        out_shape=(jax.ShapeDtypeStruct((TOP_K, ntok), jnp.int32), jax.ShapeDtypeStruct((TOP_K, ntok), F32),
                   jax.ShapeDtypeStruct((ntok, d // 2), jnp.uint32),
                   jax.ShapeDtypeStruct((TOP_K, ntok), jnp.int32), jax.ShapeDtypeStruct((n_exp, LANES), F32)),
        grid=(ntok // tm,),
        in_specs=[pl.BlockSpec((tm, d), lambda i: (i, 0)),
                  pl.BlockSpec((1, d), lambda i: (0, 0)),
                  pl.BlockSpec((n_exp, d), lambda i: (0, 0)),
                  pl.BlockSpec((n_exp, d), lambda i: (0, 0))],
        out_specs=(pl.BlockSpec((TOP_K, tm), lambda i: (0, i)), pl.BlockSpec((TOP_K, tm), lambda i: (0, i)),
                   pl.BlockSpec((tm, d // 2), lambda i: (i, 0)),
                   pl.BlockSpec((TOP_K, tm), lambda i: (0, i)), pl.BlockSpec((n_exp, LANES), lambda i: (0, 0))),
        scratch_shapes=[pltpu.VMEM((tm, tm), BF16)],
        compiler_params=_params(("arbitrary",), 32),
        name="moe_router",
    )(x2, ln.reshape(1, d), w_hi, w_lo)


def _moe_plan(idx, pos, counts, tile):
    n_exp = counts.shape[0]
    nslots = idx.size
    counts = counts[:, 0].astype(jnp.int32)
    ends = jnp.cumsum(counts)
    offs = ends - counts
    experts = jnp.arange(n_exp, dtype=jnp.int32).reshape(n_exp, 1, 1)
    rank = pos + jnp.sum(jnp.where(idx[None] == experts, offs.reshape(n_exp, 1, 1), 0), axis=0)
    n_tiles = nslots // tile
    n_visits = n_tiles + n_exp - 1
    first_tile = offs // tile
    last_tile = (ends - 1) // tile
    nvis = jnp.where(counts > 0, last_tile - first_tile + 1, 0)
    vend = jnp.cumsum(nvis)
    vstart = vend - nvis
    total = vend[-1]
    v = jnp.arange(n_visits, dtype=jnp.int32)
    vc = jnp.minimum(v, total - 1)
    e = jnp.minimum(jnp.sum((vc[:, None] >= vend[None, :]).astype(jnp.int32), axis=1), n_exp - 1)
    sel = (e[:, None] == jnp.arange(n_exp, dtype=jnp.int32)[None, :]).astype(jnp.int32)
    pick = lambda a: jnp.sum(sel * a[None, :], axis=1)
    tile_id = pick(first_tile) + vc - pick(vstart)
    lo = jnp.maximum(pick(offs), tile_id * tile) - tile_id * tile
    hi = jnp.minimum(pick(ends), (tile_id + 1) * tile) - tile_id * tile
    valid = v < total
    lo = jnp.where(valid, lo, 0)
    hi = jnp.where(valid, hi, 0)
    prev_tile = jnp.concatenate([jnp.full((1,), -1, jnp.int32), tile_id[:-1]])
    first = (valid & (tile_id != prev_tile)).astype(jnp.int32)
    next_tile = jnp.concatenate([tile_id[1:], jnp.full((1,), -1, jnp.int32)])
    last = (valid & ((tile_id != next_tile) | (v == total - 1))).astype(jnp.int32)
    meta = jnp.stack([tile_id, e, lo, hi, first, last]).astype(jnp.int32)
    return rank.astype(jnp.int32), meta


def _pack_bf16_pairs(h):
    half = h.shape[1] // 2
    bits = lax.bitcast_convert_type(h.astype(BF16).astype(F32), jnp.uint32)
    return (bits[:, half:] & jnp.uint32(0xFFFF0000)) | (bits[:, :half] >> 16)


def _unpack_pairs_f32(u):
    lo = lax.bitcast_convert_type(u << 16, F32)
    hi = lax.bitcast_convert_type(u & jnp.uint32(0xFFFF0000), F32)
    return jnp.concatenate([lo, hi], axis=1)


def _unpack_bf16_pairs(u):
    return _unpack_pairs_f32(u).astype(BF16)


SC_CORES = 2
SC_SUBCORES = 16
SC_INDEX_WINDOW = 128


def _sc_mesh():
    return plsc.VectorSubcoreMesh(core_axis_name="c", subcore_axis_name="s")


def _sc_worker_id():
    return lax.axis_index("c") * SC_SUBCORES + lax.axis_index("s")


def _sc_scatter_rows(src, rank, nrows):
    ntok, width = src.shape
    win = SC_INDEX_WINDOW
    per = ntok // (SC_CORES * SC_SUBCORES)

    @pl.kernel(out_type=jax.ShapeDtypeStruct((nrows, width), src.dtype), mesh=_sc_mesh(),
               scratch_types=[pltpu.VMEM((1, win), jnp.int32)] * TOP_K + [pltpu.VMEM((win, width), src.dtype)],
               name="moe_dispatch_sc")
    def scatter(src_hbm, rank_hbm, o_hbm, *scratch):
        idx_vmem, buf = scratch[:TOP_K], scratch[TOP_K]
        wid = _sc_worker_id()

        @pl.loop(0, per // win)
        def _(blk):
            base = wid * per + blk * win
            for k in range(TOP_K):
                pltpu.sync_copy(rank_hbm.at[pl.ds(k, 1), pl.ds(base, win)], idx_vmem[k])
            pltpu.sync_copy(src_hbm.at[pl.ds(base, win)], buf)
            for k in range(TOP_K):
                pltpu.sync_copy(buf, o_hbm.at[idx_vmem[k].at[0]])

    return scatter(src, rank)


def _sc_gather_rows(src, idx, *, sub=32):
    n = idx.shape[0]
    width = src.shape[1]
    win = SC_INDEX_WINDOW
    per = n // (SC_CORES * SC_SUBCORES)
    nsub = win // sub

    @pl.kernel(out_type=jax.ShapeDtypeStruct((n, width), src.dtype), mesh=_sc_mesh(),
               scratch_types=[pltpu.VMEM((1, win), jnp.int32)] + [pltpu.VMEM((sub, width), src.dtype)] * 2
               + [pltpu.SemaphoreType.DMA] * 4,
               name="moe_gather_sc")
    def gather(src_hbm, idx_hbm, o_hbm, i_vmem, buf0, buf1, g0, g1, w0, w1):
        bufs, gsem, wsem = (buf0, buf1), (g0, g1), (w0, w1)
        wid = _sc_worker_id()

        @pl.loop(0, per // win)
        def _(blk):
            base = wid * per + blk * win
            pltpu.sync_copy(idx_hbm.at[:, pl.ds(base, win)], i_vmem)
            gathers = [pltpu.make_async_copy(src_hbm.at[i_vmem.at[0, pl.ds(sub * j, sub)]], bufs[j % 2], gsem[j % 2])
                       for j in range(nsub)]
            writes = [pltpu.make_async_copy(bufs[j % 2], o_hbm.at[pl.ds(base + sub * j, sub)], wsem[j % 2])
                      for j in range(nsub)]
            gathers[0].start()
            for j in range(nsub):
                if j + 1 < nsub:
                    if j >= 1:
                        writes[j - 1].wait()
                    gathers[j + 1].start()
                gathers[j].wait()
                writes[j].start()
            writes[nsub - 2].wait()
            writes[nsub - 1].wait()

    return gather(src, idx.reshape(1, n))


MXU_N = 256


def _expert_kernel(meta_ref, x_ref, wg_ref, wu_ref, wd_ref, o_ref, acc_ref, xb_ref, act_ref, wgb_ref, wub_ref,
                   wdb_ref, *, ts):
    v = pl.program_id(0)
    hc = pl.program_id(1)
    lo, hi, first, last = meta_ref[2, v], meta_ref[3, v], meta_ref[4, v], meta_ref[5, v]
    tile, d = acc_ref.shape
    nsub = tile // ts
    th = wgb_ref.shape[1]
    wide = (hi - lo) * 2 > tile

    @pl.when(hc == 0)
    def _():
        for sub in range(nsub):
            xb_ref[sub * ts:(sub + 1) * ts, :] = _unpack_bf16_pairs(x_ref[sub * ts:(sub + 1) * ts, :])

    @pl.when((first == 1) & (hc == 0))
    def _():
        acc_ref[...] = jnp.zeros_like(acc_ref)

    @pl.when(wide)
    def _():
        rows = lax.broadcasted_iota(jnp.int32, (tile, 1), 0)
        mine = (rows >= lo) & (rows < hi)
        for n in range(th // MXU_N):
            cols = slice(n * MXU_N, (n + 1) * MXU_N)
            gate = _bdot(xb_ref[...], wg_ref[0, :, cols].astype(BF16))
            up = _bdot(xb_ref[...], wu_ref[0, :, cols].astype(BF16))
            act_ref[:, cols] = (_silu(gate) * up).astype(BF16)
        for n in range(d // MXU_N):
            cols = slice(n * MXU_N, (n + 1) * MXU_N)
            acc_ref[:, cols] += jnp.where(mine, _bdot(act_ref[...], wd_ref[0, :, cols].astype(BF16)), 0.0)

    @pl.when(jnp.logical_not(wide) & (hi > lo))
    def _():
        wgb_ref[...] = wg_ref[0].astype(BF16)
        wub_ref[...] = wu_ref[0].astype(BF16)
        wdb_ref[...] = wd_ref[0].astype(BF16)
        for sub in range(nsub):
            r0 = sub * ts

            @pl.when((lo < r0 + ts) & (hi > r0))
            def _():
                xs = xb_ref[r0:r0 + ts, :]
                act = (_silu(_bdot(xs, wgb_ref[...])) * _bdot(xs, wub_ref[...])).astype(BF16)
                y = _bdot(act, wdb_ref[...])
                rows = r0 + lax.broadcasted_iota(jnp.int32, (ts, 1), 0)
                acc_ref[r0:r0 + ts, :] += jnp.where((rows >= lo) & (rows < hi), y, 0.0)

    @pl.when((last == 1) & (hc == pl.num_programs(1) - 1))
    def _():
        for sub in range(nsub):
            o_ref[sub * ts:(sub + 1) * ts, :] = _pack_bf16_pairs(acc_ref[sub * ts:(sub + 1) * ts, :])


def _experts(xg, meta, w_gate_up, w_down, *, tile, th=512, ts=512):
    nrows = xg.shape[0]
    n_exp, hidden, d = w_down.shape
    n_hc = hidden // th
    ts = min(ts, tile)
    wgu = w_gate_up
    return pl.pallas_call(
        functools.partial(_expert_kernel, ts=ts),
        out_shape=jax.ShapeDtypeStruct((nrows, d // 2), jnp.uint32),
        grid_spec=pltpu.PrefetchScalarGridSpec(
            num_scalar_prefetch=1,
            grid=(meta.shape[1], n_hc),
            in_specs=[pl.BlockSpec((tile, d // 2), lambda v, c, m: (m[0, v], 0)),
                      pl.BlockSpec((1, d, th), lambda v, c, m: (m[1, v], 0, c)),
                      pl.BlockSpec((1, d, th), lambda v, c, m: (m[1, v], 0, c + n_hc)),
                      pl.BlockSpec((1, th, d), lambda v, c, m: (m[1, v], c, 0))],
            out_specs=pl.BlockSpec((tile, d // 2), lambda v, c, m: (m[0, v], 0)),
            scratch_shapes=[pltpu.VMEM((tile, d), F32), pltpu.VMEM((tile, d), BF16), pltpu.VMEM((tile, th), BF16),
                            pltpu.VMEM((d, th), BF16), pltpu.VMEM((d, th), BF16), pltpu.VMEM((th, d), BF16)]),
        compiler_params=_params(("arbitrary", "arbitrary"), 56),
        name="moe_experts",
    )(meta, xg, wgu, wgu, w_down)


def _combine_kernel(x_ref, gate_ref, fg_ref, y0_ref, y1_ref, o_ref, *, final_norm):
    g = gate_ref[...]
    out = x_ref[...] + g[:, 0:1] * _unpack_pairs_f32(y0_ref[0]) + g[:, 1:2] * _unpack_pairs_f32(y1_ref[0])
    if final_norm:
        out = _rms(out, fg_ref[...])
    o_ref[...] = out


def _combine(x2, gates_t, yk, final_gain, *, tm=512):
    ntok, d = x2.shape
    tm = min(tm, ntok)
    final_norm = final_gain is not None
    fg = (final_gain if final_norm else jnp.ones((d,), F32)).reshape(1, d)
    return pl.pallas_call(
        functools.partial(_combine_kernel, final_norm=final_norm),
        out_shape=jax.ShapeDtypeStruct((ntok, d), F32),
        grid=(ntok // tm,),
        in_specs=[pl.BlockSpec((tm, d), lambda i: (i, 0)),
                  pl.BlockSpec((tm, TOP_K), lambda i: (i, 0)),
                  pl.BlockSpec((1, d), lambda i: (0, 0)),
                  pl.BlockSpec((1, tm, d // 2), lambda i: (0, i, 0)),
                  pl.BlockSpec((1, tm, d // 2), lambda i: (1, i, 0))],
        out_specs=pl.BlockSpec((tm, d), lambda i: (i, 0)),
        compiler_params=_params(("parallel",), 40),
        name="moe_combine",
    )(x2, gates_t, fg, yk, yk)


def _moe_routed(x, ln, w_router, w_gate_up, w_down, *, tile=2048):
    bsz, seqlen, d = x.shape
    ntok = bsz * seqlen
    tile = min(tile, TOP_K * ntok)
    x2 = x.reshape(ntok, d)
    idx, gates, hp, pos, counts = _router(x2, ln, w_router)
    rank, meta = _moe_plan(idx, pos, counts, tile)
    xg = _sc_scatter_rows(hp, rank, TOP_K * ntok)
    y = _experts(xg, meta, w_gate_up, w_down, tile=tile)
    yk = _sc_gather_rows(y, rank.reshape(-1), sub=64).reshape(TOP_K, ntok, d // 2)
    return gates.T, yk


def _moe_layer(x, ln, w_router, w_gate_up, w_down, *, final_gain=None, tile=2048):
    bsz, seqlen, d = x.shape
    gates_t, yk = _moe_routed(x, ln, w_router, w_gate_up, w_down, tile=tile)
    out = _combine(x.reshape(bsz * seqlen, d), gates_t, yk, final_gain)
    return out.reshape(bsz, seqlen, d)


def kernel(x, l0_ln1, l0_s5_lam_re, l0_s5_lam_im, l0_s5_log_dt, l0_s5_b_re, l0_s5_b_im, l0_s5_c_re, l0_s5_c_im, l0_s5_d, l0_s5_w_glu, l0_s5_b_glu, l0_ln2, l0_ffn_w_gate_up, l0_ffn_w_down, l1_ln1, l1_gla_w_in, l1_gla_w_g2, l1_gla_b_g2, l1_gla_norm, l1_gla_w_out, l1_ln2, l1_moe_router, l1_moe_w_gate_up, l1_moe_w_down, l2_ln1, l2_swa_w_qkv, l2_swa_b_qkv, l2_swa_sinks, l2_swa_w_out, l2_swa_b_out, l2_ln2, l2_ffn_w_gate_up, l2_ffn_w_down, l3_ln1, l3_s5_lam_re, l3_s5_lam_im, l3_s5_log_dt, l3_s5_b_re, l3_s5_b_im, l3_s5_c_re, l3_s5_c_im, l3_s5_d, l3_s5_w_glu, l3_s5_b_glu, l3_ln2, l3_moe_router, l3_moe_w_gate_up, l3_moe_w_down, ln_f):
    s5_params = ((l0_s5_lam_re, l0_s5_lam_im, l0_s5_log_dt, l0_s5_b_re, l0_s5_b_im, l0_s5_c_re, l0_s5_c_im),
                 (l3_s5_lam_re, l3_s5_lam_im, l3_s5_log_dt, l3_s5_b_re, l3_s5_b_im, l3_s5_c_re, l3_s5_c_im))
    s5_ops = jax.vmap(_s5_operators)(*(jnp.stack(pair) for pair in zip(*s5_params)))
    x = _s5_layer(x, l0_ln1, tuple(a[0] for a in s5_ops), l0_s5_d, l0_s5_w_glu, l0_s5_b_glu)
    x = _dense_ffn_layer(x, l0_ln2, l0_ffn_w_gate_up, l0_ffn_w_down)
    x = _gla_layer(x, l1_ln1, l1_gla_w_in, l1_gla_w_g2, l1_gla_b_g2, l1_gla_norm, l1_gla_w_out)
    x = _moe_layer(x, l1_ln2, l1_moe_router, l1_moe_w_gate_up, l1_moe_w_down)
    x = _swa_layer(x, l2_ln1, l2_swa_w_qkv, l2_swa_b_qkv, l2_swa_sinks, l2_swa_w_out, l2_swa_b_out)
    x = _dense_ffn_layer(x, l2_ln2, l2_ffn_w_gate_up, l2_ffn_w_down)
    x = _s5_layer(x, l3_ln1, tuple(a[1] for a in s5_ops), l3_s5_d, l3_s5_w_glu, l3_s5_b_glu)
    return _moe_layer(x, l3_ln2, l3_moe_router, l3_moe_w_gate_up, l3_moe_w_down, final_gain=ln_f)
```

```python
import functools
import math

import jax
import jax.numpy as jnp
from jax import lax
from jax.experimental import pallas as pl
from jax.experimental.pallas import tpu as pltpu
from jax.experimental.pallas import tpu_sc as plsc

F32 = jnp.float32
BF16 = jnp.bfloat16
EPS = 1e-6
LANES = 128
MIB = 1 << 20

S5_GROUP = 16
S5_STATE = 64
S5_CHUNK = 16
S5_SLAB_GROUPS = LANES // S5_GROUP
S5_PITCH_PAD = 8
S5_SCAN_UNROLL = 8
S5_CAUSAL_BANDS = 8

GLA_HEADS = 4
GLA_GATE_RANK = 16
GLA_GATE_NORM = 16.0
GLA_CHUNK = 64

SWA_HEAD_DIM = 64
SWA_KV_HEADS = 2
SWA_WINDOW = 128
SWA_BLOCK = 128
MASK_VALUE = -1e30

TOP_K = 2


def _params(semantics, vmem_mib):
    return pltpu.CompilerParams(dimension_semantics=semantics, vmem_limit_bytes=vmem_mib * MIB)


def _resident(block_shape, index_map):
    return pl.BlockSpec(block_shape, index_map, pipeline_mode=pl.Buffered(1))


def _rms(xf, gain):
    return xf * lax.rsqrt(jnp.mean(xf * xf, axis=-1, keepdims=True) + EPS) * gain


def _gelu_tanh(x):
    return 0.5 * x * (1.0 + jnp.tanh(math.sqrt(2.0 / math.pi) * (x + 0.044715 * (x * x * x))))


def _silu(x):
    return x * jax.nn.sigmoid(x)


def _bdot(a, b):
    return jnp.dot(a, b, preferred_element_type=F32)


S5_ROW_TILE = 512


def _s5_norm_kernel(x_ref, g_ref, o_ref, scr_ref, *, nloc):
    h = _rms(x_ref[0], g_ref[...])
    nslab = scr_ref.shape[0]
    for c in range(nslab):
        scr_ref[c] = h[:, c * LANES:(c + 1) * LANES]
    for s in range(S5_CHUNK):
        rows = pl.ds(s, nloc, stride=S5_CHUNK)
        o_ref[0, s] = jnp.concatenate([scr_ref[c, rows, :] for c in range(nslab)], axis=1).astype(o_ref.dtype)


def _s5_norm(x, gain):
    bsz, seqlen, d = x.shape
    nch = seqlen // S5_CHUNK
    tm = min(S5_ROW_TILE, seqlen)
    nloc = tm // S5_CHUNK
    return pl.pallas_call(
        functools.partial(_s5_norm_kernel, nloc=nloc),
        out_shape=jax.ShapeDtypeStruct((bsz, S5_CHUNK, nch, d), BF16),
        grid=(bsz, seqlen // tm),
        in_specs=[pl.BlockSpec((1, tm, d), lambda b, i: (b, i, 0)),
                  pl.BlockSpec((1, d), lambda b, i: (0, 0))],
        out_specs=pl.BlockSpec((1, S5_CHUNK, nloc, d), lambda b, i: (b, 0, i, 0)),
        scratch_shapes=[pltpu.VMEM((d // LANES, tm, LANES), F32)],
        compiler_params=_params(("parallel", "parallel"), 32),
        name="s5_norm",
    )(x, gain.reshape(1, d))


def _tiling_matrix(rows, cols):
    p = lax.broadcasted_iota(jnp.int32, (rows, cols), 0)
    c = lax.broadcasted_iota(jnp.int32, (rows, cols), 1)
    return jnp.where(c % rows == p, 1.0, 0.0).astype(BF16)


def _same_group(shape, row_group, col_group):
    r = lax.broadcasted_iota(jnp.int32, shape, 0)
    c = lax.broadcasted_iota(jnp.int32, shape, 1)
    return (r // row_group) == (c // col_group)


def _s5_build_operators(vw_ref, mw_ref, toep_ref, win_ref, wout_ref):
    tn = (((0,), (0,)), ((), ()))
    nstate = vw_ref.shape[-1]
    half = S5_SLAB_GROUPS * nstate
    rep_ch = _tiling_matrix(S5_GROUP, LANES)
    rep_st = _tiling_matrix(nstate, half)
    diag_in = _same_group((LANES, half), S5_GROUP, nstate)
    diag_out = _same_group((half, LANES), nstate, S5_GROUP)

    def out_block(q, r):
        e = lax.dot_general(mw_ref[0, 2 * q + r].astype(BF16), rep_ch, tn, preferred_element_type=F32)
        return jnp.where(diag_out, e, 0.0).astype(BF16)

    for a in range(S5_CHUNK):
        for r in range(2):
            e = _bdot(vw_ref[0, 2 * a + r].astype(BF16), rep_st)
            win_ref[a * LANES:(a + 1) * LANES, r * half:(r + 1) * half] = jnp.where(diag_in, e, 0.0).astype(BF16)
            wout_ref[r * half:(r + 1) * half, a * LANES:(a + 1) * LANES] = out_block(a + 1, r)
    b_bar = win_ref[(S5_CHUNK - 1) * LANES:S5_CHUNK * LANES, :]
    taps = [_bdot(b_bar, jnp.concatenate([out_block(0, 0), out_block(0, 1)], axis=0)).astype(BF16)]
    for j in range(1, S5_CHUNK):
        taps.append(_bdot(b_bar, wout_ref[:, (j - 1) * LANES:j * LANES]).astype(BF16))
    zero = jnp.zeros((LANES, LANES), BF16)
    for a in range(S5_CHUNK):
        for b in range(S5_CHUNK):
            toep_ref[a * LANES:(a + 1) * LANES, b * LANES:(b + 1) * LANES] = taps[b - a] if b >= a else zero


def _s5_conv_kernel(h_ref, vw_ref, mw_ref, a_ref, d_ref, o_ref, s_ref, toep_ref, win_ref, wout_ref,
                    *, nseq, nch):
    pitch = nch + S5_PITCH_PAD
    nl = a_ref.shape[1] // 2

    @pl.when(pl.program_id(1) == 0)
    def _():
        _s5_build_operators(vw_ref, mw_ref, toep_ref, win_ref, wout_ref)

    lhs = jnp.concatenate(
        [jnp.concatenate([h_ref[bl, s] for s in range(S5_CHUNK)], axis=1) for bl in range(nseq)], axis=0)
    bc = _bdot(lhs, win_ref[...])
    for bl in range(nseq):
        for j in range(2 * nl):
            s_ref[j, bl * pitch:bl * pitch + nch, :] = bc[bl * nch:(bl + 1) * nch, j * LANES:(j + 1) * LANES]
    a_re = [a_ref[0, j:j + 1, :] for j in range(nl)]
    a_im = [a_ref[0, nl + j:nl + j + 1, :] for j in range(nl)]

    def step(n, carry):
        p_re, p_im = carry
        rows = pl.ds(n, nseq, stride=pitch)
        n_re, n_im = [], []
        for j in range(nl):
            c_re = s_ref[j, rows, :]
            c_im = s_ref[nl + j, rows, :]
            s_ref[j, rows, :] = p_re[j]
            s_ref[nl + j, rows, :] = p_im[j]
            n_re.append(a_re[j] * p_re[j] - a_im[j] * p_im[j] + c_re)
            n_im.append(a_re[j] * p_im[j] + a_im[j] * p_re[j] + c_im)
        return tuple(n_re), tuple(n_im)

    def steps(m, carry):
        for u in range(S5_SCAN_UNROLL):
            carry = step(m * S5_SCAN_UNROLL + u, carry)
        return carry

    zeros = tuple(jnp.zeros((nseq, LANES), F32) for _ in range(nl))
    lax.fori_loop(0, nch // S5_SCAN_UNROLL, steps, (zeros, zeros))
    x_prev = jnp.concatenate(
        [jnp.concatenate([s_ref[j, bl * pitch:bl * pitch + nch, :] for j in range(2 * nl)], axis=1)
         for bl in range(nseq)], axis=0).astype(BF16)
    band = S5_CHUNK // S5_CAUSAL_BANDS
    y_bands = []
    for q in range(S5_CAUSAL_BANDS):
        kk = (q + 1) * band * LANES
        cols = slice(q * band * LANES, (q + 1) * band * LANES)
        y_bands.append(_bdot(lhs[:, :kk], toep_ref[:kk, cols]) + _bdot(x_prev, wout_ref[:, cols]))
    dskip = d_ref[0]
    for bl in range(nseq):
        for s in range(S5_CHUNK):
            ys = y_bands[s // band][bl * nch:(bl + 1) * nch, (s % band) * LANES:(s % band + 1) * LANES]
            ys = ys + dskip * h_ref[bl, s].astype(F32)
            o_ref[bl, s] = _gelu_tanh(ys).astype(o_ref.dtype)


def _s5_conv(hp, vw, mw, a_pack, d_skip, *, nseq, layer):
    bsz, _, nch, d = hp.shape
    nslab = d // LANES
    kdim = S5_CHUNK * LANES
    sdim = a_pack.shape[1] * LANES
    first = layer * nslab
    blk4 = lambda a: pl.BlockSpec((1,) + a.shape[1:], lambda c, b: (first + c, 0, 0, 0))
    return pl.pallas_call(
        functools.partial(_s5_conv_kernel, nseq=nseq, nch=nch),
        out_shape=jax.ShapeDtypeStruct(hp.shape, BF16),
        grid=(nslab, bsz // nseq),
        in_specs=[pl.BlockSpec((nseq, S5_CHUNK, nch, LANES), lambda c, b: (b, 0, 0, c)),
                  blk4(vw), blk4(mw),
                  pl.BlockSpec((1, sdim // LANES, LANES), lambda c, b: (first + c, 0, 0)),
                  pl.BlockSpec((1, 1, LANES), lambda c, b: (c, 0, 0))],
        out_specs=pl.BlockSpec((nseq, S5_CHUNK, nch, LANES), lambda c, b: (b, 0, 0, c)),
        scratch_shapes=[pltpu.VMEM((sdim // LANES, nseq * (nch + S5_PITCH_PAD), LANES), F32),
                        pltpu.VMEM((kdim, kdim), BF16),
                        pltpu.VMEM((kdim, sdim), BF16),
                        pltpu.VMEM((sdim, kdim), BF16)],
        compiler_params=_params(("arbitrary", "arbitrary"), 56),
        name="s5_conv",
    )(hp, vw, mw, a_pack, d_skip.reshape(nslab, 1, LANES))


def _s5_glu_kernel(y_ref, x_ref, w_ref, b_ref, o_ref, scr_ref, *, nloc):
    nslab = scr_ref.shape[0]
    y = jnp.concatenate([y_ref[0, s] for s in range(S5_CHUNK)], axis=0)
    u = y.astype(F32) * jax.nn.sigmoid(_bdot(y, w_ref[...]) + b_ref[...])
    for s in range(S5_CHUNK):
        rows = pl.ds(s, nloc, stride=S5_CHUNK)
        for c in range(nslab):
            scr_ref[c, rows, :] = u[s * nloc:(s + 1) * nloc, c * LANES:(c + 1) * LANES]
    o_ref[0] = x_ref[0] + jnp.concatenate([scr_ref[c] for c in range(nslab)], axis=1)


def _s5_glu(yp, x, w_glu, b_glu):
    bsz, seqlen, d = x.shape
    tm = min(S5_ROW_TILE, seqlen)
    nloc = tm // S5_CHUNK
    return pl.pallas_call(
        functools.partial(_s5_glu_kernel, nloc=nloc),
        out_shape=jax.ShapeDtypeStruct(x.shape, F32),
        grid=(bsz, seqlen // tm),
        in_specs=[pl.BlockSpec((1, S5_CHUNK, nloc, d), lambda b, i: (b, 0, i, 0)),
                  pl.BlockSpec((1, tm, d), lambda b, i: (b, i, 0)),
                  _resident((d, d), lambda b, i: (0, 0)),
                  pl.BlockSpec((1, d), lambda b, i: (0, 0))],
        out_specs=pl.BlockSpec((1, tm, d), lambda b, i: (b, i, 0)),
        scratch_shapes=[pltpu.VMEM((d // LANES, tm, LANES), F32)],
        compiler_params=_params(("parallel", "parallel"), 40),
        name="s5_glu",
    )(yp, x, w_glu.astype(BF16), b_glu.reshape(1, d))


def _s5_operators(lam_re, lam_im, log_dt, b_re, b_im, c_re, c_im):
    ngroups, nstate = lam_re.shape
    gpc = S5_SLAB_GROUPS
    nslab = ngroups // gpc
    dt = jnp.exp(log_dt)[:, None]
    j = jnp.arange(S5_CHUNK + 1, dtype=F32)[:, None, None]
    mag = jnp.exp(j * (lam_re * dt)[None])
    ang = j * (lam_im * dt)[None]
    pw_re, pw_im = mag * jnp.cos(ang), mag * jnp.sin(ang)
    num_re, num_im = pw_re[1] - 1.0, pw_im[1]
    den = lam_re * lam_re + lam_im * lam_im
    f_re = (num_re * lam_re + num_im * lam_im) / den
    f_im = (num_im * lam_re - num_re * lam_im) / den
    bb_re = f_re[..., None] * b_re - f_im[..., None] * b_im
    bb_im = f_re[..., None] * b_im + f_im[..., None] * b_re
    jr = (S5_CHUNK - 1) - jnp.arange(S5_CHUNK, dtype=F32)[:, None, None]
    mag_r = jnp.exp(jr * (lam_re * dt)[None])
    ang_r = jr * (lam_im * dt)[None]
    rev_re, rev_im = mag_r * jnp.cos(ang_r), mag_r * jnp.sin(ang_r)
    slabbed = lambda a: a.reshape(a.shape[0], nslab, gpc, nstate).transpose(1, 0, 2, 3)
    rv_re, rv_im = slabbed(rev_re)[:, :, :, None, :], slabbed(rev_im)[:, :, :, None, :]
    bt_re = bb_re.transpose(0, 2, 1).reshape(nslab, 1, gpc, S5_GROUP, nstate)
    bt_im = bb_im.transpose(0, 2, 1).reshape(nslab, 1, gpc, S5_GROUP, nstate)
    vw = jnp.stack([rv_re * bt_re - rv_im * bt_im, rv_re * bt_im + rv_im * bt_re], axis=2)
    vw = vw.reshape(nslab, 2 * S5_CHUNK, LANES, nstate)
    pc_re, pc_im = slabbed(pw_re)[:, :, None, :, :], slabbed(pw_im)[:, :, None, :, :]
    ct_re = c_re.reshape(nslab, gpc, S5_GROUP, nstate).transpose(0, 2, 1, 3)[:, None]
    ct_im = c_im.reshape(nslab, gpc, S5_GROUP, nstate).transpose(0, 2, 1, 3)[:, None]
    mw = jnp.stack([ct_re * pc_re - ct_im * pc_im, -(ct_re * pc_im + ct_im * pc_re)], axis=2)
    mw = mw.reshape(nslab, 2 * (S5_CHUNK + 1), S5_GROUP, gpc * nstate)
    half = gpc * nstate // LANES
    a_pack = jnp.concatenate([pw_re[S5_CHUNK].reshape(nslab, half, LANES),
                              pw_im[S5_CHUNK].reshape(nslab, half, LANES)], axis=1)
    return vw, mw, a_pack


def _s5_layer(x, ln, operators, layer, d_skip, w_glu, b_glu, *, nseq=4):
    vw, mw, a_pack = operators
    hp = _s5_norm(x, ln)
    yp = _s5_conv(hp, vw, mw, a_pack, d_skip, nseq=min(nseq, x.shape[0]), layer=layer)
    return _s5_glu(yp, x, w_glu, b_glu)


def _dense_ffn_kernel(x_ref, g_ref, wg_ref, wu_ref, wd_ref, o_ref):
    xf = x_ref[...]
    h = _rms(xf, g_ref[...]).astype(BF16)
    act = (_silu(_bdot(h, wg_ref[...])) * _bdot(h, wu_ref[...])).astype(BF16)
    o_ref[...] = xf + _bdot(act, wd_ref[...])


def _dense_ffn_layer(x, ln, w_gate_up, w_down, *, tm=512):
    bsz, seqlen, d = x.shape
    ntok = bsz * seqlen
    hidden = w_down.shape[0]
    tm = min(tm, ntok)
    wgu = w_gate_up.astype(BF16)
    out = pl.pallas_call(
        _dense_ffn_kernel,
        out_shape=jax.ShapeDtypeStruct((ntok, d), F32),
        grid=(ntok // tm,),
        in_specs=[pl.BlockSpec((tm, d), lambda i: (i, 0)),
                  pl.BlockSpec((1, d), lambda i: (0, 0)),
                  _resident((d, hidden), lambda i: (0, 0)),
                  _resident((d, hidden), lambda i: (0, 1)),
                  _resident((hidden, d), lambda i: (0, 0))],
        out_specs=pl.BlockSpec((tm, d), lambda i: (i, 0)),
        compiler_params=_params(("parallel",), 56),
        name="dense_ffn",
    )(x.reshape(ntok, d), ln.reshape(1, d), wgu, wgu, w_down.astype(BF16))
    return out.reshape(bsz, seqlen, d)


def _log_sigmoid(z):
    return jnp.minimum(z, 0.0) - jnp.log(1.0 + jnp.exp(-jnp.abs(z)))


def _gla_kernel(x_ref, ln_ref, wm_ref, wgl_ref, wg2_ref, bg2_ref, gn_ref, wo_ref, o_ref, st_ref,
                *, tq, dk, dv, heads):
    hdk, hdv = dk // heads, dv // heads
    chunk = GLA_CHUNK
    nt = (((1,), (1,)), ((), ()))
    tn = (((0,), (0,)), ((), ()))

    @pl.when(pl.program_id(1) == 0)
    def _():
        st_ref[...] = jnp.zeros_like(st_ref)

    xf = x_ref[0]
    h = _rms(xf, ln_ref[...]).astype(BF16)
    proj = _bdot(h, wm_ref[...])
    glow = _bdot(h, wgl_ref[...]).astype(BF16)
    la = _log_sigmoid(_bdot(glow, wg2_ref[...]) + bg2_ref[...]) * (1.0 / GLA_GATE_NORM)
    row = lax.broadcasted_iota(jnp.int32, (chunk, chunk), 0)
    col = lax.broadcasted_iota(jnp.int32, (chunk, chunk), 1)
    causal = row >= col
    tri = jnp.where(causal, 1.0, 0.0).astype(BF16)
    scale = hdk ** -0.5
    outs = []
    for c in range(tq // chunk):
        r0 = c * chunk
        la_c = la[r0:r0 + chunk, :]
        la_hi = la_c.astype(BF16)
        la_lo = (la_c - la_hi.astype(F32)).astype(BF16)
        gcum_all = _bdot(tri, la_hi) + _bdot(tri, la_lo)
        head_out = []
        for hd in range(heads):
            gcum = gcum_all[:, hd * hdk:(hd + 1) * hdk]
            g_last = gcum[chunk - 1:chunk, :]
            q_c = proj[r0:r0 + chunk, hd * hdk:(hd + 1) * hdk] * scale
            k_c = proj[r0:r0 + chunk, dk + hd * hdk:dk + (hd + 1) * hdk]
            v_c = proj[r0:r0 + chunk, 2 * dk + hd * hdv:2 * dk + (hd + 1) * hdv].astype(BF16)
            q_s = (q_c * jnp.exp(gcum)).astype(BF16)
            k_s = (k_c * jnp.exp(-gcum)).astype(BF16)
            k_end = (k_c * jnp.exp(g_last - gcum)).astype(BF16)
            scores = lax.dot_general(q_s, k_s, nt, preferred_element_type=F32)
            scores = jnp.where(causal, scores, 0.0).astype(BF16)
            state_t = st_ref[hd]
            o = _bdot(scores, v_c) + lax.dot_general(q_s, state_t.astype(BF16), nt,
                                                     preferred_element_type=F32)
            kv_t = lax.dot_general(v_c, k_end, tn, preferred_element_type=F32)
            st_ref[hd] = state_t * jnp.exp(g_last) + kv_t
            head_out.append(o * lax.rsqrt(jnp.mean(o * o, axis=-1, keepdims=True) + EPS))
        outs.append(jnp.concatenate(head_out, axis=1))
    o_all = jnp.concatenate(outs, axis=0)
    r = proj[:, 2 * dk + dv:]
    o_all = (o_all * gn_ref[...] * _silu(r)).astype(BF16)
    o_ref[0] = xf + _bdot(o_all, wo_ref[...])


def _gla_layer(x, ln, w_in, w_g2, b_g2, g_norm, w_out, *, tq=256):
    bsz, seqlen, d = x.shape
    dk = w_g2.shape[1]
    dv = w_out.shape[0]
    nmain = 2 * dk + 2 * dv
    tq = min(tq, seqlen)
    w_main = w_in[:, :nmain].astype(BF16)
    w_glow = jnp.pad(w_in[:, nmain:], ((0, 0), (0, LANES - GLA_GATE_RANK))).astype(BF16)
    w_g2p = jnp.pad(w_g2, ((0, LANES - GLA_GATE_RANK), (0, 0))).astype(BF16)
    hdk, hdv = dk // GLA_HEADS, dv // GLA_HEADS
    const = lambda b, t: (0, 0)
    return pl.pallas_call(
        functools.partial(_gla_kernel, tq=tq, dk=dk, dv=dv, heads=GLA_HEADS),
        out_shape=jax.ShapeDtypeStruct(x.shape, F32),
        grid=(bsz, seqlen // tq),
        in_specs=[pl.BlockSpec((1, tq, d), lambda b, t: (b, t, 0)),
                  pl.BlockSpec((1, d), const),
                  _resident((d, nmain), const),
                  _resident((d, LANES), const),
                  _resident((LANES, dk), const),
                  pl.BlockSpec((1, dk), const),
                  pl.BlockSpec((1, dv), const),
                  _resident((dv, d), const)],
        out_specs=pl.BlockSpec((1, tq, d), lambda b, t: (b, t, 0)),
        scratch_shapes=[pltpu.VMEM((GLA_HEADS, hdv, hdk), F32)],
        compiler_params=_params(("parallel", "arbitrary"), 48),
        name="gla",
    )(x, ln.reshape(1, d), w_main, w_glow, w_g2p, b_g2.reshape(1, dk), g_norm.reshape(1, dv),
      w_out.astype(BF16))


LOG2E = math.log2(math.e)
SWA_SLOT_UNROLL = 4


def _swa_kernel(sink_ref, x_ref, ln_ref, wqkv_ref, bqkv_ref, wo_ref, bo_ref, o_ref, k_ref, v_ref,
                bias_ref, q_ref, a_ref, *, tq, q_heads):
    group = q_heads // SWA_KV_HEADS
    blk = SWA_BLOCK
    nt = (((1,), (1,)), ((), ()))
    b = pl.program_id(0)
    t = pl.program_id(1)
    nq = group * LANES

    @pl.when((b == 0) & (t == 0))
    def _():
        qi = lax.broadcasted_iota(jnp.int32, (blk, 2 * blk), 0)
        kj = lax.broadcasted_iota(jnp.int32, (blk, 2 * blk), 1)
        dist = qi + blk - kj
        in_window = (dist >= 0) & (dist < SWA_WINDOW)
        for hq in range(q_heads):
            slope = 2.0 ** (-8.0 * (hq + 1) / q_heads)
            bias_ref[hq] = jnp.where(in_window, -(slope * LOG2E) * dist.astype(F32), MASK_VALUE)

    @pl.when(t == 0)
    def _():
        k_ref[0:blk, :] = jnp.zeros((blk, LANES), BF16)
        v_ref[0:blk, :] = jnp.zeros((blk, LANES), BF16)

    xf = x_ref[0]
    h = _rms(xf, ln_ref[...]).astype(BF16)
    qkv = _bdot(h, wqkv_ref[...]) + bqkv_ref[...]
    for j in range(group):
        q_ref[j] = qkv[:, j * LANES:(j + 1) * LANES].astype(BF16)
    k_ref[blk:blk + tq, :] = qkv[:, nq:nq + LANES].astype(BF16)
    v_ref[blk:blk + tq, :] = qkv[:, nq + LANES:nq + 2 * LANES].astype(BF16)
    kj_row = lax.broadcasted_iota(jnp.int32, (1, 2 * blk), 1)
    no_prev = jnp.where(kj_row < blk, jnp.where(t == 0, MASK_VALUE, 0.0), 0.0)
    low_half = lax.broadcasted_iota(jnp.int32, (1, LANES), 1) < SWA_HEAD_DIM
    halves = (low_half, jnp.logical_not(low_half))

    def slots(jj, carry):
        for u in range(SWA_SLOT_UNROLL):
            j = jj * SWA_SLOT_UNROLL + u
            for i in range(tq // blk):
                r0 = i * blk
                q_slot = q_ref[j, r0:r0 + blk, :]
                outs = []
                for kh in range(SWA_KV_HEADS):
                    hq = kh * group + j
                    sink = sink_ref[hq] * LOG2E
                    q_h = jnp.where(halves[kh], q_slot, jnp.zeros_like(q_slot))
                    s = lax.dot_general(q_h, k_ref[r0:r0 + 2 * blk, :], nt, preferred_element_type=F32) + bias_ref[hq]
                    if i == 0:
                        s = s + no_prev
                    m = jnp.maximum(jnp.max(s, axis=-1, keepdims=True), sink)
                    p = jnp.exp2(s - m)
                    denom = jnp.sum(p, axis=-1, keepdims=True) + jnp.exp2(sink - m)
                    outs.append(_bdot(p.astype(BF16), v_ref[r0:r0 + 2 * blk, :]) * (1.0 / denom))
                a_ref[j, r0:r0 + blk, :] = jnp.where(low_half, outs[0], outs[1]).astype(BF16)
        return carry

    lax.fori_loop(0, group // SWA_SLOT_UNROLL, slots, 0)
    k_ref[0:blk, :] = k_ref[tq:tq + blk, :]
    v_ref[0:blk, :] = v_ref[tq:tq + blk, :]
    o_all = jnp.concatenate([a_ref[j] for j in range(group)], axis=1)
    o_ref[0] = xf + _bdot(o_all, wo_ref[...]) + bo_ref[...]


def _swa_layer(x, ln, w_qkv, b_qkv, sinks, w_out, b_out, *, tq=512):
    bsz, seqlen, d = x.shape
    hd = SWA_HEAD_DIM
    q_heads = sinks.shape[0]
    group = q_heads // SWA_KV_HEADS
    nq = q_heads * hd
    tq = min(tq, seqlen)
    q_scale = hd ** -0.5 * LOG2E
    wq = (w_qkv[:, :nq] * q_scale).reshape(d, SWA_KV_HEADS, group, hd).transpose(0, 2, 1, 3).reshape(d, nq)
    bq = (b_qkv[:nq] * q_scale).reshape(SWA_KV_HEADS, group, hd).transpose(1, 0, 2).reshape(nq)
    w_all = jnp.concatenate([wq, w_qkv[:, nq:]], axis=1).astype(BF16)
    b_all = jnp.concatenate([bq, b_qkv[nq:]]).reshape(1, -1)
    wo = w_out.reshape(SWA_KV_HEADS, group, hd, d).transpose(1, 0, 2, 3).reshape(nq, d).astype(BF16)
    nall = w_all.shape[1]
    const = lambda b, t, s: (0, 0)
    return pl.pallas_call(
        functools.partial(_swa_kernel, tq=tq, q_heads=q_heads),
        out_shape=jax.ShapeDtypeStruct(x.shape, F32),
        grid_spec=pltpu.PrefetchScalarGridSpec(
            num_scalar_prefetch=1,
            grid=(bsz, seqlen // tq),
            in_specs=[pl.BlockSpec((1, tq, d), lambda b, t, s: (b, t, 0)),
                      pl.BlockSpec((1, d), const),
                      _resident((d, nall), const),
                      pl.BlockSpec((1, nall), const),
                      _resident((nq, d), const),
                      pl.BlockSpec((1, d), const)],
            out_specs=pl.BlockSpec((1, tq, d), lambda b, t, s: (b, t, 0)),
            scratch_shapes=[pltpu.VMEM((SWA_BLOCK + tq, LANES), BF16), pltpu.VMEM((SWA_BLOCK + tq, LANES), BF16),
                            pltpu.VMEM((q_heads, SWA_BLOCK, 2 * SWA_BLOCK), F32),
                            pltpu.VMEM((group, tq, LANES), BF16), pltpu.VMEM((group, tq, LANES), BF16)]),
        compiler_params=_params(("arbitrary", "arbitrary"), 48),
        name="swa",
    )(sinks, x, ln.reshape(1, d), w_all, b_all, wo, b_out.reshape(1, d))


def _router_kernel(x_ref, ln_ref, whi_ref, wlo_ref, idx_ref, gate_ref, hp_ref, pos_ref, count_ref, tri_ref):
    nt = (((1,), (1,)), ((), ()))
    h = _rms(x_ref[...], ln_ref[...])
    h_hi = h.astype(BF16)
    h_lo = (h - h_hi.astype(F32)).astype(BF16)
    w_hi, w_lo = whi_ref[...], wlo_ref[...]
    logits = (lax.dot_general(w_hi, h_hi, nt, preferred_element_type=F32)
              + lax.dot_general(w_hi, h_lo, nt, preferred_element_type=F32)
              + lax.dot_general(w_lo, h_hi, nt, preferred_element_type=F32))
    n_exp = logits.shape[0]
    eid = lax.broadcasted_iota(jnp.int32, logits.shape, 0)
    m1 = jnp.max(logits, axis=0, keepdims=True)
    i1 = jnp.min(jnp.where(logits == m1, eid, n_exp), axis=0, keepdims=True)
    rest = jnp.where(eid == i1, -jnp.inf, logits)
    m2 = jnp.max(rest, axis=0, keepdims=True)
    i2 = jnp.min(jnp.where(rest == m2, eid, n_exp), axis=0, keepdims=True)
    e2 = jnp.exp(m2 - m1)
    g1 = 1.0 / (1.0 + e2)
    idx_ref[...] = jnp.concatenate([i1, i2], axis=0)
    gate_ref[...] = jnp.concatenate([g1, e2 * g1], axis=0)
    hp_ref[...] = _pack_bf16_pairs(h)
    tm = logits.shape[1]

    @pl.when(pl.program_id(0) == 0)
    def _():
        count_ref[...] = jnp.zeros_like(count_ref)
        r = lax.broadcasted_iota(jnp.int32, (tm, tm), 0)
        c = lax.broadcasted_iota(jnp.int32, (tm, tm), 1)
        tri_ref[...] = jnp.where(r < c, 1.0, 0.0).astype(BF16)

    pick1 = jnp.where(eid == i1, 1.0, 0.0)
    pick2 = jnp.where(eid == i2, 1.0, 0.0)
    picks = pick1 + pick2
    before = _bdot(picks.astype(BF16), tri_ref[...]) + count_ref[:, 0:1]
    pos_ref[...] = jnp.concatenate([jnp.sum(pick1 * before, axis=0, keepdims=True),
                                    jnp.sum(pick2 * before, axis=0, keepdims=True)], axis=0).astype(jnp.int32)
    count_ref[...] = count_ref[...] + jnp.sum(picks, axis=1, keepdims=True)


def _router(x2, ln, w_router, *, tm=512):
    ntok, d = x2.shape
    n_exp = w_router.shape[1]
    tm = min(tm, ntok)
    wt = w_router.T
    w_hi = wt.astype(BF16)
    w_lo = (wt - w_hi.astype(F32)).astype(BF16)
    return pl.pallas_call(
        _router_kernel,
        out_shape=(jax.ShapeDtypeStruct((TOP_K, ntok), jnp.int32), jax.ShapeDtypeStruct((TOP_K, ntok), F32),
                   jax.ShapeDtypeStruct((ntok, d // 2), jnp.uint32),
                   jax.ShapeDtypeStruct((TOP_K, ntok), jnp.int32), jax.ShapeDtypeStruct((n_exp, LANES), F32)),
        grid=(ntok // tm,),
        in_specs=[pl.BlockSpec((tm, d), lambda i: (i, 0)),
                  pl.BlockSpec((1, d), lambda i: (0, 0)),
                  pl.BlockSpec((n_exp, d), lambda i: (0, 0)),
                  pl.BlockSpec((n_exp, d), lambda i: (0, 0))],
        out_specs=(pl.BlockSpec((TOP_K, tm), lambda i: (0, i)), pl.BlockSpec((TOP_K, tm), lambda i: (0, i)),
                   pl.BlockSpec((tm, d // 2), lambda i: (i, 0)),
                   pl.BlockSpec((TOP_K, tm), lambda i: (0, i)), pl.BlockSpec((n_exp, LANES), lambda i: (0, 0))),
        scratch_shapes=[pltpu.VMEM((tm, tm), BF16)],
        compiler_params=_params(("arbitrary",), 32),
        name="moe_router",
    )(x2, ln.reshape(1, d), w_hi, w_lo)


def _moe_plan(idx, pos, counts, tile):
    n_exp = counts.shape[0]
    nslots = idx.size
    counts = counts[:, 0].astype(jnp.int32)
    ends = jnp.cumsum(counts)
    offs = ends - counts
    experts = jnp.arange(n_exp, dtype=jnp.int32).reshape(n_exp, 1, 1)
    rank = pos + jnp.sum(jnp.where(idx[None] == experts, offs.reshape(n_exp, 1, 1), 0), axis=0)
    n_tiles = nslots // tile
    n_visits = n_tiles + n_exp - 1
    first_tile = offs // tile
    last_tile = (ends - 1) // tile
    nvis = jnp.where(counts > 0, last_tile - first_tile + 1, 0)
    vend = jnp.cumsum(nvis)
    vstart = vend - nvis
    total = vend[-1]
    v = jnp.arange(n_visits, dtype=jnp.int32)
    vc = jnp.minimum(v, total - 1)
    e = jnp.minimum(jnp.sum((vc[:, None] >= vend[None, :]).astype(jnp.int32), axis=1), n_exp - 1)
    sel = (e[:, None] == jnp.arange(n_exp, dtype=jnp.int32)[None, :]).astype(jnp.int32)
    pick = lambda a: jnp.sum(sel * a[None, :], axis=1)
    tile_id = pick(first_tile) + vc - pick(vstart)
    lo = jnp.maximum(pick(offs), tile_id * tile) - tile_id * tile
    hi = jnp.minimum(pick(ends), (tile_id + 1) * tile) - tile_id * tile
    valid = v < total
    lo = jnp.where(valid, lo, 0)
    hi = jnp.where(valid, hi, 0)
    prev_tile = jnp.concatenate([jnp.full((1,), -1, jnp.int32), tile_id[:-1]])
    first = (valid & (tile_id != prev_tile)).astype(jnp.int32)
    next_tile = jnp.concatenate([tile_id[1:], jnp.full((1,), -1, jnp.int32)])
    last = (valid & ((tile_id != next_tile) | (v == total - 1))).astype(jnp.int32)
    meta = jnp.stack([tile_id, e, lo, hi, first, last]).astype(jnp.int32)
    return rank.astype(jnp.int32), meta


def _pack_bf16_pairs(h):
    half = h.shape[1] // 2
    bits = lax.bitcast_convert_type(h.astype(BF16).astype(F32), jnp.uint32)
    return (bits[:, half:] & jnp.uint32(0xFFFF0000)) | (bits[:, :half] >> 16)


def _unpack_pairs_f32(u):
    lo = lax.bitcast_convert_type(u << 16, F32)
    hi = lax.bitcast_convert_type(u & jnp.uint32(0xFFFF0000), F32)
    return jnp.concatenate([lo, hi], axis=1)


def _unpack_bf16_pairs(u):
    return _unpack_pairs_f32(u).astype(BF16)


SC_CORES = 2
SC_SUBCORES = 16
SC_INDEX_WINDOW = 128


def _sc_mesh():
    return plsc.VectorSubcoreMesh(core_axis_name="c", subcore_axis_name="s")


def _sc_worker_id():
    return lax.axis_index("c") * SC_SUBCORES + lax.axis_index("s")


def _sc_scatter_rows(src, rank, nrows):
    ntok, width = src.shape
    win = SC_INDEX_WINDOW
    per = ntok // (SC_CORES * SC_SUBCORES)

    @pl.kernel(out_type=jax.ShapeDtypeStruct((nrows, width), src.dtype), mesh=_sc_mesh(),
               scratch_types=[pltpu.VMEM((1, win), jnp.int32)] * TOP_K + [pltpu.VMEM((win, width), src.dtype)],
               name="moe_dispatch_sc")
    def scatter(src_hbm, rank_hbm, o_hbm, *scratch):
        idx_vmem, buf = scratch[:TOP_K], scratch[TOP_K]
        wid = _sc_worker_id()

        @pl.loop(0, per // win)
        def _(blk):
            base = wid * per + blk * win
            for k in range(TOP_K):
                pltpu.sync_copy(rank_hbm.at[pl.ds(k, 1), pl.ds(base, win)], idx_vmem[k])
            pltpu.sync_copy(src_hbm.at[pl.ds(base, win)], buf)
            for k in range(TOP_K):
                pltpu.sync_copy(buf, o_hbm.at[idx_vmem[k].at[0]])

    return scatter(src, rank)


def _sc_gather_rows(src, rank, *, sub=32):
    nslot, ntok = rank.shape
    n = nslot * ntok
    width = src.shape[1]
    win = SC_INDEX_WINDOW
    per = n // (SC_CORES * SC_SUBCORES)
    nsub = win // sub

    @pl.kernel(out_type=jax.ShapeDtypeStruct((n, width), src.dtype), mesh=_sc_mesh(),
               scratch_types=[pltpu.VMEM((1, win), jnp.int32)] + [pltpu.VMEM((sub, width), src.dtype)] * 2
               + [pltpu.SemaphoreType.DMA] * 4,
               name="moe_gather_sc")
    def gather(src_hbm, idx_hbm, o_hbm, i_vmem, buf0, buf1, g0, g1, w0, w1):
        bufs, gsem, wsem = (buf0, buf1), (g0, g1), (w0, w1)
        wid = _sc_worker_id()

        @pl.loop(0, per // win)
        def _(blk):
            base = wid * per + blk * win
            pltpu.sync_copy(idx_hbm.at[pl.ds(base // ntok, 1), pl.ds(base % ntok, win)], i_vmem)
            gathers = [pltpu.make_async_copy(src_hbm.at[i_vmem.at[0, pl.ds(sub * j, sub)]], bufs[j % 2], gsem[j % 2])
                       for j in range(nsub)]
            writes = [pltpu.make_async_copy(bufs[j % 2], o_hbm.at[pl.ds(base + sub * j, sub)], wsem[j % 2])
                      for j in range(nsub)]
            gathers[0].start()
            for j in range(nsub):
                if j + 1 < nsub:
                    if j >= 1:
                        writes[j - 1].wait()
                    gathers[j + 1].start()
                gathers[j].wait()
                writes[j].start()
            writes[nsub - 2].wait()
            writes[nsub - 1].wait()

    return gather(src, rank)


MXU_N = 256


def _expert_kernel(meta_ref, x_ref, wg_ref, wu_ref, wd_ref, o_ref, acc_ref, xb_ref, act_ref, wgb_ref, wub_ref,
                   wdb_ref, *, ts):
    v = pl.program_id(0)
    hc = pl.program_id(1)
    lo, hi, first, last = meta_ref[2, v], meta_ref[3, v], meta_ref[4, v], meta_ref[5, v]
    tile, d = acc_ref.shape
    nsub = tile // ts
    th = wgb_ref.shape[1]
    wide = (hi - lo) * 2 > tile

    @pl.when(hc == 0)
    def _():
        for sub in range(nsub):
            xb_ref[sub * ts:(sub + 1) * ts, :] = _unpack_bf16_pairs(x_ref[sub * ts:(sub + 1) * ts, :])

    @pl.when((first == 1) & (hc == 0))
    def _():
        acc_ref[...] = jnp.zeros_like(acc_ref)

    @pl.when(wide)
    def _():
        rows = lax.broadcasted_iota(jnp.int32, (tile, 1), 0)
        mine = (rows >= lo) & (rows < hi)
        for n in range(th // MXU_N):
            cols = slice(n * MXU_N, (n + 1) * MXU_N)
            gate = _bdot(xb_ref[...], wg_ref[0, :, cols].astype(BF16))
            up = _bdot(xb_ref[...], wu_ref[0, :, cols].astype(BF16))
            act_ref[:, cols] = (_silu(gate) * up).astype(BF16)
        for n in range(d // MXU_N):
            cols = slice(n * MXU_N, (n + 1) * MXU_N)
            acc_ref[:, cols] += jnp.where(mine, _bdot(act_ref[...], wd_ref[0, :, cols].astype(BF16)), 0.0)

    @pl.when(jnp.logical_not(wide) & (hi > lo))
    def _():
        wgb_ref[...] = wg_ref[0].astype(BF16)
        wub_ref[...] = wu_ref[0].astype(BF16)
        wdb_ref[...] = wd_ref[0].astype(BF16)
        for sub in range(nsub):
            r0 = sub * ts

            @pl.when((lo < r0 + ts) & (hi > r0))
            def _():
                xs = xb_ref[r0:r0 + ts, :]
                act = (_silu(_bdot(xs, wgb_ref[...])) * _bdot(xs, wub_ref[...])).astype(BF16)
                y = _bdot(act, wdb_ref[...])
                rows = r0 + lax.broadcasted_iota(jnp.int32, (ts, 1), 0)
                acc_ref[r0:r0 + ts, :] += jnp.where((rows >= lo) & (rows < hi), y, 0.0)

    @pl.when((last == 1) & (hc == pl.num_programs(1) - 1))
    def _():
        for sub in range(nsub):
            o_ref[sub * ts:(sub + 1) * ts, :] = _pack_bf16_pairs(acc_ref[sub * ts:(sub + 1) * ts, :])


def _experts(xg, meta, w_gate_up, w_down, *, tile, th=512, ts=512):
    nrows = xg.shape[0]
    n_exp, hidden, d = w_down.shape
    n_hc = hidden // th
    ts = min(ts, tile)
    wgu = w_gate_up
    return pl.pallas_call(
        functools.partial(_expert_kernel, ts=ts),
        out_shape=jax.ShapeDtypeStruct((nrows, d // 2), jnp.uint32),
        grid_spec=pltpu.PrefetchScalarGridSpec(
            num_scalar_prefetch=1,
            grid=(meta.shape[1], n_hc),
            in_specs=[pl.BlockSpec((tile, d // 2), lambda v, c, m: (m[0, v], 0)),
                      pl.BlockSpec((1, d, th), lambda v, c, m: (m[1, v], 0, c)),
                      pl.BlockSpec((1, d, th), lambda v, c, m: (m[1, v], 0, c + n_hc)),
                      pl.BlockSpec((1, th, d), lambda v, c, m: (m[1, v], c, 0))],
            out_specs=pl.BlockSpec((tile, d // 2), lambda v, c, m: (m[0, v], 0)),
            scratch_shapes=[pltpu.VMEM((tile, d), F32), pltpu.VMEM((tile, d), BF16), pltpu.VMEM((tile, th), BF16),
                            pltpu.VMEM((d, th), BF16), pltpu.VMEM((d, th), BF16), pltpu.VMEM((th, d), BF16)]),
        compiler_params=_params(("arbitrary", "arbitrary"), 56),
        name="moe_experts",
    )(meta, xg, wgu, wgu, w_down)


def _combine_kernel(x_ref, gate_ref, fg_ref, y0_ref, y1_ref, o_ref, *, final_norm):
    g = gate_ref[...]
    out = x_ref[...] + g[:, 0:1] * _unpack_pairs_f32(y0_ref[0]) + g[:, 1:2] * _unpack_pairs_f32(y1_ref[0])
    if final_norm:
        out = _rms(out, fg_ref[...])
    o_ref[...] = out


def _combine(x2, gates_t, yk, final_gain, *, tm=512):
    ntok, d = x2.shape
    tm = min(tm, ntok)
    final_norm = final_gain is not None
    fg = (final_gain if final_norm else jnp.ones((d,), F32)).reshape(1, d)
    return pl.pallas_call(
        functools.partial(_combine_kernel, final_norm=final_norm),
        out_shape=jax.ShapeDtypeStruct((ntok, d), F32),
        grid=(ntok // tm,),
        in_specs=[pl.BlockSpec((tm, d), lambda i: (i, 0)),
                  pl.BlockSpec((tm, TOP_K), lambda i: (i, 0)),
                  pl.BlockSpec((1, d), lambda i: (0, 0)),
                  pl.BlockSpec((1, tm, d // 2), lambda i: (0, i, 0)),
                  pl.BlockSpec((1, tm, d // 2), lambda i: (1, i, 0))],
        out_specs=pl.BlockSpec((tm, d), lambda i: (i, 0)),
        compiler_params=_params(("parallel",), 40),
        name="moe_combine",
    )(x2, gates_t, fg, yk, yk)


def _moe_routed(x, ln, w_router, w_gate_up, w_down, *, tile=2048):
    bsz, seqlen, d = x.shape
    ntok = bsz * seqlen
    tile = min(tile, TOP_K * ntok)
    x2 = x.reshape(ntok, d)
    idx, gates, hp, pos, counts = _router(x2, ln, w_router)
    rank, meta = _moe_plan(idx, pos, counts, tile)
    xg = _sc_scatter_rows(hp, rank, TOP_K * ntok)
    y = _experts(xg, meta, w_gate_up, w_down, tile=tile)
    yk = _sc_gather_rows(y, rank, sub=64).reshape(TOP_K, ntok, d // 2)
    return gates.T, yk


def _moe_layer(x, ln, w_router, w_gate_up, w_down, *, final_gain=None, tile=2048):
    bsz, seqlen, d = x.shape
    gates_t, yk = _moe_routed(x, ln, w_router, w_gate_up, w_down, tile=tile)
    out = _combine(x.reshape(bsz * seqlen, d), gates_t, yk, final_gain)
    return out.reshape(bsz, seqlen, d)


def kernel(x, l0_ln1, l0_s5_lam_re, l0_s5_lam_im, l0_s5_log_dt, l0_s5_b_re, l0_s5_b_im, l0_s5_c_re, l0_s5_c_im, l0_s5_d, l0_s5_w_glu, l0_s5_b_glu, l0_ln2, l0_ffn_w_gate_up, l0_ffn_w_down, l1_ln1, l1_gla_w_in, l1_gla_w_g2, l1_gla_b_g2, l1_gla_norm, l1_gla_w_out, l1_ln2, l1_moe_router, l1_moe_w_gate_up, l1_moe_w_down, l2_ln1, l2_swa_w_qkv, l2_swa_b_qkv, l2_swa_sinks, l2_swa_w_out, l2_swa_b_out, l2_ln2, l2_ffn_w_gate_up, l2_ffn_w_down, l3_ln1, l3_s5_lam_re, l3_s5_lam_im, l3_s5_log_dt, l3_s5_b_re, l3_s5_b_im, l3_s5_c_re, l3_s5_c_im, l3_s5_d, l3_s5_w_glu, l3_s5_b_glu, l3_ln2, l3_moe_router, l3_moe_w_gate_up, l3_moe_w_down, ln_f):
    s5_params = ((l0_s5_lam_re, l0_s5_lam_im, l0_s5_log_dt, l0_s5_b_re, l0_s5_b_im, l0_s5_c_re, l0_s5_c_im),
                 (l3_s5_lam_re, l3_s5_lam_im, l3_s5_log_dt, l3_s5_b_re, l3_s5_b_im, l3_s5_c_re, l3_s5_c_im))
    s5_ops = jax.vmap(_s5_operators)(*(jnp.stack(pair) for pair in zip(*s5_params)))
    s5_ops = tuple(a.reshape((-1,) + a.shape[2:]) for a in s5_ops)
    x = _s5_layer(x, l0_ln1, s5_ops, 0, l0_s5_d, l0_s5_w_glu, l0_s5_b_glu)
    x = _dense_ffn_layer(x, l0_ln2, l0_ffn_w_gate_up, l0_ffn_w_down)
    x = _gla_layer(x, l1_ln1, l1_gla_w_in, l1_gla_w_g2, l1_gla_b_g2, l1_gla_norm, l1_gla_w_out)
    x = _moe_layer(x, l1_ln2, l1_moe_router, l1_moe_w_gate_up, l1_moe_w_down)
    x = _swa_layer(x, l2_ln1, l2_swa_w_qkv, l2_swa_b_qkv, l2_swa_sinks, l2_swa_w_out, l2_swa_b_out)
    x = _dense_ffn_layer(x, l2_ln2, l2_ffn_w_gate_up, l2_ffn_w_down)
    x = _s5_layer(x, l3_ln1, s5_ops, 1, l3_s5_d, l3_s5_w_glu, l3_s5_b_glu)
    return _moe_layer(x, l3_ln2, l3_moe_router, l3_moe_w_gate_up, l3_moe_w_down, final_gain=ln_f)
```

```python
import functools
import math

import jax
import jax.numpy as jnp
from jax import lax
from jax.experimental import pallas as pl
from jax.experimental.pallas import tpu as pltpu
from jax.experimental.pallas import tpu_sc as plsc

F32 = jnp.float32
BF16 = jnp.bfloat16
EPS = 1e-6
LANES = 128
MIB = 1 << 20

S5_GROUP = 16
S5_CHUNK = 16
S5_SLAB_GROUPS = LANES // S5_GROUP
S5_PITCH_PAD = 8
S5_SCAN_UNROLL = 8
S5_CAUSAL_BANDS = 8

GLA_HEADS = 4
GLA_GATE_RANK = 16
GLA_GATE_NORM = 16.0
GLA_CHUNK = 64

SWA_HEAD_DIM = 64
SWA_KV_HEADS = 2
SWA_WINDOW = 128
SWA_BLOCK = 128
MASK_VALUE = -1e30

TOP_K = 2


def _params(semantics, vmem_mib):
    return pltpu.CompilerParams(dimension_semantics=semantics, vmem_limit_bytes=vmem_mib * MIB)


def _resident(block_shape, index_map):
    return pl.BlockSpec(block_shape, index_map, pipeline_mode=pl.Buffered(1))


def _rms(xf, gain):
    return xf * lax.rsqrt(jnp.mean(xf * xf, axis=-1, keepdims=True) + EPS) * gain


def _gelu_tanh(x):
    return 0.5 * x * (1.0 + jnp.tanh(math.sqrt(2.0 / math.pi) * (x + 0.044715 * (x * x * x))))


def _silu(x):
    return x * jax.nn.sigmoid(x)


def _bdot(a, b):
    return jnp.dot(a, b, preferred_element_type=F32)


S5_ROW_TILE = 1024


def _s5_norm_kernel(x_ref, g_ref, o_ref, scr_ref, *, nloc):
    h = _rms(x_ref[0], g_ref[...])
    nslab = scr_ref.shape[0]
    for c in range(nslab):
        scr_ref[c] = h[:, c * LANES:(c + 1) * LANES]
    for s in range(S5_CHUNK):
        rows = pl.ds(s, nloc, stride=S5_CHUNK)
        o_ref[0, s] = jnp.concatenate([scr_ref[c, rows, :] for c in range(nslab)], axis=1).astype(o_ref.dtype)


def _s5_norm(x, gain):
    bsz, seqlen, d = x.shape
    nch = seqlen // S5_CHUNK
    tm = min(S5_ROW_TILE, seqlen)
    nloc = tm // S5_CHUNK
    return pl.pallas_call(
        functools.partial(_s5_norm_kernel, nloc=nloc),
        out_shape=jax.ShapeDtypeStruct((bsz, S5_CHUNK, nch, d), BF16),
        grid=(bsz, seqlen // tm),
        in_specs=[pl.BlockSpec((1, tm, d), lambda b, i: (b, i, 0)),
                  pl.BlockSpec((1, d), lambda b, i: (0, 0))],
        out_specs=pl.BlockSpec((1, S5_CHUNK, nloc, d), lambda b, i: (b, 0, i, 0)),
        scratch_shapes=[pltpu.VMEM((d // LANES, tm, LANES), F32)],
        compiler_params=_params(("parallel", "parallel"), 32),
        name="s5_norm",
    )(x, gain.reshape(1, d))


def _tiling_matrix(rows, cols):
    p = lax.broadcasted_iota(jnp.int32, (rows, cols), 0)
    c = lax.broadcasted_iota(jnp.int32, (rows, cols), 1)
    return jnp.where(c % rows == p, 1.0, 0.0).astype(BF16)


def _same_group(shape, row_group, col_group):
    r = lax.broadcasted_iota(jnp.int32, shape, 0)
    c = lax.broadcasted_iota(jnp.int32, shape, 1)
    return (r // row_group) == (c // col_group)


def _s5_build_operators(vw_ref, mw_ref, toep_ref, win_ref, wout_ref):
    tn = (((0,), (0,)), ((), ()))
    nstate = vw_ref.shape[-1]
    half = S5_SLAB_GROUPS * nstate
    rep_ch = _tiling_matrix(S5_GROUP, LANES)
    rep_st = _tiling_matrix(nstate, half)
    diag_in = _same_group((LANES, half), S5_GROUP, nstate)
    diag_out = _same_group((half, LANES), nstate, S5_GROUP)

    def out_block(q, r):
        e = lax.dot_general(mw_ref[0, 2 * q + r].astype(BF16), rep_ch, tn, preferred_element_type=F32)
        return jnp.where(diag_out, e, 0.0).astype(BF16)

    for a in range(S5_CHUNK):
        for r in range(2):
            e = _bdot(vw_ref[0, 2 * a + r].astype(BF16), rep_st)
            win_ref[a * LANES:(a + 1) * LANES, r * half:(r + 1) * half] = jnp.where(diag_in, e, 0.0).astype(BF16)
            wout_ref[r * half:(r + 1) * half, a * LANES:(a + 1) * LANES] = out_block(a + 1, r)
    b_bar = win_ref[(S5_CHUNK - 1) * LANES:S5_CHUNK * LANES, :]
    taps = [_bdot(b_bar, jnp.concatenate([out_block(0, 0), out_block(0, 1)], axis=0)).astype(BF16)]
    for j in range(1, S5_CHUNK):
        taps.append(_bdot(b_bar, wout_ref[:, (j - 1) * LANES:j * LANES]).astype(BF16))
    zero = jnp.zeros((LANES, LANES), BF16)
    for a in range(S5_CHUNK):
        for b in range(S5_CHUNK):
            toep_ref[a * LANES:(a + 1) * LANES, b * LANES:(b + 1) * LANES] = taps[b - a] if b >= a else zero


def _s5_conv_kernel(h_ref, vw_ref, mw_ref, a_ref, d_ref, o_ref, s_ref, toep_ref, win_ref, wout_ref,
                    *, nseq, nch):
    pitch = nch + S5_PITCH_PAD
    nl = a_ref.shape[1] // 2

    @pl.when(pl.program_id(1) == 0)
    def _():
        _s5_build_operators(vw_ref, mw_ref, toep_ref, win_ref, wout_ref)

    lhs = jnp.concatenate(
        [jnp.concatenate([h_ref[bl, s] for s in range(S5_CHUNK)], axis=1) for bl in range(nseq)], axis=0)
    bc = _bdot(lhs, win_ref[...])
    for bl in range(nseq):
        for j in range(2 * nl):
            s_ref[j, bl * pitch:bl * pitch + nch, :] = bc[bl * nch:(bl + 1) * nch, j * LANES:(j + 1) * LANES]
    a_re = [a_ref[0, j:j + 1, :] for j in range(nl)]
    a_im = [a_ref[0, nl + j:nl + j + 1, :] for j in range(nl)]

    def step(n, carry):
        p_re, p_im = carry
        rows = pl.ds(n, nseq, stride=pitch)
        n_re, n_im = [], []
        for j in range(nl):
            c_re = s_ref[j, rows, :]
            c_im = s_ref[nl + j, rows, :]
            s_ref[j, rows, :] = p_re[j]
            s_ref[nl + j, rows, :] = p_im[j]
            n_re.append(a_re[j] * p_re[j] - a_im[j] * p_im[j] + c_re)
            n_im.append(a_re[j] * p_im[j] + a_im[j] * p_re[j] + c_im)
        return tuple(n_re), tuple(n_im)

    def steps(m, carry):
        for u in range(S5_SCAN_UNROLL):
            carry = step(m * S5_SCAN_UNROLL + u, carry)
        return carry

    zeros = tuple(jnp.zeros((nseq, LANES), F32) for _ in range(nl))
    lax.fori_loop(0, nch // S5_SCAN_UNROLL, steps, (zeros, zeros))
    x_prev = jnp.concatenate(
        [jnp.concatenate([s_ref[j, bl * pitch:bl * pitch + nch, :] for j in range(2 * nl)], axis=1)
         for bl in range(nseq)], axis=0).astype(BF16)
    band = S5_CHUNK // S5_CAUSAL_BANDS
    y_bands = []
    for q in range(S5_CAUSAL_BANDS):
        kk = (q + 1) * band * LANES
        cols = slice(q * band * LANES, (q + 1) * band * LANES)
        y_bands.append(_bdot(lhs[:, :kk], toep_ref[:kk, cols]) + _bdot(x_prev, wout_ref[:, cols]))
    dskip = d_ref[0]
    for bl in range(nseq):
        for s in range(S5_CHUNK):
            ys = y_bands[s // band][bl * nch:(bl + 1) * nch, (s % band) * LANES:(s % band + 1) * LANES]
            ys = ys + dskip * h_ref[bl, s].astype(F32)
            o_ref[bl, s] = _gelu_tanh(ys).astype(o_ref.dtype)


def _s5_conv(hp, vw, mw, a_pack, d_skip, *, nseq, layer):
    bsz, _, nch, d = hp.shape
    nslab = d // LANES
    kdim = S5_CHUNK * LANES
    sdim = a_pack.shape[1] * LANES
    first = layer * nslab
    blk4 = lambda a: pl.BlockSpec((1,) + a.shape[1:], lambda c, b: (first + c, 0, 0, 0))
    return pl.pallas_call(
        functools.partial(_s5_conv_kernel, nseq=nseq, nch=nch),
        out_shape=jax.ShapeDtypeStruct(hp.shape, BF16),
        grid=(nslab, bsz // nseq),
        in_specs=[pl.BlockSpec((nseq, S5_CHUNK, nch, LANES), lambda c, b: (b, 0, 0, c)),
                  blk4(vw), blk4(mw),
                  pl.BlockSpec((1, sdim // LANES, LANES), lambda c, b: (first + c, 0, 0)),
                  pl.BlockSpec((1, 1, LANES), lambda c, b: (c, 0, 0))],
        out_specs=pl.BlockSpec((nseq, S5_CHUNK, nch, LANES), lambda c, b: (b, 0, 0, c)),
        scratch_shapes=[pltpu.VMEM((sdim // LANES, nseq * (nch + S5_PITCH_PAD), LANES), F32),
                        pltpu.VMEM((kdim, kdim), BF16),
                        pltpu.VMEM((kdim, sdim), BF16),
                        pltpu.VMEM((sdim, kdim), BF16)],
        compiler_params=_params(("arbitrary", "arbitrary"), 56),
        name="s5_conv",
    )(hp, vw, mw, a_pack, d_skip.reshape(nslab, 1, LANES))


def _s5_glu_kernel(y_ref, x_ref, w_ref, b_ref, o_ref, scr_ref, *, nloc):
    nslab = scr_ref.shape[0]
    y = jnp.concatenate([y_ref[0, s] for s in range(S5_CHUNK)], axis=0)
    u = y.astype(F32) * jax.nn.sigmoid(_bdot(y, w_ref[...]) + b_ref[...])
    for s in range(S5_CHUNK):
        rows = pl.ds(s, nloc, stride=S5_CHUNK)
        for c in range(nslab):
            scr_ref[c, rows, :] = u[s * nloc:(s + 1) * nloc, c * LANES:(c + 1) * LANES]
    o_ref[0] = x_ref[0] + jnp.concatenate([scr_ref[c] for c in range(nslab)], axis=1)


def _s5_glu(yp, x, w_glu, b_glu):
    bsz, seqlen, d = x.shape
    tm = min(S5_ROW_TILE, seqlen)
    nloc = tm // S5_CHUNK
    return pl.pallas_call(
        functools.partial(_s5_glu_kernel, nloc=nloc),
        out_shape=jax.ShapeDtypeStruct(x.shape, F32),
        grid=(bsz, seqlen // tm),
        in_specs=[pl.BlockSpec((1, S5_CHUNK, nloc, d), lambda b, i: (b, 0, i, 0)),
                  pl.BlockSpec((1, tm, d), lambda b, i: (b, i, 0)),
                  _resident((d, d), lambda b, i: (0, 0)),
                  pl.BlockSpec((1, d), lambda b, i: (0, 0))],
        out_specs=pl.BlockSpec((1, tm, d), lambda b, i: (b, i, 0)),
        scratch_shapes=[pltpu.VMEM((d // LANES, tm, LANES), F32)],
        compiler_params=_params(("parallel", "parallel"), 40),
        name="s5_glu",
    )(yp, x, w_glu.astype(BF16), b_glu.reshape(1, d))


def _s5_operators(lam_re, lam_im, log_dt, b_re, b_im, c_re, c_im):
    ngroups, nstate = lam_re.shape
    gpc = S5_SLAB_GROUPS
    nslab = ngroups // gpc
    dt = jnp.exp(log_dt)[:, None]
    j = jnp.arange(S5_CHUNK + 1, dtype=F32)[:, None, None]
    mag = jnp.exp(j * (lam_re * dt)[None])
    ang = j * (lam_im * dt)[None]
    pw_re, pw_im = mag * jnp.cos(ang), mag * jnp.sin(ang)
    num_re, num_im = pw_re[1] - 1.0, pw_im[1]
    den = lam_re * lam_re + lam_im * lam_im
    f_re = (num_re * lam_re + num_im * lam_im) / den
    f_im = (num_im * lam_re - num_re * lam_im) / den
    bb_re = f_re[..., None] * b_re - f_im[..., None] * b_im
    bb_im = f_re[..., None] * b_im + f_im[..., None] * b_re
    jr = (S5_CHUNK - 1) - jnp.arange(S5_CHUNK, dtype=F32)[:, None, None]
    mag_r = jnp.exp(jr * (lam_re * dt)[None])
    ang_r = jr * (lam_im * dt)[None]
    rev_re, rev_im = mag_r * jnp.cos(ang_r), mag_r * jnp.sin(ang_r)
    slabbed = lambda a: a.reshape(a.shape[0], nslab, gpc, nstate).transpose(1, 0, 2, 3)
    rv_re, rv_im = slabbed(rev_re)[:, :, :, None, :], slabbed(rev_im)[:, :, :, None, :]
    bt_re = bb_re.transpose(0, 2, 1).reshape(nslab, 1, gpc, S5_GROUP, nstate)
    bt_im = bb_im.transpose(0, 2, 1).reshape(nslab, 1, gpc, S5_GROUP, nstate)
    vw = jnp.stack([rv_re * bt_re - rv_im * bt_im, rv_re * bt_im + rv_im * bt_re], axis=2)
    vw = vw.reshape(nslab, 2 * S5_CHUNK, LANES, nstate)
    pc_re, pc_im = slabbed(pw_re)[:, :, None, :, :], slabbed(pw_im)[:, :, None, :, :]
    ct_re = c_re.reshape(nslab, gpc, S5_GROUP, nstate).transpose(0, 2, 1, 3)[:, None]
    ct_im = c_im.reshape(nslab, gpc, S5_GROUP, nstate).transpose(0, 2, 1, 3)[:, None]
    mw = jnp.stack([ct_re * pc_re - ct_im * pc_im, -(ct_re * pc_im + ct_im * pc_re)], axis=2)
    mw = mw.reshape(nslab, 2 * (S5_CHUNK + 1), S5_GROUP, gpc * nstate)
    half = gpc * nstate // LANES
    a_pack = jnp.concatenate([pw_re[S5_CHUNK].reshape(nslab, half, LANES),
                              pw_im[S5_CHUNK].reshape(nslab, half, LANES)], axis=1)
    return vw, mw, a_pack


def _s5_layer(x, ln, operators, layer, d_skip, w_glu, b_glu, *, nseq=4):
    vw, mw, a_pack = operators
    hp = _s5_norm(x, ln)
    yp = _s5_conv(hp, vw, mw, a_pack, d_skip, nseq=min(nseq, x.shape[0]), layer=layer)
    return _s5_glu(yp, x, w_glu, b_glu)


def _dense_ffn_kernel(x_ref, g_ref, wg_ref, wu_ref, wd_ref, o_ref):
    xf = x_ref[...]
    h = _rms(xf, g_ref[...]).astype(BF16)
    act = (_silu(_bdot(h, wg_ref[...])) * _bdot(h, wu_ref[...])).astype(BF16)
    o_ref[...] = xf + _bdot(act, wd_ref[...])


def _dense_ffn_layer(x, ln, w_gate_up, w_down, *, tm=512):
    bsz, seqlen, d = x.shape
    ntok = bsz * seqlen
    hidden = w_down.shape[0]
    tm = min(tm, ntok)
    wgu = w_gate_up.astype(BF16)
    out = pl.pallas_call(
        _dense_ffn_kernel,
        out_shape=jax.ShapeDtypeStruct((ntok, d), F32),
        grid=(ntok // tm,),
        in_specs=[pl.BlockSpec((tm, d), lambda i: (i, 0)),
                  pl.BlockSpec((1, d), lambda i: (0, 0)),
                  _resident((d, hidden), lambda i: (0, 0)),
                  _resident((d, hidden), lambda i: (0, 1)),
                  _resident((hidden, d), lambda i: (0, 0))],
        out_specs=pl.BlockSpec((tm, d), lambda i: (i, 0)),
        compiler_params=_params(("parallel",), 56),
        name="dense_ffn",
    )(x.reshape(ntok, d), ln.reshape(1, d), wgu, wgu, w_down.astype(BF16))
    return out.reshape(bsz, seqlen, d)


def _log_sigmoid(z):
    return jnp.minimum(z, 0.0) - jnp.log(1.0 + jnp.exp(-jnp.abs(z)))


def _gla_kernel(x_ref, ln_ref, wm_ref, wgl_ref, wg2_ref, bg2_ref, gn_ref, wo_ref, o_ref, st_ref,
                *, tq, dk, dv, heads):
    hdk, hdv = dk // heads, dv // heads
    chunk = GLA_CHUNK
    nt = (((1,), (1,)), ((), ()))
    tn = (((0,), (0,)), ((), ()))

    @pl.when(pl.program_id(1) == 0)
    def _():
        st_ref[...] = jnp.zeros_like(st_ref)

    xf = x_ref[0]
    h = _rms(xf, ln_ref[...]).astype(BF16)
    proj = _bdot(h, wm_ref[...])
    glow = _bdot(h, wgl_ref[...]).astype(BF16)
    la = _log_sigmoid(_bdot(glow, wg2_ref[...]) + bg2_ref[...]) * (1.0 / GLA_GATE_NORM)
    row = lax.broadcasted_iota(jnp.int32, (chunk, chunk), 0)
    col = lax.broadcasted_iota(jnp.int32, (chunk, chunk), 1)
    causal = row >= col
    tri = jnp.where(causal, 1.0, 0.0).astype(BF16)
    scale = hdk ** -0.5
    outs = []
    for c in range(tq // chunk):
        r0 = c * chunk
        la_c = la[r0:r0 + chunk, :]
        la_hi = la_c.astype(BF16)
        la_lo = (la_c - la_hi.astype(F32)).astype(BF16)
        gcum_all = _bdot(tri, la_hi) + _bdot(tri, la_lo)
        head_out = []
        for hd in range(heads):
            gcum = gcum_all[:, hd * hdk:(hd + 1) * hdk]
            g_last = gcum[chunk - 1:chunk, :]
            q_c = proj[r0:r0 + chunk, hd * hdk:(hd + 1) * hdk] * scale
            k_c = proj[r0:r0 + chunk, dk + hd * hdk:dk + (hd + 1) * hdk]
            v_c = proj[r0:r0 + chunk, 2 * dk + hd * hdv:2 * dk + (hd + 1) * hdv].astype(BF16)
            q_s = (q_c * jnp.exp(gcum)).astype(BF16)
            k_s = (k_c * jnp.exp(-gcum)).astype(BF16)
            k_end = (k_c * jnp.exp(g_last - gcum)).astype(BF16)
            scores = lax.dot_general(q_s, k_s, nt, preferred_element_type=F32)
            scores = jnp.where(causal, scores, 0.0).astype(BF16)
            state_t = st_ref[hd]
            o = _bdot(scores, v_c) + lax.dot_general(q_s, state_t.astype(BF16), nt,
                                                     preferred_element_type=F32)
            kv_t = lax.dot_general(v_c, k_end, tn, preferred_element_type=F32)
            st_ref[hd] = state_t * jnp.exp(g_last) + kv_t
            head_out.append(o * lax.rsqrt(jnp.mean(o * o, axis=-1, keepdims=True) + EPS))
        outs.append(jnp.concatenate(head_out, axis=1))
    o_all = jnp.concatenate(outs, axis=0)
    r = proj[:, 2 * dk + dv:]
    o_all = (o_all * gn_ref[...] * _silu(r)).astype(BF16)
    o_ref[0] = xf + _bdot(o_all, wo_ref[...])


def _gla_layer(x, ln, w_in, w_g2, b_g2, g_norm, w_out, *, tq=256):
    bsz, seqlen, d = x.shape
    dk = w_g2.shape[1]
    dv = w_out.shape[0]
    nmain = 2 * dk + 2 * dv
    tq = min(tq, seqlen)
    w_main = w_in[:, :nmain].astype(BF16)
    w_glow = jnp.pad(w_in[:, nmain:], ((0, 0), (0, LANES - GLA_GATE_RANK))).astype(BF16)
    w_g2p = jnp.pad(w_g2, ((0, LANES - GLA_GATE_RANK), (0, 0))).astype(BF16)
    hdk, hdv = dk // GLA_HEADS, dv // GLA_HEADS
    const = lambda b, t: (0, 0)
    return pl.pallas_call(
        functools.partial(_gla_kernel, tq=tq, dk=dk, dv=dv, heads=GLA_HEADS),
        out_shape=jax.ShapeDtypeStruct(x.shape, F32),
        grid=(bsz, seqlen // tq),
        in_specs=[pl.BlockSpec((1, tq, d), lambda b, t: (b, t, 0)),
                  pl.BlockSpec((1, d), const),
                  _resident((d, nmain), const),
                  _resident((d, LANES), const),
                  _resident((LANES, dk), const),
                  pl.BlockSpec((1, dk), const),
                  pl.BlockSpec((1, dv), const),
                  _resident((dv, d), const)],
        out_specs=pl.BlockSpec((1, tq, d), lambda b, t: (b, t, 0)),
        scratch_shapes=[pltpu.VMEM((GLA_HEADS, hdv, hdk), F32)],
        compiler_params=_params(("parallel", "arbitrary"), 48),
        name="gla",
    )(x, ln.reshape(1, d), w_main, w_glow, w_g2p, b_g2.reshape(1, dk), g_norm.reshape(1, dv),
      w_out.astype(BF16))


LOG2E = math.log2(math.e)
SWA_SLOT_UNROLL = 4


def _swa_kernel(sink_ref, x_ref, ln_ref, wqkv_ref, bqkv_ref, wo_ref, bo_ref, o_ref, k_ref, v_ref,
                bias_ref, q_ref, a_ref, *, tq, q_heads):
    group = q_heads // SWA_KV_HEADS
    blk = SWA_BLOCK
    nt = (((1,), (1,)), ((), ()))
    b = pl.program_id(0)
    t = pl.program_id(1)
    nq = group * LANES

    @pl.when((b == 0) & (t == 0))
    def _():
        qi = lax.broadcasted_iota(jnp.int32, (blk, 2 * blk), 0)
        kj = lax.broadcasted_iota(jnp.int32, (blk, 2 * blk), 1)
        dist = qi + blk - kj
        in_window = (dist >= 0) & (dist < SWA_WINDOW)
        for hq in range(q_heads):
            slope = 2.0 ** (-8.0 * (hq + 1) / q_heads)
            bias_ref[hq] = jnp.where(in_window, -(slope * LOG2E) * dist.astype(F32), MASK_VALUE)

    @pl.when(t == 0)
    def _():
        k_ref[0:blk, :] = jnp.zeros((blk, LANES), BF16)
        v_ref[0:blk, :] = jnp.zeros((blk, LANES), BF16)

    xf = x_ref[0]
    h = _rms(xf, ln_ref[...]).astype(BF16)
    qkv = _bdot(h, wqkv_ref[...]) + bqkv_ref[...]
    for j in range(group):
        q_ref[j] = qkv[:, j * LANES:(j + 1) * LANES].astype(BF16)
    k_ref[blk:blk + tq, :] = qkv[:, nq:nq + LANES].astype(BF16)
    v_ref[blk:blk + tq, :] = qkv[:, nq + LANES:nq + 2 * LANES].astype(BF16)
    kj_row = lax.broadcasted_iota(jnp.int32, (1, 2 * blk), 1)
    no_prev = jnp.where(kj_row < blk, jnp.where(t == 0, MASK_VALUE, 0.0), 0.0)
    low_half = lax.broadcasted_iota(jnp.int32, (1, LANES), 1) < SWA_HEAD_DIM
    halves = (low_half, jnp.logical_not(low_half))

    def slots(jj, carry):
        for u in range(SWA_SLOT_UNROLL):
            j = jj * SWA_SLOT_UNROLL + u
            for i in range(tq // blk):
                r0 = i * blk
                q_slot = q_ref[j, r0:r0 + blk, :]
                outs = []
                for kh in range(SWA_KV_HEADS):
                    hq = kh * group + j
                    sink = sink_ref[hq] * LOG2E
                    q_h = jnp.where(halves[kh], q_slot, jnp.zeros_like(q_slot))
                    s = lax.dot_general(q_h, k_ref[r0:r0 + 2 * blk, :], nt, preferred_element_type=F32) + bias_ref[hq]
                    if i == 0:
                        s = s + no_prev
                    m = jnp.maximum(jnp.max(s, axis=-1, keepdims=True), sink)
                    p = jnp.exp2(s - m)
                    denom = jnp.sum(p, axis=-1, keepdims=True) + jnp.exp2(sink - m)
                    outs.append(_bdot(p.astype(BF16), v_ref[r0:r0 + 2 * blk, :]) * (1.0 / denom))
                a_ref[j, r0:r0 + blk, :] = jnp.where(low_half, outs[0], outs[1]).astype(BF16)
        return carry

    lax.fori_loop(0, group // SWA_SLOT_UNROLL, slots, 0)
    k_ref[0:blk, :] = k_ref[tq:tq + blk, :]
    v_ref[0:blk, :] = v_ref[tq:tq + blk, :]
    o_all = jnp.concatenate([a_ref[j] for j in range(group)], axis=1)
    o_ref[0] = xf + _bdot(o_all, wo_ref[...]) + bo_ref[...]


def _swa_layer(x, ln, w_qkv, b_qkv, sinks, w_out, b_out, *, tq=512):
    bsz, seqlen, d = x.shape
    hd = SWA_HEAD_DIM
    q_heads = sinks.shape[0]
    group = q_heads // SWA_KV_HEADS
    nq = q_heads * hd
    tq = min(tq, seqlen)
    q_scale = hd ** -0.5 * LOG2E
    wq = (w_qkv[:, :nq] * q_scale).reshape(d, SWA_KV_HEADS, group, hd).transpose(0, 2, 1, 3).reshape(d, nq)
    bq = (b_qkv[:nq] * q_scale).reshape(SWA_KV_HEADS, group, hd).transpose(1, 0, 2).reshape(nq)
    w_all = jnp.concatenate([wq, w_qkv[:, nq:]], axis=1).astype(BF16)
    b_all = jnp.concatenate([bq, b_qkv[nq:]]).reshape(1, -1)
    wo = w_out.reshape(SWA_KV_HEADS, group, hd, d).transpose(1, 0, 2, 3).reshape(nq, d).astype(BF16)
    nall = w_all.shape[1]
    const = lambda b, t, s: (0, 0)
    return pl.pallas_call(
        functools.partial(_swa_kernel, tq=tq, q_heads=q_heads),
        out_shape=jax.ShapeDtypeStruct(x.shape, F32),
        grid_spec=pltpu.PrefetchScalarGridSpec(
            num_scalar_prefetch=1,
            grid=(bsz, seqlen // tq),
            in_specs=[pl.BlockSpec((1, tq, d), lambda b, t, s: (b, t, 0)),
                      pl.BlockSpec((1, d), const),
                      _resident((d, nall), const),
                      pl.BlockSpec((1, nall), const),
                      _resident((nq, d), const),
                      pl.BlockSpec((1, d), const)],
            out_specs=pl.BlockSpec((1, tq, d), lambda b, t, s: (b, t, 0)),
            scratch_shapes=[pltpu.VMEM((SWA_BLOCK + tq, LANES), BF16), pltpu.VMEM((SWA_BLOCK + tq, LANES), BF16),
                            pltpu.VMEM((q_heads, SWA_BLOCK, 2 * SWA_BLOCK), F32),
                            pltpu.VMEM((group, tq, LANES), BF16), pltpu.VMEM((group, tq, LANES), BF16)]),
        compiler_params=_params(("arbitrary", "arbitrary"), 48),
        name="swa",
    )(sinks, x, ln.reshape(1, d), w_all, b_all, wo, b_out.reshape(1, d))


def _router_kernel(x_ref, ln_ref, whi_ref, wlo_ref, idx_ref, gate_ref, hp_ref, pos_ref, count_ref, tri_ref):
    nt = (((1,), (1,)), ((), ()))
    h = _rms(x_ref[...], ln_ref[...])
    h_hi = h.astype(BF16)
    h_lo = (h - h_hi.astype(F32)).astype(BF16)
    w_hi, w_lo = whi_ref[...], wlo_ref[...]
    logits = (lax.dot_general(w_hi, h_hi, nt, preferred_element_type=F32)
              + lax.dot_general(w_hi, h_lo, nt, preferred_element_type=F32)
              + lax.dot_general(w_lo, h_hi, nt, preferred_element_type=F32))
    n_exp = logits.shape[0]
    eid = lax.broadcasted_iota(jnp.int32, logits.shape, 0)
    m1 = jnp.max(logits, axis=0, keepdims=True)
    i1 = jnp.min(jnp.where(logits == m1, eid, n_exp), axis=0, keepdims=True)
    rest = jnp.where(eid == i1, -jnp.inf, logits)
    m2 = jnp.max(rest, axis=0, keepdims=True)
    i2 = jnp.min(jnp.where(rest == m2, eid, n_exp), axis=0, keepdims=True)
    e2 = jnp.exp(m2 - m1)
    g1 = 1.0 / (1.0 + e2)
    idx_ref[...] = jnp.concatenate([i1, i2], axis=0)
    gate_ref[...] = jnp.concatenate([g1, e2 * g1], axis=0)
    hp_ref[...] = _pack_bf16_pairs(h)
    tm = logits.shape[1]

    @pl.when(pl.program_id(0) == 0)
    def _():
        count_ref[...] = jnp.zeros_like(count_ref)
        r = lax.broadcasted_iota(jnp.int32, (tm, tm), 0)
        c = lax.broadcasted_iota(jnp.int32, (tm, tm), 1)
        tri_ref[...] = jnp.where(r < c, 1.0, 0.0).astype(BF16)

    pick1 = jnp.where(eid == i1, 1.0, 0.0)
    pick2 = jnp.where(eid == i2, 1.0, 0.0)
    picks = pick1 + pick2
    before = _bdot(picks.astype(BF16), tri_ref[...]) + count_ref[:, 0:1]
    pos_ref[...] = jnp.concatenate([jnp.sum(pick1 * before, axis=0, keepdims=True),
                                    jnp.sum(pick2 * before, axis=0, keepdims=True)], axis=0).astype(jnp.int32)
    count_ref[...] = count_ref[...] + jnp.sum(picks, axis=1, keepdims=True)


def _router(x2, ln, w_router, *, tm=512):
    ntok, d = x2.shape
    n_exp = w_router.shape[1]
    tm = min(tm, ntok)
    wt = w_router.T
    w_hi = wt.astype(BF16)
    w_lo = (wt - w_hi.astype(F32)).astype(BF16)
    return pl.pallas_call(
        _router_kernel,
        out_shape=(jax.ShapeDtypeStruct((TOP_K, ntok), jnp.int32), jax.ShapeDtypeStruct((TOP_K, ntok), F32),
                   jax.ShapeDtypeStruct((ntok, d // 2), jnp.uint32),
                   jax.ShapeDtypeStruct((TOP_K, ntok), jnp.int32), jax.ShapeDtypeStruct((n_exp, LANES), F32)),
        grid=(ntok // tm,),
        in_specs=[pl.BlockSpec((tm, d), lambda i: (i, 0)),
                  pl.BlockSpec((1, d), lambda i: (0, 0)),
                  pl.BlockSpec((n_exp, d), lambda i: (0, 0)),
                  pl.BlockSpec((n_exp, d), lambda i: (0, 0))],
        out_specs=(pl.BlockSpec((TOP_K, tm), lambda i: (0, i)), pl.BlockSpec((TOP_K, tm), lambda i: (0, i)),
                   pl.BlockSpec((tm, d // 2), lambda i: (i, 0)),
                   pl.BlockSpec((TOP_K, tm), lambda i: (0, i)), pl.BlockSpec((n_exp, LANES), lambda i: (0, 0))),
        scratch_shapes=[pltpu.VMEM((tm, tm), BF16)],
        compiler_params=_params(("arbitrary",), 32),
        name="moe_router",
    )(x2, ln.reshape(1, d), w_hi, w_lo)


def _moe_plan(idx, pos, counts, tile):
    n_exp = counts.shape[0]
    nslots = idx.size
    counts = counts[:, 0].astype(jnp.int32)
    ends = jnp.cumsum(counts)
    offs = ends - counts
    experts = jnp.arange(n_exp, dtype=jnp.int32).reshape(n_exp, 1, 1)
    rank = pos + jnp.sum(jnp.where(idx[None] == experts, offs.reshape(n_exp, 1, 1), 0), axis=0)
    n_tiles = nslots // tile
    n_visits = n_tiles + n_exp - 1
    first_tile = offs // tile
    last_tile = (ends - 1) // tile
    nvis = jnp.where(counts > 0, last_tile - first_tile + 1, 0)
    vend = jnp.cumsum(nvis)
    vstart = vend - nvis
    total = vend[-1]
    v = jnp.arange(n_visits, dtype=jnp.int32)
    vc = jnp.minimum(v, total - 1)
    e = jnp.minimum(jnp.sum((vc[:, None] >= vend[None, :]).astype(jnp.int32), axis=1), n_exp - 1)
    sel = (e[:, None] == jnp.arange(n_exp, dtype=jnp.int32)[None, :]).astype(jnp.int32)
    pick = lambda a: jnp.sum(sel * a[None, :], axis=1)
    tile_id = pick(first_tile) + vc - pick(vstart)
    lo = jnp.maximum(pick(offs), tile_id * tile) - tile_id * tile
    hi = jnp.minimum(pick(ends), (tile_id + 1) * tile) - tile_id * tile
    valid = v < total
    lo = jnp.where(valid, lo, 0)
    hi = jnp.where(valid, hi, 0)
    prev_tile = jnp.concatenate([jnp.full((1,), -1, jnp.int32), tile_id[:-1]])
    first = (valid & (tile_id != prev_tile)).astype(jnp.int32)
    next_tile = jnp.concatenate([tile_id[1:], jnp.full((1,), -1, jnp.int32)])
    last = (valid & ((tile_id != next_tile) | (v == total - 1))).astype(jnp.int32)
    meta = jnp.stack([tile_id, e, lo, hi, first, last]).astype(jnp.int32)
    return rank.astype(jnp.int32), meta


def _pack_bf16_pairs(h):
    half = h.shape[1] // 2
    bits = lax.bitcast_convert_type(h.astype(BF16).astype(F32), jnp.uint32)
    return (bits[:, half:] & jnp.uint32(0xFFFF0000)) | (bits[:, :half] >> 16)


def _unpack_pairs_f32(u):
    lo = lax.bitcast_convert_type(u << 16, F32)
    hi = lax.bitcast_convert_type(u & jnp.uint32(0xFFFF0000), F32)
    return jnp.concatenate([lo, hi], axis=1)


def _unpack_bf16_pairs(u):
    return _unpack_pairs_f32(u).astype(BF16)


SC_CORES = 2
SC_SUBCORES = 16
SC_INDEX_WINDOW = 128


def _sc_mesh():
    return plsc.VectorSubcoreMesh(core_axis_name="c", subcore_axis_name="s")


def _sc_worker_id():
    return lax.axis_index("c") * SC_SUBCORES + lax.axis_index("s")


def _sc_scatter_rows(src, rank, nrows):
    ntok, width = src.shape
    win = SC_INDEX_WINDOW
    per = ntok // (SC_CORES * SC_SUBCORES)

    @pl.kernel(out_type=jax.ShapeDtypeStruct((nrows, width), src.dtype), mesh=_sc_mesh(),
               scratch_types=[pltpu.VMEM((1, win), jnp.int32)] * TOP_K + [pltpu.VMEM((win, width), src.dtype)],
               name="moe_dispatch_sc")
    def scatter(src_hbm, rank_hbm, o_hbm, *scratch):
        idx_vmem, buf = scratch[:TOP_K], scratch[TOP_K]
        wid = _sc_worker_id()

        @pl.loop(0, per // win)
        def _(blk):
            base = wid * per + blk * win
            for k in range(TOP_K):
                pltpu.sync_copy(rank_hbm.at[pl.ds(k, 1), pl.ds(base, win)], idx_vmem[k])
            pltpu.sync_copy(src_hbm.at[pl.ds(base, win)], buf)
            for k in range(TOP_K):
                pltpu.sync_copy(buf, o_hbm.at[idx_vmem[k].at[0]])

    return scatter(src, rank)


def _sc_gather_rows(src, rank, *, sub=32):
    nslot, ntok = rank.shape
    n = nslot * ntok
    width = src.shape[1]
    win = SC_INDEX_WINDOW
    per = n // (SC_CORES * SC_SUBCORES)
    nsub = win // sub

    @pl.kernel(out_type=jax.ShapeDtypeStruct((n, width), src.dtype), mesh=_sc_mesh(),
               scratch_types=[pltpu.VMEM((1, win), jnp.int32)] + [pltpu.VMEM((sub, width), src.dtype)] * 2
               + [pltpu.SemaphoreType.DMA] * 4,
               name="moe_gather_sc")
    def gather(src_hbm, idx_hbm, o_hbm, i_vmem, buf0, buf1, g0, g1, w0, w1):
        bufs, gsem, wsem = (buf0, buf1), (g0, g1), (w0, w1)
        wid = _sc_worker_id()

        @pl.loop(0, per // win)
        def _(blk):
            base = wid * per + blk * win
            pltpu.sync_copy(idx_hbm.at[pl.ds(base // ntok, 1), pl.ds(base % ntok, win)], i_vmem)
            gathers = [pltpu.make_async_copy(src_hbm.at[i_vmem.at[0, pl.ds(sub * j, sub)]], bufs[j % 2], gsem[j % 2])
                       for j in range(nsub)]
            writes = [pltpu.make_async_copy(bufs[j % 2], o_hbm.at[pl.ds(base + sub * j, sub)], wsem[j % 2])
                      for j in range(nsub)]
            gathers[0].start()
            for j in range(nsub):
                if j + 1 < nsub:
                    if j >= 1:
                        writes[j - 1].wait()
                    gathers[j + 1].start()
                gathers[j].wait()
                writes[j].start()
            writes[nsub - 2].wait()
            writes[nsub - 1].wait()

    return gather(src, rank)


MXU_N = 256


def _expert_kernel(meta_ref, x_ref, wg_ref, wu_ref, wd_ref, o_ref, acc_ref, xb_ref, act_ref, wgb_ref, wub_ref,
                   wdb_ref, *, ts):
    v = pl.program_id(0)
    hc = pl.program_id(1)
    lo, hi, first, last = meta_ref[2, v], meta_ref[3, v], meta_ref[4, v], meta_ref[5, v]
    tile, d = acc_ref.shape
    nsub = tile // ts
    th = wgb_ref.shape[1]
    wide = (hi - lo) * 2 > tile

    @pl.when(hc == 0)
    def _():
        for sub in range(nsub):
            xb_ref[sub * ts:(sub + 1) * ts, :] = _unpack_bf16_pairs(x_ref[sub * ts:(sub + 1) * ts, :])

    @pl.when((first == 1) & (hc == 0))
    def _():
        acc_ref[...] = jnp.zeros_like(acc_ref)

    @pl.when(wide)
    def _():
        rows = lax.broadcasted_iota(jnp.int32, (tile, 1), 0)
        mine = (rows >= lo) & (rows < hi)
        for n in range(th // MXU_N):
            cols = slice(n * MXU_N, (n + 1) * MXU_N)
            gate = _bdot(xb_ref[...], wg_ref[0, :, cols].astype(BF16))
            up = _bdot(xb_ref[...], wu_ref[0, :, cols].astype(BF16))
            act_ref[:, cols] = (_silu(gate) * up).astype(BF16)
        for n in range(d // MXU_N):
            cols = slice(n * MXU_N, (n + 1) * MXU_N)
            acc_ref[:, cols] += jnp.where(mine, _bdot(act_ref[...], wd_ref[0, :, cols].astype(BF16)), 0.0)

    @pl.when(jnp.logical_not(wide) & (hi > lo))
    def _():
        wgb_ref[...] = wg_ref[0].astype(BF16)
        wub_ref[...] = wu_ref[0].astype(BF16)
        wdb_ref[...] = wd_ref[0].astype(BF16)
        for sub in range(nsub):
            r0 = sub * ts

            @pl.when((lo < r0 + ts) & (hi > r0))
            def _():
                xs = xb_ref[r0:r0 + ts, :]
                act = (_silu(_bdot(xs, wgb_ref[...])) * _bdot(xs, wub_ref[...])).astype(BF16)
                y = _bdot(act, wdb_ref[...])
                rows = r0 + lax.broadcasted_iota(jnp.int32, (ts, 1), 0)
                acc_ref[r0:r0 + ts, :] += jnp.where((rows >= lo) & (rows < hi), y, 0.0)

    @pl.when((last == 1) & (hc == pl.num_programs(1) - 1))
    def _():
        for sub in range(nsub):
            o_ref[sub * ts:(sub + 1) * ts, :] = _pack_bf16_pairs(acc_ref[sub * ts:(sub + 1) * ts, :])


def _experts(xg, meta, w_gate_up, w_down, *, tile, th=512, ts=512):
    nrows = xg.shape[0]
    n_exp, hidden, d = w_down.shape
    n_hc = hidden // th
    ts = min(ts, tile)
    wgu = w_gate_up
    return pl.pallas_call(
        functools.partial(_expert_kernel, ts=ts),
        out_shape=jax.ShapeDtypeStruct((nrows, d // 2), jnp.uint32),
        grid_spec=pltpu.PrefetchScalarGridSpec(
            num_scalar_prefetch=1,
            grid=(meta.shape[1], n_hc),
            in_specs=[pl.BlockSpec((tile, d // 2), lambda v, c, m: (m[0, v], 0)),
                      pl.BlockSpec((1, d, th), lambda v, c, m: (m[1, v], 0, c)),
                      pl.BlockSpec((1, d, th), lambda v, c, m: (m[1, v], 0, c + n_hc)),
                      pl.BlockSpec((1, th, d), lambda v, c, m: (m[1, v], c, 0))],
            out_specs=pl.BlockSpec((tile, d // 2), lambda v, c, m: (m[0, v], 0)),
            scratch_shapes=[pltpu.VMEM((tile, d), F32), pltpu.VMEM((tile, d), BF16), pltpu.VMEM((tile, th), BF16),
                            pltpu.VMEM((d, th), BF16), pltpu.VMEM((d, th), BF16), pltpu.VMEM((th, d), BF16)]),
        compiler_params=_params(("arbitrary", "arbitrary"), 56),
        name="moe_experts",
    )(meta, xg, wgu, wgu, w_down)


def _combine_kernel(x_ref, gate_ref, fg_ref, y0_ref, y1_ref, o_ref, *, final_norm):
    g = gate_ref[...]
    out = x_ref[...] + g[:, 0:1] * _unpack_pairs_f32(y0_ref[0]) + g[:, 1:2] * _unpack_pairs_f32(y1_ref[0])
    if final_norm:
        out = _rms(out, fg_ref[...])
    o_ref[...] = out


def _combine(x2, gates_t, yk, final_gain, *, tm=512):
    ntok, d = x2.shape
    tm = min(tm, ntok)
    final_norm = final_gain is not None
    fg = (final_gain if final_norm else jnp.ones((d,), F32)).reshape(1, d)
    return pl.pallas_call(
        functools.partial(_combine_kernel, final_norm=final_norm),
        out_shape=jax.ShapeDtypeStruct((ntok, d), F32),
        grid=(ntok // tm,),
        in_specs=[pl.BlockSpec((tm, d), lambda i: (i, 0)),
                  pl.BlockSpec((tm, TOP_K), lambda i: (i, 0)),
                  pl.BlockSpec((1, d), lambda i: (0, 0)),
                  pl.BlockSpec((1, tm, d // 2), lambda i: (0, i, 0)),
                  pl.BlockSpec((1, tm, d // 2), lambda i: (1, i, 0))],
        out_specs=pl.BlockSpec((tm, d), lambda i: (i, 0)),
        compiler_params=_params(("parallel",), 40),
        name="moe_combine",
    )(x2, gates_t, fg, yk, yk)


def _moe_routed(x, ln, w_router, w_gate_up, w_down, *, tile=2048):
    bsz, seqlen, d = x.shape
    ntok = bsz * seqlen
    tile = min(tile, TOP_K * ntok)
    x2 = x.reshape(ntok, d)
    idx, gates, hp, pos, counts = _router(x2, ln, w_router)
    rank, meta = _moe_plan(idx, pos, counts, tile)
    xg = _sc_scatter_rows(hp, rank, TOP_K * ntok)
    y = _experts(xg, meta, w_gate_up, w_down, tile=tile)
    yk = _sc_gather_rows(y, rank, sub=64).reshape(TOP_K, ntok, d // 2)
    return gates.T, yk


def _moe_layer(x, ln, w_router, w_gate_up, w_down, *, final_gain=None, tile=2048):
    bsz, seqlen, d = x.shape
    gates_t, yk = _moe_routed(x, ln, w_router, w_gate_up, w_down, tile=tile)
    out = _combine(x.reshape(bsz * seqlen, d), gates_t, yk, final_gain)
    return out.reshape(bsz, seqlen, d)


def kernel(x, l0_ln1, l0_s5_lam_re, l0_s5_lam_im, l0_s5_log_dt, l0_s5_b_re, l0_s5_b_im, l0_s5_c_re, l0_s5_c_im, l0_s5_d, l0_s5_w_glu, l0_s5_b_glu, l0_ln2, l0_ffn_w_gate_up, l0_ffn_w_down, l1_ln1, l1_gla_w_in, l1_gla_w_g2, l1_gla_b_g2, l1_gla_norm, l1_gla_w_out, l1_ln2, l1_moe_router, l1_moe_w_gate_up, l1_moe_w_down, l2_ln1, l2_swa_w_qkv, l2_swa_b_qkv, l2_swa_sinks, l2_swa_w_out, l2_swa_b_out, l2_ln2, l2_ffn_w_gate_up, l2_ffn_w_down, l3_ln1, l3_s5_lam_re, l3_s5_lam_im, l3_s5_log_dt, l3_s5_b_re, l3_s5_b_im, l3_s5_c_re, l3_s5_c_im, l3_s5_d, l3_s5_w_glu, l3_s5_b_glu, l3_ln2, l3_moe_router, l3_moe_w_gate_up, l3_moe_w_down, ln_f):
    s5_params = ((l0_s5_lam_re, l0_s5_lam_im, l0_s5_log_dt, l0_s5_b_re, l0_s5_b_im, l0_s5_c_re, l0_s5_c_im),
                 (l3_s5_lam_re, l3_s5_lam_im, l3_s5_log_dt, l3_s5_b_re, l3_s5_b_im, l3_s5_c_re, l3_s5_c_im))
    s5_ops = jax.vmap(_s5_operators)(*(jnp.stack(pair) for pair in zip(*s5_params)))
    s5_ops = tuple(a.reshape((-1,) + a.shape[2:]) for a in s5_ops)
    x = _s5_layer(x, l0_ln1, s5_ops, 0, l0_s5_d, l0_s5_w_glu, l0_s5_b_glu)
    x = _dense_ffn_layer(x, l0_ln2, l0_ffn_w_gate_up, l0_ffn_w_down)
    x = _gla_layer(x, l1_ln1, l1_gla_w_in, l1_gla_w_g2, l1_gla_b_g2, l1_gla_norm, l1_gla_w_out)
    x = _moe_layer(x, l1_ln2, l1_moe_router, l1_moe_w_gate_up, l1_moe_w_down)
    x = _swa_layer(x, l2_ln1, l2_swa_w_qkv, l2_swa_b_qkv, l2_swa_sinks, l2_swa_w_out, l2_swa_b_out)
    x = _dense_ffn_layer(x, l2_ln2, l2_ffn_w_gate_up, l2_ffn_w_down)
    x = _s5_layer(x, l3_ln1, s5_ops, 1, l3_s5_d, l3_s5_w_glu, l3_s5_b_glu)
    return _moe_layer(x, l3_ln2, l3_moe_router, l3_moe_w_gate_up, l3_moe_w_down, final_gain=ln_f)
```

```python
import functools
import math

import jax
import jax.numpy as jnp
from jax import lax
from jax.experimental import pallas as pl
from jax.experimental.pallas import tpu as pltpu
from jax.experimental.pallas import tpu_sc as plsc

F32 = jnp.float32
BF16 = jnp.bfloat16
EPS = 1e-6
LANES = 128
MIB = 1 << 20

S5_GROUP = 16
S5_CHUNK = 16
S5_SLAB_GROUPS = LANES // S5_GROUP
S5_PITCH_PAD = 8
S5_SCAN_UNROLL = 8
S5_CAUSAL_BANDS = 8

GLA_HEADS = 4
GLA_GATE_RANK = 16
GLA_GATE_NORM = 16.0
GLA_CHUNK = 64

SWA_HEAD_DIM = 64
SWA_KV_HEADS = 2
SWA_WINDOW = 128
SWA_BLOCK = 128
MASK_VALUE = -1e30

TOP_K = 2


def _params(semantics, vmem_mib):
    return pltpu.CompilerParams(dimension_semantics=semantics, vmem_limit_bytes=vmem_mib * MIB)


def _resident(block_shape, index_map):
    return pl.BlockSpec(block_shape, index_map, pipeline_mode=pl.Buffered(1))


def _rms(xf, gain):
    return xf * lax.rsqrt(jnp.mean(xf * xf, axis=-1, keepdims=True) + EPS) * gain


def _gelu_tanh(x):
    return 0.5 * x * (1.0 + jnp.tanh(math.sqrt(2.0 / math.pi) * (x + 0.044715 * (x * x * x))))


def _silu(x):
    return x * jax.nn.sigmoid(x)


def _bdot(a, b):
    return jnp.dot(a, b, preferred_element_type=F32)


S5_ROW_TILE = 1024


def _s5_norm_kernel(x_ref, g_ref, o_ref, scr_ref, *, nloc):
    h = _rms(x_ref[0], g_ref[...])
    nslab = scr_ref.shape[0]
    for c in range(nslab):
        scr_ref[c] = h[:, c * LANES:(c + 1) * LANES]
    for s in range(S5_CHUNK):
        rows = pl.ds(s, nloc, stride=S5_CHUNK)
        o_ref[0, s] = jnp.concatenate([scr_ref[c, rows, :] for c in range(nslab)], axis=1).astype(o_ref.dtype)


def _s5_norm(x, gain):
    bsz, seqlen, d = x.shape
    nch = seqlen // S5_CHUNK
    tm = min(S5_ROW_TILE, seqlen)
    nloc = tm // S5_CHUNK
    return pl.pallas_call(
        functools.partial(_s5_norm_kernel, nloc=nloc),
        out_shape=jax.ShapeDtypeStruct((bsz, S5_CHUNK, nch, d), BF16),
        grid=(bsz, seqlen // tm),
        in_specs=[pl.BlockSpec((1, tm, d), lambda b, i: (b, i, 0)),
                  pl.BlockSpec((1, d), lambda b, i: (0, 0))],
        out_specs=pl.BlockSpec((1, S5_CHUNK, nloc, d), lambda b, i: (b, 0, i, 0)),
        scratch_shapes=[pltpu.VMEM((d // LANES, tm, LANES), F32)],
        compiler_params=_params(("parallel", "parallel"), 32),
        name="s5_norm",
    )(x, gain.reshape(1, d))


def _tiling_matrix(rows, cols):
    p = lax.broadcasted_iota(jnp.int32, (rows, cols), 0)
    c = lax.broadcasted_iota(jnp.int32, (rows, cols), 1)
    return jnp.where(c % rows == p, 1.0, 0.0).astype(BF16)


def _same_group(shape, row_group, col_group):
    r = lax.broadcasted_iota(jnp.int32, shape, 0)
    c = lax.broadcasted_iota(jnp.int32, shape, 1)
    return (r // row_group) == (c // col_group)


def _s5_build_operators(vw_ref, mw_ref, toep_ref, win_ref, wout_ref):
    tn = (((0,), (0,)), ((), ()))
    nstate = vw_ref.shape[-1]
    half = S5_SLAB_GROUPS * nstate
    rep_ch = _tiling_matrix(S5_GROUP, LANES)
    rep_st = _tiling_matrix(nstate, half)
    diag_in = _same_group((LANES, half), S5_GROUP, nstate)
    diag_out = _same_group((half, LANES), nstate, S5_GROUP)

    def out_block(q, r):
        e = lax.dot_general(mw_ref[0, 2 * q + r].astype(BF16), rep_ch, tn, preferred_element_type=F32)
        return jnp.where(diag_out, e, 0.0).astype(BF16)

    for a in range(S5_CHUNK):
        for r in range(2):
            e = _bdot(vw_ref[0, 2 * a + r].astype(BF16), rep_st)
            win_ref[a * LANES:(a + 1) * LANES, r * half:(r + 1) * half] = jnp.where(diag_in, e, 0.0).astype(BF16)
            wout_ref[r * half:(r + 1) * half, a * LANES:(a + 1) * LANES] = out_block(a + 1, r)
    b_bar = win_ref[(S5_CHUNK - 1) * LANES:S5_CHUNK * LANES, :]
    taps = [_bdot(b_bar, jnp.concatenate([out_block(0, 0), out_block(0, 1)], axis=0)).astype(BF16)]
    for j in range(1, S5_CHUNK):
        taps.append(_bdot(b_bar, wout_ref[:, (j - 1) * LANES:j * LANES]).astype(BF16))
    zero = jnp.zeros((LANES, LANES), BF16)
    for a in range(S5_CHUNK):
        for b in range(S5_CHUNK):
            toep_ref[a * LANES:(a + 1) * LANES, b * LANES:(b + 1) * LANES] = taps[b - a] if b >= a else zero


def _s5_conv_kernel(h_ref, vw_ref, mw_ref, a_ref, d_ref, o_ref, s_ref, toep_ref, win_ref, wout_ref,
                    *, nseq, nch):
    pitch = nch + S5_PITCH_PAD
    nl = a_ref.shape[1] // 2

    @pl.when(pl.program_id(1) == 0)
    def _():
        _s5_build_operators(vw_ref, mw_ref, toep_ref, win_ref, wout_ref)

    lhs = jnp.concatenate(
        [jnp.concatenate([h_ref[bl, s] for s in range(S5_CHUNK)], axis=1) for bl in range(nseq)], axis=0)
    bc = _bdot(lhs, win_ref[...])
    for bl in range(nseq):
        for j in range(2 * nl):
            s_ref[j, bl * pitch:bl * pitch + nch, :] = bc[bl * nch:(bl + 1) * nch, j * LANES:(j + 1) * LANES]
    a_re = [a_ref[0, j:j + 1, :] for j in range(nl)]
    a_im = [a_ref[0, nl + j:nl + j + 1, :] for j in range(nl)]

    def step(n, carry):
        p_re, p_im = carry
        rows = pl.ds(n, nseq, stride=pitch)
        n_re, n_im = [], []
        for j in range(nl):
            c_re = s_ref[j, rows, :]
            c_im = s_ref[nl + j, rows, :]
            s_ref[j, rows, :] = p_re[j]
            s_ref[nl + j, rows, :] = p_im[j]
            n_re.append(a_re[j] * p_re[j] - a_im[j] * p_im[j] + c_re)
            n_im.append(a_re[j] * p_im[j] + a_im[j] * p_re[j] + c_im)
        return tuple(n_re), tuple(n_im)

    def steps(m, carry):
        for u in range(S5_SCAN_UNROLL):
            carry = step(m * S5_SCAN_UNROLL + u, carry)
        return carry

    zeros = tuple(jnp.zeros((nseq, LANES), F32) for _ in range(nl))
    lax.fori_loop(0, nch // S5_SCAN_UNROLL, steps, (zeros, zeros))
    x_prev = jnp.concatenate(
        [jnp.concatenate([s_ref[j, bl * pitch:bl * pitch + nch, :] for j in range(2 * nl)], axis=1)
         for bl in range(nseq)], axis=0).astype(BF16)
    band = S5_CHUNK // S5_CAUSAL_BANDS
    y_bands = []
    for q in range(S5_CAUSAL_BANDS):
        kk = (q + 1) * band * LANES
        cols = slice(q * band * LANES, (q + 1) * band * LANES)
        y_bands.append(_bdot(lhs[:, :kk], toep_ref[:kk, cols]) + _bdot(x_prev, wout_ref[:, cols]))
    dskip = d_ref[0]
    for bl in range(nseq):
        for s in range(S5_CHUNK):
            ys = y_bands[s // band][bl * nch:(bl + 1) * nch, (s % band) * LANES:(s % band + 1) * LANES]
            ys = ys + dskip * h_ref[bl, s].astype(F32)
            o_ref[bl, s] = _gelu_tanh(ys).astype(o_ref.dtype)


def _s5_conv(hp, vw, mw, a_pack, d_skip, *, nseq, layer):
    bsz, _, nch, d = hp.shape
    nslab = d // LANES
    kdim = S5_CHUNK * LANES
    sdim = a_pack.shape[1] * LANES
    first = layer * nslab
    blk4 = lambda a: pl.BlockSpec((1,) + a.shape[1:], lambda c, b: (first + c, 0, 0, 0))
    return pl.pallas_call(
        functools.partial(_s5_conv_kernel, nseq=nseq, nch=nch),
        out_shape=jax.ShapeDtypeStruct(hp.shape, BF16),
        grid=(nslab, bsz // nseq),
        in_specs=[pl.BlockSpec((nseq, S5_CHUNK, nch, LANES), lambda c, b: (b, 0, 0, c)),
                  blk4(vw), blk4(mw),
                  pl.BlockSpec((1, sdim // LANES, LANES), lambda c, b: (first + c, 0, 0)),
                  pl.BlockSpec((1, 1, LANES), lambda c, b: (c, 0, 0))],
        out_specs=pl.BlockSpec((nseq, S5_CHUNK, nch, LANES), lambda c, b: (b, 0, 0, c)),
        scratch_shapes=[pltpu.VMEM((sdim // LANES, nseq * (nch + S5_PITCH_PAD), LANES), F32),
                        pltpu.VMEM((kdim, kdim), BF16),
                        pltpu.VMEM((kdim, sdim), BF16),
                        pltpu.VMEM((sdim, kdim), BF16)],
        compiler_params=_params(("arbitrary", "arbitrary"), 56),
        name="s5_conv",
    )(hp, vw, mw, a_pack, d_skip.reshape(nslab, 1, LANES))


def _s5_glu_kernel(y_ref, x_ref, w_ref, b_ref, o_ref, scr_ref, *, nloc):
    nslab = scr_ref.shape[0]
    y = jnp.concatenate([y_ref[0, s] for s in range(S5_CHUNK)], axis=0)
    u = y.astype(F32) * jax.nn.sigmoid(_bdot(y, w_ref[...]) + b_ref[...])
    for s in range(S5_CHUNK):
        rows = pl.ds(s, nloc, stride=S5_CHUNK)
        for c in range(nslab):
            scr_ref[c, rows, :] = u[s * nloc:(s + 1) * nloc, c * LANES:(c + 1) * LANES]
    o_ref[0] = x_ref[0] + jnp.concatenate([scr_ref[c] for c in range(nslab)], axis=1)


def _s5_glu(yp, x, w_glu, b_glu):
    bsz, seqlen, d = x.shape
    tm = min(S5_ROW_TILE, seqlen)
    nloc = tm // S5_CHUNK
    return pl.pallas_call(
        functools.partial(_s5_glu_kernel, nloc=nloc),
        out_shape=jax.ShapeDtypeStruct(x.shape, F32),
        grid=(bsz, seqlen // tm),
        in_specs=[pl.BlockSpec((1, S5_CHUNK, nloc, d), lambda b, i: (b, 0, i, 0)),
                  pl.BlockSpec((1, tm, d), lambda b, i: (b, i, 0)),
                  _resident((d, d), lambda b, i: (0, 0)),
                  pl.BlockSpec((1, d), lambda b, i: (0, 0))],
        out_specs=pl.BlockSpec((1, tm, d), lambda b, i: (b, i, 0)),
        scratch_shapes=[pltpu.VMEM((d // LANES, tm, LANES), F32)],
        compiler_params=_params(("parallel", "parallel"), 40),
        name="s5_glu",
    )(yp, x, w_glu.astype(BF16), b_glu.reshape(1, d))


def _s5_operators(lam_re, lam_im, log_dt, b_re, b_im, c_re, c_im):
    ngroups, nstate = lam_re.shape
    gpc = S5_SLAB_GROUPS
    nslab = ngroups // gpc
    dt = jnp.exp(log_dt)[:, None]
    j = jnp.arange(S5_CHUNK + 1, dtype=F32)[:, None, None]
    mag = jnp.exp(j * (lam_re * dt)[None])
    ang = j * (lam_im * dt)[None]
    pw_re, pw_im = mag * jnp.cos(ang), mag * jnp.sin(ang)
    num_re, num_im = pw_re[1] - 1.0, pw_im[1]
    den = lam_re * lam_re + lam_im * lam_im
    f_re = (num_re * lam_re + num_im * lam_im) / den
    f_im = (num_im * lam_re - num_re * lam_im) / den
    bb_re = f_re[..., None] * b_re - f_im[..., None] * b_im
    bb_im = f_re[..., None] * b_im + f_im[..., None] * b_re
    jr = (S5_CHUNK - 1) - jnp.arange(S5_CHUNK, dtype=F32)[:, None, None]
    mag_r = jnp.exp(jr * (lam_re * dt)[None])
    ang_r = jr * (lam_im * dt)[None]
    rev_re, rev_im = mag_r * jnp.cos(ang_r), mag_r * jnp.sin(ang_r)
    slabbed = lambda a: a.reshape(a.shape[0], nslab, gpc, nstate).transpose(1, 0, 2, 3)
    rv_re, rv_im = slabbed(rev_re)[:, :, :, None, :], slabbed(rev_im)[:, :, :, None, :]
    bt_re = bb_re.transpose(0, 2, 1).reshape(nslab, 1, gpc, S5_GROUP, nstate)
    bt_im = bb_im.transpose(0, 2, 1).reshape(nslab, 1, gpc, S5_GROUP, nstate)
    vw = jnp.stack([rv_re * bt_re - rv_im * bt_im, rv_re * bt_im + rv_im * bt_re], axis=2)
    vw = vw.reshape(nslab, 2 * S5_CHUNK, LANES, nstate)
    pc_re, pc_im = slabbed(pw_re)[:, :, None, :, :], slabbed(pw_im)[:, :, None, :, :]
    ct_re = c_re.reshape(nslab, gpc, S5_GROUP, nstate).transpose(0, 2, 1, 3)[:, None]
    ct_im = c_im.reshape(nslab, gpc, S5_GROUP, nstate).transpose(0, 2, 1, 3)[:, None]
    mw = jnp.stack([ct_re * pc_re - ct_im * pc_im, -(ct_re * pc_im + ct_im * pc_re)], axis=2)
    mw = mw.reshape(nslab, 2 * (S5_CHUNK + 1), S5_GROUP, gpc * nstate)
    half = gpc * nstate // LANES
    a_pack = jnp.concatenate([pw_re[S5_CHUNK].reshape(nslab, half, LANES),
                              pw_im[S5_CHUNK].reshape(nslab, half, LANES)], axis=1)
    return vw, mw, a_pack


def _s5_layer(x, ln, operators, layer, d_skip, w_glu, b_glu, *, nseq=4):
    vw, mw, a_pack = operators
    hp = _s5_norm(x, ln)
    yp = _s5_conv(hp, vw, mw, a_pack, d_skip, nseq=min(nseq, x.shape[0]), layer=layer)
    return _s5_glu(yp, x, w_glu, b_glu)


def _dense_ffn_kernel(x_ref, g_ref, wg_ref, wu_ref, wd_ref, o_ref):
    xf = x_ref[...]
    h = _rms(xf, g_ref[...]).astype(BF16)
    act = (_silu(_bdot(h, wg_ref[...])) * _bdot(h, wu_ref[...])).astype(BF16)
    o_ref[...] = xf + _bdot(act, wd_ref[...])


def _dense_ffn_layer(x, ln, w_gate_up, w_down, *, tm=512):
    bsz, seqlen, d = x.shape
    ntok = bsz * seqlen
    hidden = w_down.shape[0]
    tm = min(tm, ntok)
    wgu = w_gate_up.astype(BF16)
    out = pl.pallas_call(
        _dense_ffn_kernel,
        out_shape=jax.ShapeDtypeStruct((ntok, d), F32),
        grid=(ntok // tm,),
        in_specs=[pl.BlockSpec((tm, d), lambda i: (i, 0)),
                  pl.BlockSpec((1, d), lambda i: (0, 0)),
                  _resident((d, hidden), lambda i: (0, 0)),
                  _resident((d, hidden), lambda i: (0, 1)),
                  _resident((hidden, d), lambda i: (0, 0))],
        out_specs=pl.BlockSpec((tm, d), lambda i: (i, 0)),
        compiler_params=_params(("parallel",), 56),
        name="dense_ffn",
    )(x.reshape(ntok, d), ln.reshape(1, d), wgu, wgu, w_down.astype(BF16))
    return out.reshape(bsz, seqlen, d)


def _log_sigmoid(z):
    return jnp.minimum(z, 0.0) - jnp.log(1.0 + jnp.exp(-jnp.abs(z)))


def _gla_kernel(x_ref, ln_ref, wm_ref, wgl_ref, wg2_ref, bg2_ref, gn_ref, wo_ref, o_ref, st_ref,
                *, tq, dk, dv, heads):
    hdk, hdv = dk // heads, dv // heads
    chunk = GLA_CHUNK
    nt = (((1,), (1,)), ((), ()))
    tn = (((0,), (0,)), ((), ()))

    @pl.when(pl.program_id(1) == 0)
    def _():
        st_ref[...] = jnp.zeros_like(st_ref)

    xf = x_ref[0]
    h = _rms(xf, ln_ref[...]).astype(BF16)
    proj = _bdot(h, wm_ref[...])
    glow = _bdot(h, wgl_ref[...]).astype(BF16)
    la = _log_sigmoid(_bdot(glow, wg2_ref[...]) + bg2_ref[...]) * (1.0 / GLA_GATE_NORM)
    row = lax.broadcasted_iota(jnp.int32, (chunk, chunk), 0)
    col = lax.broadcasted_iota(jnp.int32, (chunk, chunk), 1)
    causal = row >= col
    tri = jnp.where(causal, 1.0, 0.0).astype(BF16)
    scale = hdk ** -0.5
    outs = []
    for c in range(tq // chunk):
        r0 = c * chunk
        la_c = la[r0:r0 + chunk, :]
        la_hi = la_c.astype(BF16)
        la_lo = (la_c - la_hi.astype(F32)).astype(BF16)
        gcum_all = _bdot(tri, la_hi) + _bdot(tri, la_lo)
        head_out = []
        for hd in range(heads):
            gcum = gcum_all[:, hd * hdk:(hd + 1) * hdk]
            g_last = gcum[chunk - 1:chunk, :]
            q_c = proj[r0:r0 + chunk, hd * hdk:(hd + 1) * hdk] * scale
            k_c = proj[r0:r0 + chunk, dk + hd * hdk:dk + (hd + 1) * hdk]
            v_c = proj[r0:r0 + chunk, 2 * dk + hd * hdv:2 * dk + (hd + 1) * hdv].astype(BF16)
            q_s = (q_c * jnp.exp(gcum)).astype(BF16)
            k_s = (k_c * jnp.exp(-gcum)).astype(BF16)
            k_end = (k_c * jnp.exp(g_last - gcum)).astype(BF16)
            scores = lax.dot_general(q_s, k_s, nt, preferred_element_type=F32)
            scores = jnp.where(causal, scores, 0.0).astype(BF16)
            state_t = st_ref[hd]
            o = _bdot(scores, v_c) + lax.dot_general(q_s, state_t.astype(BF16), nt,
                                                     preferred_element_type=F32)
            kv_t = lax.dot_general(v_c, k_end, tn, preferred_element_type=F32)
            st_ref[hd] = state_t * jnp.exp(g_last) + kv_t
            head_out.append(o * lax.rsqrt(jnp.mean(o * o, axis=-1, keepdims=True) + EPS))
        outs.append(jnp.concatenate(head_out, axis=1))
    o_all = jnp.concatenate(outs, axis=0)
    r = proj[:, 2 * dk + dv:]
    o_all = (o_all * gn_ref[...] * _silu(r)).astype(BF16)
    o_ref[0] = xf + _bdot(o_all, wo_ref[...])


def _gla_layer(x, ln, w_in, w_g2, b_g2, g_norm, w_out, *, tq=256):
    bsz, seqlen, d = x.shape
    dk = w_g2.shape[1]
    dv = w_out.shape[0]
    nmain = 2 * dk + 2 * dv
    tq = min(tq, seqlen)
    w_main = w_in[:, :nmain].astype(BF16)
    w_glow = jnp.pad(w_in[:, nmain:], ((0, 0), (0, LANES - GLA_GATE_RANK))).astype(BF16)
    w_g2p = jnp.pad(w_g2, ((0, LANES - GLA_GATE_RANK), (0, 0))).astype(BF16)
    hdk, hdv = dk // GLA_HEADS, dv // GLA_HEADS
    const = lambda b, t: (0, 0)
    return pl.pallas_call(
        functools.partial(_gla_kernel, tq=tq, dk=dk, dv=dv, heads=GLA_HEADS),
        out_shape=jax.ShapeDtypeStruct(x.shape, F32),
        grid=(bsz, seqlen // tq),
        in_specs=[pl.BlockSpec((1, tq, d), lambda b, t: (b, t, 0)),
                  pl.BlockSpec((1, d), const),
                  _resident((d, nmain), const),
                  _resident((d, LANES), const),
                  _resident((LANES, dk), const),
                  pl.BlockSpec((1, dk), const),
                  pl.BlockSpec((1, dv), const),
                  _resident((dv, d), const)],
        out_specs=pl.BlockSpec((1, tq, d), lambda b, t: (b, t, 0)),
        scratch_shapes=[pltpu.VMEM((GLA_HEADS, hdv, hdk), F32)],
        compiler_params=_params(("parallel", "arbitrary"), 48),
        name="gla",
    )(x, ln.reshape(1, d), w_main, w_glow, w_g2p, b_g2.reshape(1, dk), g_norm.reshape(1, dv),
      w_out.astype(BF16))


LOG2E = math.log2(math.e)
SWA_SLOT_UNROLL = 4


def _swa_kernel(sink_ref, x_ref, ln_ref, wqkv_ref, bqkv_ref, wo_ref, bo_ref, o_ref, k_ref, v_ref,
                bias_ref, q_ref, a_ref, *, tq, q_heads):
    group = q_heads // SWA_KV_HEADS
    blk = SWA_BLOCK
    nt = (((1,), (1,)), ((), ()))
    b = pl.program_id(0)
    t = pl.program_id(1)
    nq = group * LANES

    @pl.when((b == 0) & (t == 0))
    def _():
        qi = lax.broadcasted_iota(jnp.int32, (blk, 2 * blk), 0)
        kj = lax.broadcasted_iota(jnp.int32, (blk, 2 * blk), 1)
        dist = qi + blk - kj
        in_window = (dist >= 0) & (dist < SWA_WINDOW)
        for hq in range(q_heads):
            slope = 2.0 ** (-8.0 * (hq + 1) / q_heads)
            bias_ref[hq] = jnp.where(in_window, -(slope * LOG2E) * dist.astype(F32), MASK_VALUE)

    @pl.when(t == 0)
    def _():
        k_ref[0:blk, :] = jnp.zeros((blk, LANES), BF16)
        v_ref[0:blk, :] = jnp.zeros((blk, LANES), BF16)

    xf = x_ref[0]
    h = _rms(xf, ln_ref[...]).astype(BF16)
    qkv = _bdot(h, wqkv_ref[...]) + bqkv_ref[...]
    for j in range(group):
        q_ref[j] = qkv[:, j * LANES:(j + 1) * LANES].astype(BF16)
    k_ref[blk:blk + tq, :] = qkv[:, nq:nq + LANES].astype(BF16)
    v_ref[blk:blk + tq, :] = qkv[:, nq + LANES:nq + 2 * LANES].astype(BF16)
    kj_row = lax.broadcasted_iota(jnp.int32, (1, 2 * blk), 1)
    no_prev = jnp.where(kj_row < blk, jnp.where(t == 0, MASK_VALUE, 0.0), 0.0)
    low_half = lax.broadcasted_iota(jnp.int32, (1, LANES), 1) < SWA_HEAD_DIM
    halves = (low_half, jnp.logical_not(low_half))

    def slots(jj, carry):
        for u in range(SWA_SLOT_UNROLL):
            j = jj * SWA_SLOT_UNROLL + u
            for i in range(tq // blk):
                r0 = i * blk
                q_slot = q_ref[j, r0:r0 + blk, :]
                outs = []
                for kh in range(SWA_KV_HEADS):
                    hq = kh * group + j
                    sink = sink_ref[hq] * LOG2E
                    q_h = jnp.where(halves[kh], q_slot, jnp.zeros_like(q_slot))
                    s = lax.dot_general(q_h, k_ref[r0:r0 + 2 * blk, :], nt, preferred_element_type=F32) + bias_ref[hq]
                    if i == 0:
                        s = s + no_prev
                    m = jnp.maximum(jnp.max(s, axis=-1, keepdims=True), sink)
                    p = jnp.exp2(s - m)
                    denom = jnp.sum(p, axis=-1, keepdims=True) + jnp.exp2(sink - m)
                    outs.append(_bdot(p.astype(BF16), v_ref[r0:r0 + 2 * blk, :]) * (1.0 / denom))
                a_ref[j, r0:r0 + blk, :] = jnp.where(low_half, outs[0], outs[1]).astype(BF16)
        return carry

    lax.fori_loop(0, group // SWA_SLOT_UNROLL, slots, 0)
    k_ref[0:blk, :] = k_ref[tq:tq + blk, :]
    v_ref[0:blk, :] = v_ref[tq:tq + blk, :]
    o_all = jnp.concatenate([a_ref[j] for j in range(group)], axis=1)
    o_ref[0] = xf + _bdot(o_all, wo_ref[...]) + bo_ref[...]


def _swa_layer(x, ln, w_qkv, b_qkv, sinks, w_out, b_out, *, tq=512):
    bsz, seqlen, d = x.shape
    hd = SWA_HEAD_DIM
    q_heads = sinks.shape[0]
    group = q_heads // SWA_KV_HEADS
    nq = q_heads * hd
    tq = min(tq, seqlen)
    q_scale = hd ** -0.5 * LOG2E
    wq = (w_qkv[:, :nq] * q_scale).reshape(d, SWA_KV_HEADS, group, hd).transpose(0, 2, 1, 3).reshape(d, nq)
    bq = (b_qkv[:nq] * q_scale).reshape(SWA_KV_HEADS, group, hd).transpose(1, 0, 2).reshape(nq)
    w_all = jnp.concatenate([wq, w_qkv[:, nq:]], axis=1).astype(BF16)
    b_all = jnp.concatenate([bq, b_qkv[nq:]]).reshape(1, -1)
    wo = w_out.reshape(SWA_KV_HEADS, group, hd, d).transpose(1, 0, 2, 3).reshape(nq, d).astype(BF16)
    nall = w_all.shape[1]
    const = lambda b, t, s: (0, 0)
    return pl.pallas_call(
        functools.partial(_swa_kernel, tq=tq, q_heads=q_heads),
        out_shape=jax.ShapeDtypeStruct(x.shape, F32),
        grid_spec=pltpu.PrefetchScalarGridSpec(
            num_scalar_prefetch=1,
            grid=(bsz, seqlen // tq),
            in_specs=[pl.BlockSpec((1, tq, d), lambda b, t, s: (b, t, 0)),
                      pl.BlockSpec((1, d), const),
                      _resident((d, nall), const),
                      pl.BlockSpec((1, nall), const),
                      _resident((nq, d), const),
                      pl.BlockSpec((1, d), const)],
            out_specs=pl.BlockSpec((1, tq, d), lambda b, t, s: (b, t, 0)),
            scratch_shapes=[pltpu.VMEM((SWA_BLOCK + tq, LANES), BF16), pltpu.VMEM((SWA_BLOCK + tq, LANES), BF16),
                            pltpu.VMEM((q_heads, SWA_BLOCK, 2 * SWA_BLOCK), F32),
                            pltpu.VMEM((group, tq, LANES), BF16), pltpu.VMEM((group, tq, LANES), BF16)]),
        compiler_params=_params(("arbitrary", "arbitrary"), 48),
        name="swa",
    )(sinks, x, ln.reshape(1, d), w_all, b_all, wo, b_out.reshape(1, d))


def _router_kernel(x_ref, ln_ref, whi_ref, wlo_ref, idx_ref, gate_ref, hp_ref, pos_ref, count_ref, tri_ref):
    nt = (((1,), (1,)), ((), ()))
    h = _rms(x_ref[...], ln_ref[...])
    h_hi = h.astype(BF16)
    h_lo = (h - h_hi.astype(F32)).astype(BF16)
    w_hi, w_lo = whi_ref[...], wlo_ref[...]
    logits = (lax.dot_general(w_hi, h_hi, nt, preferred_element_type=F32)
              + lax.dot_general(w_hi, h_lo, nt, preferred_element_type=F32)
              + lax.dot_general(w_lo, h_hi, nt, preferred_element_type=F32))
    n_exp = logits.shape[0]
    eid = lax.broadcasted_iota(jnp.int32, logits.shape, 0)
    m1 = jnp.max(logits, axis=0, keepdims=True)
    i1 = jnp.min(jnp.where(logits == m1, eid, n_exp), axis=0, keepdims=True)
    rest = jnp.where(eid == i1, -jnp.inf, logits)
    m2 = jnp.max(rest, axis=0, keepdims=True)
    i2 = jnp.min(jnp.where(rest == m2, eid, n_exp), axis=0, keepdims=True)
    e2 = jnp.exp(m2 - m1)
    g1 = 1.0 / (1.0 + e2)
    idx_ref[...] = jnp.concatenate([i1, i2], axis=0)
    gate_ref[...] = jnp.concatenate([g1, e2 * g1], axis=0)
    hp_ref[...] = _pack_bf16_pairs(h)
    tm = logits.shape[1]

    @pl.when(pl.program_id(0) == 0)
    def _():
        count_ref[...] = jnp.zeros_like(count_ref)
        r = lax.broadcasted_iota(jnp.int32, (tm, tm), 0)
        c = lax.broadcasted_iota(jnp.int32, (tm, tm), 1)
        tri_ref[...] = jnp.where(r < c, 1.0, 0.0).astype(BF16)

    pick1 = jnp.where(eid == i1, 1.0, 0.0)
    pick2 = jnp.where(eid == i2, 1.0, 0.0)
    picks = pick1 + pick2
    before = _bdot(picks.astype(BF16), tri_ref[...]) + count_ref[:, 0:1]
    pos_ref[...] = jnp.concatenate([jnp.sum(pick1 * before, axis=0, keepdims=True),
                                    jnp.sum(pick2 * before, axis=0, keepdims=True)], axis=0).astype(jnp.int32)
    count_ref[...] = count_ref[...] + jnp.sum(picks, axis=1, keepdims=True)


def _router(x2, ln, w_router, *, tm=512):
    ntok, d = x2.shape
    n_exp = w_router.shape[1]
    tm = min(tm, ntok)
    wt = w_router.T
    w_hi = wt.astype(BF16)
    w_lo = (wt - w_hi.astype(F32)).astype(BF16)
    return pl.pallas_call(
        _router_kernel,
        out_shape=(jax.ShapeDtypeStruct((TOP_K, ntok), jnp.int32), jax.ShapeDtypeStruct((TOP_K, ntok), F32),
                   jax.ShapeDtypeStruct((ntok, d // 2), jnp.uint32),
                   jax.ShapeDtypeStruct((TOP_K, ntok), jnp.int32), jax.ShapeDtypeStruct((n_exp, LANES), F32)),
        grid=(ntok // tm,),
        in_specs=[pl.BlockSpec((tm, d), lambda i: (i, 0)),
                  pl.BlockSpec((1, d), lambda i: (0, 0)),
                  pl.BlockSpec((n_exp, d), lambda i: (0, 0)),
                  pl.BlockSpec((n_exp, d), lambda i: (0, 0))],
        out_specs=(pl.BlockSpec((TOP_K, tm), lambda i: (0, i)), pl.BlockSpec((TOP_K, tm), lambda i: (0, i)),
                   pl.BlockSpec((tm, d // 2), lambda i: (i, 0)),
                   pl.BlockSpec((TOP_K, tm), lambda i: (0, i)), pl.BlockSpec((n_exp, LANES), lambda i: (0, 0))),
        scratch_shapes=[pltpu.VMEM((tm, tm), BF16)],
        compiler_params=_params(("arbitrary",), 32),
        name="moe_router",
    )(x2, ln.reshape(1, d), w_hi, w_lo)


def _moe_plan(idx, pos, counts, tile):
    n_exp = counts.shape[0]
    nslots = idx.size
    counts = counts[:, 0].astype(jnp.int32)
    ends = jnp.cumsum(counts)
    offs = ends - counts
    experts = jnp.arange(n_exp, dtype=jnp.int32).reshape(n_exp, 1, 1)
    rank = pos + jnp.sum(jnp.where(idx[None] == experts, offs.reshape(n_exp, 1, 1), 0), axis=0)
    n_tiles = nslots // tile
    n_visits = n_tiles + n_exp - 1
    first_tile = offs // tile
    last_tile = (ends - 1) // tile
    nvis = jnp.where(counts > 0, last_tile - first_tile + 1, 0)
    vend = jnp.cumsum(nvis)
    vstart = vend - nvis
    total = vend[-1]
    v = jnp.arange(n_visits, dtype=jnp.int32)
    vc = jnp.minimum(v, total - 1)
    e = jnp.minimum(jnp.sum((vc[:, None] >= vend[None, :]).astype(jnp.int32), axis=1), n_exp - 1)
    sel = (e[:, None] == jnp.arange(n_exp, dtype=jnp.int32)[None, :]).astype(jnp.int32)
    pick = lambda a: jnp.sum(sel * a[None, :], axis=1)
    tile_id = pick(first_tile) + vc - pick(vstart)
    lo = jnp.maximum(pick(offs), tile_id * tile) - tile_id * tile
    hi = jnp.minimum(pick(ends), (tile_id + 1) * tile) - tile_id * tile
    valid = v < total
    lo = jnp.where(valid, lo, 0)
    hi = jnp.where(valid, hi, 0)
    prev_tile = jnp.concatenate([jnp.full((1,), -1, jnp.int32), tile_id[:-1]])
    first = (valid & (tile_id != prev_tile)).astype(jnp.int32)
    next_tile = jnp.concatenate([tile_id[1:], jnp.full((1,), -1, jnp.int32)])
    last = (valid & ((tile_id != next_tile) | (v == total - 1))).astype(jnp.int32)
    meta = jnp.stack([tile_id, e, lo, hi, first, last]).astype(jnp.int32)
    return rank.astype(jnp.int32), meta


def _pack_bf16_pairs(h):
    half = h.shape[1] // 2
    bits = lax.bitcast_convert_type(h.astype(BF16).astype(F32), jnp.uint32)
    return (bits[:, half:] & jnp.uint32(0xFFFF0000)) | (bits[:, :half] >> 16)


def _unpack_pairs_f32(u):
    lo = lax.bitcast_convert_type(u << 16, F32)
    hi = lax.bitcast_convert_type(u & jnp.uint32(0xFFFF0000), F32)
    return jnp.concatenate([lo, hi], axis=1)


def _unpack_bf16_pairs(u):
    return _unpack_pairs_f32(u).astype(BF16)


SC_CORES = 2
SC_SUBCORES = 16
SC_INDEX_WINDOW = 128


def _sc_mesh():
    return plsc.VectorSubcoreMesh(core_axis_name="c", subcore_axis_name="s")


def _sc_worker_id():
    return lax.axis_index("c") * SC_SUBCORES + lax.axis_index("s")


def _sc_scatter_rows(src, rank, nrows):
    ntok, width = src.shape
    win = SC_INDEX_WINDOW
    per = ntok // (SC_CORES * SC_SUBCORES)

    @pl.kernel(out_type=jax.ShapeDtypeStruct((nrows, width), src.dtype), mesh=_sc_mesh(),
               scratch_types=[pltpu.VMEM((1, win), jnp.int32)] * TOP_K + [pltpu.VMEM((win, width), src.dtype)],
               name="moe_dispatch_sc")
    def scatter(src_hbm, rank_hbm, o_hbm, *scratch):
        idx_vmem, buf = scratch[:TOP_K], scratch[TOP_K]
        wid = _sc_worker_id()

        @pl.loop(0, per // win)
        def _(blk):
            base = wid * per + blk * win
            for k in range(TOP_K):
                pltpu.sync_copy(rank_hbm.at[pl.ds(k, 1), pl.ds(base, win)], idx_vmem[k])
            pltpu.sync_copy(src_hbm.at[pl.ds(base, win)], buf)
            for k in range(TOP_K):
                pltpu.sync_copy(buf, o_hbm.at[idx_vmem[k].at[0]])

    return scatter(src, rank)


def _sc_gather_rows(src, rank, *, sub=32):
    nslot, ntok = rank.shape
    n = nslot * ntok
    width = src.shape[1]
    win = SC_INDEX_WINDOW
    per = n // (SC_CORES * SC_SUBCORES)
    nsub = win // sub

    @pl.kernel(out_type=jax.ShapeDtypeStruct((n, width), src.dtype), mesh=_sc_mesh(),
               scratch_types=[pltpu.VMEM((1, win), jnp.int32)] + [pltpu.VMEM((sub, width), src.dtype)] * 2
               + [pltpu.SemaphoreType.DMA] * 4,
               name="moe_gather_sc")
    def gather(src_hbm, idx_hbm, o_hbm, i_vmem, buf0, buf1, g0, g1, w0, w1):
        bufs, gsem, wsem = (buf0, buf1), (g0, g1), (w0, w1)
        wid = _sc_worker_id()

        @pl.loop(0, per // win)
        def _(blk):
            base = wid * per + blk * win
            pltpu.sync_copy(idx_hbm.at[pl.ds(base // ntok, 1), pl.ds(base % ntok, win)], i_vmem)
            gathers = [pltpu.make_async_copy(src_hbm.at[i_vmem.at[0, pl.ds(sub * j, sub)]], bufs[j % 2], gsem[j % 2])
                       for j in range(nsub)]
            writes = [pltpu.make_async_copy(bufs[j % 2], o_hbm.at[pl.ds(base + sub * j, sub)], wsem[j % 2])
                      for j in range(nsub)]
            gathers[0].start()
            for j in range(nsub):
                if j + 1 < nsub:
                    if j >= 1:
                        writes[j - 1].wait()
                    gathers[j + 1].start()
                gathers[j].wait()
                writes[j].start()
            writes[nsub - 2].wait()
            writes[nsub - 1].wait()

    return gather(src, rank)


MXU_N = 256


def _expert_kernel(meta_ref, x_ref, wg_ref, wu_ref, wd_ref, o_ref, acc_ref, xb_ref, act_ref, wgb_ref, wub_ref,
                   wdb_ref, *, ts):
    v = pl.program_id(0)
    hc = pl.program_id(1)
    lo, hi, first, last = meta_ref[2, v], meta_ref[3, v], meta_ref[4, v], meta_ref[5, v]
    tile, d = acc_ref.shape
    nsub = tile // ts
    th = wgb_ref.shape[1]
    wide = (hi - lo) * 2 > tile

    @pl.when(hc == 0)
    def _():
        for sub in range(nsub):
            xb_ref[sub * ts:(sub + 1) * ts, :] = _unpack_bf16_pairs(x_ref[sub * ts:(sub + 1) * ts, :])

    @pl.when((first == 1) & (hc == 0))
    def _():
        acc_ref[...] = jnp.zeros_like(acc_ref)

    @pl.when(wide)
    def _():
        rows = lax.broadcasted_iota(jnp.int32, (tile, 1), 0)
        mine = (rows >= lo) & (rows < hi)
        for n in range(th // MXU_N):
            cols = slice(n * MXU_N, (n + 1) * MXU_N)
            gate = _bdot(xb_ref[...], wg_ref[0, :, cols].astype(BF16))
            up = _bdot(xb_ref[...], wu_ref[0, :, cols].astype(BF16))
            act_ref[:, cols] = (_silu(gate) * up).astype(BF16)
        for n in range(d // MXU_N):
            cols = slice(n * MXU_N, (n + 1) * MXU_N)
            acc_ref[:, cols] += jnp.where(mine, _bdot(act_ref[...], wd_ref[0, :, cols].astype(BF16)), 0.0)

    @pl.when(jnp.logical_not(wide) & (hi > lo))
    def _():
        wgb_ref[...] = wg_ref[0].astype(BF16)
        wub_ref[...] = wu_ref[0].astype(BF16)
        wdb_ref[...] = wd_ref[0].astype(BF16)
        for sub in range(nsub):
            r0 = sub * ts

            @pl.when((lo < r0 + ts) & (hi > r0))
            def _():
                xs = xb_ref[r0:r0 + ts, :]
                act = (_silu(_bdot(xs, wgb_ref[...])) * _bdot(xs, wub_ref[...])).astype(BF16)
                y = _bdot(act, wdb_ref[...])
                rows = r0 + lax.broadcasted_iota(jnp.int32, (ts, 1), 0)
                acc_ref[r0:r0 + ts, :] += jnp.where((rows >= lo) & (rows < hi), y, 0.0)

    @pl.when((last == 1) & (hc == pl.num_programs(1) - 1))
    def _():
        for sub in range(nsub):
            o_ref[sub * ts:(sub + 1) * ts, :] = _pack_bf16_pairs(acc_ref[sub * ts:(sub + 1) * ts, :])


def _experts(xg, meta, w_gate_up, w_down, *, tile, th=512, ts=256):
    nrows = xg.shape[0]
    n_exp, hidden, d = w_down.shape
    n_hc = hidden // th
    ts = min(ts, tile)
    wgu = w_gate_up
    return pl.pallas_call(
        functools.partial(_expert_kernel, ts=ts),
        out_shape=jax.ShapeDtypeStruct((nrows, d // 2), jnp.uint32),
        grid_spec=pltpu.PrefetchScalarGridSpec(
            num_scalar_prefetch=1,
            grid=(meta.shape[1], n_hc),
            in_specs=[pl.BlockSpec((tile, d // 2), lambda v, c, m: (m[0, v], 0)),
                      pl.BlockSpec((1, d, th), lambda v, c, m: (m[1, v], 0, c)),
                      pl.BlockSpec((1, d, th), lambda v, c, m: (m[1, v], 0, c + n_hc)),
                      pl.BlockSpec((1, th, d), lambda v, c, m: (m[1, v], c, 0))],
            out_specs=pl.BlockSpec((tile, d // 2), lambda v, c, m: (m[0, v], 0)),
            scratch_shapes=[pltpu.VMEM((tile, d), F32), pltpu.VMEM((tile, d), BF16), pltpu.VMEM((tile, th), BF16),
                            pltpu.VMEM((d, th), BF16), pltpu.VMEM((d, th), BF16), pltpu.VMEM((th, d), BF16)]),
        compiler_params=_params(("arbitrary", "arbitrary"), 56),
        name="moe_experts",
    )(meta, xg, wgu, wgu, w_down)


def _combine_kernel(x_ref, gate_ref, fg_ref, y0_ref, y1_ref, o_ref, *, final_norm):
    g = gate_ref[...]
    out = x_ref[...] + g[:, 0:1] * _unpack_pairs_f32(y0_ref[0]) + g[:, 1:2] * _unpack_pairs_f32(y1_ref[0])
    if final_norm:
        out = _rms(out, fg_ref[...])
    o_ref[...] = out


def _combine(x2, gates_t, yk, final_gain, *, tm=512):
    ntok, d = x2.shape
    tm = min(tm, ntok)
    final_norm = final_gain is not None
    fg = (final_gain if final_norm else jnp.ones((d,), F32)).reshape(1, d)
    return pl.pallas_call(
        functools.partial(_combine_kernel, final_norm=final_norm),
        out_shape=jax.ShapeDtypeStruct((ntok, d), F32),
        grid=(ntok // tm,),
        in_specs=[pl.BlockSpec((tm, d), lambda i: (i, 0)),
                  pl.BlockSpec((tm, TOP_K), lambda i: (i, 0)),
                  pl.BlockSpec((1, d), lambda i: (0, 0)),
                  pl.BlockSpec((1, tm, d // 2), lambda i: (0, i, 0)),
                  pl.BlockSpec((1, tm, d // 2), lambda i: (1, i, 0))],
        out_specs=pl.BlockSpec((tm, d), lambda i: (i, 0)),
        compiler_params=_params(("parallel",), 40),
        name="moe_combine",
    )(x2, gates_t, fg, yk, yk)


def _moe_routed(x, ln, w_router, w_gate_up, w_down, *, tile=2048):
    bsz, seqlen, d = x.shape
    ntok = bsz * seqlen
    tile = min(tile, TOP_K * ntok)
    x2 = x.reshape(ntok, d)
    idx, gates, hp, pos, counts = _router(x2, ln, w_router)
    rank, meta = _moe_plan(idx, pos, counts, tile)
    xg = _sc_scatter_rows(hp, rank, TOP_K * ntok)
    y = _experts(xg, meta, w_gate_up, w_down, tile=tile)
    yk = _sc_gather_rows(y, rank, sub=64).reshape(TOP_K, ntok, d // 2)
    return gates.T, yk


def _moe_layer(x, ln, w_router, w_gate_up, w_down, *, final_gain=None, tile=2048):
    bsz, seqlen, d = x.shape
    gates_t, yk = _moe_routed(x, ln, w_router, w_gate_up, w_down, tile=tile)
    out = _combine(x.reshape(bsz * seqlen, d), gates_t, yk, final_gain)
    return out.reshape(bsz, seqlen, d)


def kernel(x, l0_ln1, l0_s5_lam_re, l0_s5_lam_im, l0_s5_log_dt, l0_s5_b_re, l0_s5_b_im, l0_s5_c_re, l0_s5_c_im, l0_s5_d, l0_s5_w_glu, l0_s5_b_glu, l0_ln2, l0_ffn_w_gate_up, l0_ffn_w_down, l1_ln1, l1_gla_w_in, l1_gla_w_g2, l1_gla_b_g2, l1_gla_norm, l1_gla_w_out, l1_ln2, l1_moe_router, l1_moe_w_gate_up, l1_moe_w_down, l2_ln1, l2_swa_w_qkv, l2_swa_b_qkv, l2_swa_sinks, l2_swa_w_out, l2_swa_b_out, l2_ln2, l2_ffn_w_gate_up, l2_ffn_w_down, l3_ln1, l3_s5_lam_re, l3_s5_lam_im, l3_s5_log_dt, l3_s5_b_re, l3_s5_b_im, l3_s5_c_re, l3_s5_c_im, l3_s5_d, l3_s5_w_glu, l3_s5_b_glu, l3_ln2, l3_moe_router, l3_moe_w_gate_up, l3_moe_w_down, ln_f):
    s5_params = ((l0_s5_lam_re, l0_s5_lam_im, l0_s5_log_dt, l0_s5_b_re, l0_s5_b_im, l0_s5_c_re, l0_s5_c_im),
                 (l3_s5_lam_re, l3_s5_lam_im, l3_s5_log_dt, l3_s5_b_re, l3_s5_b_im, l3_s5_c_re, l3_s5_c_im))
    s5_ops = jax.vmap(_s5_operators)(*(jnp.stack(pair) for pair in zip(*s5_params)))
    s5_ops = tuple(a.reshape((-1,) + a.shape[2:]) for a in s5_ops)
    x = _s5_layer(x, l0_ln1, s5_ops, 0, l0_s5_d, l0_s5_w_glu, l0_s5_b_glu)
    x = _dense_ffn_layer(x, l0_ln2, l0_ffn_w_gate_up, l0_ffn_w_down)
    x = _gla_layer(x, l1_ln1, l1_gla_w_in, l1_gla_w_g2, l1_gla_b_g2, l1_gla_norm, l1_gla_w_out)
    x = _moe_layer(x, l1_ln2, l1_moe_router, l1_moe_w_gate_up, l1_moe_w_down)
    x = _swa_layer(x, l2_ln1, l2_swa_w_qkv, l2_swa_b_qkv, l2_swa_sinks, l2_swa_w_out, l2_swa_b_out)
    x = _dense_ffn_layer(x, l2_ln2, l2_ffn_w_gate_up, l2_ffn_w_down)
    x = _s5_layer(x, l3_ln1, s5_ops, 1, l3_s5_d, l3_s5_w_glu, l3_s5_b_glu)
    return _moe_layer(x, l3_ln2, l3_moe_router, l3_moe_w_gate_up, l3_moe_w_down, final_gain=ln_f)
```

```python
import functools
import math

import jax
import jax.numpy as jnp
from jax import lax
from jax.experimental import pallas as pl
from jax.experimental.pallas import tpu as pltpu
from jax.experimental.pallas import tpu_sc as plsc

F32 = jnp.float32
BF16 = jnp.bfloat16
EPS = 1e-6
LANES = 128
MIB = 1 << 20

S5_GROUP = 16
S5_CHUNK = 16
S5_SLAB_GROUPS = LANES // S5_GROUP
S5_PITCH_PAD = 8
S5_SCAN_UNROLL = 8
S5_CAUSAL_BANDS = 8

GLA_HEADS = 4
GLA_GATE_RANK = 16
GLA_GATE_NORM = 16.0
GLA_CHUNK = 64

SWA_HEAD_DIM = 64
SWA_KV_HEADS = 2
SWA_WINDOW = 128
SWA_BLOCK = 128
MASK_VALUE = -1e30

TOP_K = 2


def _params(semantics, vmem_mib):
    return pltpu.CompilerParams(dimension_semantics=semantics, vmem_limit_bytes=vmem_mib * MIB)


def _resident(block_shape, index_map):
    return pl.BlockSpec(block_shape, index_map, pipeline_mode=pl.Buffered(1))


def _rms(xf, gain):
    return xf * lax.rsqrt(jnp.mean(xf * xf, axis=-1, keepdims=True) + EPS) * gain


def _gelu_tanh(x):
    return 0.5 * x * (1.0 + jnp.tanh(math.sqrt(2.0 / math.pi) * (x + 0.044715 * (x * x * x))))


def _silu(x):
    return x * jax.nn.sigmoid(x)


def _bdot(a, b):
    return jnp.dot(a, b, preferred_element_type=F32)


S5_ROW_TILE = 1024


def _s5_norm_kernel(x_ref, g_ref, o_ref, scr_ref, *, nloc):
    h = _rms(x_ref[0], g_ref[...])
    nslab = scr_ref.shape[0]
    for c in range(nslab):
        scr_ref[c] = h[:, c * LANES:(c + 1) * LANES]
    for s in range(S5_CHUNK):
        rows = pl.ds(s, nloc, stride=S5_CHUNK)
        o_ref[0, s] = jnp.concatenate([scr_ref[c, rows, :] for c in range(nslab)], axis=1).astype(o_ref.dtype)


def _s5_norm(x, gain):
    bsz, seqlen, d = x.shape
    nch = seqlen // S5_CHUNK
    tm = min(S5_ROW_TILE, seqlen)
    nloc = tm // S5_CHUNK
    return pl.pallas_call(
        functools.partial(_s5_norm_kernel, nloc=nloc),
        out_shape=jax.ShapeDtypeStruct((bsz, S5_CHUNK, nch, d), BF16),
        grid=(bsz, seqlen // tm),
        in_specs=[pl.BlockSpec((1, tm, d), lambda b, i: (b, i, 0)),
                  pl.BlockSpec((1, d), lambda b, i: (0, 0))],
        out_specs=pl.BlockSpec((1, S5_CHUNK, nloc, d), lambda b, i: (b, 0, i, 0)),
        scratch_shapes=[pltpu.VMEM((d // LANES, tm, LANES), F32)],
        compiler_params=_params(("parallel", "parallel"), 32),
        name="s5_norm",
    )(x, gain.reshape(1, d))


def _tiling_matrix(rows, cols):
    p = lax.broadcasted_iota(jnp.int32, (rows, cols), 0)
    c = lax.broadcasted_iota(jnp.int32, (rows, cols), 1)
    return jnp.where(c % rows == p, 1.0, 0.0).astype(BF16)


def _same_group(shape, row_group, col_group):
    r = lax.broadcasted_iota(jnp.int32, shape, 0)
    c = lax.broadcasted_iota(jnp.int32, shape, 1)
    return (r // row_group) == (c // col_group)


def _s5_build_operators(vw_ref, mw_ref, toep_ref, win_ref, wout_ref):
    tn = (((0,), (0,)), ((), ()))
    nstate = vw_ref.shape[-1]
    half = S5_SLAB_GROUPS * nstate
    rep_ch = _tiling_matrix(S5_GROUP, LANES)
    rep_st = _tiling_matrix(nstate, half)
    diag_in = _same_group((LANES, half), S5_GROUP, nstate)
    diag_out = _same_group((half, LANES), nstate, S5_GROUP)

    def out_block(q, r):
        e = lax.dot_general(mw_ref[0, 2 * q + r].astype(BF16), rep_ch, tn, preferred_element_type=F32)
        return jnp.where(diag_out, e, 0.0).astype(BF16)

    for a in range(S5_CHUNK):
        for r in range(2):
            e = _bdot(vw_ref[0, 2 * a + r].astype(BF16), rep_st)
            win_ref[a * LANES:(a + 1) * LANES, r * half:(r + 1) * half] = jnp.where(diag_in, e, 0.0).astype(BF16)
            wout_ref[r * half:(r + 1) * half, a * LANES:(a + 1) * LANES] = out_block(a + 1, r)
    b_bar = win_ref[(S5_CHUNK - 1) * LANES:S5_CHUNK * LANES, :]
    taps = [_bdot(b_bar, jnp.concatenate([out_block(0, 0), out_block(0, 1)], axis=0)).astype(BF16)]
    for j in range(1, S5_CHUNK):
        taps.append(_bdot(b_bar, wout_ref[:, (j - 1) * LANES:j * LANES]).astype(BF16))
    zero = jnp.zeros((LANES, LANES), BF16)
    for a in range(S5_CHUNK):
        for b in range(S5_CHUNK):
            toep_ref[a * LANES:(a + 1) * LANES, b * LANES:(b + 1) * LANES] = taps[b - a] if b >= a else zero


def _s5_conv_kernel(h_ref, vw_ref, mw_ref, a_ref, d_ref, o_ref, s_ref, toep_ref, win_ref, wout_ref,
                    *, nseq, nch):
    pitch = nch + S5_PITCH_PAD
    nl = a_ref.shape[1] // 2

    @pl.when(pl.program_id(1) == 0)
    def _():
        _s5_build_operators(vw_ref, mw_ref, toep_ref, win_ref, wout_ref)

    lhs = jnp.concatenate(
        [jnp.concatenate([h_ref[bl, s] for s in range(S5_CHUNK)], axis=1) for bl in range(nseq)], axis=0)
    bc = _bdot(lhs, win_ref[...])
    for bl in range(nseq):
        for j in range(2 * nl):
            s_ref[j, bl * pitch:bl * pitch + nch, :] = bc[bl * nch:(bl + 1) * nch, j * LANES:(j + 1) * LANES]
    a_re = [a_ref[0, j:j + 1, :] for j in range(nl)]
    a_im = [a_ref[0, nl + j:nl + j + 1, :] for j in range(nl)]

    def step(n, carry):
        p_re, p_im = carry
        rows = pl.ds(n, nseq, stride=pitch)
        n_re, n_im = [], []
        for j in range(nl):
            c_re = s_ref[j, rows, :]
            c_im = s_ref[nl + j, rows, :]
            s_ref[j, rows, :] = p_re[j]
            s_ref[nl + j, rows, :] = p_im[j]
            n_re.append(a_re[j] * p_re[j] - a_im[j] * p_im[j] + c_re)
            n_im.append(a_re[j] * p_im[j] + a_im[j] * p_re[j] + c_im)
        return tuple(n_re), tuple(n_im)

    def steps(m, carry):
        for u in range(S5_SCAN_UNROLL):
            carry = step(m * S5_SCAN_UNROLL + u, carry)
        return carry

    zeros = tuple(jnp.zeros((nseq, LANES), F32) for _ in range(nl))
    lax.fori_loop(0, nch // S5_SCAN_UNROLL, steps, (zeros, zeros))
    x_prev = jnp.concatenate(
        [jnp.concatenate([s_ref[j, bl * pitch:bl * pitch + nch, :] for j in range(2 * nl)], axis=1)
         for bl in range(nseq)], axis=0).astype(BF16)
    band = S5_CHUNK // S5_CAUSAL_BANDS
    y_bands = []
    for q in range(S5_CAUSAL_BANDS):
        kk = (q + 1) * band * LANES
        cols = slice(q * band * LANES, (q + 1) * band * LANES)
        y_bands.append(_bdot(lhs[:, :kk], toep_ref[:kk, cols]) + _bdot(x_prev, wout_ref[:, cols]))
    dskip = d_ref[0]
    for bl in range(nseq):
        for s in range(S5_CHUNK):
            ys = y_bands[s // band][bl * nch:(bl + 1) * nch, (s % band) * LANES:(s % band + 1) * LANES]
            ys = ys + dskip * h_ref[bl, s].astype(F32)
            o_ref[bl, s] = _gelu_tanh(ys).astype(o_ref.dtype)


def _s5_conv(hp, vw, mw, a_pack, d_skip, *, nseq, layer):
    bsz, _, nch, d = hp.shape
    nslab = d // LANES
    kdim = S5_CHUNK * LANES
    sdim = a_pack.shape[1] * LANES
    first = layer * nslab
    blk4 = lambda a: pl.BlockSpec((1,) + a.shape[1:], lambda c, b: (first + c, 0, 0, 0))
    return pl.pallas_call(
        functools.partial(_s5_conv_kernel, nseq=nseq, nch=nch),
        out_shape=jax.ShapeDtypeStruct(hp.shape, BF16),
        grid=(nslab, bsz // nseq),
        in_specs=[pl.BlockSpec((nseq, S5_CHUNK, nch, LANES), lambda c, b: (b, 0, 0, c)),
                  blk4(vw), blk4(mw),
                  pl.BlockSpec((1, sdim // LANES, LANES), lambda c, b: (first + c, 0, 0)),
                  pl.BlockSpec((1, 1, LANES), lambda c, b: (c, 0, 0))],
        out_specs=pl.BlockSpec((nseq, S5_CHUNK, nch, LANES), lambda c, b: (b, 0, 0, c)),
        scratch_shapes=[pltpu.VMEM((sdim // LANES, nseq * (nch + S5_PITCH_PAD), LANES), F32),
                        pltpu.VMEM((kdim, kdim), BF16),
                        pltpu.VMEM((kdim, sdim), BF16),
                        pltpu.VMEM((sdim, kdim), BF16)],
        compiler_params=_params(("arbitrary", "arbitrary"), 56),
        name="s5_conv",
    )(hp, vw, mw, a_pack, d_skip.reshape(nslab, 1, LANES))


def _s5_glu_kernel(y_ref, x_ref, w_ref, b_ref, o_ref, scr_ref, *, nloc):
    nslab = scr_ref.shape[0]
    y = jnp.concatenate([y_ref[0, s] for s in range(S5_CHUNK)], axis=0)
    u = y.astype(F32) * jax.nn.sigmoid(_bdot(y, w_ref[...]) + b_ref[...])
    for s in range(S5_CHUNK):
        rows = pl.ds(s, nloc, stride=S5_CHUNK)
        for c in range(nslab):
            scr_ref[c, rows, :] = u[s * nloc:(s + 1) * nloc, c * LANES:(c + 1) * LANES]
    o_ref[0] = x_ref[0] + jnp.concatenate([scr_ref[c] for c in range(nslab)], axis=1)


def _s5_glu(yp, x, w_glu, b_glu):
    bsz, seqlen, d = x.shape
    tm = min(S5_ROW_TILE, seqlen)
    nloc = tm // S5_CHUNK
    return pl.pallas_call(
        functools.partial(_s5_glu_kernel, nloc=nloc),
        out_shape=jax.ShapeDtypeStruct(x.shape, F32),
        grid=(bsz, seqlen // tm),
        in_specs=[pl.BlockSpec((1, S5_CHUNK, nloc, d), lambda b, i: (b, 0, i, 0)),
                  pl.BlockSpec((1, tm, d), lambda b, i: (b, i, 0)),
                  _resident((d, d), lambda b, i: (0, 0)),
                  pl.BlockSpec((1, d), lambda b, i: (0, 0))],
        out_specs=pl.BlockSpec((1, tm, d), lambda b, i: (b, i, 0)),
        scratch_shapes=[pltpu.VMEM((d // LANES, tm, LANES), F32)],
        compiler_params=_params(("parallel", "parallel"), 40),
        name="s5_glu",
    )(yp, x, w_glu.astype(BF16), b_glu.reshape(1, d))


def _s5_operators(lam_re, lam_im, log_dt, b_re, b_im, c_re, c_im):
    ngroups, nstate = lam_re.shape
    gpc = S5_SLAB_GROUPS
    nslab = ngroups // gpc
    dt = jnp.exp(log_dt)[:, None]
    j = jnp.arange(S5_CHUNK + 1, dtype=F32)[:, None, None]
    mag = jnp.exp(j * (lam_re * dt)[None])
    ang = j * (lam_im * dt)[None]
    pw_re, pw_im = mag * jnp.cos(ang), mag * jnp.sin(ang)
    num_re, num_im = pw_re[1] - 1.0, pw_im[1]
    den = lam_re * lam_re + lam_im * lam_im
    f_re = (num_re * lam_re + num_im * lam_im) / den
    f_im = (num_im * lam_re - num_re * lam_im) / den
    bb_re = f_re[..., None] * b_re - f_im[..., None] * b_im
    bb_im = f_re[..., None] * b_im + f_im[..., None] * b_re
    jr = (S5_CHUNK - 1) - jnp.arange(S5_CHUNK, dtype=F32)[:, None, None]
    mag_r = jnp.exp(jr * (lam_re * dt)[None])
    ang_r = jr * (lam_im * dt)[None]
    rev_re, rev_im = mag_r * jnp.cos(ang_r), mag_r * jnp.sin(ang_r)
    slabbed = lambda a: a.reshape(a.shape[0], nslab, gpc, nstate).transpose(1, 0, 2, 3)
    rv_re, rv_im = slabbed(rev_re)[:, :, :, None, :], slabbed(rev_im)[:, :, :, None, :]
    bt_re = bb_re.transpose(0, 2, 1).reshape(nslab, 1, gpc, S5_GROUP, nstate)
    bt_im = bb_im.transpose(0, 2, 1).reshape(nslab, 1, gpc, S5_GROUP, nstate)
    vw = jnp.stack([rv_re * bt_re - rv_im * bt_im, rv_re * bt_im + rv_im * bt_re], axis=2)
    vw = vw.reshape(nslab, 2 * S5_CHUNK, LANES, nstate)
    pc_re, pc_im = slabbed(pw_re)[:, :, None, :, :], slabbed(pw_im)[:, :, None, :, :]
    ct_re = c_re.reshape(nslab, gpc, S5_GROUP, nstate).transpose(0, 2, 1, 3)[:, None]
    ct_im = c_im.reshape(nslab, gpc, S5_GROUP, nstate).transpose(0, 2, 1, 3)[:, None]
    mw = jnp.stack([ct_re * pc_re - ct_im * pc_im, -(ct_re * pc_im + ct_im * pc_re)], axis=2)
    mw = mw.reshape(nslab, 2 * (S5_CHUNK + 1), S5_GROUP, gpc * nstate)
    half = gpc * nstate // LANES
    a_pack = jnp.concatenate([pw_re[S5_CHUNK].reshape(nslab, half, LANES),
                              pw_im[S5_CHUNK].reshape(nslab, half, LANES)], axis=1)
    return vw, mw, a_pack


def _s5_layer(x, ln, operators, layer, d_skip, w_glu, b_glu, *, nseq=4):
    vw, mw, a_pack = operators
    hp = _s5_norm(x, ln)
    yp = _s5_conv(hp, vw, mw, a_pack, d_skip, nseq=min(nseq, x.shape[0]), layer=layer)
    return _s5_glu(yp, x, w_glu, b_glu)


def _dense_ffn_kernel(x_ref, g_ref, wg_ref, wu_ref, wd_ref, o_ref):
    xf = x_ref[...]
    h = _rms(xf, g_ref[...]).astype(BF16)
    act = (_silu(_bdot(h, wg_ref[...])) * _bdot(h, wu_ref[...])).astype(BF16)
    o_ref[...] = xf + _bdot(act, wd_ref[...])


def _dense_ffn_layer(x, ln, w_gate_up, w_down, *, tm=512):
    bsz, seqlen, d = x.shape
    ntok = bsz * seqlen
    hidden = w_down.shape[0]
    tm = min(tm, ntok)
    wgu = w_gate_up.astype(BF16)
    out = pl.pallas_call(
        _dense_ffn_kernel,
        out_shape=jax.ShapeDtypeStruct((ntok, d), F32),
        grid=(ntok // tm,),
        in_specs=[pl.BlockSpec((tm, d), lambda i: (i, 0)),
                  pl.BlockSpec((1, d), lambda i: (0, 0)),
                  _resident((d, hidden), lambda i: (0, 0)),
                  _resident((d, hidden), lambda i: (0, 1)),
                  _resident((hidden, d), lambda i: (0, 0))],
        out_specs=pl.BlockSpec((tm, d), lambda i: (i, 0)),
        compiler_params=_params(("parallel",), 56),
        name="dense_ffn",
    )(x.reshape(ntok, d), ln.reshape(1, d), wgu, wgu, w_down.astype(BF16))
    return out.reshape(bsz, seqlen, d)


def _log_sigmoid(z):
    return jnp.minimum(z, 0.0) - jnp.log(1.0 + jnp.exp(-jnp.abs(z)))


def _gla_kernel(x_ref, ln_ref, wm_ref, wgl_ref, wg2_ref, bg2_ref, gn_ref, wo_ref, o_ref, st_ref,
                *, tq, dk, dv, heads):
    hdk, hdv = dk // heads, dv // heads
    chunk = GLA_CHUNK
    nt = (((1,), (1,)), ((), ()))
    tn = (((0,), (0,)), ((), ()))

    @pl.when(pl.program_id(1) == 0)
    def _():
        st_ref[...] = jnp.zeros_like(st_ref)

    xf = x_ref[0]
    h = _rms(xf, ln_ref[...]).astype(BF16)
    proj = _bdot(h, wm_ref[...])
    glow = _bdot(h, wgl_ref[...]).astype(BF16)
    la = _log_sigmoid(_bdot(glow, wg2_ref[...]) + bg2_ref[...]) * (1.0 / GLA_GATE_NORM)
    row = lax.broadcasted_iota(jnp.int32, (chunk, chunk), 0)
    col = lax.broadcasted_iota(jnp.int32, (chunk, chunk), 1)
    causal = row >= col
    tri = jnp.where(causal, 1.0, 0.0).astype(BF16)
    scale = hdk ** -0.5
    outs = []
    for c in range(tq // chunk):
        r0 = c * chunk
        la_c = la[r0:r0 + chunk, :]
        la_hi = la_c.astype(BF16)
        la_lo = (la_c - la_hi.astype(F32)).astype(BF16)
        gcum_all = _bdot(tri, la_hi) + _bdot(tri, la_lo)
        head_out = []
        for hd in range(heads):
            gcum = gcum_all[:, hd * hdk:(hd + 1) * hdk]
            g_last = gcum[chunk - 1:chunk, :]
            q_c = proj[r0:r0 + chunk, hd * hdk:(hd + 1) * hdk] * scale
            k_c = proj[r0:r0 + chunk, dk + hd * hdk:dk + (hd + 1) * hdk]
            v_c = proj[r0:r0 + chunk, 2 * dk + hd * hdv:2 * dk + (hd + 1) * hdv].astype(BF16)
            q_s = (q_c * jnp.exp(gcum)).astype(BF16)
            k_s = (k_c * jnp.exp(-gcum)).astype(BF16)
            k_end = (k_c * jnp.exp(g_last - gcum)).astype(BF16)
            scores = lax.dot_general(q_s, k_s, nt, preferred_element_type=F32)
            scores = jnp.where(causal, scores, 0.0).astype(BF16)
            state_t = st_ref[hd]
            o = _bdot(scores, v_c) + lax.dot_general(q_s, state_t.astype(BF16), nt,
                                                     preferred_element_type=F32)
            kv_t = lax.dot_general(v_c, k_end, tn, preferred_element_type=F32)
            st_ref[hd] = state_t * jnp.exp(g_last) + kv_t
            head_out.append(o * lax.rsqrt(jnp.mean(o * o, axis=-1, keepdims=True) + EPS))
        outs.append(jnp.concatenate(head_out, axis=1))
    o_all = jnp.concatenate(outs, axis=0)
    r = proj[:, 2 * dk + dv:]
    o_all = (o_all * gn_ref[...] * _silu(r)).astype(BF16)
    o_ref[0] = xf + _bdot(o_all, wo_ref[...])


def _gla_layer(x, ln, w_in, w_g2, b_g2, g_norm, w_out, *, tq=256):
    bsz, seqlen, d = x.shape
    dk = w_g2.shape[1]
    dv = w_out.shape[0]
    nmain = 2 * dk + 2 * dv
    tq = min(tq, seqlen)
    w_main = w_in[:, :nmain].astype(BF16)
    w_glow = jnp.pad(w_in[:, nmain:], ((0, 0), (0, LANES - GLA_GATE_RANK))).astype(BF16)
    w_g2p = jnp.pad(w_g2, ((0, LANES - GLA_GATE_RANK), (0, 0))).astype(BF16)
    hdk, hdv = dk // GLA_HEADS, dv // GLA_HEADS
    const = lambda b, t: (0, 0)
    return pl.pallas_call(
        functools.partial(_gla_kernel, tq=tq, dk=dk, dv=dv, heads=GLA_HEADS),
        out_shape=jax.ShapeDtypeStruct(x.shape, F32),
        grid=(bsz, seqlen // tq),
        in_specs=[pl.BlockSpec((1, tq, d), lambda b, t: (b, t, 0)),
                  pl.BlockSpec((1, d), const),
                  _resident((d, nmain), const),
                  _resident((d, LANES), const),
                  _resident((LANES, dk), const),
                  pl.BlockSpec((1, dk), const),
                  pl.BlockSpec((1, dv), const),
                  _resident((dv, d), const)],
        out_specs=pl.BlockSpec((1, tq, d), lambda b, t: (b, t, 0)),
        scratch_shapes=[pltpu.VMEM((GLA_HEADS, hdv, hdk), F32)],
        compiler_params=_params(("parallel", "arbitrary"), 48),
        name="gla",
    )(x, ln.reshape(1, d), w_main, w_glow, w_g2p, b_g2.reshape(1, dk), g_norm.reshape(1, dv),
      w_out.astype(BF16))


LOG2E = math.log2(math.e)
SWA_SLOT_UNROLL = 4


def _swa_kernel(sink_ref, x_ref, ln_ref, wqkv_ref, bqkv_ref, wo_ref, bo_ref, o_ref, k_ref, v_ref,
                bias_ref, q_ref, a_ref, *, tq, q_heads):
    group = q_heads // SWA_KV_HEADS
    blk = SWA_BLOCK
    nt = (((1,), (1,)), ((), ()))
    b = pl.program_id(0)
    t = pl.program_id(1)
    nq = group * LANES

    @pl.when((b == 0) & (t == 0))
    def _():
        qi = lax.broadcasted_iota(jnp.int32, (blk, 2 * blk), 0)
        kj = lax.broadcasted_iota(jnp.int32, (blk, 2 * blk), 1)
        dist = qi + blk - kj
        in_window = (dist >= 0) & (dist < SWA_WINDOW)
        for hq in range(q_heads):
            slope = 2.0 ** (-8.0 * (hq + 1) / q_heads)
            bias_ref[hq] = jnp.where(in_window, -(slope * LOG2E) * dist.astype(F32), MASK_VALUE)

    @pl.when(t == 0)
    def _():
        k_ref[0:blk, :] = jnp.zeros((blk, LANES), BF16)
        v_ref[0:blk, :] = jnp.zeros((blk, LANES), BF16)

    xf = x_ref[0]
    h = _rms(xf, ln_ref[...]).astype(BF16)
    qkv = _bdot(h, wqkv_ref[...]) + bqkv_ref[...]
    for j in range(group):
        q_ref[j] = qkv[:, j * LANES:(j + 1) * LANES].astype(BF16)
    k_ref[blk:blk + tq, :] = qkv[:, nq:nq + LANES].astype(BF16)
    v_ref[blk:blk + tq, :] = qkv[:, nq + LANES:nq + 2 * LANES].astype(BF16)
    kj_row = lax.broadcasted_iota(jnp.int32, (1, 2 * blk), 1)
    no_prev = jnp.where(kj_row < blk, jnp.where(t == 0, MASK_VALUE, 0.0), 0.0)
    low_half = lax.broadcasted_iota(jnp.int32, (1, LANES), 1) < SWA_HEAD_DIM
    halves = (low_half, jnp.logical_not(low_half))

    def slots(jj, carry):
        for u in range(SWA_SLOT_UNROLL):
            j = jj * SWA_SLOT_UNROLL + u
            for i in range(tq // blk):
                r0 = i * blk
                q_slot = q_ref[j, r0:r0 + blk, :]
                outs = []
                for kh in range(SWA_KV_HEADS):
                    hq = kh * group + j
                    sink = sink_ref[hq] * LOG2E
                    q_h = jnp.where(halves[kh], q_slot, jnp.zeros_like(q_slot))
                    s = lax.dot_general(q_h, k_ref[r0:r0 + 2 * blk, :], nt, preferred_element_type=F32) + bias_ref[hq]
                    if i == 0:
                        s = s + no_prev
                    m = jnp.maximum(jnp.max(s, axis=-1, keepdims=True), sink)
                    p = jnp.exp2(s - m)
                    denom = jnp.sum(p, axis=-1, keepdims=True) + jnp.exp2(sink - m)
                    outs.append(_bdot(p.astype(BF16), v_ref[r0:r0 + 2 * blk, :]) * (1.0 / denom))
                a_ref[j, r0:r0 + blk, :] = jnp.where(low_half, outs[0], outs[1]).astype(BF16)
        return carry

    lax.fori_loop(0, group // SWA_SLOT_UNROLL, slots, 0)
    k_ref[0:blk, :] = k_ref[tq:tq + blk, :]
    v_ref[0:blk, :] = v_ref[tq:tq + blk, :]
    o_all = jnp.concatenate([a_ref[j] for j in range(group)], axis=1)
    o_ref[0] = xf + _bdot(o_all, wo_ref[...]) + bo_ref[...]


def _swa_layer(x, ln, w_qkv, b_qkv, sinks, w_out, b_out, *, tq=512):
    bsz, seqlen, d = x.shape
    hd = SWA_HEAD_DIM
    q_heads = sinks.shape[0]
    group = q_heads // SWA_KV_HEADS
    nq = q_heads * hd
    tq = min(tq, seqlen)
    q_scale = hd ** -0.5 * LOG2E
    wq = (w_qkv[:, :nq] * q_scale).reshape(d, SWA_KV_HEADS, group, hd).transpose(0, 2, 1, 3).reshape(d, nq)
    bq = (b_qkv[:nq] * q_scale).reshape(SWA_KV_HEADS, group, hd).transpose(1, 0, 2).reshape(nq)
    w_all = jnp.concatenate([wq, w_qkv[:, nq:]], axis=1).astype(BF16)
    b_all = jnp.concatenate([bq, b_qkv[nq:]]).reshape(1, -1)
    wo = w_out.reshape(SWA_KV_HEADS, group, hd, d).transpose(1, 0, 2, 3).reshape(nq, d).astype(BF16)
    nall = w_all.shape[1]
    const = lambda b, t, s: (0, 0)
    return pl.pallas_call(
        functools.partial(_swa_kernel, tq=tq, q_heads=q_heads),
        out_shape=jax.ShapeDtypeStruct(x.shape, F32),
        grid_spec=pltpu.PrefetchScalarGridSpec(
            num_scalar_prefetch=1,
            grid=(bsz, seqlen // tq),
            in_specs=[pl.BlockSpec((1, tq, d), lambda b, t, s: (b, t, 0)),
                      pl.BlockSpec((1, d), const),
                      _resident((d, nall), const),
                      pl.BlockSpec((1, nall), const),
                      _resident((nq, d), const),
                      pl.BlockSpec((1, d), const)],
            out_specs=pl.BlockSpec((1, tq, d), lambda b, t, s: (b, t, 0)),
            scratch_shapes=[pltpu.VMEM((SWA_BLOCK + tq, LANES), BF16), pltpu.VMEM((SWA_BLOCK + tq, LANES), BF16),
                            pltpu.VMEM((q_heads, SWA_BLOCK, 2 * SWA_BLOCK), F32),
                            pltpu.VMEM((group, tq, LANES), BF16), pltpu.VMEM((group, tq, LANES), BF16)]),
        compiler_params=_params(("arbitrary", "arbitrary"), 48),
        name="swa",
    )(sinks, x, ln.reshape(1, d), w_all, b_all, wo, b_out.reshape(1, d))


def _router_kernel(x_ref, ln_ref, whi_ref, wlo_ref, idx_ref, gate_ref, hp_ref, pos_ref, count_ref, tri_ref):
    nt = (((1,), (1,)), ((), ()))
    h = _rms(x_ref[...], ln_ref[...])
    h_hi = h.astype(BF16)
    h_lo = (h - h_hi.astype(F32)).astype(BF16)
    w_hi, w_lo = whi_ref[...], wlo_ref[...]
    logits = (lax.dot_general(w_hi, h_hi, nt, preferred_element_type=F32)
              + lax.dot_general(w_hi, h_lo, nt, preferred_element_type=F32)
              + lax.dot_general(w_lo, h_hi, nt, preferred_element_type=F32))
    n_exp = logits.shape[0]
    eid = lax.broadcasted_iota(jnp.int32, logits.shape, 0)
    m1 = jnp.max(logits, axis=0, keepdims=True)
    i1 = jnp.min(jnp.where(logits == m1, eid, n_exp), axis=0, keepdims=True)
    rest = jnp.where(eid == i1, -jnp.inf, logits)
    m2 = jnp.max(rest, axis=0, keepdims=True)
    i2 = jnp.min(jnp.where(rest == m2, eid, n_exp), axis=0, keepdims=True)
    e2 = jnp.exp(m2 - m1)
    g1 = 1.0 / (1.0 + e2)
    idx_ref[...] = jnp.concatenate([i1, i2], axis=0)
    gate_ref[...] = jnp.concatenate([g1, e2 * g1], axis=0)
    hp_ref[...] = _pack_bf16_pairs(h)
    tm = logits.shape[1]

    @pl.when(pl.program_id(0) == 0)
    def _():
        count_ref[...] = jnp.zeros_like(count_ref)
        r = lax.broadcasted_iota(jnp.int32, (tm, tm), 0)
        c = lax.broadcasted_iota(jnp.int32, (tm, tm), 1)
        tri_ref[...] = jnp.where(r < c, 1.0, 0.0).astype(BF16)

    pick1 = jnp.where(eid == i1, 1.0, 0.0)
    pick2 = jnp.where(eid == i2, 1.0, 0.0)
    picks = pick1 + pick2
    before = _bdot(picks.astype(BF16), tri_ref[...]) + count_ref[:, 0:1]
    pos_ref[...] = jnp.concatenate([jnp.sum(pick1 * before, axis=0, keepdims=True),
                                    jnp.sum(pick2 * before, axis=0, keepdims=True)], axis=0).astype(jnp.int32)
    count_ref[...] = count_ref[...] + jnp.sum(picks, axis=1, keepdims=True)


def _router(x2, ln, w_router, *, tm=512):
    ntok, d = x2.shape
    n_exp = w_router.shape[1]
    tm = min(tm, ntok)
    wt = w_router.T
    w_hi = wt.astype(BF16)
    w_lo = (wt - w_hi.astype(F32)).astype(BF16)
    return pl.pallas_call(
        _router_kernel,
        out_shape=(jax.ShapeDtypeStruct((TOP_K, ntok), jnp.int32), jax.ShapeDtypeStruct((TOP_K, ntok), F32),
                   jax.ShapeDtypeStruct((ntok, d // 2), jnp.uint32),
                   jax.ShapeDtypeStruct((TOP_K, ntok), jnp.int32), jax.ShapeDtypeStruct((n_exp, LANES), F32)),
        grid=(ntok // tm,),
        in_specs=[pl.BlockSpec((tm, d), lambda i: (i, 0)),
                  pl.BlockSpec((1, d), lambda i: (0, 0)),
                  pl.BlockSpec((n_exp, d), lambda i: (0, 0)),
                  pl.BlockSpec((n_exp, d), lambda i: (0, 0))],
        out_specs=(pl.BlockSpec((TOP_K, tm), lambda i: (0, i)), pl.BlockSpec((TOP_K, tm), lambda i: (0, i)),
                   pl.BlockSpec((tm, d // 2), lambda i: (i, 0)),
                   pl.BlockSpec((TOP_K, tm), lambda i: (0, i)), pl.BlockSpec((n_exp, LANES), lambda i: (0, 0))),
        scratch_shapes=[pltpu.VMEM((tm, tm), BF16)],
        compiler_params=_params(("arbitrary",), 32),
        name="moe_router",
    )(x2, ln.reshape(1, d), w_hi, w_lo)


def _moe_plan(idx, pos, counts, tile):
    n_exp = counts.shape[0]
    nslots = idx.size
    counts = counts[:, 0].astype(jnp.int32)
    ends = jnp.cumsum(counts)
    offs = ends - counts
    experts = jnp.arange(n_exp, dtype=jnp.int32).reshape(n_exp, 1, 1)
    rank = pos + jnp.sum(jnp.where(idx[None] == experts, offs.reshape(n_exp, 1, 1), 0), axis=0)
    n_tiles = nslots // tile
    n_visits = n_tiles + n_exp - 1
    first_tile = offs // tile
    last_tile = (ends - 1) // tile
    nvis = jnp.where(counts > 0, last_tile - first_tile + 1, 0)
    vend = jnp.cumsum(nvis)
    vstart = vend - nvis
    total = vend[-1]
    v = jnp.arange(n_visits, dtype=jnp.int32)
    vc = jnp.minimum(v, total - 1)
    e = jnp.minimum(jnp.sum((vc[:, None] >= vend[None, :]).astype(jnp.int32), axis=1), n_exp - 1)
    sel = (e[:, None] == jnp.arange(n_exp, dtype=jnp.int32)[None, :]).astype(jnp.int32)
    pick = lambda a: jnp.sum(sel * a[None, :], axis=1)
    tile_id = pick(first_tile) + vc - pick(vstart)
    lo = jnp.maximum(pick(offs), tile_id * tile) - tile_id * tile
    hi = jnp.minimum(pick(ends), (tile_id + 1) * tile) - tile_id * tile
    valid = v < total
    lo = jnp.where(valid, lo, 0)
    hi = jnp.where(valid, hi, 0)
    prev_tile = jnp.concatenate([jnp.full((1,), -1, jnp.int32), tile_id[:-1]])
    first = (valid & (tile_id != prev_tile)).astype(jnp.int32)
    next_tile = jnp.concatenate([tile_id[1:], jnp.full((1,), -1, jnp.int32)])
    last = (valid & ((tile_id != next_tile) | (v == total - 1))).astype(jnp.int32)
    meta = jnp.stack([tile_id, e, lo, hi, first, last]).astype(jnp.int32)
    return rank.astype(jnp.int32), meta


def _pack_bf16_pairs(h):
    half = h.shape[1] // 2
    bits = lax.bitcast_convert_type(h.astype(BF16).astype(F32), jnp.uint32)
    return (bits[:, half:] & jnp.uint32(0xFFFF0000)) | (bits[:, :half] >> 16)


def _unpack_pairs_f32(u):
    lo = lax.bitcast_convert_type(u << 16, F32)
    hi = lax.bitcast_convert_type(u & jnp.uint32(0xFFFF0000), F32)
    return jnp.concatenate([lo, hi], axis=1)


def _unpack_bf16_pairs(u):
    return _unpack_pairs_f32(u).astype(BF16)


SC_CORES = 2
SC_SUBCORES = 16
SC_INDEX_WINDOW = 128


def _sc_mesh():
    return plsc.VectorSubcoreMesh(core_axis_name="c", subcore_axis_name="s")


def _sc_worker_id():
    return lax.axis_index("c") * SC_SUBCORES + lax.axis_index("s")


def _sc_scatter_rows(src, rank, nrows):
    ntok, width = src.shape
    win = SC_INDEX_WINDOW
    per = ntok // (SC_CORES * SC_SUBCORES)

    @pl.kernel(out_type=jax.ShapeDtypeStruct((nrows, width), src.dtype), mesh=_sc_mesh(),
               scratch_types=[pltpu.VMEM((1, win), jnp.int32)] * TOP_K + [pltpu.VMEM((win, width), src.dtype)],
               name="moe_dispatch_sc")
    def scatter(src_hbm, rank_hbm, o_hbm, *scratch):
        idx_vmem, buf = scratch[:TOP_K], scratch[TOP_K]
        wid = _sc_worker_id()

        @pl.loop(0, per // win)
        def _(blk):
            base = wid * per + blk * win
            for k in range(TOP_K):
                pltpu.sync_copy(rank_hbm.at[pl.ds(k, 1), pl.ds(base, win)], idx_vmem[k])
            pltpu.sync_copy(src_hbm.at[pl.ds(base, win)], buf)
            for k in range(TOP_K):
                pltpu.sync_copy(buf, o_hbm.at[idx_vmem[k].at[0]])

    return scatter(src, rank)


def _sc_gather_rows(src, rank, *, sub=32):
    nslot, ntok = rank.shape
    n = nslot * ntok
    width = src.shape[1]
    win = SC_INDEX_WINDOW
    per = n // (SC_CORES * SC_SUBCORES)
    nsub = win // sub

    @pl.kernel(out_type=jax.ShapeDtypeStruct((n, width), src.dtype), mesh=_sc_mesh(),
               scratch_types=[pltpu.VMEM((1, win), jnp.int32)] + [pltpu.VMEM((sub, width), src.dtype)] * 2
               + [pltpu.SemaphoreType.DMA] * 4,
               name="moe_gather_sc")
    def gather(src_hbm, idx_hbm, o_hbm, i_vmem, buf0, buf1, g0, g1, w0, w1):
        bufs, gsem, wsem = (buf0, buf1), (g0, g1), (w0, w1)
        wid = _sc_worker_id()

        @pl.loop(0, per // win)
        def _(blk):
            base = wid * per + blk * win
            pltpu.sync_copy(idx_hbm.at[pl.ds(base // ntok, 1), pl.ds(base % ntok, win)], i_vmem)
            gathers = [pltpu.make_async_copy(src_hbm.at[i_vmem.at[0, pl.ds(sub * j, sub)]], bufs[j % 2], gsem[j % 2])
                       for j in range(nsub)]
            writes = [pltpu.make_async_copy(bufs[j % 2], o_hbm.at[pl.ds(base + sub * j, sub)], wsem[j % 2])
                      for j in range(nsub)]
            gathers[0].start()
            for j in range(nsub):
                if j + 1 < nsub:
                    if j >= 1:
                        writes[j - 1].wait()
                    gathers[j + 1].start()
                gathers[j].wait()
                writes[j].start()
            writes[nsub - 2].wait()
            writes[nsub - 1].wait()

    return gather(src, rank)


MXU_N = 256


def _expert_kernel(meta_ref, x_ref, wg_ref, wu_ref, wd_ref, o_ref, acc_ref, xb_ref, act_ref, wgb_ref, wub_ref,
                   wdb_ref, *, ts):
    v = pl.program_id(0)
    hc = pl.program_id(1)
    lo, hi, first, last = meta_ref[2, v], meta_ref[3, v], meta_ref[4, v], meta_ref[5, v]
    tile, d = acc_ref.shape
    nsub = tile // ts
    th = wgb_ref.shape[1]
    wide = (hi - lo) * 2 > tile

    @pl.when(hc == 0)
    def _():
        for sub in range(nsub):
            xb_ref[sub * ts:(sub + 1) * ts, :] = _unpack_bf16_pairs(x_ref[sub * ts:(sub + 1) * ts, :])

    fresh = (first == 1) & (hc == 0)

    @pl.when(fresh & jnp.logical_not(wide))
    def _():
        acc_ref[...] = jnp.zeros_like(acc_ref)

    def whole_tile(assign):
        rows = lax.broadcasted_iota(jnp.int32, (tile, 1), 0)
        mine = (rows >= lo) & (rows < hi)
        for n in range(th // MXU_N):
            cols = slice(n * MXU_N, (n + 1) * MXU_N)
            gate = _bdot(xb_ref[...], wg_ref[0, :, cols].astype(BF16))
            up = _bdot(xb_ref[...], wu_ref[0, :, cols].astype(BF16))
            act_ref[:, cols] = (_silu(gate) * up).astype(BF16)
        for n in range(d // MXU_N):
            cols = slice(n * MXU_N, (n + 1) * MXU_N)
            part = jnp.where(mine, _bdot(act_ref[...], wd_ref[0, :, cols].astype(BF16)), 0.0)
            acc_ref[:, cols] = part if assign else acc_ref[:, cols] + part

    pl.when(wide & fresh)(functools.partial(whole_tile, True))
    pl.when(wide & jnp.logical_not(fresh))(functools.partial(whole_tile, False))

    @pl.when(jnp.logical_not(wide) & (hi > lo))
    def _():
        wgb_ref[...] = wg_ref[0].astype(BF16)
        wub_ref[...] = wu_ref[0].astype(BF16)
        wdb_ref[...] = wd_ref[0].astype(BF16)
        for sub in range(nsub):
            r0 = sub * ts

            @pl.when((lo < r0 + ts) & (hi > r0))
            def _():
                xs = xb_ref[r0:r0 + ts, :]
                act = (_silu(_bdot(xs, wgb_ref[...])) * _bdot(xs, wub_ref[...])).astype(BF16)
                y = _bdot(act, wdb_ref[...])
                rows = r0 + lax.broadcasted_iota(jnp.int32, (ts, 1), 0)
                acc_ref[r0:r0 + ts, :] += jnp.where((rows >= lo) & (rows < hi), y, 0.0)

    @pl.when((last == 1) & (hc == pl.num_programs(1) - 1))
    def _():
        for sub in range(nsub):
            o_ref[sub * ts:(sub + 1) * ts, :] = _pack_bf16_pairs(acc_ref[sub * ts:(sub + 1) * ts, :])


def _experts(xg, meta, w_gate_up, w_down, *, tile, th=512, ts=256):
    nrows = xg.shape[0]
    n_exp, hidden, d = w_down.shape
    n_hc = hidden // th
    ts = min(ts, tile)
    wgu = w_gate_up
    return pl.pallas_call(
        functools.partial(_expert_kernel, ts=ts),
        out_shape=jax.ShapeDtypeStruct((nrows, d // 2), jnp.uint32),
        grid_spec=pltpu.PrefetchScalarGridSpec(
            num_scalar_prefetch=1,
            grid=(meta.shape[1], n_hc),
            in_specs=[pl.BlockSpec((tile, d // 2), lambda v, c, m: (m[0, v], 0)),
                      pl.BlockSpec((1, d, th), lambda v, c, m: (m[1, v], 0, c)),
                      pl.BlockSpec((1, d, th), lambda v, c, m: (m[1, v], 0, c + n_hc)),
                      pl.BlockSpec((1, th, d), lambda v, c, m: (m[1, v], c, 0))],
            out_specs=pl.BlockSpec((tile, d // 2), lambda v, c, m: (m[0, v], 0)),
            scratch_shapes=[pltpu.VMEM((tile, d), F32), pltpu.VMEM((tile, d), BF16), pltpu.VMEM((tile, th), BF16),
                            pltpu.VMEM((d, th), BF16), pltpu.VMEM((d, th), BF16), pltpu.VMEM((th, d), BF16)]),
        compiler_params=_params(("arbitrary", "arbitrary"), 56),
        name="moe_experts",
    )(meta, xg, wgu, wgu, w_down)


def _combine_kernel(x_ref, gate_ref, fg_ref, y0_ref, y1_ref, o_ref, *, final_norm):
    g = gate_ref[...]
    out = x_ref[...] + g[:, 0:1] * _unpack_pairs_f32(y0_ref[0]) + g[:, 1:2] * _unpack_pairs_f32(y1_ref[0])
    if final_norm:
        out = _rms(out, fg_ref[...])
    o_ref[...] = out


def _combine(x2, gates_t, yk, final_gain, *, tm=512):
    ntok, d = x2.shape
    tm = min(tm, ntok)
    final_norm = final_gain is not None
    fg = (final_gain if final_norm else jnp.ones((d,), F32)).reshape(1, d)
    return pl.pallas_call(
        functools.partial(_combine_kernel, final_norm=final_norm),
        out_shape=jax.ShapeDtypeStruct((ntok, d), F32),
        grid=(ntok // tm,),
        in_specs=[pl.BlockSpec((tm, d), lambda i: (i, 0)),
                  pl.BlockSpec((tm, TOP_K), lambda i: (i, 0)),
                  pl.BlockSpec((1, d), lambda i: (0, 0)),
                  pl.BlockSpec((1, tm, d // 2), lambda i: (0, i, 0)),
                  pl.BlockSpec((1, tm, d // 2), lambda i: (1, i, 0))],
        out_specs=pl.BlockSpec((tm, d), lambda i: (i, 0)),
        compiler_params=_params(("parallel",), 40),
        name="moe_combine",
    )(x2, gates_t, fg, yk, yk)


def _moe_routed(x, ln, w_router, w_gate_up, w_down, *, tile=2048):
    bsz, seqlen, d = x.shape
    ntok = bsz * seqlen
    tile = min(tile, TOP_K * ntok)
    x2 = x.reshape(ntok, d)
    idx, gates, hp, pos, counts = _router(x2, ln, w_router)
    rank, meta = _moe_plan(idx, pos, counts, tile)
    xg = _sc_scatter_rows(hp, rank, TOP_K * ntok)
    y = _experts(xg, meta, w_gate_up, w_down, tile=tile)
    yk = _sc_gather_rows(y, rank, sub=64).reshape(TOP_K, ntok, d // 2)
    return gates.T, yk


def _moe_layer(x, ln, w_router, w_gate_up, w_down, *, final_gain=None, tile=2048):
    bsz, seqlen, d = x.shape
    gates_t, yk = _moe_routed(x, ln, w_router, w_gate_up, w_down, tile=tile)
    out = _combine(x.reshape(bsz * seqlen, d), gates_t, yk, final_gain)
    return out.reshape(bsz, seqlen, d)


def kernel(x, l0_ln1, l0_s5_lam_re, l0_s5_lam_im, l0_s5_log_dt, l0_s5_b_re, l0_s5_b_im, l0_s5_c_re, l0_s5_c_im, l0_s5_d, l0_s5_w_glu, l0_s5_b_glu, l0_ln2, l0_ffn_w_gate_up, l0_ffn_w_down, l1_ln1, l1_gla_w_in, l1_gla_w_g2, l1_gla_b_g2, l1_gla_norm, l1_gla_w_out, l1_ln2, l1_moe_router, l1_moe_w_gate_up, l1_moe_w_down, l2_ln1, l2_swa_w_qkv, l2_swa_b_qkv, l2_swa_sinks, l2_swa_w_out, l2_swa_b_out, l2_ln2, l2_ffn_w_gate_up, l2_ffn_w_down, l3_ln1, l3_s5_lam_re, l3_s5_lam_im, l3_s5_log_dt, l3_s5_b_re, l3_s5_b_im, l3_s5_c_re, l3_s5_c_im, l3_s5_d, l3_s5_w_glu, l3_s5_b_glu, l3_ln2, l3_moe_router, l3_moe_w_gate_up, l3_moe_w_down, ln_f):
    s5_params = ((l0_s5_lam_re, l0_s5_lam_im, l0_s5_log_dt, l0_s5_b_re, l0_s5_b_im, l0_s5_c_re, l0_s5_c_im),
                 (l3_s5_lam_re, l3_s5_lam_im, l3_s5_log_dt, l3_s5_b_re, l3_s5_b_im, l3_s5_c_re, l3_s5_c_im))
    s5_ops = jax.vmap(_s5_operators)(*(jnp.stack(pair) for pair in zip(*s5_params)))
    s5_ops = tuple(a.reshape((-1,) + a.shape[2:]) for a in s5_ops)
    x = _s5_layer(x, l0_ln1, s5_ops, 0, l0_s5_d, l0_s5_w_glu, l0_s5_b_glu)
    x = _dense_ffn_layer(x, l0_ln2, l0_ffn_w_gate_up, l0_ffn_w_down)
    x = _gla_layer(x, l1_ln1, l1_gla_w_in, l1_gla_w_g2, l1_gla_b_g2, l1_gla_norm, l1_gla_w_out)
    x = _moe_layer(x, l1_ln2, l1_moe_router, l1_moe_w_gate_up, l1_moe_w_down)
    x = _swa_layer(x, l2_ln1, l2_swa_w_qkv, l2_swa_b_qkv, l2_swa_sinks, l2_swa_w_out, l2_swa_b_out)
    x = _dense_ffn_layer(x, l2_ln2, l2_ffn_w_gate_up, l2_ffn_w_down)
    x = _s5_layer(x, l3_ln1, s5_ops, 1, l3_s5_d, l3_s5_w_glu, l3_s5_b_glu)
    return _moe_layer(x, l3_ln2, l3_moe_router, l3_moe_w_gate_up, l3_moe_w_down, final_gain=ln_f)
```

```python
import functools
import math

import jax
import jax.numpy as jnp
from jax import lax
from jax.experimental import pallas as pl
from jax.experimental.pallas import tpu as pltpu
from jax.experimental.pallas import tpu_sc as plsc

F32 = jnp.float32
BF16 = jnp.bfloat16
EPS = 1e-6
LANES = 128
MIB = 1 << 20

S5_GROUP = 16
S5_CHUNK = 16
S5_SLAB_GROUPS = LANES // S5_GROUP
S5_PITCH_PAD = 8
S5_SCAN_UNROLL = 8
S5_CAUSAL_BANDS = 8

GLA_HEADS = 4
GLA_GATE_RANK = 16
GLA_GATE_NORM = 16.0
GLA_CHUNK = 64

SWA_HEAD_DIM = 64
SWA_KV_HEADS = 2
SWA_WINDOW = 128
SWA_BLOCK = 128
MASK_VALUE = -1e30

TOP_K = 2


def _params(semantics, vmem_mib):
    return pltpu.CompilerParams(dimension_semantics=semantics, vmem_limit_bytes=vmem_mib * MIB)


def _resident(block_shape, index_map):
    return pl.BlockSpec(block_shape, index_map, pipeline_mode=pl.Buffered(1))


def _rms(xf, gain):
    return xf * lax.rsqrt(jnp.mean(xf * xf, axis=-1, keepdims=True) + EPS) * gain


def _gelu_tanh(x):
    return 0.5 * x * (1.0 + jnp.tanh(math.sqrt(2.0 / math.pi) * (x + 0.044715 * (x * x * x))))


def _silu(x):
    return x * jax.nn.sigmoid(x)


def _bdot(a, b):
    return jnp.dot(a, b, preferred_element_type=F32)


S5_ROW_TILE = 1024


def _s5_norm_kernel(x_ref, g_ref, o_ref, scr_ref, *, nloc):
    h = _rms(x_ref[0], g_ref[...])
    nslab = scr_ref.shape[0]
    for c in range(nslab):
        scr_ref[c] = h[:, c * LANES:(c + 1) * LANES]
    for s in range(S5_CHUNK):
        rows = pl.ds(s, nloc, stride=S5_CHUNK)
        o_ref[0, s] = jnp.concatenate([scr_ref[c, rows, :] for c in range(nslab)], axis=1).astype(o_ref.dtype)


def _s5_norm(x, gain):
    bsz, seqlen, d = x.shape
    nch = seqlen // S5_CHUNK
    tm = min(S5_ROW_TILE, seqlen)
    nloc = tm // S5_CHUNK
    return pl.pallas_call(
        functools.partial(_s5_norm_kernel, nloc=nloc),
        out_shape=jax.ShapeDtypeStruct((bsz, S5_CHUNK, nch, d), BF16),
        grid=(bsz, seqlen // tm),
        in_specs=[pl.BlockSpec((1, tm, d), lambda b, i: (b, i, 0)),
                  pl.BlockSpec((1, d), lambda b, i: (0, 0))],
        out_specs=pl.BlockSpec((1, S5_CHUNK, nloc, d), lambda b, i: (b, 0, i, 0)),
        scratch_shapes=[pltpu.VMEM((d // LANES, tm, LANES), F32)],
        compiler_params=_params(("parallel", "parallel"), 32),
        name="s5_norm",
    )(x, gain.reshape(1, d))


def _tiling_matrix(rows, cols):
    p = lax.broadcasted_iota(jnp.int32, (rows, cols), 0)
    c = lax.broadcasted_iota(jnp.int32, (rows, cols), 1)
    return jnp.where(c % rows == p, 1.0, 0.0).astype(BF16)


def _same_group(shape, row_group, col_group):
    r = lax.broadcasted_iota(jnp.int32, shape, 0)
    c = lax.broadcasted_iota(jnp.int32, shape, 1)
    return (r // row_group) == (c // col_group)


def _s5_build_operators(vw_ref, mw_ref, toep_ref, win_ref, wout_ref):
    tn = (((0,), (0,)), ((), ()))
    nstate = vw_ref.shape[-1]
    half = S5_SLAB_GROUPS * nstate
    rep_ch = _tiling_matrix(S5_GROUP, LANES)
    rep_st = _tiling_matrix(nstate, half)
    diag_in = _same_group((LANES, half), S5_GROUP, nstate)
    diag_out = _same_group((half, LANES), nstate, S5_GROUP)

    def out_block(q, r):
        e = lax.dot_general(mw_ref[0, 2 * q + r].astype(BF16), rep_ch, tn, preferred_element_type=F32)
        return jnp.where(diag_out, e, 0.0).astype(BF16)

    for a in range(S5_CHUNK):
        for r in range(2):
            e = _bdot(vw_ref[0, 2 * a + r].astype(BF16), rep_st)
            win_ref[a * LANES:(a + 1) * LANES, r * half:(r + 1) * half] = jnp.where(diag_in, e, 0.0).astype(BF16)
            wout_ref[r * half:(r + 1) * half, a * LANES:(a + 1) * LANES] = out_block(a + 1, r)
    b_bar = win_ref[(S5_CHUNK - 1) * LANES:S5_CHUNK * LANES, :]
    taps = [_bdot(b_bar, jnp.concatenate([out_block(0, 0), out_block(0, 1)], axis=0)).astype(BF16)]
    for j in range(1, S5_CHUNK):
        taps.append(_bdot(b_bar, wout_ref[:, (j - 1) * LANES:j * LANES]).astype(BF16))
    zero = jnp.zeros((LANES, LANES), BF16)
    for a in range(S5_CHUNK):
        for b in range(S5_CHUNK):
            toep_ref[a * LANES:(a + 1) * LANES, b * LANES:(b + 1) * LANES] = taps[b - a] if b >= a else zero


def _s5_conv_kernel(h_ref, vw_ref, mw_ref, a_ref, d_ref, o_ref, s_ref, toep_ref, win_ref, wout_ref,
                    *, nseq, nch):
    pitch = nch + S5_PITCH_PAD
    nl = a_ref.shape[1] // 2

    @pl.when(pl.program_id(1) == 0)
    def _():
        _s5_build_operators(vw_ref, mw_ref, toep_ref, win_ref, wout_ref)

    lhs = jnp.concatenate(
        [jnp.concatenate([h_ref[bl, s] for s in range(S5_CHUNK)], axis=1) for bl in range(nseq)], axis=0)
    bc = _bdot(lhs, win_ref[...])
    for bl in range(nseq):
        for j in range(2 * nl):
            s_ref[j, bl * pitch:bl * pitch + nch, :] = bc[bl * nch:(bl + 1) * nch, j * LANES:(j + 1) * LANES]
    a_re = [a_ref[0, j:j + 1, :] for j in range(nl)]
    a_im = [a_ref[0, nl + j:nl + j + 1, :] for j in range(nl)]

    def step(n, carry):
        p_re, p_im = carry
        rows = pl.ds(n, nseq, stride=pitch)
        n_re, n_im = [], []
        for j in range(nl):
            c_re = s_ref[j, rows, :]
            c_im = s_ref[nl + j, rows, :]
            s_ref[j, rows, :] = p_re[j]
            s_ref[nl + j, rows, :] = p_im[j]
            n_re.append(a_re[j] * p_re[j] - a_im[j] * p_im[j] + c_re)
            n_im.append(a_re[j] * p_im[j] + a_im[j] * p_re[j] + c_im)
        return tuple(n_re), tuple(n_im)

    def steps(m, carry):
        for u in range(S5_SCAN_UNROLL):
            carry = step(m * S5_SCAN_UNROLL + u, carry)
        return carry

    zeros = tuple(jnp.zeros((nseq, LANES), F32) for _ in range(nl))
    lax.fori_loop(0, nch // S5_SCAN_UNROLL, steps, (zeros, zeros))
    x_prev = jnp.concatenate(
        [jnp.concatenate([s_ref[j, bl * pitch:bl * pitch + nch, :] for j in range(2 * nl)], axis=1)
         for bl in range(nseq)], axis=0).astype(BF16)
    band = S5_CHUNK // S5_CAUSAL_BANDS
    y_bands = []
    for q in range(S5_CAUSAL_BANDS):
        kk = (q + 1) * band * LANES
        cols = slice(q * band * LANES, (q + 1) * band * LANES)
        y_bands.append(_bdot(lhs[:, :kk], toep_ref[:kk, cols]) + _bdot(x_prev, wout_ref[:, cols]))
    dskip = d_ref[0]
    for bl in range(nseq):
        for s in range(S5_CHUNK):
            ys = y_bands[s // band][bl * nch:(bl + 1) * nch, (s % band) * LANES:(s % band + 1) * LANES]
            ys = ys + dskip * h_ref[bl, s].astype(F32)
            o_ref[bl, s] = _gelu_tanh(ys).astype(o_ref.dtype)


def _s5_conv(hp, vw, mw, a_pack, d_skip, *, nseq, layer):
    bsz, _, nch, d = hp.shape
    nslab = d // LANES
    kdim = S5_CHUNK * LANES
    sdim = a_pack.shape[1] * LANES
    first = layer * nslab
    blk4 = lambda a: pl.BlockSpec((1,) + a.shape[1:], lambda c, b: (first + c, 0, 0, 0))
    return pl.pallas_call(
        functools.partial(_s5_conv_kernel, nseq=nseq, nch=nch),
        out_shape=jax.ShapeDtypeStruct(hp.shape, BF16),
        grid=(nslab, bsz // nseq),
        in_specs=[pl.BlockSpec((nseq, S5_CHUNK, nch, LANES), lambda c, b: (b, 0, 0, c)),
                  blk4(vw), blk4(mw),
                  pl.BlockSpec((1, sdim // LANES, LANES), lambda c, b: (first + c, 0, 0)),
                  pl.BlockSpec((1, 1, LANES), lambda c, b: (c, 0, 0))],
        out_specs=pl.BlockSpec((nseq, S5_CHUNK, nch, LANES), lambda c, b: (b, 0, 0, c)),
        scratch_shapes=[pltpu.VMEM((sdim // LANES, nseq * (nch + S5_PITCH_PAD), LANES), F32),
                        pltpu.VMEM((kdim, kdim), BF16),
                        pltpu.VMEM((kdim, sdim), BF16),
                        pltpu.VMEM((sdim, kdim), BF16)],
        compiler_params=_params(("arbitrary", "arbitrary"), 56),
        name="s5_conv",
    )(hp, vw, mw, a_pack, d_skip.reshape(nslab, 1, LANES))


def _s5_glu_kernel(y_ref, x_ref, w_ref, b_ref, o_ref, scr_ref, *, nloc):
    nslab = scr_ref.shape[0]
    y = jnp.concatenate([y_ref[0, s] for s in range(S5_CHUNK)], axis=0)
    u = y.astype(F32) * jax.nn.sigmoid(_bdot(y, w_ref[...]) + b_ref[...])
    for s in range(S5_CHUNK):
        rows = pl.ds(s, nloc, stride=S5_CHUNK)
        for c in range(nslab):
            scr_ref[c, rows, :] = u[s * nloc:(s + 1) * nloc, c * LANES:(c + 1) * LANES]
    o_ref[0] = x_ref[0] + jnp.concatenate([scr_ref[c] for c in range(nslab)], axis=1)


def _s5_glu(yp, x, w_glu, b_glu):
    bsz, seqlen, d = x.shape
    tm = min(S5_ROW_TILE, seqlen)
    nloc = tm // S5_CHUNK
    return pl.pallas_call(
        functools.partial(_s5_glu_kernel, nloc=nloc),
        out_shape=jax.ShapeDtypeStruct(x.shape, F32),
        grid=(bsz, seqlen // tm),
        in_specs=[pl.BlockSpec((1, S5_CHUNK, nloc, d), lambda b, i: (b, 0, i, 0)),
                  pl.BlockSpec((1, tm, d), lambda b, i: (b, i, 0)),
                  _resident((d, d), lambda b, i: (0, 0)),
                  pl.BlockSpec((1, d), lambda b, i: (0, 0))],
        out_specs=pl.BlockSpec((1, tm, d), lambda b, i: (b, i, 0)),
        scratch_shapes=[pltpu.VMEM((d // LANES, tm, LANES), F32)],
        compiler_params=_params(("parallel", "parallel"), 40),
        name="s5_glu",
    )(yp, x, w_glu.astype(BF16), b_glu.reshape(1, d))


def _s5_operators(lam_re, lam_im, log_dt, b_re, b_im, c_re, c_im):
    ngroups, nstate = lam_re.shape
    gpc = S5_SLAB_GROUPS
    nslab = ngroups // gpc
    dt = jnp.exp(log_dt)[:, None]
    j = jnp.arange(S5_CHUNK + 1, dtype=F32)[:, None, None]
    mag = jnp.exp(j * (lam_re * dt)[None])
    ang = j * (lam_im * dt)[None]
    pw_re, pw_im = mag * jnp.cos(ang), mag * jnp.sin(ang)
    num_re, num_im = pw_re[1] - 1.0, pw_im[1]
    den = lam_re * lam_re + lam_im * lam_im
    f_re = (num_re * lam_re + num_im * lam_im) / den
    f_im = (num_im * lam_re - num_re * lam_im) / den
    bb_re = f_re[..., None] * b_re - f_im[..., None] * b_im
    bb_im = f_re[..., None] * b_im + f_im[..., None] * b_re
    jr = (S5_CHUNK - 1) - jnp.arange(S5_CHUNK, dtype=F32)[:, None, None]
    mag_r = jnp.exp(jr * (lam_re * dt)[None])
    ang_r = jr * (lam_im * dt)[None]
    rev_re, rev_im = mag_r * jnp.cos(ang_r), mag_r * jnp.sin(ang_r)
    slabbed = lambda a: a.reshape(a.shape[0], nslab, gpc, nstate).transpose(1, 0, 2, 3)
    rv_re, rv_im = slabbed(rev_re)[:, :, :, None, :], slabbed(rev_im)[:, :, :, None, :]
    bt_re = bb_re.transpose(0, 2, 1).reshape(nslab, 1, gpc, S5_GROUP, nstate)
    bt_im = bb_im.transpose(0, 2, 1).reshape(nslab, 1, gpc, S5_GROUP, nstate)
    vw = jnp.stack([rv_re * bt_re - rv_im * bt_im, rv_re * bt_im + rv_im * bt_re], axis=2)
    vw = vw.reshape(nslab, 2 * S5_CHUNK, LANES, nstate)
    pc_re, pc_im = slabbed(pw_re)[:, :, None, :, :], slabbed(pw_im)[:, :, None, :, :]
    ct_re = c_re.reshape(nslab, gpc, S5_GROUP, nstate).transpose(0, 2, 1, 3)[:, None]
    ct_im = c_im.reshape(nslab, gpc, S5_GROUP, nstate).transpose(0, 2, 1, 3)[:, None]
    mw = jnp.stack([ct_re * pc_re - ct_im * pc_im, -(ct_re * pc_im + ct_im * pc_re)], axis=2)
    mw = mw.reshape(nslab, 2 * (S5_CHUNK + 1), S5_GROUP, gpc * nstate)
    half = gpc * nstate // LANES
    a_pack = jnp.concatenate([pw_re[S5_CHUNK].reshape(nslab, half, LANES),
                              pw_im[S5_CHUNK].reshape(nslab, half, LANES)], axis=1)
    return vw, mw, a_pack


def _s5_layer(x, ln, operators, layer, d_skip, w_glu, b_glu, *, nseq=4):
    vw, mw, a_pack = operators
    hp = _s5_norm(x, ln)
    yp = _s5_conv(hp, vw, mw, a_pack, d_skip, nseq=min(nseq, x.shape[0]), layer=layer)
    return _s5_glu(yp, x, w_glu, b_glu)


def _dense_ffn_kernel(x_ref, g_ref, wg_ref, wu_ref, wd_ref, o_ref):
    xf = x_ref[...]
    h = _rms(xf, g_ref[...]).astype(BF16)
    act = (_silu(_bdot(h, wg_ref[...])) * _bdot(h, wu_ref[...])).astype(BF16)
    o_ref[...] = xf + _bdot(act, wd_ref[...])


def _dense_ffn_layer(x, ln, w_gate_up, w_down, *, tm=512):
    bsz, seqlen, d = x.shape
    ntok = bsz * seqlen
    hidden = w_down.shape[0]
    tm = min(tm, ntok)
    wgu = w_gate_up.astype(BF16)
    out = pl.pallas_call(
        _dense_ffn_kernel,
        out_shape=jax.ShapeDtypeStruct((ntok, d), F32),
        grid=(ntok // tm,),
        in_specs=[pl.BlockSpec((tm, d), lambda i: (i, 0)),
                  pl.BlockSpec((1, d), lambda i: (0, 0)),
                  _resident((d, hidden), lambda i: (0, 0)),
                  _resident((d, hidden), lambda i: (0, 1)),
                  _resident((hidden, d), lambda i: (0, 0))],
        out_specs=pl.BlockSpec((tm, d), lambda i: (i, 0)),
        compiler_params=_params(("parallel",), 56),
        name="dense_ffn",
    )(x.reshape(ntok, d), ln.reshape(1, d), wgu, wgu, w_down.astype(BF16))
    return out.reshape(bsz, seqlen, d)


def _log_sigmoid(z):
    return jnp.minimum(z, 0.0) - jnp.log(1.0 + jnp.exp(-jnp.abs(z)))


def _gla_kernel(x_ref, ln_ref, wm_ref, wgl_ref, wg2_ref, bg2_ref, gn_ref, wo_ref, o_ref, st_ref,
                *, tq, dk, dv, heads):
    hdk, hdv = dk // heads, dv // heads
    chunk = GLA_CHUNK
    nt = (((1,), (1,)), ((), ()))
    tn = (((0,), (0,)), ((), ()))

    @pl.when(pl.program_id(1) == 0)
    def _():
        st_ref[...] = jnp.zeros_like(st_ref)

    xf = x_ref[0]
    h = _rms(xf, ln_ref[...]).astype(BF16)
    proj = _bdot(h, wm_ref[...])
    glow = _bdot(h, wgl_ref[...]).astype(BF16)
    la = _log_sigmoid(_bdot(glow, wg2_ref[...]) + bg2_ref[...]) * (1.0 / GLA_GATE_NORM)
    row = lax.broadcasted_iota(jnp.int32, (chunk, chunk), 0)
    col = lax.broadcasted_iota(jnp.int32, (chunk, chunk), 1)
    causal = row >= col
    tri = jnp.where(causal, 1.0, 0.0).astype(BF16)
    scale = hdk ** -0.5
    outs = []
    for c in range(tq // chunk):
        r0 = c * chunk
        la_c = la[r0:r0 + chunk, :]
        la_hi = la_c.astype(BF16)
        la_lo = (la_c - la_hi.astype(F32)).astype(BF16)
        gcum_all = _bdot(tri, la_hi) + _bdot(tri, la_lo)
        head_out = []
        for hd in range(heads):
            gcum = gcum_all[:, hd * hdk:(hd + 1) * hdk]
            g_last = gcum[chunk - 1:chunk, :]
            q_c = proj[r0:r0 + chunk, hd * hdk:(hd + 1) * hdk] * scale
            k_c = proj[r0:r0 + chunk, dk + hd * hdk:dk + (hd + 1) * hdk]
            v_c = proj[r0:r0 + chunk, 2 * dk + hd * hdv:2 * dk + (hd + 1) * hdv].astype(BF16)
            q_s = (q_c * jnp.exp(gcum)).astype(BF16)
            k_s = (k_c * jnp.exp(-gcum)).astype(BF16)
            k_end = (k_c * jnp.exp(g_last - gcum)).astype(BF16)
            scores = lax.dot_general(q_s, k_s, nt, preferred_element_type=F32)
            scores = jnp.where(causal, scores, 0.0).astype(BF16)
            state_t = st_ref[hd]
            o = _bdot(scores, v_c) + lax.dot_general(q_s, state_t.astype(BF16), nt,
                                                     preferred_element_type=F32)
            kv_t = lax.dot_general(v_c, k_end, tn, preferred_element_type=F32)
            st_ref[hd] = state_t * jnp.exp(g_last) + kv_t
            head_out.append(o * lax.rsqrt(jnp.mean(o * o, axis=-1, keepdims=True) + EPS))
        outs.append(jnp.concatenate(head_out, axis=1))
    o_all = jnp.concatenate(outs, axis=0)
    r = proj[:, 2 * dk + dv:]
    o_all = (o_all * gn_ref[...] * _silu(r)).astype(BF16)
    o_ref[0] = xf + _bdot(o_all, wo_ref[...])


def _gla_layer(x, ln, w_in, w_g2, b_g2, g_norm, w_out, *, tq=256):
    bsz, seqlen, d = x.shape
    dk = w_g2.shape[1]
    dv = w_out.shape[0]
    nmain = 2 * dk + 2 * dv
    tq = min(tq, seqlen)
    w_main = w_in[:, :nmain].astype(BF16)
    w_glow = jnp.pad(w_in[:, nmain:], ((0, 0), (0, LANES - GLA_GATE_RANK))).astype(BF16)
    w_g2p = jnp.pad(w_g2, ((0, LANES - GLA_GATE_RANK), (0, 0))).astype(BF16)
    hdk, hdv = dk // GLA_HEADS, dv // GLA_HEADS
    const = lambda b, t: (0, 0)
    return pl.pallas_call(
        functools.partial(_gla_kernel, tq=tq, dk=dk, dv=dv, heads=GLA_HEADS),
        out_shape=jax.ShapeDtypeStruct(x.shape, F32),
        grid=(bsz, seqlen // tq),
        in_specs=[pl.BlockSpec((1, tq, d), lambda b, t: (b, t, 0)),
                  pl.BlockSpec((1, d), const),
                  _resident((d, nmain), const),
                  _resident((d, LANES), const),
                  _resident((LANES, dk), const),
                  pl.BlockSpec((1, dk), const),
                  pl.BlockSpec((1, dv), const),
                  _resident((dv, d), const)],
        out_specs=pl.BlockSpec((1, tq, d), lambda b, t: (b, t, 0)),
        scratch_shapes=[pltpu.VMEM((GLA_HEADS, hdv, hdk), F32)],
        compiler_params=_params(("parallel", "arbitrary"), 48),
        name="gla",
    )(x, ln.reshape(1, d), w_main, w_glow, w_g2p, b_g2.reshape(1, dk), g_norm.reshape(1, dv),
      w_out.astype(BF16))


LOG2E = math.log2(math.e)
SWA_SLOT_UNROLL = 4


def _swa_kernel(sink_ref, x_ref, ln_ref, wqkv_ref, bqkv_ref, wo_ref, bo_ref, o_ref, k_ref, v_ref,
                bias_ref, q_ref, a_ref, *, tq, q_heads):
    group = q_heads // SWA_KV_HEADS
    blk = SWA_BLOCK
    nt = (((1,), (1,)), ((), ()))
    b = pl.program_id(0)
    t = pl.program_id(1)
    nq = group * LANES

    @pl.when((b == 0) & (t == 0))
    def _():
        qi = lax.broadcasted_iota(jnp.int32, (blk, 2 * blk), 0)
        kj = lax.broadcasted_iota(jnp.int32, (blk, 2 * blk), 1)
        dist = qi + blk - kj
        in_window = (dist >= 0) & (dist < SWA_WINDOW)
        for hq in range(q_heads):
            slope = 2.0 ** (-8.0 * (hq + 1) / q_heads)
            bias_ref[hq] = jnp.where(in_window, -(slope * LOG2E) * dist.astype(F32), MASK_VALUE)

    @pl.when(t == 0)
    def _():
        k_ref[0:blk, :] = jnp.zeros((blk, LANES), BF16)
        v_ref[0:blk, :] = jnp.zeros((blk, LANES), BF16)

    xf = x_ref[0]
    h = _rms(xf, ln_ref[...]).astype(BF16)
    qkv = _bdot(h, wqkv_ref[...]) + bqkv_ref[...]
    for j in range(group):
        q_ref[j] = qkv[:, j * LANES:(j + 1) * LANES].astype(BF16)
    k_ref[blk:blk + tq, :] = qkv[:, nq:nq + LANES].astype(BF16)
    v_ref[blk:blk + tq, :] = qkv[:, nq + LANES:nq + 2 * LANES].astype(BF16)
    kj_row = lax.broadcasted_iota(jnp.int32, (1, 2 * blk), 1)
    no_prev = jnp.where(kj_row < blk, jnp.where(t == 0, MASK_VALUE, 0.0), 0.0)
    low_half = lax.broadcasted_iota(jnp.int32, (1, LANES), 1) < SWA_HEAD_DIM
    halves = (low_half, jnp.logical_not(low_half))

    def slots(jj, carry):
        for u in range(SWA_SLOT_UNROLL):
            j = jj * SWA_SLOT_UNROLL + u
            for i in range(tq // blk):
                r0 = i * blk
                q_slot = q_ref[j, r0:r0 + blk, :]
                outs = []
                for kh in range(SWA_KV_HEADS):
                    hq = kh * group + j
                    sink = sink_ref[hq] * LOG2E
                    q_h = jnp.where(halves[kh], q_slot, jnp.zeros_like(q_slot))
                    s = lax.dot_general(q_h, k_ref[r0:r0 + 2 * blk, :], nt, preferred_element_type=F32) + bias_ref[hq]
                    if i == 0:
                        s = s + no_prev
                    m = jnp.maximum(jnp.max(s, axis=-1, keepdims=True), sink)
                    p = jnp.exp2(s - m)
                    denom = jnp.sum(p, axis=-1, keepdims=True) + jnp.exp2(sink - m)
                    outs.append(_bdot(p.astype(BF16), v_ref[r0:r0 + 2 * blk, :]) * (1.0 / denom))
                a_ref[j, r0:r0 + blk, :] = jnp.where(low_half, outs[0], outs[1]).astype(BF16)
        return carry

    lax.fori_loop(0, group // SWA_SLOT_UNROLL, slots, 0)
    k_ref[0:blk, :] = k_ref[tq:tq + blk, :]
    v_ref[0:blk, :] = v_ref[tq:tq + blk, :]
    o_all = jnp.concatenate([a_ref[j] for j in range(group)], axis=1)
    o_ref[0] = xf + _bdot(o_all, wo_ref[...]) + bo_ref[...]


def _swa_layer(x, ln, w_qkv, b_qkv, sinks, w_out, b_out, *, tq=512):
    bsz, seqlen, d = x.shape
    hd = SWA_HEAD_DIM
    q_heads = sinks.shape[0]
    group = q_heads // SWA_KV_HEADS
    nq = q_heads * hd
    tq = min(tq, seqlen)
    q_scale = hd ** -0.5 * LOG2E
    wq = (w_qkv[:, :nq] * q_scale).reshape(d, SWA_KV_HEADS, group, hd).transpose(0, 2, 1, 3).reshape(d, nq)
    bq = (b_qkv[:nq] * q_scale).reshape(SWA_KV_HEADS, group, hd).transpose(1, 0, 2).reshape(nq)
    w_all = jnp.concatenate([wq, w_qkv[:, nq:]], axis=1).astype(BF16)
    b_all = jnp.concatenate([bq, b_qkv[nq:]]).reshape(1, -1)
    wo = w_out.reshape(SWA_KV_HEADS, group, hd, d).transpose(1, 0, 2, 3).reshape(nq, d).astype(BF16)
    nall = w_all.shape[1]
    const = lambda b, t, s: (0, 0)
    return pl.pallas_call(
        functools.partial(_swa_kernel, tq=tq, q_heads=q_heads),
        out_shape=jax.ShapeDtypeStruct(x.shape, F32),
        grid_spec=pltpu.PrefetchScalarGridSpec(
            num_scalar_prefetch=1,
            grid=(bsz, seqlen // tq),
            in_specs=[pl.BlockSpec((1, tq, d), lambda b, t, s: (b, t, 0)),
                      pl.BlockSpec((1, d), const),
                      _resident((d, nall), const),
                      pl.BlockSpec((1, nall), const),
                      _resident((nq, d), const),
                      pl.BlockSpec((1, d), const)],
            out_specs=pl.BlockSpec((1, tq, d), lambda b, t, s: (b, t, 0)),
            scratch_shapes=[pltpu.VMEM((SWA_BLOCK + tq, LANES), BF16), pltpu.VMEM((SWA_BLOCK + tq, LANES), BF16),
                            pltpu.VMEM((q_heads, SWA_BLOCK, 2 * SWA_BLOCK), F32),
                            pltpu.VMEM((group, tq, LANES), BF16), pltpu.VMEM((group, tq, LANES), BF16)]),
        compiler_params=_params(("arbitrary", "arbitrary"), 48),
        name="swa",
    )(sinks, x, ln.reshape(1, d), w_all, b_all, wo, b_out.reshape(1, d))


def _router_kernel(x_ref, ln_ref, whi_ref, wlo_ref, idx_ref, gate_ref, hp_ref, pos_ref, count_ref, tri_ref):
    nt = (((1,), (1,)), ((), ()))
    h = _rms(x_ref[...], ln_ref[...])
    h_hi = h.astype(BF16)
    h_lo = (h - h_hi.astype(F32)).astype(BF16)
    w_hi, w_lo = whi_ref[...], wlo_ref[...]
    logits = (lax.dot_general(w_hi, h_hi, nt, preferred_element_type=F32)
              + lax.dot_general(w_hi, h_lo, nt, preferred_element_type=F32)
              + lax.dot_general(w_lo, h_hi, nt, preferred_element_type=F32))
    n_exp = logits.shape[0]
    eid = lax.broadcasted_iota(jnp.int32, logits.shape, 0)
    m1 = jnp.max(logits, axis=0, keepdims=True)
    i1 = jnp.min(jnp.where(logits == m1, eid, n_exp), axis=0, keepdims=True)
    rest = jnp.where(eid == i1, -jnp.inf, logits)
    m2 = jnp.max(rest, axis=0, keepdims=True)
    i2 = jnp.min(jnp.where(rest == m2, eid, n_exp), axis=0, keepdims=True)
    e2 = jnp.exp(m2 - m1)
    g1 = 1.0 / (1.0 + e2)
    idx_ref[...] = jnp.concatenate([i1, i2], axis=0)
    gate_ref[...] = jnp.concatenate([g1, e2 * g1], axis=0)
    hp_ref[...] = _pack_bf16_pairs(h)
    tm = logits.shape[1]

    @pl.when(pl.program_id(0) == 0)
    def _():
        count_ref[...] = jnp.zeros_like(count_ref)
        r = lax.broadcasted_iota(jnp.int32, (tm, tm), 0)
        c = lax.broadcasted_iota(jnp.int32, (tm, tm), 1)
        tri_ref[...] = jnp.where(r < c, 1.0, 0.0).astype(BF16)

    pick1 = jnp.where(eid == i1, 1.0, 0.0)
    pick2 = jnp.where(eid == i2, 1.0, 0.0)
    picks = pick1 + pick2
    before = _bdot(picks.astype(BF16), tri_ref[...]) + count_ref[:, 0:1]
    pos_ref[...] = jnp.concatenate([jnp.sum(pick1 * before, axis=0, keepdims=True),
                                    jnp.sum(pick2 * before, axis=0, keepdims=True)], axis=0).astype(jnp.int32)
    count_ref[...] = count_ref[...] + jnp.sum(picks, axis=1, keepdims=True)


def _router(x2, ln, w_router, *, tm=512):
    ntok, d = x2.shape
    n_exp = w_router.shape[1]
    tm = min(tm, ntok)
    wt = w_router.T
    w_hi = wt.astype(BF16)
    w_lo = (wt - w_hi.astype(F32)).astype(BF16)
    return pl.pallas_call(
        _router_kernel,
        out_shape=(jax.ShapeDtypeStruct((TOP_K, ntok), jnp.int32), jax.ShapeDtypeStruct((TOP_K, ntok), F32),
                   jax.ShapeDtypeStruct((ntok, d // 2), jnp.uint32),
                   jax.ShapeDtypeStruct((TOP_K, ntok), jnp.int32), jax.ShapeDtypeStruct((n_exp, LANES), F32)),
        grid=(ntok // tm,),
        in_specs=[pl.BlockSpec((tm, d), lambda i: (i, 0)),
                  pl.BlockSpec((1, d), lambda i: (0, 0)),
                  pl.BlockSpec((n_exp, d), lambda i: (0, 0)),
                  pl.BlockSpec((n_exp, d), lambda i: (0, 0))],
        out_specs=(pl.BlockSpec((TOP_K, tm), lambda i: (0, i)), pl.BlockSpec((TOP_K, tm), lambda i: (0, i)),
                   pl.BlockSpec((tm, d // 2), lambda i: (i, 0)),
                   pl.BlockSpec((TOP_K, tm), lambda i: (0, i)), pl.BlockSpec((n_exp, LANES), lambda i: (0, 0))),
        scratch_shapes=[pltpu.VMEM((tm, tm), BF16)],
        compiler_params=_params(("arbitrary",), 32),
        name="moe_router",
    )(x2, ln.reshape(1, d), w_hi, w_lo)


def _moe_plan(idx, pos, counts, tile):
    n_exp = counts.shape[0]
    nslots = idx.size
    counts = counts[:, 0].astype(jnp.int32)
    ends = jnp.cumsum(counts)
    offs = ends - counts
    experts = jnp.arange(n_exp, dtype=jnp.int32).reshape(n_exp, 1, 1)
    rank = pos + jnp.sum(jnp.where(idx[None] == experts, offs.reshape(n_exp, 1, 1), 0), axis=0)
    n_tiles = nslots // tile
    n_visits = n_tiles + n_exp - 1
    first_tile = offs // tile
    last_tile = (ends - 1) // tile
    nvis = jnp.where(counts > 0, last_tile - first_tile + 1, 0)
    vend = jnp.cumsum(nvis)
    vstart = vend - nvis
    total = vend[-1]
    v = jnp.arange(n_visits, dtype=jnp.int32)
    vc = jnp.minimum(v, total - 1)
    e = jnp.minimum(jnp.sum((vc[:, None] >= vend[None, :]).astype(jnp.int32), axis=1), n_exp - 1)
    sel = (e[:, None] == jnp.arange(n_exp, dtype=jnp.int32)[None, :]).astype(jnp.int32)
    pick = lambda a: jnp.sum(sel * a[None, :], axis=1)
    tile_id = pick(first_tile) + vc - pick(vstart)
    lo = jnp.maximum(pick(offs), tile_id * tile) - tile_id * tile
    hi = jnp.minimum(pick(ends), (tile_id + 1) * tile) - tile_id * tile
    valid = v < total
    lo = jnp.where(valid, lo, 0)
    hi = jnp.where(valid, hi, 0)
    prev_tile = jnp.concatenate([jnp.full((1,), -1, jnp.int32), tile_id[:-1]])
    first = (valid & (tile_id != prev_tile)).astype(jnp.int32)
    next_tile = jnp.concatenate([tile_id[1:], jnp.full((1,), -1, jnp.int32)])
    last = (valid & ((tile_id != next_tile) | (v == total - 1))).astype(jnp.int32)
    meta = jnp.stack([tile_id, e, lo, hi, first, last]).astype(jnp.int32)
    return rank.astype(jnp.int32), meta


def _pack_bf16_pairs(h):
    half = h.shape[1] // 2
    return _pack_halves(h[:, :half], h[:, half:])


def _pack_halves(lo, hi):
    bits = lambda a: lax.bitcast_convert_type(a.astype(BF16).astype(F32), jnp.uint32)
    return (bits(hi) & jnp.uint32(0xFFFF0000)) | (bits(lo) >> 16)


def _unpack_pairs_f32(u):
    lo = lax.bitcast_convert_type(u << 16, F32)
    hi = lax.bitcast_convert_type(u & jnp.uint32(0xFFFF0000), F32)
    return jnp.concatenate([lo, hi], axis=1)


def _unpack_bf16_pairs(u):
    return _unpack_pairs_f32(u).astype(BF16)


SC_CORES = 2
SC_SUBCORES = 16
SC_INDEX_WINDOW = 128


def _sc_mesh():
    return plsc.VectorSubcoreMesh(core_axis_name="c", subcore_axis_name="s")


def _sc_worker_id():
    return lax.axis_index("c") * SC_SUBCORES + lax.axis_index("s")


def _sc_scatter_rows(src, rank, nrows):
    ntok, width = src.shape
    win = SC_INDEX_WINDOW
    per = ntok // (SC_CORES * SC_SUBCORES)

    @pl.kernel(out_type=jax.ShapeDtypeStruct((nrows, width), src.dtype), mesh=_sc_mesh(),
               scratch_types=[pltpu.VMEM((1, win), jnp.int32)] * TOP_K + [pltpu.VMEM((win, width), src.dtype)],
               name="moe_dispatch_sc")
    def scatter(src_hbm, rank_hbm, o_hbm, *scratch):
        idx_vmem, buf = scratch[:TOP_K], scratch[TOP_K]
        wid = _sc_worker_id()

        @pl.loop(0, per // win)
        def _(blk):
            base = wid * per + blk * win
            for k in range(TOP_K):
                pltpu.sync_copy(rank_hbm.at[pl.ds(k, 1), pl.ds(base, win)], idx_vmem[k])
            pltpu.sync_copy(src_hbm.at[pl.ds(base, win)], buf)
            for k in range(TOP_K):
                pltpu.sync_copy(buf, o_hbm.at[idx_vmem[k].at[0]])

    return scatter(src, rank)


def _sc_gather_rows(src, rank, *, sub=32):
    nslot, ntok = rank.shape
    n = nslot * ntok
    width = src.shape[1]
    win = SC_INDEX_WINDOW
    per = n // (SC_CORES * SC_SUBCORES)
    nsub = win // sub

    @pl.kernel(out_type=jax.ShapeDtypeStruct((n, width), src.dtype), mesh=_sc_mesh(),
               scratch_types=[pltpu.VMEM((1, win), jnp.int32)] + [pltpu.VMEM((sub, width), src.dtype)] * 2
               + [pltpu.SemaphoreType.DMA] * 4,
               name="moe_gather_sc")
    def gather(src_hbm, idx_hbm, o_hbm, i_vmem, buf0, buf1, g0, g1, w0, w1):
        bufs, gsem, wsem = (buf0, buf1), (g0, g1), (w0, w1)
        wid = _sc_worker_id()

        @pl.loop(0, per // win)
        def _(blk):
            base = wid * per + blk * win
            pltpu.sync_copy(idx_hbm.at[pl.ds(base // ntok, 1), pl.ds(base % ntok, win)], i_vmem)
            gathers = [pltpu.make_async_copy(src_hbm.at[i_vmem.at[0, pl.ds(sub * j, sub)]], bufs[j % 2], gsem[j % 2])
                       for j in range(nsub)]
            writes = [pltpu.make_async_copy(bufs[j % 2], o_hbm.at[pl.ds(base + sub * j, sub)], wsem[j % 2])
                      for j in range(nsub)]
            gathers[0].start()
            for j in range(nsub):
                if j + 1 < nsub:
                    if j >= 1:
                        writes[j - 1].wait()
                    gathers[j + 1].start()
                gathers[j].wait()
                writes[j].start()
            writes[nsub - 2].wait()
            writes[nsub - 1].wait()

    return gather(src, rank)


MXU_N = 256


def _expert_kernel(meta_ref, x_ref, wg_ref, wu_ref, wd_ref, o_ref, acc_ref, xb_ref, act_ref, wgb_ref, wub_ref,
                   wdb_ref, *, ts):
    v = pl.program_id(0)
    hc = pl.program_id(1)
    lo, hi, first, last = meta_ref[2, v], meta_ref[3, v], meta_ref[4, v], meta_ref[5, v]
    tile, d = acc_ref.shape
    nsub = tile // ts
    th = wgb_ref.shape[1]
    wide = (hi - lo) * 2 > tile

    @pl.when(hc == 0)
    def _():
        for sub in range(nsub):
            xb_ref[sub * ts:(sub + 1) * ts, :] = _unpack_bf16_pairs(x_ref[sub * ts:(sub + 1) * ts, :])

    fresh = (first == 1) & (hc == 0)

    @pl.when(fresh & jnp.logical_not(wide))
    def _():
        acc_ref[...] = jnp.zeros_like(acc_ref)

    finishing = (last == 1) & (hc == pl.num_programs(1) - 1)
    half_chunks = d // 2 // MXU_N

    def whole_tile(assign, finish):
        rows = lax.broadcasted_iota(jnp.int32, (tile, 1), 0)
        mine = (rows >= lo) & (rows < hi)
        for n in range(th // MXU_N):
            cols = slice(n * MXU_N, (n + 1) * MXU_N)
            gate = _bdot(xb_ref[...], wg_ref[0, :, cols].astype(BF16))
            up = _bdot(xb_ref[...], wu_ref[0, :, cols].astype(BF16))
            act_ref[:, cols] = (_silu(gate) * up).astype(BF16)
        for n in range(d // MXU_N):
            cols = slice(n * MXU_N, (n + 1) * MXU_N)
            part = jnp.where(mine, _bdot(act_ref[...], wd_ref[0, :, cols].astype(BF16)), 0.0)
            total = part if assign else acc_ref[:, cols] + part
            if finish and n >= half_chunks:
                low = slice((n - half_chunks) * MXU_N, (n - half_chunks + 1) * MXU_N)
                o_ref[:, low] = _pack_halves(acc_ref[:, low], total)
            else:
                acc_ref[:, cols] = total

    pl.when(wide & fresh)(functools.partial(whole_tile, True, False))
    pl.when(wide & jnp.logical_not(fresh) & jnp.logical_not(finishing))(functools.partial(whole_tile, False, False))
    pl.when(wide & jnp.logical_not(fresh) & finishing)(functools.partial(whole_tile, False, True))

    @pl.when(jnp.logical_not(wide) & (hi > lo))
    def _():
        wgb_ref[...] = wg_ref[0].astype(BF16)
        wub_ref[...] = wu_ref[0].astype(BF16)
        wdb_ref[...] = wd_ref[0].astype(BF16)
        for sub in range(nsub):
            r0 = sub * ts

            @pl.when((lo < r0 + ts) & (hi > r0))
            def _():
                xs = xb_ref[r0:r0 + ts, :]
                act = (_silu(_bdot(xs, wgb_ref[...])) * _bdot(xs, wub_ref[...])).astype(BF16)
                y = _bdot(act, wdb_ref[...])
                rows = r0 + lax.broadcasted_iota(jnp.int32, (ts, 1), 0)
                acc_ref[r0:r0 + ts, :] += jnp.where((rows >= lo) & (rows < hi), y, 0.0)

    @pl.when(finishing & jnp.logical_not(wide))
    def _():
        for sub in range(nsub):
            o_ref[sub * ts:(sub + 1) * ts, :] = _pack_bf16_pairs(acc_ref[sub * ts:(sub + 1) * ts, :])


def _experts(xg, meta, w_gate_up, w_down, *, tile, th=512, ts=256):
    nrows = xg.shape[0]
    n_exp, hidden, d = w_down.shape
    n_hc = hidden // th
    ts = min(ts, tile)
    wgu = w_gate_up
    return pl.pallas_call(
        functools.partial(_expert_kernel, ts=ts),
        out_shape=jax.ShapeDtypeStruct((nrows, d // 2), jnp.uint32),
        grid_spec=pltpu.PrefetchScalarGridSpec(
            num_scalar_prefetch=1,
            grid=(meta.shape[1], n_hc),
            in_specs=[pl.BlockSpec((tile, d // 2), lambda v, c, m: (m[0, v], 0)),
                      pl.BlockSpec((1, d, th), lambda v, c, m: (m[1, v], 0, c)),
                      pl.BlockSpec((1, d, th), lambda v, c, m: (m[1, v], 0, c + n_hc)),
                      pl.BlockSpec((1, th, d), lambda v, c, m: (m[1, v], c, 0))],
            out_specs=pl.BlockSpec((tile, d // 2), lambda v, c, m: (m[0, v], 0)),
            scratch_shapes=[pltpu.VMEM((tile, d), F32), pltpu.VMEM((tile, d), BF16), pltpu.VMEM((tile, th), BF16),
                            pltpu.VMEM((d, th), BF16), pltpu.VMEM((d, th), BF16), pltpu.VMEM((th, d), BF16)]),
        compiler_params=_params(("arbitrary", "arbitrary"), 56),
        name="moe_experts",
    )(meta, xg, wgu, wgu, w_down)


def _combine_kernel(x_ref, gate_ref, fg_ref, y0_ref, y1_ref, o_ref, *, final_norm):
    g = gate_ref[...]
    out = x_ref[...] + g[:, 0:1] * _unpack_pairs_f32(y0_ref[0]) + g[:, 1:2] * _unpack_pairs_f32(y1_ref[0])
    if final_norm:
        out = _rms(out, fg_ref[...])
    o_ref[...] = out


def _combine(x2, gates_t, yk, final_gain, *, tm=512):
    ntok, d = x2.shape
    tm = min(tm, ntok)
    final_norm = final_gain is not None
    fg = (final_gain if final_norm else jnp.ones((d,), F32)).reshape(1, d)
    return pl.pallas_call(
        functools.partial(_combine_kernel, final_norm=final_norm),
        out_shape=jax.ShapeDtypeStruct((ntok, d), F32),
        grid=(ntok // tm,),
        in_specs=[pl.BlockSpec((tm, d), lambda i: (i, 0)),
                  pl.BlockSpec((tm, TOP_K), lambda i: (i, 0)),
                  pl.BlockSpec((1, d), lambda i: (0, 0)),
                  pl.BlockSpec((1, tm, d // 2), lambda i: (0, i, 0)),
                  pl.BlockSpec((1, tm, d // 2), lambda i: (1, i, 0))],
        out_specs=pl.BlockSpec((tm, d), lambda i: (i, 0)),
        compiler_params=_params(("parallel",), 40),
        name="moe_combine",
    )(x2, gates_t, fg, yk, yk)


def _moe_routed(x, ln, w_router, w_gate_up, w_down, *, tile=2048):
    bsz, seqlen, d = x.shape
    ntok = bsz * seqlen
    tile = min(tile, TOP_K * ntok)
    x2 = x.reshape(ntok, d)
    idx, gates, hp, pos, counts = _router(x2, ln, w_router)
    rank, meta = _moe_plan(idx, pos, counts, tile)
    xg = _sc_scatter_rows(hp, rank, TOP_K * ntok)
    y = _experts(xg, meta, w_gate_up, w_down, tile=tile)
    yk = _sc_gather_rows(y, rank, sub=64).reshape(TOP_K, ntok, d // 2)
    return gates.T, yk


def _moe_layer(x, ln, w_router, w_gate_up, w_down, *, final_gain=None, tile=2048):
    bsz, seqlen, d = x.shape
    gates_t, yk = _moe_routed(x, ln, w_router, w_gate_up, w_down, tile=tile)
    out = _combine(x.reshape(bsz * seqlen, d), gates_t, yk, final_gain)
    return out.reshape(bsz, seqlen, d)


def kernel(x, l0_ln1, l0_s5_lam_re, l0_s5_lam_im, l0_s5_log_dt, l0_s5_b_re, l0_s5_b_im, l0_s5_c_re, l0_s5_c_im, l0_s5_d, l0_s5_w_glu, l0_s5_b_glu, l0_ln2, l0_ffn_w_gate_up, l0_ffn_w_down, l1_ln1, l1_gla_w_in, l1_gla_w_g2, l1_gla_b_g2, l1_gla_norm, l1_gla_w_out, l1_ln2, l1_moe_router, l1_moe_w_gate_up, l1_moe_w_down, l2_ln1, l2_swa_w_qkv, l2_swa_b_qkv, l2_swa_sinks, l2_swa_w_out, l2_swa_b_out, l2_ln2, l2_ffn_w_gate_up, l2_ffn_w_down, l3_ln1, l3_s5_lam_re, l3_s5_lam_im, l3_s5_log_dt, l3_s5_b_re, l3_s5_b_im, l3_s5_c_re, l3_s5_c_im, l3_s5_d, l3_s5_w_glu, l3_s5_b_glu, l3_ln2, l3_moe_router, l3_moe_w_gate_up, l3_moe_w_down, ln_f):
    s5_params = ((l0_s5_lam_re, l0_s5_lam_im, l0_s5_log_dt, l0_s5_b_re, l0_s5_b_im, l0_s5_c_re, l0_s5_c_im),
                 (l3_s5_lam_re, l3_s5_lam_im, l3_s5_log_dt, l3_s5_b_re, l3_s5_b_im, l3_s5_c_re, l3_s5_c_im))
    s5_ops = jax.vmap(_s5_operators)(*(jnp.stack(pair) for pair in zip(*s5_params)))
    s5_ops = tuple(a.reshape((-1,) + a.shape[2:]) for a in s5_ops)
    x = _s5_layer(x, l0_ln1, s5_ops, 0, l0_s5_d, l0_s5_w_glu, l0_s5_b_glu)
    x = _dense_ffn_layer(x, l0_ln2, l0_ffn_w_gate_up, l0_ffn_w_down)
    x = _gla_layer(x, l1_ln1, l1_gla_w_in, l1_gla_w_g2, l1_gla_b_g2, l1_gla_norm, l1_gla_w_out)
    x = _moe_layer(x, l1_ln2, l1_moe_router, l1_moe_w_gate_up, l1_moe_w_down)
    x = _swa_layer(x, l2_ln1, l2_swa_w_qkv, l2_swa_b_qkv, l2_swa_sinks, l2_swa_w_out, l2_swa_b_out)
    x = _dense_ffn_layer(x, l2_ln2, l2_ffn_w_gate_up, l2_ffn_w_down)
    x = _s5_layer(x, l3_ln1, s5_ops, 1, l3_s5_d, l3_s5_w_glu, l3_s5_b_glu)
    return _moe_layer(x, l3_ln2, l3_moe_router, l3_moe_w_gate_up, l3_moe_w_down, final_gain=ln_f)
```

```python
import functools
import math

import jax
import jax.numpy as jnp
from jax import lax
from jax.experimental import pallas as pl
from jax.experimental.pallas import tpu as pltpu
from jax.experimental.pallas import tpu_sc as plsc

F32 = jnp.float32
BF16 = jnp.bfloat16
EPS = 1e-6
LANES = 128
MIB = 1 << 20

S5_GROUP = 16
S5_CHUNK = 16
S5_SLAB_GROUPS = LANES // S5_GROUP
S5_PITCH_PAD = 8
S5_SCAN_UNROLL = 8
S5_CAUSAL_BANDS = 8

GLA_HEADS = 4
GLA_GATE_RANK = 16
GLA_GATE_NORM = 16.0
GLA_CHUNK = 64

SWA_HEAD_DIM = 64
SWA_KV_HEADS = 2
SWA_WINDOW = 128
SWA_BLOCK = 128
MASK_VALUE = -1e30

TOP_K = 2


def _params(semantics, vmem_mib):
    return pltpu.CompilerParams(dimension_semantics=semantics, vmem_limit_bytes=vmem_mib * MIB)


def _resident(block_shape, index_map):
    return pl.BlockSpec(block_shape, index_map, pipeline_mode=pl.Buffered(1))


def _rms(xf, gain):
    return xf * lax.rsqrt(jnp.mean(xf * xf, axis=-1, keepdims=True) + EPS) * gain


def _gelu_tanh(x):
    return 0.5 * x * (1.0 + jnp.tanh(math.sqrt(2.0 / math.pi) * (x + 0.044715 * (x * x * x))))


def _silu(x):
    return x * jax.nn.sigmoid(x)


def _bdot(a, b):
    return jnp.dot(a, b, preferred_element_type=F32)


S5_ROW_TILE = 1024


def _s5_norm_kernel(x_ref, g_ref, o_ref, scr_ref, *, nloc):
    h = _rms(x_ref[0], g_ref[...])
    nslab = scr_ref.shape[0]
    for c in range(nslab):
        scr_ref[c] = h[:, c * LANES:(c + 1) * LANES]
    for s in range(S5_CHUNK):
        rows = pl.ds(s, nloc, stride=S5_CHUNK)
        o_ref[0, s] = jnp.concatenate([scr_ref[c, rows, :] for c in range(nslab)], axis=1).astype(o_ref.dtype)


def _s5_norm(x, gain):
    bsz, seqlen, d = x.shape
    nch = seqlen // S5_CHUNK
    tm = min(S5_ROW_TILE, seqlen)
    nloc = tm // S5_CHUNK
    return pl.pallas_call(
        functools.partial(_s5_norm_kernel, nloc=nloc),
        out_shape=jax.ShapeDtypeStruct((bsz, S5_CHUNK, nch, d), BF16),
        grid=(bsz, seqlen // tm),
        in_specs=[pl.BlockSpec((1, tm, d), lambda b, i: (b, i, 0)),
                  pl.BlockSpec((1, d), lambda b, i: (0, 0))],
        out_specs=pl.BlockSpec((1, S5_CHUNK, nloc, d), lambda b, i: (b, 0, i, 0)),
        scratch_shapes=[pltpu.VMEM((d // LANES, tm, LANES), F32)],
        compiler_params=_params(("parallel", "parallel"), 32),
        name="s5_norm",
    )(x, gain.reshape(1, d))


def _tiling_matrix(rows, cols):
    p = lax.broadcasted_iota(jnp.int32, (rows, cols), 0)
    c = lax.broadcasted_iota(jnp.int32, (rows, cols), 1)
    return jnp.where(c % rows == p, 1.0, 0.0).astype(BF16)


def _same_group(shape, row_group, col_group):
    r = lax.broadcasted_iota(jnp.int32, shape, 0)
    c = lax.broadcasted_iota(jnp.int32, shape, 1)
    return (r // row_group) == (c // col_group)


def _s5_build_operators(vw_ref, mw_ref, toep_ref, win_ref, wout_ref):
    tn = (((0,), (0,)), ((), ()))
    nstate = vw_ref.shape[-1]
    half = S5_SLAB_GROUPS * nstate
    rep_ch = _tiling_matrix(S5_GROUP, LANES)
    rep_st = _tiling_matrix(nstate, half)
    diag_in = _same_group((LANES, half), S5_GROUP, nstate)
    diag_out = _same_group((half, LANES), nstate, S5_GROUP)

    def out_block(q, r):
        e = lax.dot_general(mw_ref[0, 2 * q + r].astype(BF16), rep_ch, tn, preferred_element_type=F32)
        return jnp.where(diag_out, e, 0.0).astype(BF16)

    for a in range(S5_CHUNK):
        for r in range(2):
            e = _bdot(vw_ref[0, 2 * a + r].astype(BF16), rep_st)
            win_ref[a * LANES:(a + 1) * LANES, r * half:(r + 1) * half] = jnp.where(diag_in, e, 0.0).astype(BF16)
            wout_ref[r * half:(r + 1) * half, a * LANES:(a + 1) * LANES] = out_block(a + 1, r)
    b_bar = win_ref[(S5_CHUNK - 1) * LANES:S5_CHUNK * LANES, :]
    taps = [_bdot(b_bar, jnp.concatenate([out_block(0, 0), out_block(0, 1)], axis=0)).astype(BF16)]
    for j in range(1, S5_CHUNK):
        taps.append(_bdot(b_bar, wout_ref[:, (j - 1) * LANES:j * LANES]).astype(BF16))
    zero = jnp.zeros((LANES, LANES), BF16)
    for a in range(S5_CHUNK):
        for b in range(S5_CHUNK):
            toep_ref[a * LANES:(a + 1) * LANES, b * LANES:(b + 1) * LANES] = taps[b - a] if b >= a else zero


def _s5_conv_kernel(h_ref, vw_ref, mw_ref, a_ref, d_ref, o_ref, s_ref, toep_ref, win_ref, wout_ref,
                    *, nseq, nch):
    pitch = nch + S5_PITCH_PAD
    nl = a_ref.shape[1] // 2

    @pl.when(pl.program_id(1) == 0)
    def _():
        _s5_build_operators(vw_ref, mw_ref, toep_ref, win_ref, wout_ref)

    lhs = jnp.concatenate(
        [jnp.concatenate([h_ref[bl, s] for s in range(S5_CHUNK)], axis=1) for bl in range(nseq)], axis=0)
    bc = _bdot(lhs, win_ref[...])
    for bl in range(nseq):
        for j in range(2 * nl):
            s_ref[j, bl * pitch:bl * pitch + nch, :] = bc[bl * nch:(bl + 1) * nch, j * LANES:(j + 1) * LANES]
    a_re = [a_ref[0, j:j + 1, :] for j in range(nl)]
    a_im = [a_ref[0, nl + j:nl + j + 1, :] for j in range(nl)]

    def step(n, carry):
        p_re, p_im = carry
        rows = pl.ds(n, nseq, stride=pitch)
        n_re, n_im = [], []
        for j in range(nl):
            c_re = s_ref[j, rows, :]
            c_im = s_ref[nl + j, rows, :]
            s_ref[j, rows, :] = p_re[j]
            s_ref[nl + j, rows, :] = p_im[j]
            n_re.append(a_re[j] * p_re[j] - a_im[j] * p_im[j] + c_re)
            n_im.append(a_re[j] * p_im[j] + a_im[j] * p_re[j] + c_im)
        return tuple(n_re), tuple(n_im)

    def steps(m, carry):
        for u in range(S5_SCAN_UNROLL):
            carry = step(m * S5_SCAN_UNROLL + u, carry)
        return carry

    zeros = tuple(jnp.zeros((nseq, LANES), F32) for _ in range(nl))
    lax.fori_loop(0, nch // S5_SCAN_UNROLL, steps, (zeros, zeros))
    x_prev = jnp.concatenate(
        [jnp.concatenate([s_ref[j, bl * pitch:bl * pitch + nch, :] for j in range(2 * nl)], axis=1)
         for bl in range(nseq)], axis=0).astype(BF16)
    band = S5_CHUNK // S5_CAUSAL_BANDS
    y_bands = []
    for q in range(S5_CAUSAL_BANDS):
        kk = (q + 1) * band * LANES
        cols = slice(q * band * LANES, (q + 1) * band * LANES)
        y_bands.append(_bdot(lhs[:, :kk], toep_ref[:kk, cols]) + _bdot(x_prev, wout_ref[:, cols]))
    dskip = d_ref[0]
    for bl in range(nseq):
        for s in range(S5_CHUNK):
            ys = y_bands[s // band][bl * nch:(bl + 1) * nch, (s % band) * LANES:(s % band + 1) * LANES]
            ys = ys + dskip * h_ref[bl, s].astype(F32)
            o_ref[bl, s] = _gelu_tanh(ys).astype(o_ref.dtype)


def _s5_conv(hp, vw, mw, a_pack, d_skip, *, nseq, layer):
    bsz, _, nch, d = hp.shape
    nslab = d // LANES
    kdim = S5_CHUNK * LANES
    sdim = a_pack.shape[1] * LANES
    first = layer * nslab
    blk4 = lambda a: pl.BlockSpec((1,) + a.shape[1:], lambda c, b: (first + c, 0, 0, 0))
    return pl.pallas_call(
        functools.partial(_s5_conv_kernel, nseq=nseq, nch=nch),
        out_shape=jax.ShapeDtypeStruct(hp.shape, BF16),
        grid=(nslab, bsz // nseq),
        in_specs=[pl.BlockSpec((nseq, S5_CHUNK, nch, LANES), lambda c, b: (b, 0, 0, c)),
                  blk4(vw), blk4(mw),
                  pl.BlockSpec((1, sdim // LANES, LANES), lambda c, b: (first + c, 0, 0)),
                  pl.BlockSpec((1, 1, LANES), lambda c, b: (c, 0, 0))],
        out_specs=pl.BlockSpec((nseq, S5_CHUNK, nch, LANES), lambda c, b: (b, 0, 0, c)),
        scratch_shapes=[pltpu.VMEM((sdim // LANES, nseq * (nch + S5_PITCH_PAD), LANES), F32),
                        pltpu.VMEM((kdim, kdim), BF16),
                        pltpu.VMEM((kdim, sdim), BF16),
                        pltpu.VMEM((sdim, kdim), BF16)],
        compiler_params=_params(("arbitrary", "arbitrary"), 56),
        name="s5_conv",
    )(hp, vw, mw, a_pack, d_skip.reshape(nslab, 1, LANES))


def _s5_glu_kernel(y_ref, x_ref, w_ref, b_ref, o_ref, scr_ref, *, nloc):
    nslab = scr_ref.shape[0]
    y = jnp.concatenate([y_ref[0, s] for s in range(S5_CHUNK)], axis=0)
    u = y.astype(F32) * jax.nn.sigmoid(_bdot(y, w_ref[...]) + b_ref[...])
    for s in range(S5_CHUNK):
        rows = pl.ds(s, nloc, stride=S5_CHUNK)
        for c in range(nslab):
            scr_ref[c, rows, :] = u[s * nloc:(s + 1) * nloc, c * LANES:(c + 1) * LANES]
    o_ref[0] = x_ref[0] + jnp.concatenate([scr_ref[c] for c in range(nslab)], axis=1)


def _s5_glu(yp, x, w_glu, b_glu):
    bsz, seqlen, d = x.shape
    tm = min(S5_ROW_TILE, seqlen)
    nloc = tm // S5_CHUNK
    return pl.pallas_call(
        functools.partial(_s5_glu_kernel, nloc=nloc),
        out_shape=jax.ShapeDtypeStruct(x.shape, F32),
        grid=(bsz, seqlen // tm),
        in_specs=[pl.BlockSpec((1, S5_CHUNK, nloc, d), lambda b, i: (b, 0, i, 0)),
                  pl.BlockSpec((1, tm, d), lambda b, i: (b, i, 0)),
                  _resident((d, d), lambda b, i: (0, 0)),
                  pl.BlockSpec((1, d), lambda b, i: (0, 0))],
        out_specs=pl.BlockSpec((1, tm, d), lambda b, i: (b, i, 0)),
        scratch_shapes=[pltpu.VMEM((d // LANES, tm, LANES), F32)],
        compiler_params=_params(("parallel", "parallel"), 40),
        name="s5_glu",
    )(yp, x, w_glu.astype(BF16), b_glu.reshape(1, d))


def _s5_operators(lam_re, lam_im, log_dt, b_re, b_im, c_re, c_im):
    ngroups, nstate = lam_re.shape
    gpc = S5_SLAB_GROUPS
    nslab = ngroups // gpc
    dt = jnp.exp(log_dt)[:, None]
    j = jnp.arange(S5_CHUNK + 1, dtype=F32)[:, None, None]
    mag = jnp.exp(j * (lam_re * dt)[None])
    ang = j * (lam_im * dt)[None]
    pw_re, pw_im = mag * jnp.cos(ang), mag * jnp.sin(ang)
    num_re, num_im = pw_re[1] - 1.0, pw_im[1]
    den = lam_re * lam_re + lam_im * lam_im
    f_re = (num_re * lam_re + num_im * lam_im) / den
    f_im = (num_im * lam_re - num_re * lam_im) / den
    bb_re = f_re[..., None] * b_re - f_im[..., None] * b_im
    bb_im = f_re[..., None] * b_im + f_im[..., None] * b_re
    jr = (S5_CHUNK - 1) - jnp.arange(S5_CHUNK, dtype=F32)[:, None, None]
    mag_r = jnp.exp(jr * (lam_re * dt)[None])
    ang_r = jr * (lam_im * dt)[None]
    rev_re, rev_im = mag_r * jnp.cos(ang_r), mag_r * jnp.sin(ang_r)
    slabbed = lambda a: a.reshape(a.shape[0], nslab, gpc, nstate).transpose(1, 0, 2, 3)
    rv_re, rv_im = slabbed(rev_re)[:, :, :, None, :], slabbed(rev_im)[:, :, :, None, :]
    bt_re = bb_re.transpose(0, 2, 1).reshape(nslab, 1, gpc, S5_GROUP, nstate)
    bt_im = bb_im.transpose(0, 2, 1).reshape(nslab, 1, gpc, S5_GROUP, nstate)
    vw = jnp.stack([rv_re * bt_re - rv_im * bt_im, rv_re * bt_im + rv_im * bt_re], axis=2)
    vw = vw.reshape(nslab, 2 * S5_CHUNK, LANES, nstate)
    pc_re, pc_im = slabbed(pw_re)[:, :, None, :, :], slabbed(pw_im)[:, :, None, :, :]
    ct_re = c_re.reshape(nslab, gpc, S5_GROUP, nstate).transpose(0, 2, 1, 3)[:, None]
    ct_im = c_im.reshape(nslab, gpc, S5_GROUP, nstate).transpose(0, 2, 1, 3)[:, None]
    mw = jnp.stack([ct_re * pc_re - ct_im * pc_im, -(ct_re * pc_im + ct_im * pc_re)], axis=2)
    mw = mw.reshape(nslab, 2 * (S5_CHUNK + 1), S5_GROUP, gpc * nstate)
    half = gpc * nstate // LANES
    a_pack = jnp.concatenate([pw_re[S5_CHUNK].reshape(nslab, half, LANES),
                              pw_im[S5_CHUNK].reshape(nslab, half, LANES)], axis=1)
    return vw, mw, a_pack


def _s5_layer(x, ln, operators, layer, d_skip, w_glu, b_glu, *, nseq=4):
    vw, mw, a_pack = operators
    hp = _s5_norm(x, ln)
    yp = _s5_conv(hp, vw, mw, a_pack, d_skip, nseq=min(nseq, x.shape[0]), layer=layer)
    return _s5_glu(yp, x, w_glu, b_glu)


def _dense_ffn_kernel(x_ref, g_ref, wg_ref, wu_ref, wd_ref, o_ref):
    xf = x_ref[...]
    h = _rms(xf, g_ref[...]).astype(BF16)
    act = (_silu(_bdot(h, wg_ref[...])) * _bdot(h, wu_ref[...])).astype(BF16)
    o_ref[...] = xf + _bdot(act, wd_ref[...])


def _dense_ffn_layer(x, ln, w_gate_up, w_down, *, tm=512):
    bsz, seqlen, d = x.shape
    ntok = bsz * seqlen
    hidden = w_down.shape[0]
    tm = min(tm, ntok)
    wgu = w_gate_up.astype(BF16)
    out = pl.pallas_call(
        _dense_ffn_kernel,
        out_shape=jax.ShapeDtypeStruct((ntok, d), F32),
        grid=(ntok // tm,),
        in_specs=[pl.BlockSpec((tm, d), lambda i: (i, 0)),
                  pl.BlockSpec((1, d), lambda i: (0, 0)),
                  _resident((d, hidden), lambda i: (0, 0)),
                  _resident((d, hidden), lambda i: (0, 1)),
                  _resident((hidden, d), lambda i: (0, 0))],
        out_specs=pl.BlockSpec((tm, d), lambda i: (i, 0)),
        compiler_params=_params(("parallel",), 56),
        name="dense_ffn",
    )(x.reshape(ntok, d), ln.reshape(1, d), wgu, wgu, w_down.astype(BF16))
    return out.reshape(bsz, seqlen, d)


def _log_sigmoid(z):
    return jnp.minimum(z, 0.0) - jnp.log(1.0 + jnp.exp(-jnp.abs(z)))


def _gla_kernel(x_ref, ln_ref, wm_ref, wgl_ref, wg2_ref, bg2_ref, gn_ref, wo_ref, o_ref, st_ref,
                *, tq, dk, dv, heads):
    hdk, hdv = dk // heads, dv // heads
    chunk = GLA_CHUNK
    nt = (((1,), (1,)), ((), ()))
    tn = (((0,), (0,)), ((), ()))

    @pl.when(pl.program_id(1) == 0)
    def _():
        st_ref[...] = jnp.zeros_like(st_ref)

    xf = x_ref[0]
    h = _rms(xf, ln_ref[...]).astype(BF16)
    proj = _bdot(h, wm_ref[...])
    glow = _bdot(h, wgl_ref[...]).astype(BF16)
    la = _log_sigmoid(_bdot(glow, wg2_ref[...]) + bg2_ref[...]) * (1.0 / GLA_GATE_NORM)
    row = lax.broadcasted_iota(jnp.int32, (chunk, chunk), 0)
    col = lax.broadcasted_iota(jnp.int32, (chunk, chunk), 1)
    causal = row >= col
    tri = jnp.where(causal, 1.0, 0.0).astype(BF16)
    scale = hdk ** -0.5
    outs = []
    for c in range(tq // chunk):
        r0 = c * chunk
        la_c = la[r0:r0 + chunk, :]
        la_hi = la_c.astype(BF16)
        la_lo = (la_c - la_hi.astype(F32)).astype(BF16)
        gcum_all = _bdot(tri, la_hi) + _bdot(tri, la_lo)
        head_out = []
        for hd in range(heads):
            gcum = gcum_all[:, hd * hdk:(hd + 1) * hdk]
            g_last = gcum[chunk - 1:chunk, :]
            q_c = proj[r0:r0 + chunk, hd * hdk:(hd + 1) * hdk] * scale
            k_c = proj[r0:r0 + chunk, dk + hd * hdk:dk + (hd + 1) * hdk]
            v_c = proj[r0:r0 + chunk, 2 * dk + hd * hdv:2 * dk + (hd + 1) * hdv].astype(BF16)
            q_s = (q_c * jnp.exp(gcum)).astype(BF16)
            k_s = (k_c * jnp.exp(-gcum)).astype(BF16)
            k_end = (k_c * jnp.exp(g_last - gcum)).astype(BF16)
            scores = lax.dot_general(q_s, k_s, nt, preferred_element_type=F32)
            scores = jnp.where(causal, scores, 0.0).astype(BF16)
            state_t = st_ref[hd]
            o = _bdot(scores, v_c) + lax.dot_general(q_s, state_t.astype(BF16), nt,
                                                     preferred_element_type=F32)
            kv_t = lax.dot_general(v_c, k_end, tn, preferred_element_type=F32)
            st_ref[hd] = state_t * jnp.exp(g_last) + kv_t
            head_out.append(o * lax.rsqrt(jnp.mean(o * o, axis=-1, keepdims=True) + EPS))
        outs.append(jnp.concatenate(head_out, axis=1))
    o_all = jnp.concatenate(outs, axis=0)
    r = proj[:, 2 * dk + dv:]
    o_all = (o_all * gn_ref[...] * _silu(r)).astype(BF16)
    o_ref[0] = xf + _bdot(o_all, wo_ref[...])


def _gla_layer(x, ln, w_in, w_g2, b_g2, g_norm, w_out, *, tq=256):
    bsz, seqlen, d = x.shape
    dk = w_g2.shape[1]
    dv = w_out.shape[0]
    nmain = 2 * dk + 2 * dv
    tq = min(tq, seqlen)
    w_main = w_in[:, :nmain].astype(BF16)
    w_glow = jnp.pad(w_in[:, nmain:], ((0, 0), (0, LANES - GLA_GATE_RANK))).astype(BF16)
    w_g2p = jnp.pad(w_g2, ((0, LANES - GLA_GATE_RANK), (0, 0))).astype(BF16)
    hdk, hdv = dk // GLA_HEADS, dv // GLA_HEADS
    const = lambda b, t: (0, 0)
    return pl.pallas_call(
        functools.partial(_gla_kernel, tq=tq, dk=dk, dv=dv, heads=GLA_HEADS),
        out_shape=jax.ShapeDtypeStruct(x.shape, F32),
        grid=(bsz, seqlen // tq),
        in_specs=[pl.BlockSpec((1, tq, d), lambda b, t: (b, t, 0)),
                  pl.BlockSpec((1, d), const),
                  _resident((d, nmain), const),
                  _resident((d, LANES), const),
                  _resident((LANES, dk), const),
                  pl.BlockSpec((1, dk), const),
                  pl.BlockSpec((1, dv), const),
                  _resident((dv, d), const)],
        out_specs=pl.BlockSpec((1, tq, d), lambda b, t: (b, t, 0)),
        scratch_shapes=[pltpu.VMEM((GLA_HEADS, hdv, hdk), F32)],
        compiler_params=_params(("parallel", "arbitrary"), 48),
        name="gla",
    )(x, ln.reshape(1, d), w_main, w_glow, w_g2p, b_g2.reshape(1, dk), g_norm.reshape(1, dv),
      w_out.astype(BF16))


LOG2E = math.log2(math.e)
SWA_SLOT_UNROLL = 4


def _swa_kernel(sink_ref, x_ref, gate_ref, y0_ref, y1_ref, ln_ref, wqkv_ref, bqkv_ref, wo_ref, bo_ref, o_ref,
                k_ref, v_ref, bias_ref, q_ref, a_ref, *, tq, q_heads):
    group = q_heads // SWA_KV_HEADS
    blk = SWA_BLOCK
    nt = (((1,), (1,)), ((), ()))
    b = pl.program_id(0)
    t = pl.program_id(1)
    nq = group * LANES

    @pl.when((b == 0) & (t == 0))
    def _():
        qi = lax.broadcasted_iota(jnp.int32, (blk, 2 * blk), 0)
        kj = lax.broadcasted_iota(jnp.int32, (blk, 2 * blk), 1)
        dist = qi + blk - kj
        in_window = (dist >= 0) & (dist < SWA_WINDOW)
        for hq in range(q_heads):
            slope = 2.0 ** (-8.0 * (hq + 1) / q_heads)
            bias_ref[hq] = jnp.where(in_window, -(slope * LOG2E) * dist.astype(F32), MASK_VALUE)

    @pl.when(t == 0)
    def _():
        k_ref[0:blk, :] = jnp.zeros((blk, LANES), BF16)
        v_ref[0:blk, :] = jnp.zeros((blk, LANES), BF16)

    g = gate_ref[...]
    xf = x_ref[0] + g[:, 0:1] * _unpack_pairs_f32(y0_ref[0]) + g[:, 1:2] * _unpack_pairs_f32(y1_ref[0])
    h = _rms(xf, ln_ref[...]).astype(BF16)
    qkv = _bdot(h, wqkv_ref[...]) + bqkv_ref[...]
    for j in range(group):
        q_ref[j] = qkv[:, j * LANES:(j + 1) * LANES].astype(BF16)
    k_ref[blk:blk + tq, :] = qkv[:, nq:nq + LANES].astype(BF16)
    v_ref[blk:blk + tq, :] = qkv[:, nq + LANES:nq + 2 * LANES].astype(BF16)
    kj_row = lax.broadcasted_iota(jnp.int32, (1, 2 * blk), 1)
    no_prev = jnp.where(kj_row < blk, jnp.where(t == 0, MASK_VALUE, 0.0), 0.0)
    low_half = lax.broadcasted_iota(jnp.int32, (1, LANES), 1) < SWA_HEAD_DIM
    halves = (low_half, jnp.logical_not(low_half))

    def slots(jj, carry):
        for u in range(SWA_SLOT_UNROLL):
            j = jj * SWA_SLOT_UNROLL + u
            for i in range(tq // blk):
                r0 = i * blk
                q_slot = q_ref[j, r0:r0 + blk, :]
                outs = []
                for kh in range(SWA_KV_HEADS):
                    hq = kh * group + j
                    sink = sink_ref[hq] * LOG2E
                    q_h = jnp.where(halves[kh], q_slot, jnp.zeros_like(q_slot))
                    s = lax.dot_general(q_h, k_ref[r0:r0 + 2 * blk, :], nt, preferred_element_type=F32) + bias_ref[hq]
                    if i == 0:
                        s = s + no_prev
                    m = jnp.maximum(jnp.max(s, axis=-1, keepdims=True), sink)
                    p = jnp.exp2(s - m)
                    denom = jnp.sum(p, axis=-1, keepdims=True) + jnp.exp2(sink - m)
                    outs.append(_bdot(p.astype(BF16), v_ref[r0:r0 + 2 * blk, :]) * (1.0 / denom))
                a_ref[j, r0:r0 + blk, :] = jnp.where(low_half, outs[0], outs[1]).astype(BF16)
        return carry

    lax.fori_loop(0, group // SWA_SLOT_UNROLL, slots, 0)
    k_ref[0:blk, :] = k_ref[tq:tq + blk, :]
    v_ref[0:blk, :] = v_ref[tq:tq + blk, :]
    o_all = jnp.concatenate([a_ref[j] for j in range(group)], axis=1)
    o_ref[0] = xf + _bdot(o_all, wo_ref[...]) + bo_ref[...]


def _swa_layer(x, gates_t, yk, ln, w_qkv, b_qkv, sinks, w_out, b_out, *, tq=512):
    bsz, seqlen, d = x.shape
    hd = SWA_HEAD_DIM
    q_heads = sinks.shape[0]
    group = q_heads // SWA_KV_HEADS
    nq = q_heads * hd
    tq = min(tq, seqlen)
    q_scale = hd ** -0.5 * LOG2E
    wq = (w_qkv[:, :nq] * q_scale).reshape(d, SWA_KV_HEADS, group, hd).transpose(0, 2, 1, 3).reshape(d, nq)
    bq = (b_qkv[:nq] * q_scale).reshape(SWA_KV_HEADS, group, hd).transpose(1, 0, 2).reshape(nq)
    w_all = jnp.concatenate([wq, w_qkv[:, nq:]], axis=1).astype(BF16)
    b_all = jnp.concatenate([bq, b_qkv[nq:]]).reshape(1, -1)
    wo = w_out.reshape(SWA_KV_HEADS, group, hd, d).transpose(1, 0, 2, 3).reshape(nq, d).astype(BF16)
    nall = w_all.shape[1]
    steps = seqlen // tq
    const = lambda b, t, s: (0, 0)
    return pl.pallas_call(
        functools.partial(_swa_kernel, tq=tq, q_heads=q_heads),
        out_shape=jax.ShapeDtypeStruct(x.shape, F32),
        grid_spec=pltpu.PrefetchScalarGridSpec(
            num_scalar_prefetch=1,
            grid=(bsz, seqlen // tq),
            in_specs=[pl.BlockSpec((1, tq, d), lambda b, t, s: (b, t, 0)),
                      pl.BlockSpec((tq, TOP_K), lambda b, t, s: (b * steps + t, 0)),
                      pl.BlockSpec((1, tq, d // 2), lambda b, t, s: (0, b * steps + t, 0)),
                      pl.BlockSpec((1, tq, d // 2), lambda b, t, s: (1, b * steps + t, 0)),
                      pl.BlockSpec((1, d), const),
                      _resident((d, nall), const),
                      pl.BlockSpec((1, nall), const),
                      _resident((nq, d), const),
                      pl.BlockSpec((1, d), const)],
            out_specs=pl.BlockSpec((1, tq, d), lambda b, t, s: (b, t, 0)),
            scratch_shapes=[pltpu.VMEM((SWA_BLOCK + tq, LANES), BF16), pltpu.VMEM((SWA_BLOCK + tq, LANES), BF16),
                            pltpu.VMEM((q_heads, SWA_BLOCK, 2 * SWA_BLOCK), F32),
                            pltpu.VMEM((group, tq, LANES), BF16), pltpu.VMEM((group, tq, LANES), BF16)]),
        compiler_params=_params(("arbitrary", "arbitrary"), 48),
        name="swa",
    )(sinks, x, gates_t, yk, yk, ln.reshape(1, d), w_all, b_all, wo, b_out.reshape(1, d))


def _router_kernel(x_ref, ln_ref, whi_ref, wlo_ref, idx_ref, gate_ref, hp_ref, pos_ref, count_ref, tri_ref):
    nt = (((1,), (1,)), ((), ()))
    h = _rms(x_ref[...], ln_ref[...])
    h_hi = h.astype(BF16)
    h_lo = (h - h_hi.astype(F32)).astype(BF16)
    w_hi, w_lo = whi_ref[...], wlo_ref[...]
    logits = (lax.dot_general(w_hi, h_hi, nt, preferred_element_type=F32)
              + lax.dot_general(w_hi, h_lo, nt, preferred_element_type=F32)
              + lax.dot_general(w_lo, h_hi, nt, preferred_element_type=F32))
    n_exp = logits.shape[0]
    eid = lax.broadcasted_iota(jnp.int32, logits.shape, 0)
    m1 = jnp.max(logits, axis=0, keepdims=True)
    i1 = jnp.min(jnp.where(logits == m1, eid, n_exp), axis=0, keepdims=True)
    rest = jnp.where(eid == i1, -jnp.inf, logits)
    m2 = jnp.max(rest, axis=0, keepdims=True)
    i2 = jnp.min(jnp.where(rest == m2, eid, n_exp), axis=0, keepdims=True)
    e2 = jnp.exp(m2 - m1)
    g1 = 1.0 / (1.0 + e2)
    idx_ref[...] = jnp.concatenate([i1, i2], axis=0)
    gate_ref[...] = jnp.concatenate([g1, e2 * g1], axis=0)
    hp_ref[...] = _pack_bf16_pairs(h)
    tm = logits.shape[1]

    @pl.when(pl.program_id(0) == 0)
    def _():
        count_ref[...] = jnp.zeros_like(count_ref)
        r = lax.broadcasted_iota(jnp.int32, (tm, tm), 0)
        c = lax.broadcasted_iota(jnp.int32, (tm, tm), 1)
        tri_ref[...] = jnp.where(r < c, 1.0, 0.0).astype(BF16)

    pick1 = jnp.where(eid == i1, 1.0, 0.0)
    pick2 = jnp.where(eid == i2, 1.0, 0.0)
    picks = pick1 + pick2
    before = _bdot(picks.astype(BF16), tri_ref[...]) + count_ref[:, 0:1]
    pos_ref[...] = jnp.concatenate([jnp.sum(pick1 * before, axis=0, keepdims=True),
                                    jnp.sum(pick2 * before, axis=0, keepdims=True)], axis=0).astype(jnp.int32)
    count_ref[...] = count_ref[...] + jnp.sum(picks, axis=1, keepdims=True)


def _router(x2, ln, w_router, *, tm=512):
    ntok, d = x2.shape
    n_exp = w_router.shape[1]
    tm = min(tm, ntok)
    wt = w_router.T
    w_hi = wt.astype(BF16)
    w_lo = (wt - w_hi.astype(F32)).astype(BF16)
    return pl.pallas_call(
        _router_kernel,
        out_shape=(jax.ShapeDtypeStruct((TOP_K, ntok), jnp.int32), jax.ShapeDtypeStruct((TOP_K, ntok), F32),
                   jax.ShapeDtypeStruct((ntok, d // 2), jnp.uint32),
                   jax.ShapeDtypeStruct((TOP_K, ntok), jnp.int32), jax.ShapeDtypeStruct((n_exp, LANES), F32)),
        grid=(ntok // tm,),
        in_specs=[pl.BlockSpec((tm, d), lambda i: (i, 0)),
                  pl.BlockSpec((1, d), lambda i: (0, 0)),
                  pl.BlockSpec((n_exp, d), lambda i: (0, 0)),
                  pl.BlockSpec((n_exp, d), lambda i: (0, 0))],
        out_specs=(pl.BlockSpec((TOP_K, tm), lambda i: (0, i)), pl.BlockSpec((TOP_K, tm), lambda i: (0, i)),
                   pl.BlockSpec((tm, d // 2), lambda i: (i, 0)),
                   pl.BlockSpec((TOP_K, tm), lambda i: (0, i)), pl.BlockSpec((n_exp, LANES), lambda i: (0, 0))),
        scratch_shapes=[pltpu.VMEM((tm, tm), BF16)],
        compiler_params=_params(("arbitrary",), 32),
        name="moe_router",
    )(x2, ln.reshape(1, d), w_hi, w_lo)


def _moe_plan(idx, pos, counts, tile):
    n_exp = counts.shape[0]
    nslots = idx.size
    counts = counts[:, 0].astype(jnp.int32)
    ends = jnp.cumsum(counts)
    offs = ends - counts
    experts = jnp.arange(n_exp, dtype=jnp.int32).reshape(n_exp, 1, 1)
    rank = pos + jnp.sum(jnp.where(idx[None] == experts, offs.reshape(n_exp, 1, 1), 0), axis=0)
    n_tiles = nslots // tile
    n_visits = n_tiles + n_exp - 1
    first_tile = offs // tile
    last_tile = (ends - 1) // tile
    nvis = jnp.where(counts > 0, last_tile - first_tile + 1, 0)
    vend = jnp.cumsum(nvis)
    vstart = vend - nvis
    total = vend[-1]
    v = jnp.arange(n_visits, dtype=jnp.int32)
    vc = jnp.minimum(v, total - 1)
    e = jnp.minimum(jnp.sum((vc[:, None] >= vend[None, :]).astype(jnp.int32), axis=1), n_exp - 1)
    sel = (e[:, None] == jnp.arange(n_exp, dtype=jnp.int32)[None, :]).astype(jnp.int32)
    pick = lambda a: jnp.sum(sel * a[None, :], axis=1)
    tile_id = pick(first_tile) + vc - pick(vstart)
    lo = jnp.maximum(pick(offs), tile_id * tile) - tile_id * tile
    hi = jnp.minimum(pick(ends), (tile_id + 1) * tile) - tile_id * tile
    valid = v < total
    lo = jnp.where(valid, lo, 0)
    hi = jnp.where(valid, hi, 0)
    prev_tile = jnp.concatenate([jnp.full((1,), -1, jnp.int32), tile_id[:-1]])
    first = (valid & (tile_id != prev_tile)).astype(jnp.int32)
    next_tile = jnp.concatenate([tile_id[1:], jnp.full((1,), -1, jnp.int32)])
    last = (valid & ((tile_id != next_tile) | (v == total - 1))).astype(jnp.int32)
    meta = jnp.stack([tile_id, e, lo, hi, first, last]).astype(jnp.int32)
    return rank.astype(jnp.int32), meta


def _pack_bf16_pairs(h):
    half = h.shape[1] // 2
    return _pack_halves(h[:, :half], h[:, half:])


def _pack_halves(lo, hi):
    bits = lambda a: lax.bitcast_convert_type(a.astype(BF16).astype(F32), jnp.uint32)
    return (bits(hi) & jnp.uint32(0xFFFF0000)) | (bits(lo) >> 16)


def _unpack_pairs_f32(u):
    lo = lax.bitcast_convert_type(u << 16, F32)
    hi = lax.bitcast_convert_type(u & jnp.uint32(0xFFFF0000), F32)
    return jnp.concatenate([lo, hi], axis=1)


def _unpack_bf16_pairs(u):
    return _unpack_pairs_f32(u).astype(BF16)


SC_CORES = 2
SC_SUBCORES = 16
SC_INDEX_WINDOW = 128


def _sc_mesh():
    return plsc.VectorSubcoreMesh(core_axis_name="c", subcore_axis_name="s")


def _sc_worker_id():
    return lax.axis_index("c") * SC_SUBCORES + lax.axis_index("s")


def _sc_scatter_rows(src, rank, nrows):
    ntok, width = src.shape
    win = SC_INDEX_WINDOW
    per = ntok // (SC_CORES * SC_SUBCORES)

    @pl.kernel(out_type=jax.ShapeDtypeStruct((nrows, width), src.dtype), mesh=_sc_mesh(),
               scratch_types=[pltpu.VMEM((1, win), jnp.int32)] * TOP_K + [pltpu.VMEM((win, width), src.dtype)],
               name="moe_dispatch_sc")
    def scatter(src_hbm, rank_hbm, o_hbm, *scratch):
        idx_vmem, buf = scratch[:TOP_K], scratch[TOP_K]
        wid = _sc_worker_id()

        @pl.loop(0, per // win)
        def _(blk):
            base = wid * per + blk * win
            for k in range(TOP_K):
                pltpu.sync_copy(rank_hbm.at[pl.ds(k, 1), pl.ds(base, win)], idx_vmem[k])
            pltpu.sync_copy(src_hbm.at[pl.ds(base, win)], buf)
            for k in range(TOP_K):
                pltpu.sync_copy(buf, o_hbm.at[idx_vmem[k].at[0]])

    return scatter(src, rank)


def _sc_gather_rows(src, rank, *, sub=32):
    nslot, ntok = rank.shape
    n = nslot * ntok
    width = src.shape[1]
    win = SC_INDEX_WINDOW
    per = n // (SC_CORES * SC_SUBCORES)
    nsub = win // sub

    @pl.kernel(out_type=jax.ShapeDtypeStruct((n, width), src.dtype), mesh=_sc_mesh(),
               scratch_types=[pltpu.VMEM((1, win), jnp.int32)] + [pltpu.VMEM((sub, width), src.dtype)] * 2
               + [pltpu.SemaphoreType.DMA] * 4,
               name="moe_gather_sc")
    def gather(src_hbm, idx_hbm, o_hbm, i_vmem, buf0, buf1, g0, g1, w0, w1):
        bufs, gsem, wsem = (buf0, buf1), (g0, g1), (w0, w1)
        wid = _sc_worker_id()

        @pl.loop(0, per // win)
        def _(blk):
            base = wid * per + blk * win
            pltpu.sync_copy(idx_hbm.at[pl.ds(base // ntok, 1), pl.ds(base % ntok, win)], i_vmem)
            gathers = [pltpu.make_async_copy(src_hbm.at[i_vmem.at[0, pl.ds(sub * j, sub)]], bufs[j % 2], gsem[j % 2])
                       for j in range(nsub)]
            writes = [pltpu.make_async_copy(bufs[j % 2], o_hbm.at[pl.ds(base + sub * j, sub)], wsem[j % 2])
                      for j in range(nsub)]
            gathers[0].start()
            for j in range(nsub):
                if j + 1 < nsub:
                    if j >= 1:
                        writes[j - 1].wait()
                    gathers[j + 1].start()
                gathers[j].wait()
                writes[j].start()
            writes[nsub - 2].wait()
            writes[nsub - 1].wait()

    return gather(src, rank)


MXU_N = 256


def _expert_kernel(meta_ref, x_ref, wg_ref, wu_ref, wd_ref, o_ref, acc_ref, xb_ref, act_ref, wgb_ref, wub_ref,
                   wdb_ref, *, ts):
    v = pl.program_id(0)
    hc = pl.program_id(1)
    lo, hi, first, last = meta_ref[2, v], meta_ref[3, v], meta_ref[4, v], meta_ref[5, v]
    tile, d = acc_ref.shape
    nsub = tile // ts
    th = wgb_ref.shape[1]
    wide = (hi - lo) * 2 > tile

    @pl.when(hc == 0)
    def _():
        for sub in range(nsub):
            xb_ref[sub * ts:(sub + 1) * ts, :] = _unpack_bf16_pairs(x_ref[sub * ts:(sub + 1) * ts, :])

    fresh = (first == 1) & (hc == 0)

    @pl.when(fresh & jnp.logical_not(wide))
    def _():
        acc_ref[...] = jnp.zeros_like(acc_ref)

    finishing = (last == 1) & (hc == pl.num_programs(1) - 1)
    half_chunks = d // 2 // MXU_N

    def whole_tile(assign, finish):
        rows = lax.broadcasted_iota(jnp.int32, (tile, 1), 0)
        mine = (rows >= lo) & (rows < hi)
        for n in range(th // MXU_N):
            cols = slice(n * MXU_N, (n + 1) * MXU_N)
            gate = _bdot(xb_ref[...], wg_ref[0, :, cols].astype(BF16))
            up = _bdot(xb_ref[...], wu_ref[0, :, cols].astype(BF16))
            act_ref[:, cols] = (_silu(gate) * up).astype(BF16)
        for n in range(d // MXU_N):
            cols = slice(n * MXU_N, (n + 1) * MXU_N)
            part = jnp.where(mine, _bdot(act_ref[...], wd_ref[0, :, cols].astype(BF16)), 0.0)
            total = part if assign else acc_ref[:, cols] + part
            if finish and n >= half_chunks:
                low = slice((n - half_chunks) * MXU_N, (n - half_chunks + 1) * MXU_N)
                o_ref[:, low] = _pack_halves(acc_ref[:, low], total)
            else:
                acc_ref[:, cols] = total

    pl.when(wide & fresh)(functools.partial(whole_tile, True, False))
    pl.when(wide & jnp.logical_not(fresh) & jnp.logical_not(finishing))(functools.partial(whole_tile, False, False))
    pl.when(wide & jnp.logical_not(fresh) & finishing)(functools.partial(whole_tile, False, True))

    @pl.when(jnp.logical_not(wide) & (hi > lo))
    def _():
        wgb_ref[...] = wg_ref[0].astype(BF16)
        wub_ref[...] = wu_ref[0].astype(BF16)
        wdb_ref[...] = wd_ref[0].astype(BF16)
        for sub in range(nsub):
            r0 = sub * ts

            @pl.when((lo < r0 + ts) & (hi > r0))
            def _():
                xs = xb_ref[r0:r0 + ts, :]
                act = (_silu(_bdot(xs, wgb_ref[...])) * _bdot(xs, wub_ref[...])).astype(BF16)
                y = _bdot(act, wdb_ref[...])
                rows = r0 + lax.broadcasted_iota(jnp.int32, (ts, 1), 0)
                acc_ref[r0:r0 + ts, :] += jnp.where((rows >= lo) & (rows < hi), y, 0.0)

    @pl.when(finishing & jnp.logical_not(wide))
    def _():
        for sub in range(nsub):
            o_ref[sub * ts:(sub + 1) * ts, :] = _pack_bf16_pairs(acc_ref[sub * ts:(sub + 1) * ts, :])


def _experts(xg, meta, w_gate_up, w_down, *, tile, th=512, ts=256):
    nrows = xg.shape[0]
    n_exp, hidden, d = w_down.shape
    n_hc = hidden // th
    ts = min(ts, tile)
    wgu = w_gate_up
    return pl.pallas_call(
        functools.partial(_expert_kernel, ts=ts),
        out_shape=jax.ShapeDtypeStruct((nrows, d // 2), jnp.uint32),
        grid_spec=pltpu.PrefetchScalarGridSpec(
            num_scalar_prefetch=1,
            grid=(meta.shape[1], n_hc),
            in_specs=[pl.BlockSpec((tile, d // 2), lambda v, c, m: (m[0, v], 0)),
                      pl.BlockSpec((1, d, th), lambda v, c, m: (m[1, v], 0, c)),
                      pl.BlockSpec((1, d, th), lambda v, c, m: (m[1, v], 0, c + n_hc)),
                      pl.BlockSpec((1, th, d), lambda v, c, m: (m[1, v], c, 0))],
            out_specs=pl.BlockSpec((tile, d // 2), lambda v, c, m: (m[0, v], 0)),
            scratch_shapes=[pltpu.VMEM((tile, d), F32), pltpu.VMEM((tile, d), BF16), pltpu.VMEM((tile, th), BF16),
                            pltpu.VMEM((d, th), BF16), pltpu.VMEM((d, th), BF16), pltpu.VMEM((th, d), BF16)]),
        compiler_params=_params(("arbitrary", "arbitrary"), 56),
        name="moe_experts",
    )(meta, xg, wgu, wgu, w_down)


def _combine_kernel(x_ref, gate_ref, fg_ref, y0_ref, y1_ref, o_ref, *, final_norm):
    g = gate_ref[...]
    out = x_ref[...] + g[:, 0:1] * _unpack_pairs_f32(y0_ref[0]) + g[:, 1:2] * _unpack_pairs_f32(y1_ref[0])
    if final_norm:
        out = _rms(out, fg_ref[...])
    o_ref[...] = out


def _combine(x2, gates_t, yk, final_gain, *, tm=512):
    ntok, d = x2.shape
    tm = min(tm, ntok)
    final_norm = final_gain is not None
    fg = (final_gain if final_norm else jnp.ones((d,), F32)).reshape(1, d)
    return pl.pallas_call(
        functools.partial(_combine_kernel, final_norm=final_norm),
        out_shape=jax.ShapeDtypeStruct((ntok, d), F32),
        grid=(ntok // tm,),
        in_specs=[pl.BlockSpec((tm, d), lambda i: (i, 0)),
                  pl.BlockSpec((tm, TOP_K), lambda i: (i, 0)),
                  pl.BlockSpec((1, d), lambda i: (0, 0)),
                  pl.BlockSpec((1, tm, d // 2), lambda i: (0, i, 0)),
                  pl.BlockSpec((1, tm, d // 2), lambda i: (1, i, 0))],
        out_specs=pl.BlockSpec((tm, d), lambda i: (i, 0)),
        compiler_params=_params(("parallel",), 40),
        name="moe_combine",
    )(x2, gates_t, fg, yk, yk)


def _moe_routed(x, ln, w_router, w_gate_up, w_down, *, tile=2048):
    bsz, seqlen, d = x.shape
    ntok = bsz * seqlen
    tile = min(tile, TOP_K * ntok)
    x2 = x.reshape(ntok, d)
    idx, gates, hp, pos, counts = _router(x2, ln, w_router)
    rank, meta = _moe_plan(idx, pos, counts, tile)
    xg = _sc_scatter_rows(hp, rank, TOP_K * ntok)
    y = _experts(xg, meta, w_gate_up, w_down, tile=tile)
    yk = _sc_gather_rows(y, rank, sub=64).reshape(TOP_K, ntok, d // 2)
    return gates.T, yk


def _moe_layer(x, ln, w_router, w_gate_up, w_down, *, final_gain=None, tile=2048):
    bsz, seqlen, d = x.shape
    gates_t, yk = _moe_routed(x, ln, w_router, w_gate_up, w_down, tile=tile)
    out = _combine(x.reshape(bsz * seqlen, d), gates_t, yk, final_gain)
    return out.reshape(bsz, seqlen, d)


def kernel(x, l0_ln1, l0_s5_lam_re, l0_s5_lam_im, l0_s5_log_dt, l0_s5_b_re, l0_s5_b_im, l0_s5_c_re, l0_s5_c_im, l0_s5_d, l0_s5_w_glu, l0_s5_b_glu, l0_ln2, l0_ffn_w_gate_up, l0_ffn_w_down, l1_ln1, l1_gla_w_in, l1_gla_w_g2, l1_gla_b_g2, l1_gla_norm, l1_gla_w_out, l1_ln2, l1_moe_router, l1_moe_w_gate_up, l1_moe_w_down, l2_ln1, l2_swa_w_qkv, l2_swa_b_qkv, l2_swa_sinks, l2_swa_w_out, l2_swa_b_out, l2_ln2, l2_ffn_w_gate_up, l2_ffn_w_down, l3_ln1, l3_s5_lam_re, l3_s5_lam_im, l3_s5_log_dt, l3_s5_b_re, l3_s5_b_im, l3_s5_c_re, l3_s5_c_im, l3_s5_d, l3_s5_w_glu, l3_s5_b_glu, l3_ln2, l3_moe_router, l3_moe_w_gate_up, l3_moe_w_down, ln_f):
    s5_params = ((l0_s5_lam_re, l0_s5_lam_im, l0_s5_log_dt, l0_s5_b_re, l0_s5_b_im, l0_s5_c_re, l0_s5_c_im),
                 (l3_s5_lam_re, l3_s5_lam_im, l3_s5_log_dt, l3_s5_b_re, l3_s5_b_im, l3_s5_c_re, l3_s5_c_im))
    s5_ops = jax.vmap(_s5_operators)(*(jnp.stack(pair) for pair in zip(*s5_params)))
    s5_ops = tuple(a.reshape((-1,) + a.shape[2:]) for a in s5_ops)
    x = _s5_layer(x, l0_ln1, s5_ops, 0, l0_s5_d, l0_s5_w_glu, l0_s5_b_glu)
    x = _dense_ffn_layer(x, l0_ln2, l0_ffn_w_gate_up, l0_ffn_w_down)
    x = _gla_layer(x, l1_ln1, l1_gla_w_in, l1_gla_w_g2, l1_gla_b_g2, l1_gla_norm, l1_gla_w_out)
    gates_t, yk = _moe_routed(x, l1_ln2, l1_moe_router, l1_moe_w_gate_up, l1_moe_w_down)
    x = _swa_layer(x, gates_t, yk, l2_ln1, l2_swa_w_qkv, l2_swa_b_qkv, l2_swa_sinks, l2_swa_w_out, l2_swa_b_out)
    x = _dense_ffn_layer(x, l2_ln2, l2_ffn_w_gate_up, l2_ffn_w_down)
    x = _s5_layer(x, l3_ln1, s5_ops, 1, l3_s5_d, l3_s5_w_glu, l3_s5_b_glu)
    return _moe_layer(x, l3_ln2, l3_moe_router, l3_moe_w_gate_up, l3_moe_w_down, final_gain=ln_f)
```

```python
import functools
import math

import jax
import jax.numpy as jnp
from jax import lax
from jax.experimental import pallas as pl
from jax.experimental.pallas import tpu as pltpu
from jax.experimental.pallas import tpu_sc as plsc

F32 = jnp.float32
BF16 = jnp.bfloat16
EPS = 1e-6
LANES = 128
MIB = 1 << 20

S5_GROUP = 16
S5_CHUNK = 16
S5_SLAB_GROUPS = LANES // S5_GROUP
S5_PITCH_PAD = 8
S5_SCAN_UNROLL = 8
S5_CAUSAL_BANDS = 8

GLA_HEADS = 4
GLA_GATE_RANK = 16
GLA_GATE_NORM = 16.0
GLA_CHUNK = 64

SWA_HEAD_DIM = 64
SWA_KV_HEADS = 2
SWA_WINDOW = 128
SWA_BLOCK = 128
MASK_VALUE = -1e30

TOP_K = 2


def _params(semantics, vmem_mib):
    return pltpu.CompilerParams(dimension_semantics=semantics, vmem_limit_bytes=vmem_mib * MIB)


def _resident(block_shape, index_map):
    return pl.BlockSpec(block_shape, index_map, pipeline_mode=pl.Buffered(1))


def _rms(xf, gain):
    return xf * lax.rsqrt(jnp.mean(xf * xf, axis=-1, keepdims=True) + EPS) * gain


def _gelu_tanh(x):
    return 0.5 * x * (1.0 + jnp.tanh(math.sqrt(2.0 / math.pi) * (x + 0.044715 * (x * x * x))))


def _silu(x):
    return x * jax.nn.sigmoid(x)


def _bdot(a, b):
    return jnp.dot(a, b, preferred_element_type=F32)


S5_ROW_TILE = 1024


def _s5_norm_kernel(x_ref, g_ref, o_ref, scr_ref, *, nloc):
    h = _rms(x_ref[0], g_ref[...])
    nslab = scr_ref.shape[0]
    for c in range(nslab):
        scr_ref[c] = h[:, c * LANES:(c + 1) * LANES]
    for s in range(S5_CHUNK):
        rows = pl.ds(s, nloc, stride=S5_CHUNK)
        o_ref[0, s] = jnp.concatenate([scr_ref[c, rows, :] for c in range(nslab)], axis=1).astype(o_ref.dtype)


def _s5_norm(x, gain):
    bsz, seqlen, d = x.shape
    nch = seqlen // S5_CHUNK
    tm = min(S5_ROW_TILE, seqlen)
    nloc = tm // S5_CHUNK
    return pl.pallas_call(
        functools.partial(_s5_norm_kernel, nloc=nloc),
        out_shape=jax.ShapeDtypeStruct((bsz, S5_CHUNK, nch, d), BF16),
        grid=(bsz, seqlen // tm),
        in_specs=[pl.BlockSpec((1, tm, d), lambda b, i: (b, i, 0)),
                  pl.BlockSpec((1, d), lambda b, i: (0, 0))],
        out_specs=pl.BlockSpec((1, S5_CHUNK, nloc, d), lambda b, i: (b, 0, i, 0)),
        scratch_shapes=[pltpu.VMEM((d // LANES, tm, LANES), F32)],
        compiler_params=_params(("parallel", "parallel"), 32),
        name="s5_norm",
    )(x, gain.reshape(1, d))


def _tiling_matrix(rows, cols):
    p = lax.broadcasted_iota(jnp.int32, (rows, cols), 0)
    c = lax.broadcasted_iota(jnp.int32, (rows, cols), 1)
    return jnp.where(c % rows == p, 1.0, 0.0).astype(BF16)


def _same_group(shape, row_group, col_group):
    r = lax.broadcasted_iota(jnp.int32, shape, 0)
    c = lax.broadcasted_iota(jnp.int32, shape, 1)
    return (r // row_group) == (c // col_group)


def _s5_build_operators(vw_ref, mw_ref, toep_ref, win_ref, wout_ref):
    tn = (((0,), (0,)), ((), ()))
    nstate = vw_ref.shape[-1]
    half = S5_SLAB_GROUPS * nstate
    rep_ch = _tiling_matrix(S5_GROUP, LANES)
    rep_st = _tiling_matrix(nstate, half)
    diag_in = _same_group((LANES, half), S5_GROUP, nstate)
    diag_out = _same_group((half, LANES), nstate, S5_GROUP)

    def out_block(q, r):
        e = lax.dot_general(mw_ref[0, 2 * q + r].astype(BF16), rep_ch, tn, preferred_element_type=F32)
        return jnp.where(diag_out, e, 0.0).astype(BF16)

    for a in range(S5_CHUNK):
        for r in range(2):
            e = _bdot(vw_ref[0, 2 * a + r].astype(BF16), rep_st)
            win_ref[a * LANES:(a + 1) * LANES, r * half:(r + 1) * half] = jnp.where(diag_in, e, 0.0).astype(BF16)
            wout_ref[r * half:(r + 1) * half, a * LANES:(a + 1) * LANES] = out_block(a + 1, r)
    b_bar = win_ref[(S5_CHUNK - 1) * LANES:S5_CHUNK * LANES, :]
    taps = [_bdot(b_bar, jnp.concatenate([out_block(0, 0), out_block(0, 1)], axis=0)).astype(BF16)]
    for j in range(1, S5_CHUNK):
        taps.append(_bdot(b_bar, wout_ref[:, (j - 1) * LANES:j * LANES]).astype(BF16))
    zero = jnp.zeros((LANES, LANES), BF16)
    for a in range(S5_CHUNK):
        for b in range(S5_CHUNK):
            toep_ref[a * LANES:(a + 1) * LANES, b * LANES:(b + 1) * LANES] = taps[b - a] if b >= a else zero


def _s5_conv_kernel(h_ref, vw_ref, mw_ref, a_ref, d_ref, o_ref, s_ref, toep_ref, win_ref, wout_ref,
                    *, nseq, nch):
    pitch = nch + S5_PITCH_PAD
    nl = a_ref.shape[1] // 2

    @pl.when(pl.program_id(1) == 0)
    def _():
        _s5_build_operators(vw_ref, mw_ref, toep_ref, win_ref, wout_ref)

    lhs = jnp.concatenate(
        [jnp.concatenate([h_ref[bl, s] for s in range(S5_CHUNK)], axis=1) for bl in range(nseq)], axis=0)
    bc = _bdot(lhs, win_ref[...])
    for bl in range(nseq):
        for j in range(2 * nl):
            s_ref[j, bl * pitch:bl * pitch + nch, :] = bc[bl * nch:(bl + 1) * nch, j * LANES:(j + 1) * LANES]
    a_re = [a_ref[0, j:j + 1, :] for j in range(nl)]
    a_im = [a_ref[0, nl + j:nl + j + 1, :] for j in range(nl)]

    def step(n, carry):
        p_re, p_im = carry
        rows = pl.ds(n, nseq, stride=pitch)
        n_re, n_im = [], []
        for j in range(nl):
            c_re = s_ref[j, rows, :]
            c_im = s_ref[nl + j, rows, :]
            s_ref[j, rows, :] = p_re[j]
            s_ref[nl + j, rows, :] = p_im[j]
            n_re.append(a_re[j] * p_re[j] - a_im[j] * p_im[j] + c_re)
            n_im.append(a_re[j] * p_im[j] + a_im[j] * p_re[j] + c_im)
        return tuple(n_re), tuple(n_im)

    def steps(m, carry):
        for u in range(S5_SCAN_UNROLL):
            carry = step(m * S5_SCAN_UNROLL + u, carry)
        return carry

    zeros = tuple(jnp.zeros((nseq, LANES), F32) for _ in range(nl))
    lax.fori_loop(0, nch // S5_SCAN_UNROLL, steps, (zeros, zeros))
    x_prev = jnp.concatenate(
        [jnp.concatenate([s_ref[j, bl * pitch:bl * pitch + nch, :] for j in range(2 * nl)], axis=1)
         for bl in range(nseq)], axis=0).astype(BF16)
    band = S5_CHUNK // S5_CAUSAL_BANDS
    y_bands = []
    for q in range(S5_CAUSAL_BANDS):
        kk = (q + 1) * band * LANES
        cols = slice(q * band * LANES, (q + 1) * band * LANES)
        y_bands.append(_bdot(lhs[:, :kk], toep_ref[:kk, cols]) + _bdot(x_prev, wout_ref[:, cols]))
    dskip = d_ref[0]
    for bl in range(nseq):
        for s in range(S5_CHUNK):
            ys = y_bands[s // band][bl * nch:(bl + 1) * nch, (s % band) * LANES:(s % band + 1) * LANES]
            ys = ys + dskip * h_ref[bl, s].astype(F32)
            o_ref[bl, s] = _gelu_tanh(ys).astype(o_ref.dtype)


def _s5_conv(hp, vw, mw, a_pack, d_skip, *, nseq, layer):
    bsz, _, nch, d = hp.shape
    nslab = d // LANES
    kdim = S5_CHUNK * LANES
    sdim = a_pack.shape[2] * LANES
    blk4 = lambda a: pl.BlockSpec((None, 1) + a.shape[2:], lambda c, b: (layer, c, 0, 0, 0))
    return pl.pallas_call(
        functools.partial(_s5_conv_kernel, nseq=nseq, nch=nch),
        out_shape=jax.ShapeDtypeStruct(hp.shape, BF16),
        grid=(nslab, bsz // nseq),
        in_specs=[pl.BlockSpec((nseq, S5_CHUNK, nch, LANES), lambda c, b: (b, 0, 0, c)),
                  blk4(vw), blk4(mw),
                  pl.BlockSpec((None, 1, sdim // LANES, LANES), lambda c, b: (layer, c, 0, 0)),
                  pl.BlockSpec((1, 1, LANES), lambda c, b: (c, 0, 0))],
        out_specs=pl.BlockSpec((nseq, S5_CHUNK, nch, LANES), lambda c, b: (b, 0, 0, c)),
        scratch_shapes=[pltpu.VMEM((sdim // LANES, nseq * (nch + S5_PITCH_PAD), LANES), F32),
                        pltpu.VMEM((kdim, kdim), BF16),
                        pltpu.VMEM((kdim, sdim), BF16),
                        pltpu.VMEM((sdim, kdim), BF16)],
        compiler_params=_params(("arbitrary", "arbitrary"), 56),
        name="s5_conv",
    )(hp, vw, mw, a_pack, d_skip.reshape(nslab, 1, LANES))


def _s5_glu_kernel(y_ref, x_ref, w_ref, b_ref, o_ref, scr_ref, *, nloc):
    nslab = scr_ref.shape[0]
    y = jnp.concatenate([y_ref[0, s] for s in range(S5_CHUNK)], axis=0)
    u = y.astype(F32) * jax.nn.sigmoid(_bdot(y, w_ref[...]) + b_ref[...])
    for s in range(S5_CHUNK):
        rows = pl.ds(s, nloc, stride=S5_CHUNK)
        for c in range(nslab):
            scr_ref[c, rows, :] = u[s * nloc:(s + 1) * nloc, c * LANES:(c + 1) * LANES]
    o_ref[0] = x_ref[0] + jnp.concatenate([scr_ref[c] for c in range(nslab)], axis=1)


def _s5_glu(yp, x, w_glu, b_glu):
    bsz, seqlen, d = x.shape
    tm = min(S5_ROW_TILE, seqlen)
    nloc = tm // S5_CHUNK
    return pl.pallas_call(
        functools.partial(_s5_glu_kernel, nloc=nloc),
        out_shape=jax.ShapeDtypeStruct(x.shape, F32),
        grid=(bsz, seqlen // tm),
        in_specs=[pl.BlockSpec((1, S5_CHUNK, nloc, d), lambda b, i: (b, 0, i, 0)),
                  pl.BlockSpec((1, tm, d), lambda b, i: (b, i, 0)),
                  _resident((d, d), lambda b, i: (0, 0)),
                  pl.BlockSpec((1, d), lambda b, i: (0, 0))],
        out_specs=pl.BlockSpec((1, tm, d), lambda b, i: (b, i, 0)),
        scratch_shapes=[pltpu.VMEM((d // LANES, tm, LANES), F32)],
        compiler_params=_params(("parallel", "parallel"), 40),
        name="s5_glu",
    )(yp, x, w_glu.astype(BF16), b_glu.reshape(1, d))


def _s5_operators(lam_re, lam_im, log_dt, b_re, b_im, c_re, c_im):
    ngroups, nstate = lam_re.shape
    gpc = S5_SLAB_GROUPS
    nslab = ngroups // gpc
    dt = jnp.exp(log_dt)[:, None]
    j = jnp.arange(S5_CHUNK + 1, dtype=F32)[:, None, None]
    mag = jnp.exp(j * (lam_re * dt)[None])
    ang = j * (lam_im * dt)[None]
    pw_re, pw_im = mag * jnp.cos(ang), mag * jnp.sin(ang)
    num_re, num_im = pw_re[1] - 1.0, pw_im[1]
    den = lam_re * lam_re + lam_im * lam_im
    f_re = (num_re * lam_re + num_im * lam_im) / den
    f_im = (num_im * lam_re - num_re * lam_im) / den
    bb_re = f_re[..., None] * b_re - f_im[..., None] * b_im
    bb_im = f_re[..., None] * b_im + f_im[..., None] * b_re
    jr = (S5_CHUNK - 1) - jnp.arange(S5_CHUNK, dtype=F32)[:, None, None]
    mag_r = jnp.exp(jr * (lam_re * dt)[None])
    ang_r = jr * (lam_im * dt)[None]
    rev_re, rev_im = mag_r * jnp.cos(ang_r), mag_r * jnp.sin(ang_r)
    slabbed = lambda a: a.reshape(a.shape[0], nslab, gpc, nstate).transpose(1, 0, 2, 3)
    rv_re, rv_im = slabbed(rev_re)[:, :, :, None, :], slabbed(rev_im)[:, :, :, None, :]
    bt_re = bb_re.transpose(0, 2, 1).reshape(nslab, 1, gpc, S5_GROUP, nstate)
    bt_im = bb_im.transpose(0, 2, 1).reshape(nslab, 1, gpc, S5_GROUP, nstate)
    vw = jnp.stack([rv_re * bt_re - rv_im * bt_im, rv_re * bt_im + rv_im * bt_re], axis=2)
    vw = vw.reshape(nslab, 2 * S5_CHUNK, LANES, nstate)
    pc_re, pc_im = slabbed(pw_re)[:, :, None, :, :], slabbed(pw_im)[:, :, None, :, :]
    ct_re = c_re.reshape(nslab, gpc, S5_GROUP, nstate).transpose(0, 2, 1, 3)[:, None]
    ct_im = c_im.reshape(nslab, gpc, S5_GROUP, nstate).transpose(0, 2, 1, 3)[:, None]
    mw = jnp.stack([ct_re * pc_re - ct_im * pc_im, -(ct_re * pc_im + ct_im * pc_re)], axis=2)
    mw = mw.reshape(nslab, 2 * (S5_CHUNK + 1), S5_GROUP, gpc * nstate)
    half = gpc * nstate // LANES
    a_pack = jnp.concatenate([pw_re[S5_CHUNK].reshape(nslab, half, LANES),
                              pw_im[S5_CHUNK].reshape(nslab, half, LANES)], axis=1)
    return vw, mw, a_pack


def _s5_layer(x, ln, operators, layer, d_skip, w_glu, b_glu, *, nseq=4):
    vw, mw, a_pack = operators
    hp = _s5_norm(x, ln)
    yp = _s5_conv(hp, vw, mw, a_pack, d_skip, nseq=min(nseq, x.shape[0]), layer=layer)
    return _s5_glu(yp, x, w_glu, b_glu)


def _dense_ffn_kernel(x_ref, g_ref, wg_ref, wu_ref, wd_ref, o_ref):
    xf = x_ref[...]
    h = _rms(xf, g_ref[...]).astype(BF16)
    act = (_silu(_bdot(h, wg_ref[...])) * _bdot(h, wu_ref[...])).astype(BF16)
    o_ref[...] = xf + _bdot(act, wd_ref[...])


def _dense_ffn_layer(x, ln, w_gate_up, w_down, *, tm=512):
    bsz, seqlen, d = x.shape
    ntok = bsz * seqlen
    hidden = w_down.shape[0]
    tm = min(tm, ntok)
    wgu = w_gate_up.astype(BF16)
    out = pl.pallas_call(
        _dense_ffn_kernel,
        out_shape=jax.ShapeDtypeStruct((ntok, d), F32),
        grid=(ntok // tm,),
        in_specs=[pl.BlockSpec((tm, d), lambda i: (i, 0)),
                  pl.BlockSpec((1, d), lambda i: (0, 0)),
                  _resident((d, hidden), lambda i: (0, 0)),
                  _resident((d, hidden), lambda i: (0, 1)),
                  _resident((hidden, d), lambda i: (0, 0))],
        out_specs=pl.BlockSpec((tm, d), lambda i: (i, 0)),
        compiler_params=_params(("parallel",), 56),
        name="dense_ffn",
    )(x.reshape(ntok, d), ln.reshape(1, d), wgu, wgu, w_down.astype(BF16))
    return out.reshape(bsz, seqlen, d)


def _log_sigmoid(z):
    return jnp.minimum(z, 0.0) - jnp.log(1.0 + jnp.exp(-jnp.abs(z)))


def _gla_kernel(x_ref, ln_ref, wm_ref, wgl_ref, wg2_ref, bg2_ref, gn_ref, wo_ref, o_ref, st_ref,
                *, tq, dk, dv, heads):
    hdk, hdv = dk // heads, dv // heads
    chunk = GLA_CHUNK
    nt = (((1,), (1,)), ((), ()))
    tn = (((0,), (0,)), ((), ()))

    @pl.when(pl.program_id(1) == 0)
    def _():
        st_ref[...] = jnp.zeros_like(st_ref)

    xf = x_ref[0]
    h = _rms(xf, ln_ref[...]).astype(BF16)
    proj = _bdot(h, wm_ref[...])
    glow = _bdot(h, wgl_ref[...]).astype(BF16)
    la = _log_sigmoid(_bdot(glow, wg2_ref[...]) + bg2_ref[...]) * (1.0 / GLA_GATE_NORM)
    row = lax.broadcasted_iota(jnp.int32, (chunk, chunk), 0)
    col = lax.broadcasted_iota(jnp.int32, (chunk, chunk), 1)
    causal = row >= col
    tri = jnp.where(causal, 1.0, 0.0).astype(BF16)
    scale = hdk ** -0.5
    outs = []
    for c in range(tq // chunk):
        r0 = c * chunk
        la_c = la[r0:r0 + chunk, :]
        la_hi = la_c.astype(BF16)
        la_lo = (la_c - la_hi.astype(F32)).astype(BF16)
        gcum_all = _bdot(tri, la_hi) + _bdot(tri, la_lo)
        head_out = []
        for hd in range(heads):
            gcum = gcum_all[:, hd * hdk:(hd + 1) * hdk]
            g_last = gcum[chunk - 1:chunk, :]
            q_c = proj[r0:r0 + chunk, hd * hdk:(hd + 1) * hdk] * scale
            k_c = proj[r0:r0 + chunk, dk + hd * hdk:dk + (hd + 1) * hdk]
            v_c = proj[r0:r0 + chunk, 2 * dk + hd * hdv:2 * dk + (hd + 1) * hdv].astype(BF16)
            q_s = (q_c * jnp.exp(gcum)).astype(BF16)
            k_s = (k_c * jnp.exp(-gcum)).astype(BF16)
            k_end = (k_c * jnp.exp(g_last - gcum)).astype(BF16)
            scores = lax.dot_general(q_s, k_s, nt, preferred_element_type=F32)
            scores = jnp.where(causal, scores, 0.0).astype(BF16)
            state_t = st_ref[hd]
            o = _bdot(scores, v_c) + lax.dot_general(q_s, state_t.astype(BF16), nt,
                                                     preferred_element_type=F32)
            kv_t = lax.dot_general(v_c, k_end, tn, preferred_element_type=F32)
            st_ref[hd] = state_t * jnp.exp(g_last) + kv_t
            head_out.append(o * lax.rsqrt(jnp.mean(o * o, axis=-1, keepdims=True) + EPS))
        outs.append(jnp.concatenate(head_out, axis=1))
    o_all = jnp.concatenate(outs, axis=0)
    r = proj[:, 2 * dk + dv:]
    o_all = (o_all * gn_ref[...] * _silu(r)).astype(BF16)
    o_ref[0] = xf + _bdot(o_all, wo_ref[...])


def _gla_layer(x, ln, w_in, w_g2, b_g2, g_norm, w_out, *, tq=256):
    bsz, seqlen, d = x.shape
    dk = w_g2.shape[1]
    dv = w_out.shape[0]
    nmain = 2 * dk + 2 * dv
    tq = min(tq, seqlen)
    w_main = w_in[:, :nmain].astype(BF16)
    w_glow = jnp.pad(w_in[:, nmain:], ((0, 0), (0, LANES - GLA_GATE_RANK))).astype(BF16)
    w_g2p = jnp.pad(w_g2, ((0, LANES - GLA_GATE_RANK), (0, 0))).astype(BF16)
    hdk, hdv = dk // GLA_HEADS, dv // GLA_HEADS
    const = lambda b, t: (0, 0)
    return pl.pallas_call(
        functools.partial(_gla_kernel, tq=tq, dk=dk, dv=dv, heads=GLA_HEADS),
        out_shape=jax.ShapeDtypeStruct(x.shape, F32),
        grid=(bsz, seqlen // tq),
        in_specs=[pl.BlockSpec((1, tq, d), lambda b, t: (b, t, 0)),
                  pl.BlockSpec((1, d), const),
                  _resident((d, nmain), const),
                  _resident((d, LANES), const),
                  _resident((LANES, dk), const),
                  pl.BlockSpec((1, dk), const),
                  pl.BlockSpec((1, dv), const),
                  _resident((dv, d), const)],
        out_specs=pl.BlockSpec((1, tq, d), lambda b, t: (b, t, 0)),
        scratch_shapes=[pltpu.VMEM((GLA_HEADS, hdv, hdk), F32)],
        compiler_params=_params(("parallel", "arbitrary"), 48),
        name="gla",
    )(x, ln.reshape(1, d), w_main, w_glow, w_g2p, b_g2.reshape(1, dk), g_norm.reshape(1, dv),
      w_out.astype(BF16))


LOG2E = math.log2(math.e)
SWA_SLOT_UNROLL = 4


def _swa_kernel(sink_ref, x_ref, gate_ref, y0_ref, y1_ref, ln_ref, wqkv_ref, bqkv_ref, wo_ref, bo_ref, o_ref,
                k_ref, v_ref, bias_ref, q_ref, a_ref, *, tq, q_heads):
    group = q_heads // SWA_KV_HEADS
    blk = SWA_BLOCK
    nt = (((1,), (1,)), ((), ()))
    b = pl.program_id(0)
    t = pl.program_id(1)
    nq = group * LANES

    @pl.when((b == 0) & (t == 0))
    def _():
        qi = lax.broadcasted_iota(jnp.int32, (blk, 2 * blk), 0)
        kj = lax.broadcasted_iota(jnp.int32, (blk, 2 * blk), 1)
        dist = qi + blk - kj
        in_window = (dist >= 0) & (dist < SWA_WINDOW)
        for hq in range(q_heads):
            slope = 2.0 ** (-8.0 * (hq + 1) / q_heads)
            bias_ref[hq] = jnp.where(in_window, -(slope * LOG2E) * dist.astype(F32), MASK_VALUE)

    @pl.when(t == 0)
    def _():
        k_ref[0:blk, :] = jnp.zeros((blk, LANES), BF16)
        v_ref[0:blk, :] = jnp.zeros((blk, LANES), BF16)

    g = gate_ref[...]
    xf = x_ref[0] + g[:, 0:1] * _unpack_pairs_f32(y0_ref[0]) + g[:, 1:2] * _unpack_pairs_f32(y1_ref[0])
    h = _rms(xf, ln_ref[...]).astype(BF16)
    qkv = _bdot(h, wqkv_ref[...]) + bqkv_ref[...]
    for j in range(group):
        q_ref[j] = qkv[:, j * LANES:(j + 1) * LANES].astype(BF16)
    k_ref[blk:blk + tq, :] = qkv[:, nq:nq + LANES].astype(BF16)
    v_ref[blk:blk + tq, :] = qkv[:, nq + LANES:nq + 2 * LANES].astype(BF16)
    kj_row = lax.broadcasted_iota(jnp.int32, (1, 2 * blk), 1)
    no_prev = jnp.where(kj_row < blk, jnp.where(t == 0, MASK_VALUE, 0.0), 0.0)
    low_half = lax.broadcasted_iota(jnp.int32, (1, LANES), 1) < SWA_HEAD_DIM
    halves = (low_half, jnp.logical_not(low_half))

    def slots(jj, carry):
        for u in range(SWA_SLOT_UNROLL):
            j = jj * SWA_SLOT_UNROLL + u
            for i in range(tq // blk):
                r0 = i * blk
                q_slot = q_ref[j, r0:r0 + blk, :]
                outs = []
                for kh in range(SWA_KV_HEADS):
                    hq = kh * group + j
                    sink = sink_ref[hq] * LOG2E
                    q_h = jnp.where(halves[kh], q_slot, jnp.zeros_like(q_slot))
                    s = lax.dot_general(q_h, k_ref[r0:r0 + 2 * blk, :], nt, preferred_element_type=F32) + bias_ref[hq]
                    if i == 0:
                        s = s + no_prev
                    m = jnp.maximum(jnp.max(s, axis=-1, keepdims=True), sink)
                    p = jnp.exp2(s - m)
                    denom = jnp.sum(p, axis=-1, keepdims=True) + jnp.exp2(sink - m)
                    outs.append(_bdot(p.astype(BF16), v_ref[r0:r0 + 2 * blk, :]) * (1.0 / denom))
                a_ref[j, r0:r0 + blk, :] = jnp.where(low_half, outs[0], outs[1]).astype(BF16)
        return carry

    lax.fori_loop(0, group // SWA_SLOT_UNROLL, slots, 0)
    k_ref[0:blk, :] = k_ref[tq:tq + blk, :]
    v_ref[0:blk, :] = v_ref[tq:tq + blk, :]
    o_all = jnp.concatenate([a_ref[j] for j in range(group)], axis=1)
    o_ref[0] = xf + _bdot(o_all, wo_ref[...]) + bo_ref[...]


def _swa_layer(x, gates_t, yk, ln, w_qkv, b_qkv, sinks, w_out, b_out, *, tq=512):
    bsz, seqlen, d = x.shape
    hd = SWA_HEAD_DIM
    q_heads = sinks.shape[0]
    group = q_heads // SWA_KV_HEADS
    nq = q_heads * hd
    tq = min(tq, seqlen)
    q_scale = hd ** -0.5 * LOG2E
    wq = (w_qkv[:, :nq] * q_scale).reshape(d, SWA_KV_HEADS, group, hd).transpose(0, 2, 1, 3).reshape(d, nq)
    bq = (b_qkv[:nq] * q_scale).reshape(SWA_KV_HEADS, group, hd).transpose(1, 0, 2).reshape(nq)
    w_all = jnp.concatenate([wq, w_qkv[:, nq:]], axis=1).astype(BF16)
    b_all = jnp.concatenate([bq, b_qkv[nq:]]).reshape(1, -1)
    wo = w_out.reshape(SWA_KV_HEADS, group, hd, d).transpose(1, 0, 2, 3).reshape(nq, d).astype(BF16)
    nall = w_all.shape[1]
    steps = seqlen // tq
    const = lambda b, t, s: (0, 0)
    return pl.pallas_call(
        functools.partial(_swa_kernel, tq=tq, q_heads=q_heads),
        out_shape=jax.ShapeDtypeStruct(x.shape, F32),
        grid_spec=pltpu.PrefetchScalarGridSpec(
            num_scalar_prefetch=1,
            grid=(bsz, seqlen // tq),
            in_specs=[pl.BlockSpec((1, tq, d), lambda b, t, s: (b, t, 0)),
                      pl.BlockSpec((tq, TOP_K), lambda b, t, s: (b * steps + t, 0)),
                      pl.BlockSpec((1, tq, d // 2), lambda b, t, s: (0, b * steps + t, 0)),
                      pl.BlockSpec((1, tq, d // 2), lambda b, t, s: (1, b * steps + t, 0)),
                      pl.BlockSpec((1, d), const),
                      _resident((d, nall), const),
                      pl.BlockSpec((1, nall), const),
                      _resident((nq, d), const),
                      pl.BlockSpec((1, d), const)],
            out_specs=pl.BlockSpec((1, tq, d), lambda b, t, s: (b, t, 0)),
            scratch_shapes=[pltpu.VMEM((SWA_BLOCK + tq, LANES), BF16), pltpu.VMEM((SWA_BLOCK + tq, LANES), BF16),
                            pltpu.VMEM((q_heads, SWA_BLOCK, 2 * SWA_BLOCK), F32),
                            pltpu.VMEM((group, tq, LANES), BF16), pltpu.VMEM((group, tq, LANES), BF16)]),
        compiler_params=_params(("arbitrary", "arbitrary"), 48),
        name="swa",
    )(sinks, x, gates_t, yk, yk, ln.reshape(1, d), w_all, b_all, wo, b_out.reshape(1, d))


def _router_kernel(x_ref, ln_ref, whi_ref, wlo_ref, idx_ref, gate_ref, hp_ref, pos_ref, count_ref, tri_ref):
    nt = (((1,), (1,)), ((), ()))
    h = _rms(x_ref[...], ln_ref[...])
    h_hi = h.astype(BF16)
    h_lo = (h - h_hi.astype(F32)).astype(BF16)
    w_hi, w_lo = whi_ref[...], wlo_ref[...]
    logits = (lax.dot_general(w_hi, h_hi, nt, preferred_element_type=F32)
              + lax.dot_general(w_hi, h_lo, nt, preferred_element_type=F32)
              + lax.dot_general(w_lo, h_hi, nt, preferred_element_type=F32))
    n_exp = logits.shape[0]
    eid = lax.broadcasted_iota(jnp.int32, logits.shape, 0)
    m1 = jnp.max(logits, axis=0, keepdims=True)
    i1 = jnp.min(jnp.where(logits == m1, eid, n_exp), axis=0, keepdims=True)
    rest = jnp.where(eid == i1, -jnp.inf, logits)
    m2 = jnp.max(rest, axis=0, keepdims=True)
    i2 = jnp.min(jnp.where(rest == m2, eid, n_exp), axis=0, keepdims=True)
    e2 = jnp.exp(m2 - m1)
    g1 = 1.0 / (1.0 + e2)
    idx_ref[...] = jnp.concatenate([i1, i2], axis=0)
    gate_ref[...] = jnp.concatenate([g1, e2 * g1], axis=0)
    hp_ref[...] = _pack_bf16_pairs(h)
    tm = logits.shape[1]

    @pl.when(pl.program_id(0) == 0)
    def _():
        count_ref[...] = jnp.zeros_like(count_ref)
        r = lax.broadcasted_iota(jnp.int32, (tm, tm), 0)
        c = lax.broadcasted_iota(jnp.int32, (tm, tm), 1)
        tri_ref[...] = jnp.where(r < c, 1.0, 0.0).astype(BF16)

    pick1 = jnp.where(eid == i1, 1.0, 0.0)
    pick2 = jnp.where(eid == i2, 1.0, 0.0)
    picks = pick1 + pick2
    before = _bdot(picks.astype(BF16), tri_ref[...]) + count_ref[:, 0:1]
    pos_ref[...] = jnp.concatenate([jnp.sum(pick1 * before, axis=0, keepdims=True),
                                    jnp.sum(pick2 * before, axis=0, keepdims=True)], axis=0).astype(jnp.int32)
    count_ref[...] = count_ref[...] + jnp.sum(picks, axis=1, keepdims=True)


def _router(x2, ln, w_router, *, tm=512):
    ntok, d = x2.shape
    n_exp = w_router.shape[1]
    tm = min(tm, ntok)
    wt = w_router.T
    w_hi = wt.astype(BF16)
    w_lo = (wt - w_hi.astype(F32)).astype(BF16)
    return pl.pallas_call(
        _router_kernel,
        out_shape=(jax.ShapeDtypeStruct((TOP_K, ntok), jnp.int32), jax.ShapeDtypeStruct((TOP_K, ntok), F32),
                   jax.ShapeDtypeStruct((ntok, d // 2), jnp.uint32),
                   jax.ShapeDtypeStruct((TOP_K, ntok), jnp.int32), jax.ShapeDtypeStruct((n_exp, LANES), F32)),
        grid=(ntok // tm,),
        in_specs=[pl.BlockSpec((tm, d), lambda i: (i, 0)),
                  pl.BlockSpec((1, d), lambda i: (0, 0)),
                  pl.BlockSpec((n_exp, d), lambda i: (0, 0)),
                  pl.BlockSpec((n_exp, d), lambda i: (0, 0))],
        out_specs=(pl.BlockSpec((TOP_K, tm), lambda i: (0, i)), pl.BlockSpec((TOP_K, tm), lambda i: (0, i)),
                   pl.BlockSpec((tm, d // 2), lambda i: (i, 0)),
                   pl.BlockSpec((TOP_K, tm), lambda i: (0, i)), pl.BlockSpec((n_exp, LANES), lambda i: (0, 0))),
        scratch_shapes=[pltpu.VMEM((tm, tm), BF16)],
        compiler_params=_params(("arbitrary",), 32),
        name="moe_router",
    )(x2, ln.reshape(1, d), w_hi, w_lo)


def _moe_plan(idx, pos, counts, tile):
    n_exp = counts.shape[0]
    nslots = idx.size
    counts = counts[:, 0].astype(jnp.int32)
    ends = jnp.cumsum(counts)
    offs = ends - counts
    experts = jnp.arange(n_exp, dtype=jnp.int32).reshape(n_exp, 1, 1)
    rank = pos + jnp.sum(jnp.where(idx[None] == experts, offs.reshape(n_exp, 1, 1), 0), axis=0)
    n_tiles = nslots // tile
    n_visits = n_tiles + n_exp - 1
    first_tile = offs // tile
    last_tile = (ends - 1) // tile
    nvis = jnp.where(counts > 0, last_tile - first_tile + 1, 0)
    vend = jnp.cumsum(nvis)
    vstart = vend - nvis
    total = vend[-1]
    v = jnp.arange(n_visits, dtype=jnp.int32)
    vc = jnp.minimum(v, total - 1)
    e = jnp.minimum(jnp.sum((vc[:, None] >= vend[None, :]).astype(jnp.int32), axis=1), n_exp - 1)
    sel = (e[:, None] == jnp.arange(n_exp, dtype=jnp.int32)[None, :]).astype(jnp.int32)
    pick = lambda a: jnp.sum(sel * a[None, :], axis=1)
    tile_id = pick(first_tile) + vc - pick(vstart)
    lo = jnp.maximum(pick(offs), tile_id * tile) - tile_id * tile
    hi = jnp.minimum(pick(ends), (tile_id + 1) * tile) - tile_id * tile
    valid = v < total
    lo = jnp.where(valid, lo, 0)
    hi = jnp.where(valid, hi, 0)
    prev_tile = jnp.concatenate([jnp.full((1,), -1, jnp.int32), tile_id[:-1]])
    first = (valid & (tile_id != prev_tile)).astype(jnp.int32)
    next_tile = jnp.concatenate([tile_id[1:], jnp.full((1,), -1, jnp.int32)])
    last = (valid & ((tile_id != next_tile) | (v == total - 1))).astype(jnp.int32)
    meta = jnp.stack([tile_id, e, lo, hi, first, last]).astype(jnp.int32)
    return rank.astype(jnp.int32), meta


def _pack_bf16_pairs(h):
    half = h.shape[1] // 2
    return _pack_halves(h[:, :half], h[:, half:])


def _pack_halves(lo, hi):
    bits = lambda a: lax.bitcast_convert_type(a.astype(BF16).astype(F32), jnp.uint32)
    return (bits(hi) & jnp.uint32(0xFFFF0000)) | (bits(lo) >> 16)


def _unpack_pairs_f32(u):
    lo = lax.bitcast_convert_type(u << 16, F32)
    hi = lax.bitcast_convert_type(u & jnp.uint32(0xFFFF0000), F32)
    return jnp.concatenate([lo, hi], axis=1)


def _unpack_bf16_pairs(u):
    return _unpack_pairs_f32(u).astype(BF16)


SC_CORES = 2
SC_SUBCORES = 16
SC_INDEX_WINDOW = 128


def _sc_mesh():
    return plsc.VectorSubcoreMesh(core_axis_name="c", subcore_axis_name="s")


def _sc_worker_id():
    return lax.axis_index("c") * SC_SUBCORES + lax.axis_index("s")


def _sc_scatter_rows(src, rank, nrows):
    ntok, width = src.shape
    win = SC_INDEX_WINDOW
    per = ntok // (SC_CORES * SC_SUBCORES)

    @pl.kernel(out_type=jax.ShapeDtypeStruct((nrows, width), src.dtype), mesh=_sc_mesh(),
               scratch_types=[pltpu.VMEM((1, win), jnp.int32)] * TOP_K + [pltpu.VMEM((win, width), src.dtype)],
               name="moe_dispatch_sc")
    def scatter(src_hbm, rank_hbm, o_hbm, *scratch):
        idx_vmem, buf = scratch[:TOP_K], scratch[TOP_K]
        wid = _sc_worker_id()

        @pl.loop(0, per // win)
        def _(blk):
            base = wid * per + blk * win
            for k in range(TOP_K):
                pltpu.sync_copy(rank_hbm.at[pl.ds(k, 1), pl.ds(base, win)], idx_vmem[k])
            pltpu.sync_copy(src_hbm.at[pl.ds(base, win)], buf)
            for k in range(TOP_K):
                pltpu.sync_copy(buf, o_hbm.at[idx_vmem[k].at[0]])

    return scatter(src, rank)


def _sc_gather_rows(src, rank, *, sub=32):
    nslot, ntok = rank.shape
    n = nslot * ntok
    width = src.shape[1]
    win = SC_INDEX_WINDOW
    per = n // (SC_CORES * SC_SUBCORES)
    nsub = win // sub

    @pl.kernel(out_type=jax.ShapeDtypeStruct((n, width), src.dtype), mesh=_sc_mesh(),
               scratch_types=[pltpu.VMEM((1, win), jnp.int32)] + [pltpu.VMEM((sub, width), src.dtype)] * 2
               + [pltpu.SemaphoreType.DMA] * 4,
               name="moe_gather_sc")
    def gather(src_hbm, idx_hbm, o_hbm, i_vmem, buf0, buf1, g0, g1, w0, w1):
        bufs, gsem, wsem = (buf0, buf1), (g0, g1), (w0, w1)
        wid = _sc_worker_id()

        @pl.loop(0, per // win)
        def _(blk):
            base = wid * per + blk * win
            pltpu.sync_copy(idx_hbm.at[pl.ds(base // ntok, 1), pl.ds(base % ntok, win)], i_vmem)
            gathers = [pltpu.make_async_copy(src_hbm.at[i_vmem.at[0, pl.ds(sub * j, sub)]], bufs[j % 2], gsem[j % 2])
                       for j in range(nsub)]
            writes = [pltpu.make_async_copy(bufs[j % 2], o_hbm.at[pl.ds(base + sub * j, sub)], wsem[j % 2])
                      for j in range(nsub)]
            gathers[0].start()
            for j in range(nsub):
                if j + 1 < nsub:
                    if j >= 1:
                        writes[j - 1].wait()
                    gathers[j + 1].start()
                gathers[j].wait()
                writes[j].start()
            writes[nsub - 2].wait()
            writes[nsub - 1].wait()

    return gather(src, rank)


MXU_N = 256


def _expert_kernel(meta_ref, x_ref, wg_ref, wu_ref, wd_ref, o_ref, acc_ref, xb_ref, act_ref, wgb_ref, wub_ref,
                   wdb_ref, *, ts):
    v = pl.program_id(0)
    hc = pl.program_id(1)
    lo, hi, first, last = meta_ref[2, v], meta_ref[3, v], meta_ref[4, v], meta_ref[5, v]
    tile, d = acc_ref.shape
    nsub = tile // ts
    th = wgb_ref.shape[1]
    wide = (hi - lo) * 2 > tile

    @pl.when(hc == 0)
    def _():
        for sub in range(nsub):
            xb_ref[sub * ts:(sub + 1) * ts, :] = _unpack_bf16_pairs(x_ref[sub * ts:(sub + 1) * ts, :])

    fresh = (first == 1) & (hc == 0)

    @pl.when(fresh & jnp.logical_not(wide))
    def _():
        acc_ref[...] = jnp.zeros_like(acc_ref)

    finishing = (last == 1) & (hc == pl.num_programs(1) - 1)
    half_chunks = d // 2 // MXU_N

    def whole_tile(assign, finish):
        rows = lax.broadcasted_iota(jnp.int32, (tile, 1), 0)
        mine = (rows >= lo) & (rows < hi)
        for n in range(th // MXU_N):
            cols = slice(n * MXU_N, (n + 1) * MXU_N)
            gate = _bdot(xb_ref[...], wg_ref[0, :, cols].astype(BF16))
            up = _bdot(xb_ref[...], wu_ref[0, :, cols].astype(BF16))
            act_ref[:, cols] = (_silu(gate) * up).astype(BF16)
        for n in range(d // MXU_N):
            cols = slice(n * MXU_N, (n + 1) * MXU_N)
            part = jnp.where(mine, _bdot(act_ref[...], wd_ref[0, :, cols].astype(BF16)), 0.0)
            total = part if assign else acc_ref[:, cols] + part
            if finish and n >= half_chunks:
                low = slice((n - half_chunks) * MXU_N, (n - half_chunks + 1) * MXU_N)
                o_ref[:, low] = _pack_halves(acc_ref[:, low], total)
            else:
                acc_ref[:, cols] = total

    pl.when(wide & fresh)(functools.partial(whole_tile, True, False))
    pl.when(wide & jnp.logical_not(fresh) & jnp.logical_not(finishing))(functools.partial(whole_tile, False, False))
    pl.when(wide & jnp.logical_not(fresh) & finishing)(functools.partial(whole_tile, False, True))

    @pl.when(jnp.logical_not(wide) & (hi > lo))
    def _():
        wgb_ref[...] = wg_ref[0].astype(BF16)
        wub_ref[...] = wu_ref[0].astype(BF16)
        wdb_ref[...] = wd_ref[0].astype(BF16)
        for sub in range(nsub):
            r0 = sub * ts

            @pl.when((lo < r0 + ts) & (hi > r0))
            def _():
                xs = xb_ref[r0:r0 + ts, :]
                act = (_silu(_bdot(xs, wgb_ref[...])) * _bdot(xs, wub_ref[...])).astype(BF16)
                y = _bdot(act, wdb_ref[...])
                rows = r0 + lax.broadcasted_iota(jnp.int32, (ts, 1), 0)
                acc_ref[r0:r0 + ts, :] += jnp.where((rows >= lo) & (rows < hi), y, 0.0)

    @pl.when(finishing & jnp.logical_not(wide))
    def _():
        for sub in range(nsub):
            o_ref[sub * ts:(sub + 1) * ts, :] = _pack_bf16_pairs(acc_ref[sub * ts:(sub + 1) * ts, :])


def _experts(xg, meta, w_gate_up, w_down, *, tile, th=512, ts=256):
    nrows = xg.shape[0]
    n_exp, hidden, d = w_down.shape
    n_hc = hidden // th
    ts = min(ts, tile)
    wgu = w_gate_up
    return pl.pallas_call(
        functools.partial(_expert_kernel, ts=ts),
        out_shape=jax.ShapeDtypeStruct((nrows, d // 2), jnp.uint32),
        grid_spec=pltpu.PrefetchScalarGridSpec(
            num_scalar_prefetch=1,
            grid=(meta.shape[1], n_hc),
            in_specs=[pl.BlockSpec((tile, d // 2), lambda v, c, m: (m[0, v], 0)),
                      pl.BlockSpec((1, d, th), lambda v, c, m: (m[1, v], 0, c)),
                      pl.BlockSpec((1, d, th), lambda v, c, m: (m[1, v], 0, c + n_hc)),
                      pl.BlockSpec((1, th, d), lambda v, c, m: (m[1, v], c, 0))],
            out_specs=pl.BlockSpec((tile, d // 2), lambda v, c, m: (m[0, v], 0)),
            scratch_shapes=[pltpu.VMEM((tile, d), F32), pltpu.VMEM((tile, d), BF16), pltpu.VMEM((tile, th), BF16),
                            pltpu.VMEM((d, th), BF16), pltpu.VMEM((d, th), BF16), pltpu.VMEM((th, d), BF16)]),
        compiler_params=_params(("arbitrary", "arbitrary"), 56),
        name="moe_experts",
    )(meta, xg, wgu, wgu, w_down)


def _combine_kernel(x_ref, gate_ref, fg_ref, y0_ref, y1_ref, o_ref, *, final_norm):
    g = gate_ref[...]
    out = x_ref[...] + g[:, 0:1] * _unpack_pairs_f32(y0_ref[0]) + g[:, 1:2] * _unpack_pairs_f32(y1_ref[0])
    if final_norm:
        out = _rms(out, fg_ref[...])
    o_ref[...] = out


def _combine(x2, gates_t, yk, final_gain, *, tm=512):
    ntok, d = x2.shape
    tm = min(tm, ntok)
    final_norm = final_gain is not None
    fg = (final_gain if final_norm else jnp.ones((d,), F32)).reshape(1, d)
    return pl.pallas_call(
        functools.partial(_combine_kernel, final_norm=final_norm),
        out_shape=jax.ShapeDtypeStruct((ntok, d), F32),
        grid=(ntok // tm,),
        in_specs=[pl.BlockSpec((tm, d), lambda i: (i, 0)),
                  pl.BlockSpec((tm, TOP_K), lambda i: (i, 0)),
                  pl.BlockSpec((1, d), lambda i: (0, 0)),
                  pl.BlockSpec((1, tm, d // 2), lambda i: (0, i, 0)),
                  pl.BlockSpec((1, tm, d // 2), lambda i: (1, i, 0))],
        out_specs=pl.BlockSpec((tm, d), lambda i: (i, 0)),
        compiler_params=_params(("parallel",), 40),
        name="moe_combine",
    )(x2, gates_t, fg, yk, yk)


def _moe_routed(x, ln, w_router, w_gate_up, w_down, *, tile=2048):
    bsz, seqlen, d = x.shape
    ntok = bsz * seqlen
    tile = min(tile, TOP_K * ntok)
    x2 = x.reshape(ntok, d)
    idx, gates, hp, pos, counts = _router(x2, ln, w_router)
    rank, meta = _moe_plan(idx, pos, counts, tile)
    xg = _sc_scatter_rows(hp, rank, TOP_K * ntok)
    y = _experts(xg, meta, w_gate_up, w_down, tile=tile)
    yk = _sc_gather_rows(y, rank, sub=64).reshape(TOP_K, ntok, d // 2)
    return gates.T, yk


def _moe_layer(x, ln, w_router, w_gate_up, w_down, *, final_gain=None, tile=2048):
    bsz, seqlen, d = x.shape
    gates_t, yk = _moe_routed(x, ln, w_router, w_gate_up, w_down, tile=tile)
    out = _combine(x.reshape(bsz * seqlen, d), gates_t, yk, final_gain)
    return out.reshape(bsz, seqlen, d)


def kernel(x, l0_ln1, l0_s5_lam_re, l0_s5_lam_im, l0_s5_log_dt, l0_s5_b_re, l0_s5_b_im, l0_s5_c_re, l0_s5_c_im, l0_s5_d, l0_s5_w_glu, l0_s5_b_glu, l0_ln2, l0_ffn_w_gate_up, l0_ffn_w_down, l1_ln1, l1_gla_w_in, l1_gla_w_g2, l1_gla_b_g2, l1_gla_norm, l1_gla_w_out, l1_ln2, l1_moe_router, l1_moe_w_gate_up, l1_moe_w_down, l2_ln1, l2_swa_w_qkv, l2_swa_b_qkv, l2_swa_sinks, l2_swa_w_out, l2_swa_b_out, l2_ln2, l2_ffn_w_gate_up, l2_ffn_w_down, l3_ln1, l3_s5_lam_re, l3_s5_lam_im, l3_s5_log_dt, l3_s5_b_re, l3_s5_b_im, l3_s5_c_re, l3_s5_c_im, l3_s5_d, l3_s5_w_glu, l3_s5_b_glu, l3_ln2, l3_moe_router, l3_moe_w_gate_up, l3_moe_w_down, ln_f):
    s5_params = ((l0_s5_lam_re, l0_s5_lam_im, l0_s5_log_dt, l0_s5_b_re, l0_s5_b_im, l0_s5_c_re, l0_s5_c_im),
                 (l3_s5_lam_re, l3_s5_lam_im, l3_s5_log_dt, l3_s5_b_re, l3_s5_b_im, l3_s5_c_re, l3_s5_c_im))
    s5_ops = jax.vmap(_s5_operators)(*(jnp.stack(pair) for pair in zip(*s5_params)))
    x = _s5_layer(x, l0_ln1, s5_ops, 0, l0_s5_d, l0_s5_w_glu, l0_s5_b_glu)
    x = _dense_ffn_layer(x, l0_ln2, l0_ffn_w_gate_up, l0_ffn_w_down)
    x = _gla_layer(x, l1_ln1, l1_gla_w_in, l1_gla_w_g2, l1_gla_b_g2, l1_gla_norm, l1_gla_w_out)
    gates_t, yk = _moe_routed(x, l1_ln2, l1_moe_router, l1_moe_w_gate_up, l1_moe_w_down)
    x = _swa_layer(x, gates_t, yk, l2_ln1, l2_swa_w_qkv, l2_swa_b_qkv, l2_swa_sinks, l2_swa_w_out, l2_swa_b_out)
    x = _dense_ffn_layer(x, l2_ln2, l2_ffn_w_gate_up, l2_ffn_w_down)
    x = _s5_layer(x, l3_ln1, s5_ops, 1, l3_s5_d, l3_s5_w_glu, l3_s5_b_glu)
    return _moe_layer(x, l3_ln2, l3_moe_router, l3_moe_w_gate_up, l3_moe_w_down, final_gain=ln_f)
```

```python
import functools
import math

import jax
import jax.numpy as jnp
from jax import lax
from jax.experimental import pallas as pl
from jax.experimental.pallas import tpu as pltpu
from jax.experimental.pallas import tpu_sc as plsc

F32 = jnp.float32
BF16 = jnp.bfloat16
EPS = 1e-6
LANES = 128
MIB = 1 << 20

S5_GROUP = 16
S5_CHUNK = 16
S5_SLAB_GROUPS = LANES // S5_GROUP
S5_PITCH_PAD = 8
S5_SCAN_UNROLL = 8
S5_CAUSAL_BANDS = 8

GLA_HEADS = 4
GLA_GATE_RANK = 16
GLA_GATE_NORM = 16.0
GLA_CHUNK = 64

SWA_HEAD_DIM = 64
SWA_KV_HEADS = 2
SWA_WINDOW = 128
SWA_BLOCK = 128
MASK_VALUE = -1e30

TOP_K = 2


def _params(semantics, vmem_mib):
    return pltpu.CompilerParams(dimension_semantics=semantics, vmem_limit_bytes=vmem_mib * MIB)


def _resident(block_shape, index_map):
    return pl.BlockSpec(block_shape, index_map, pipeline_mode=pl.Buffered(1))


def _rms(xf, gain):
    return xf * lax.rsqrt(jnp.mean(xf * xf, axis=-1, keepdims=True) + EPS) * gain


def _gelu_tanh(x):
    return 0.5 * x * (1.0 + jnp.tanh(math.sqrt(2.0 / math.pi) * (x + 0.044715 * (x * x * x))))


def _silu(x):
    return x * jax.nn.sigmoid(x)


def _bdot(a, b):
    return jnp.dot(a, b, preferred_element_type=F32)


S5_ROW_TILE = 1024


def _s5_norm_kernel(x_ref, g_ref, o_ref, scr_ref, *, nloc):
    h = _rms(x_ref[0], g_ref[...])
    nslab = scr_ref.shape[0]
    for c in range(nslab):
        scr_ref[c] = h[:, c * LANES:(c + 1) * LANES]
    for s in range(S5_CHUNK):
        rows = pl.ds(s, nloc, stride=S5_CHUNK)
        o_ref[0, s] = jnp.concatenate([scr_ref[c, rows, :] for c in range(nslab)], axis=1).astype(o_ref.dtype)


def _s5_norm(x, gain):
    bsz, seqlen, d = x.shape
    nch = seqlen // S5_CHUNK
    tm = min(S5_ROW_TILE, seqlen)
    nloc = tm // S5_CHUNK
    return pl.pallas_call(
        functools.partial(_s5_norm_kernel, nloc=nloc),
        out_shape=jax.ShapeDtypeStruct((bsz, S5_CHUNK, nch, d), BF16),
        grid=(bsz, seqlen // tm),
        in_specs=[pl.BlockSpec((1, tm, d), lambda b, i: (b, i, 0)),
                  pl.BlockSpec((1, d), lambda b, i: (0, 0))],
        out_specs=pl.BlockSpec((1, S5_CHUNK, nloc, d), lambda b, i: (b, 0, i, 0)),
        scratch_shapes=[pltpu.VMEM((d // LANES, tm, LANES), F32)],
        compiler_params=_params(("parallel", "parallel"), 32),
        name="s5_norm",
    )(x, gain.reshape(1, d))


def _tiling_matrix(rows, cols):
    p = lax.broadcasted_iota(jnp.int32, (rows, cols), 0)
    c = lax.broadcasted_iota(jnp.int32, (rows, cols), 1)
    return jnp.where(c % rows == p, 1.0, 0.0).astype(BF16)


def _same_group(shape, row_group, col_group):
    r = lax.broadcasted_iota(jnp.int32, shape, 0)
    c = lax.broadcasted_iota(jnp.int32, shape, 1)
    return (r // row_group) == (c // col_group)


def _s5_build_operators(vw_ref, mw_ref, toep_ref, win_ref, wout_ref):
    tn = (((0,), (0,)), ((), ()))
    nstate = vw_ref.shape[-1]
    half = S5_SLAB_GROUPS * nstate
    rep_ch = _tiling_matrix(S5_GROUP, LANES)
    rep_st = _tiling_matrix(nstate, half)
    diag_in = _same_group((LANES, half), S5_GROUP, nstate)
    diag_out = _same_group((half, LANES), nstate, S5_GROUP)

    def out_block(q, r):
        e = lax.dot_general(mw_ref[0, 2 * q + r].astype(BF16), rep_ch, tn, preferred_element_type=F32)
        return jnp.where(diag_out, e, 0.0).astype(BF16)

    for a in range(S5_CHUNK):
        for r in range(2):
            e = _bdot(vw_ref[0, 2 * a + r].astype(BF16), rep_st)
            win_ref[a * LANES:(a + 1) * LANES, r * half:(r + 1) * half] = jnp.where(diag_in, e, 0.0).astype(BF16)
            wout_ref[r * half:(r + 1) * half, a * LANES:(a + 1) * LANES] = out_block(a + 1, r)
    b_bar = win_ref[(S5_CHUNK - 1) * LANES:S5_CHUNK * LANES, :]
    taps = [_bdot(b_bar, jnp.concatenate([out_block(0, 0), out_block(0, 1)], axis=0)).astype(BF16)]
    for j in range(1, S5_CHUNK):
        taps.append(_bdot(b_bar, wout_ref[:, (j - 1) * LANES:j * LANES]).astype(BF16))
    zero = jnp.zeros((LANES, LANES), BF16)
    for a in range(S5_CHUNK):
        for b in range(S5_CHUNK):
            toep_ref[a * LANES:(a + 1) * LANES, b * LANES:(b + 1) * LANES] = taps[b - a] if b >= a else zero


def _s5_conv_kernel(h_ref, vw_ref, mw_ref, a_ref, d_ref, o_ref, s_ref, toep_ref, win_ref, wout_ref,
                    *, nseq, nch):
    pitch = nch + S5_PITCH_PAD
    nl = a_ref.shape[1] // 2

    @pl.when(pl.program_id(1) == 0)
    def _():
        _s5_build_operators(vw_ref, mw_ref, toep_ref, win_ref, wout_ref)

    lhs = jnp.concatenate(
        [jnp.concatenate([h_ref[bl, s] for s in range(S5_CHUNK)], axis=1) for bl in range(nseq)], axis=0)
    bc = _bdot(lhs, win_ref[...])
    for bl in range(nseq):
        for j in range(2 * nl):
            s_ref[j, bl * pitch:bl * pitch + nch, :] = bc[bl * nch:(bl + 1) * nch, j * LANES:(j + 1) * LANES]
    a_re = [a_ref[0, j:j + 1, :] for j in range(nl)]
    a_im = [a_ref[0, nl + j:nl + j + 1, :] for j in range(nl)]

    def step(n, carry):
        p_re, p_im = carry
        rows = pl.ds(n, nseq, stride=pitch)
        n_re, n_im = [], []
        for j in range(nl):
            c_re = s_ref[j, rows, :]
            c_im = s_ref[nl + j, rows, :]
            s_ref[j, rows, :] = p_re[j]
            s_ref[nl + j, rows, :] = p_im[j]
            n_re.append(a_re[j] * p_re[j] - a_im[j] * p_im[j] + c_re)
            n_im.append(a_re[j] * p_im[j] + a_im[j] * p_re[j] + c_im)
        return tuple(n_re), tuple(n_im)

    def steps(m, carry):
        for u in range(S5_SCAN_UNROLL):
            carry = step(m * S5_SCAN_UNROLL + u, carry)
        return carry

    zeros = tuple(jnp.zeros((nseq, LANES), F32) for _ in range(nl))
    lax.fori_loop(0, nch // S5_SCAN_UNROLL, steps, (zeros, zeros))
    x_prev = jnp.concatenate(
        [jnp.concatenate([s_ref[j, bl * pitch:bl * pitch + nch, :] for j in range(2 * nl)], axis=1)
         for bl in range(nseq)], axis=0).astype(BF16)
    band = S5_CHUNK // S5_CAUSAL_BANDS
    y_bands = []
    for q in range(S5_CAUSAL_BANDS):
        kk = (q + 1) * band * LANES
        cols = slice(q * band * LANES, (q + 1) * band * LANES)
        y_bands.append(_bdot(lhs[:, :kk], toep_ref[:kk, cols]) + _bdot(x_prev, wout_ref[:, cols]))
    dskip = d_ref[0]
    for bl in range(nseq):
        for s in range(S5_CHUNK):
            ys = y_bands[s // band][bl * nch:(bl + 1) * nch, (s % band) * LANES:(s % band + 1) * LANES]
            ys = ys + dskip * h_ref[bl, s].astype(F32)
            o_ref[bl, s] = _gelu_tanh(ys).astype(o_ref.dtype)


def _s5_conv(hp, vw, mw, a_pack, d_skip, *, nseq, layer):
    bsz, _, nch, d = hp.shape
    nslab = d // LANES
    kdim = S5_CHUNK * LANES
    sdim = a_pack.shape[2] * LANES
    blk4 = lambda a: pl.BlockSpec((None, 1) + a.shape[2:], lambda c, b: (layer, c, 0, 0, 0))
    return pl.pallas_call(
        functools.partial(_s5_conv_kernel, nseq=nseq, nch=nch),
        out_shape=jax.ShapeDtypeStruct(hp.shape, BF16),
        grid=(nslab, bsz // nseq),
        in_specs=[pl.BlockSpec((nseq, S5_CHUNK, nch, LANES), lambda c, b: (b, 0, 0, c)),
                  blk4(vw), blk4(mw),
                  pl.BlockSpec((None, 1, sdim // LANES, LANES), lambda c, b: (layer, c, 0, 0)),
                  pl.BlockSpec((1, 1, LANES), lambda c, b: (c, 0, 0))],
        out_specs=pl.BlockSpec((nseq, S5_CHUNK, nch, LANES), lambda c, b: (b, 0, 0, c)),
        scratch_shapes=[pltpu.VMEM((sdim // LANES, nseq * (nch + S5_PITCH_PAD), LANES), F32),
                        pltpu.VMEM((kdim, kdim), BF16),
                        pltpu.VMEM((kdim, sdim), BF16),
                        pltpu.VMEM((sdim, kdim), BF16)],
        compiler_params=_params(("arbitrary", "arbitrary"), 56),
        name="s5_conv",
    )(hp, vw, mw, a_pack, d_skip.reshape(nslab, 1, LANES))


def _s5_glu_kernel(y_ref, x_ref, w_ref, b_ref, o_ref, scr_ref, *, nloc):
    nslab = scr_ref.shape[0]
    y = jnp.concatenate([y_ref[0, s] for s in range(S5_CHUNK)], axis=0)
    u = y.astype(F32) * jax.nn.sigmoid(_bdot(y, w_ref[...]) + b_ref[...])
    for s in range(S5_CHUNK):
        rows = pl.ds(s, nloc, stride=S5_CHUNK)
        for c in range(nslab):
            scr_ref[c, rows, :] = u[s * nloc:(s + 1) * nloc, c * LANES:(c + 1) * LANES]
    o_ref[0] = x_ref[0] + jnp.concatenate([scr_ref[c] for c in range(nslab)], axis=1)


def _s5_glu(yp, x, w_glu, b_glu):
    bsz, seqlen, d = x.shape
    tm = min(S5_ROW_TILE, seqlen)
    nloc = tm // S5_CHUNK
    return pl.pallas_call(
        functools.partial(_s5_glu_kernel, nloc=nloc),
        out_shape=jax.ShapeDtypeStruct(x.shape, F32),
        grid=(bsz, seqlen // tm),
        in_specs=[pl.BlockSpec((1, S5_CHUNK, nloc, d), lambda b, i: (b, 0, i, 0)),
                  pl.BlockSpec((1, tm, d), lambda b, i: (b, i, 0)),
                  _resident((d, d), lambda b, i: (0, 0)),
                  pl.BlockSpec((1, d), lambda b, i: (0, 0))],
        out_specs=pl.BlockSpec((1, tm, d), lambda b, i: (b, i, 0)),
        scratch_shapes=[pltpu.VMEM((d // LANES, tm, LANES), F32)],
        compiler_params=_params(("parallel", "parallel"), 40),
        name="s5_glu",
    )(yp, x, w_glu.astype(BF16), b_glu.reshape(1, d))


def _s5_operators(lam_re, lam_im, log_dt, b_re, b_im, c_re, c_im):
    ngroups, nstate = lam_re.shape
    gpc = S5_SLAB_GROUPS
    nslab = ngroups // gpc
    dt = jnp.exp(log_dt)[:, None]
    j = jnp.arange(S5_CHUNK + 1, dtype=F32)[:, None, None]
    mag = jnp.exp(j * (lam_re * dt)[None])
    ang = j * (lam_im * dt)[None]
    pw_re, pw_im = mag * jnp.cos(ang), mag * jnp.sin(ang)
    num_re, num_im = pw_re[1] - 1.0, pw_im[1]
    den = lam_re * lam_re + lam_im * lam_im
    f_re = (num_re * lam_re + num_im * lam_im) / den
    f_im = (num_im * lam_re - num_re * lam_im) / den
    bb_re = f_re[..., None] * b_re - f_im[..., None] * b_im
    bb_im = f_re[..., None] * b_im + f_im[..., None] * b_re
    jr = (S5_CHUNK - 1) - jnp.arange(S5_CHUNK, dtype=F32)[:, None, None]
    mag_r = jnp.exp(jr * (lam_re * dt)[None])
    ang_r = jr * (lam_im * dt)[None]
    rev_re, rev_im = mag_r * jnp.cos(ang_r), mag_r * jnp.sin(ang_r)
    slabbed = lambda a: a.reshape(a.shape[0], nslab, gpc, nstate).transpose(1, 0, 2, 3)
    rv_re, rv_im = slabbed(rev_re)[:, :, :, None, :], slabbed(rev_im)[:, :, :, None, :]
    bt_re = bb_re.transpose(0, 2, 1).reshape(nslab, 1, gpc, S5_GROUP, nstate)
    bt_im = bb_im.transpose(0, 2, 1).reshape(nslab, 1, gpc, S5_GROUP, nstate)
    vw = jnp.stack([rv_re * bt_re - rv_im * bt_im, rv_re * bt_im + rv_im * bt_re], axis=2)
    vw = vw.reshape(nslab, 2 * S5_CHUNK, LANES, nstate)
    pc_re, pc_im = slabbed(pw_re)[:, :, None, :, :], slabbed(pw_im)[:, :, None, :, :]
    ct_re = c_re.reshape(nslab, gpc, S5_GROUP, nstate).transpose(0, 2, 1, 3)[:, None]
    ct_im = c_im.reshape(nslab, gpc, S5_GROUP, nstate).transpose(0, 2, 1, 3)[:, None]
    mw = jnp.stack([ct_re * pc_re - ct_im * pc_im, -(ct_re * pc_im + ct_im * pc_re)], axis=2)
    mw = mw.reshape(nslab, 2 * (S5_CHUNK + 1), S5_GROUP, gpc * nstate)
    half = gpc * nstate // LANES
    a_pack = jnp.concatenate([pw_re[S5_CHUNK].reshape(nslab, half, LANES),
                              pw_im[S5_CHUNK].reshape(nslab, half, LANES)], axis=1)
    return vw, mw, a_pack


def _s5_layer(x, ln, operators, layer, d_skip, w_glu, b_glu, *, nseq=4):
    vw, mw, a_pack = operators
    hp = _s5_norm(x, ln)
    yp = _s5_conv(hp, vw, mw, a_pack, d_skip, nseq=min(nseq, x.shape[0]), layer=layer)
    return _s5_glu(yp, x, w_glu, b_glu)


def _dense_ffn_kernel(x_ref, g_ref, wg_ref, wu_ref, wd_ref, o_ref):
    xf = x_ref[...]
    h = _rms(xf, g_ref[...]).astype(BF16)
    act = (_silu(_bdot(h, wg_ref[...])) * _bdot(h, wu_ref[...])).astype(BF16)
    o_ref[...] = xf + _bdot(act, wd_ref[...])


def _dense_ffn_layer(x, ln, w_gate_up, w_down, *, tm=512):
    bsz, seqlen, d = x.shape
    ntok = bsz * seqlen
    hidden = w_down.shape[0]
    tm = min(tm, ntok)
    wgu = w_gate_up.astype(BF16)
    out = pl.pallas_call(
        _dense_ffn_kernel,
        out_shape=jax.ShapeDtypeStruct((ntok, d), F32),
        grid=(ntok // tm,),
        in_specs=[pl.BlockSpec((tm, d), lambda i: (i, 0)),
                  pl.BlockSpec((1, d), lambda i: (0, 0)),
                  _resident((d, hidden), lambda i: (0, 0)),
                  _resident((d, hidden), lambda i: (0, 1)),
                  _resident((hidden, d), lambda i: (0, 0))],
        out_specs=pl.BlockSpec((tm, d), lambda i: (i, 0)),
        compiler_params=_params(("parallel",), 56),
        name="dense_ffn",
    )(x.reshape(ntok, d), ln.reshape(1, d), wgu, wgu, w_down.astype(BF16))
    return out.reshape(bsz, seqlen, d)


def _log_sigmoid(z):
    return jnp.minimum(z, 0.0) - jnp.log(1.0 + jnp.exp(-jnp.abs(z)))


def _gla_kernel(x_ref, ln_ref, wm_ref, wgl_ref, wg2_ref, bg2_ref, gn_ref, wo_ref, o_ref, st_ref,
                *, tq, dk, dv, heads):
    hdk, hdv = dk // heads, dv // heads
    chunk = GLA_CHUNK
    nt = (((1,), (1,)), ((), ()))
    tn = (((0,), (0,)), ((), ()))

    @pl.when(pl.program_id(1) == 0)
    def _():
        st_ref[...] = jnp.zeros_like(st_ref)

    xf = x_ref[0]
    h = _rms(xf, ln_ref[...]).astype(BF16)
    proj = _bdot(h, wm_ref[...])
    glow = _bdot(h, wgl_ref[...]).astype(BF16)
    la = _log_sigmoid(_bdot(glow, wg2_ref[...]) + bg2_ref[...]) * (1.0 / GLA_GATE_NORM)
    row = lax.broadcasted_iota(jnp.int32, (chunk, chunk), 0)
    col = lax.broadcasted_iota(jnp.int32, (chunk, chunk), 1)
    causal = row >= col
    tri = jnp.where(causal, 1.0, 0.0).astype(BF16)
    scale = hdk ** -0.5
    outs = []
    for c in range(tq // chunk):
        r0 = c * chunk
        la_c = la[r0:r0 + chunk, :]
        la_hi = la_c.astype(BF16)
        la_lo = (la_c - la_hi.astype(F32)).astype(BF16)
        gcum_all = _bdot(tri, la_hi) + _bdot(tri, la_lo)
        head_out = []
        for hd in range(heads):
            gcum = gcum_all[:, hd * hdk:(hd + 1) * hdk]
            g_last = gcum[chunk - 1:chunk, :]
            q_c = proj[r0:r0 + chunk, hd * hdk:(hd + 1) * hdk] * scale
            k_c = proj[r0:r0 + chunk, dk + hd * hdk:dk + (hd + 1) * hdk]
            v_c = proj[r0:r0 + chunk, 2 * dk + hd * hdv:2 * dk + (hd + 1) * hdv].astype(BF16)
            q_s = (q_c * jnp.exp(gcum)).astype(BF16)
            k_s = (k_c * jnp.exp(-gcum)).astype(BF16)
            k_end = (k_c * jnp.exp(g_last - gcum)).astype(BF16)
            scores = lax.dot_general(q_s, k_s, nt, preferred_element_type=F32)
            scores = jnp.where(causal, scores, 0.0).astype(BF16)
            state_t = st_ref[hd]
            o = _bdot(scores, v_c) + lax.dot_general(q_s, state_t.astype(BF16), nt,
                                                     preferred_element_type=F32)
            kv_t = lax.dot_general(v_c, k_end, tn, preferred_element_type=F32)
            st_ref[hd] = state_t * jnp.exp(g_last) + kv_t
            head_out.append(o * lax.rsqrt(jnp.mean(o * o, axis=-1, keepdims=True) + EPS))
        outs.append(jnp.concatenate(head_out, axis=1))
    o_all = jnp.concatenate(outs, axis=0)
    r = proj[:, 2 * dk + dv:]
    o_all = (o_all * gn_ref[...] * _silu(r)).astype(BF16)
    o_ref[0] = xf + _bdot(o_all, wo_ref[...])


def _gla_layer(x, ln, w_in, w_g2, b_g2, g_norm, w_out, *, tq=256):
    bsz, seqlen, d = x.shape
    dk = w_g2.shape[1]
    dv = w_out.shape[0]
    nmain = 2 * dk + 2 * dv
    tq = min(tq, seqlen)
    w_main = w_in[:, :nmain].astype(BF16)
    w_glow = jnp.pad(w_in[:, nmain:], ((0, 0), (0, LANES - GLA_GATE_RANK))).astype(BF16)
    w_g2p = jnp.pad(w_g2, ((0, LANES - GLA_GATE_RANK), (0, 0))).astype(BF16)
    hdk, hdv = dk // GLA_HEADS, dv // GLA_HEADS
    const = lambda b, t: (0, 0)
    return pl.pallas_call(
        functools.partial(_gla_kernel, tq=tq, dk=dk, dv=dv, heads=GLA_HEADS),
        out_shape=jax.ShapeDtypeStruct(x.shape, F32),
        grid=(bsz, seqlen // tq),
        in_specs=[pl.BlockSpec((1, tq, d), lambda b, t: (b, t, 0)),
                  pl.BlockSpec((1, d), const),
                  _resident((d, nmain), const),
                  _resident((d, LANES), const),
                  _resident((LANES, dk), const),
                  pl.BlockSpec((1, dk), const),
                  pl.BlockSpec((1, dv), const),
                  _resident((dv, d), const)],
        out_specs=pl.BlockSpec((1, tq, d), lambda b, t: (b, t, 0)),
        scratch_shapes=[pltpu.VMEM((GLA_HEADS, hdv, hdk), F32)],
        compiler_params=_params(("parallel", "arbitrary"), 48),
        name="gla",
    )(x, ln.reshape(1, d), w_main, w_glow, w_g2p, b_g2.reshape(1, dk), g_norm.reshape(1, dv),
      w_out.astype(BF16))


LOG2E = math.log2(math.e)
SWA_SLOT_UNROLL = 4


def _swa_kernel(sink_ref, x_ref, gate_ref, y0_ref, y1_ref, ln_ref, wqkv_ref, bqkv_ref, wo_ref, bo_ref, o_ref,
                k_ref, v_ref, bias_ref, q_ref, a_ref, *, tq, q_heads):
    group = q_heads // SWA_KV_HEADS
    blk = SWA_BLOCK
    nt = (((1,), (1,)), ((), ()))
    b = pl.program_id(0)
    t = pl.program_id(1)
    nq = group * LANES

    @pl.when((b == 0) & (t == 0))
    def _():
        qi = lax.broadcasted_iota(jnp.int32, (blk, 2 * blk), 0)
        kj = lax.broadcasted_iota(jnp.int32, (blk, 2 * blk), 1)
        dist = qi + blk - kj
        in_window = (dist >= 0) & (dist < SWA_WINDOW)
        for hq in range(q_heads):
            slope = 2.0 ** (-8.0 * (hq + 1) / q_heads)
            bias_ref[hq] = jnp.where(in_window, -(slope * LOG2E) * dist.astype(F32), MASK_VALUE)

    @pl.when(t == 0)
    def _():
        k_ref[0:blk, :] = jnp.zeros((blk, LANES), BF16)
        v_ref[0:blk, :] = jnp.zeros((blk, LANES), BF16)

    g = gate_ref[...]
    xf = x_ref[0] + g[:, 0:1] * _unpack_pairs_f32(y0_ref[0]) + g[:, 1:2] * _unpack_pairs_f32(y1_ref[0])
    h = _rms(xf, ln_ref[...]).astype(BF16)
    qkv = _bdot(h, wqkv_ref[...]) + bqkv_ref[...]
    for j in range(group):
        q_ref[j] = qkv[:, j * LANES:(j + 1) * LANES].astype(BF16)
    k_ref[blk:blk + tq, :] = qkv[:, nq:nq + LANES].astype(BF16)
    v_ref[blk:blk + tq, :] = qkv[:, nq + LANES:nq + 2 * LANES].astype(BF16)
    kj_row = lax.broadcasted_iota(jnp.int32, (1, 2 * blk), 1)
    no_prev = jnp.where(kj_row < blk, jnp.where(t == 0, MASK_VALUE, 0.0), 0.0)
    low_half = lax.broadcasted_iota(jnp.int32, (1, LANES), 1) < SWA_HEAD_DIM
    halves = (low_half, jnp.logical_not(low_half))

    def slots(jj, carry):
        for u in range(SWA_SLOT_UNROLL):
            j = jj * SWA_SLOT_UNROLL + u
            for i in range(tq // blk):
                r0 = i * blk
                q_slot = q_ref[j, r0:r0 + blk, :]
                outs = []
                for kh in range(SWA_KV_HEADS):
                    hq = kh * group + j
                    sink = sink_ref[hq] * LOG2E
                    q_h = jnp.where(halves[kh], q_slot, jnp.zeros_like(q_slot))
                    s = lax.dot_general(q_h, k_ref[r0:r0 + 2 * blk, :], nt, preferred_element_type=F32) + bias_ref[hq]
                    if i == 0:
                        s = s + no_prev
                    m = jnp.maximum(jnp.max(s, axis=-1, keepdims=True), sink)
                    p = jnp.exp2(s - m)
                    denom = jnp.sum(p, axis=-1, keepdims=True) + jnp.exp2(sink - m)
                    outs.append(_bdot(p.astype(BF16), v_ref[r0:r0 + 2 * blk, :]) * (1.0 / denom))
                a_ref[j, r0:r0 + blk, :] = jnp.where(low_half, outs[0], outs[1]).astype(BF16)
        return carry

    lax.fori_loop(0, group // SWA_SLOT_UNROLL, slots, 0)
    k_ref[0:blk, :] = k_ref[tq:tq + blk, :]
    v_ref[0:blk, :] = v_ref[tq:tq + blk, :]
    o_all = jnp.concatenate([a_ref[j] for j in range(group)], axis=1)
    o_ref[0] = xf + _bdot(o_all, wo_ref[...]) + bo_ref[...]


def _swa_layer(x, gates_t, yk, ln, w_qkv, b_qkv, sinks, w_out, b_out, *, tq=512):
    bsz, seqlen, d = x.shape
    hd = SWA_HEAD_DIM
    q_heads = sinks.shape[0]
    group = q_heads // SWA_KV_HEADS
    nq = q_heads * hd
    tq = min(tq, seqlen)
    q_scale = hd ** -0.5 * LOG2E
    wq = (w_qkv[:, :nq] * q_scale).reshape(d, SWA_KV_HEADS, group, hd).transpose(0, 2, 1, 3).reshape(d, nq)
    bq = (b_qkv[:nq] * q_scale).reshape(SWA_KV_HEADS, group, hd).transpose(1, 0, 2).reshape(nq)
    w_all = jnp.concatenate([wq, w_qkv[:, nq:]], axis=1).astype(BF16)
    b_all = jnp.concatenate([bq, b_qkv[nq:]]).reshape(1, -1)
    wo = w_out.reshape(SWA_KV_HEADS, group, hd, d).transpose(1, 0, 2, 3).reshape(nq, d).astype(BF16)
    nall = w_all.shape[1]
    steps = seqlen // tq
    const = lambda b, t, s: (0, 0)
    return pl.pallas_call(
        functools.partial(_swa_kernel, tq=tq, q_heads=q_heads),
        out_shape=jax.ShapeDtypeStruct(x.shape, F32),
        grid_spec=pltpu.PrefetchScalarGridSpec(
            num_scalar_prefetch=1,
            grid=(bsz, seqlen // tq),
            in_specs=[pl.BlockSpec((1, tq, d), lambda b, t, s: (b, t, 0)),
                      pl.BlockSpec((tq, TOP_K), lambda b, t, s: (b * steps + t, 0)),
                      pl.BlockSpec((1, tq, d // 2), lambda b, t, s: (0, b * steps + t, 0)),
                      pl.BlockSpec((1, tq, d // 2), lambda b, t, s: (1, b * steps + t, 0)),
                      pl.BlockSpec((1, d), const),
                      _resident((d, nall), const),
                      pl.BlockSpec((1, nall), const),
                      _resident((nq, d), const),
                      pl.BlockSpec((1, d), const)],
            out_specs=pl.BlockSpec((1, tq, d), lambda b, t, s: (b, t, 0)),
            scratch_shapes=[pltpu.VMEM((SWA_BLOCK + tq, LANES), BF16), pltpu.VMEM((SWA_BLOCK + tq, LANES), BF16),
                            pltpu.VMEM((q_heads, SWA_BLOCK, 2 * SWA_BLOCK), F32),
                            pltpu.VMEM((group, tq, LANES), BF16), pltpu.VMEM((group, tq, LANES), BF16)]),
        compiler_params=_params(("arbitrary", "arbitrary"), 48),
        name="swa",
    )(sinks, x, gates_t, yk, yk, ln.reshape(1, d), w_all, b_all, wo, b_out.reshape(1, d))


def _router_kernel(x_ref, ln_ref, whi_ref, wlo_ref, idx_ref, gate_ref, hp_ref, pos_ref, count_ref, tri_ref):
    nt = (((1,), (1,)), ((), ()))
    h = _rms(x_ref[...], ln_ref[...])
    h_hi = h.astype(BF16)
    h_lo = (h - h_hi.astype(F32)).astype(BF16)
    w_hi, w_lo = whi_ref[...], wlo_ref[...]
    logits = (lax.dot_general(w_hi, h_hi, nt, preferred_element_type=F32)
              + lax.dot_general(w_hi, h_lo, nt, preferred_element_type=F32)
              + lax.dot_general(w_lo, h_hi, nt, preferred_element_type=F32))
    n_exp = logits.shape[0]
    eid = lax.broadcasted_iota(jnp.int32, logits.shape, 0)
    m1 = jnp.max(logits, axis=0, keepdims=True)
    i1 = jnp.min(jnp.where(logits == m1, eid, n_exp), axis=0, keepdims=True)
    rest = jnp.where(eid == i1, -jnp.inf, logits)
    m2 = jnp.max(rest, axis=0, keepdims=True)
    i2 = jnp.min(jnp.where(rest == m2, eid, n_exp), axis=0, keepdims=True)
    e2 = jnp.exp(m2 - m1)
    g1 = 1.0 / (1.0 + e2)
    idx_ref[...] = jnp.concatenate([i1, i2], axis=0)
    gate_ref[...] = jnp.concatenate([g1, e2 * g1], axis=0)
    hp_ref[...] = _pack_bf16_pairs(h)
    tm = logits.shape[1]

    @pl.when(pl.program_id(0) == 0)
    def _():
        count_ref[...] = jnp.zeros_like(count_ref)
        r = lax.broadcasted_iota(jnp.int32, (tm, tm), 0)
        c = lax.broadcasted_iota(jnp.int32, (tm, tm), 1)
        tri_ref[...] = jnp.where(r < c, 1.0, 0.0).astype(BF16)

    pick1 = jnp.where(eid == i1, 1.0, 0.0)
    pick2 = jnp.where(eid == i2, 1.0, 0.0)
    picks = pick1 + pick2
    before = _bdot(picks.astype(BF16), tri_ref[...]) + count_ref[:, 0:1]
    pos_ref[...] = jnp.concatenate([jnp.sum(pick1 * before, axis=0, keepdims=True),
                                    jnp.sum(pick2 * before, axis=0, keepdims=True)], axis=0).astype(jnp.int32)
    count_ref[...] = count_ref[...] + jnp.sum(picks, axis=1, keepdims=True)


def _router(x2, ln, w_router, *, tm=512):
    ntok, d = x2.shape
    n_exp = w_router.shape[1]
    tm = min(tm, ntok)
    wt = w_router.T
    w_hi = wt.astype(BF16)
    w_lo = (wt - w_hi.astype(F32)).astype(BF16)
    return pl.pallas_call(
        _router_kernel,
        out_shape=(jax.ShapeDtypeStruct((TOP_K, ntok), jnp.int32), jax.ShapeDtypeStruct((TOP_K, ntok), F32),
                   jax.ShapeDtypeStruct((ntok, d // 2), jnp.uint32),
                   jax.ShapeDtypeStruct((TOP_K, ntok), jnp.int32), jax.ShapeDtypeStruct((n_exp, LANES), F32)),
        grid=(ntok // tm,),
        in_specs=[pl.BlockSpec((tm, d), lambda i: (i, 0)),
                  pl.BlockSpec((1, d), lambda i: (0, 0)),
                  pl.BlockSpec((n_exp, d), lambda i: (0, 0)),
                  pl.BlockSpec((n_exp, d), lambda i: (0, 0))],
        out_specs=(pl.BlockSpec((TOP_K, tm), lambda i: (0, i)), pl.BlockSpec((TOP_K, tm), lambda i: (0, i)),
                   pl.BlockSpec((tm, d // 2), lambda i: (i, 0)),
                   pl.BlockSpec((TOP_K, tm), lambda i: (0, i)), pl.BlockSpec((n_exp, LANES), lambda i: (0, 0))),
        scratch_shapes=[pltpu.VMEM((tm, tm), BF16)],
        compiler_params=_params(("arbitrary",), 32),
        name="moe_router",
    )(x2, ln.reshape(1, d), w_hi, w_lo)


def _moe_plan(idx, pos, counts, tile):
    n_exp = counts.shape[0]
    nslots = idx.size
    counts = counts[:, 0].astype(jnp.int32)
    ends = jnp.cumsum(counts)
    offs = ends - counts
    experts = jnp.arange(n_exp, dtype=jnp.int32).reshape(n_exp, 1, 1)
    rank = pos + jnp.sum(jnp.where(idx[None] == experts, offs.reshape(n_exp, 1, 1), 0), axis=0)
    n_tiles = nslots // tile
    n_visits = n_tiles + n_exp - 1
    first_tile = offs // tile
    last_tile = (ends - 1) // tile
    nvis = jnp.where(counts > 0, last_tile - first_tile + 1, 0)
    vend = jnp.cumsum(nvis)
    vstart = vend - nvis
    total = vend[-1]
    v = jnp.arange(n_visits, dtype=jnp.int32)
    vc = jnp.minimum(v, total - 1)
    e = jnp.minimum(jnp.sum((vc[:, None] >= vend[None, :]).astype(jnp.int32), axis=1), n_exp - 1)
    sel = (e[:, None] == jnp.arange(n_exp, dtype=jnp.int32)[None, :]).astype(jnp.int32)
    pick = lambda a: jnp.sum(sel * a[None, :], axis=1)
    tile_id = pick(first_tile) + vc - pick(vstart)
    lo = jnp.maximum(pick(offs), tile_id * tile) - tile_id * tile
    hi = jnp.minimum(pick(ends), (tile_id + 1) * tile) - tile_id * tile
    valid = v < total
    lo = jnp.where(valid, lo, 0)
    hi = jnp.where(valid, hi, 0)
    prev_tile = jnp.concatenate([jnp.full((1,), -1, jnp.int32), tile_id[:-1]])
    first = (valid & (tile_id != prev_tile)).astype(jnp.int32)
    next_tile = jnp.concatenate([tile_id[1:], jnp.full((1,), -1, jnp.int32)])
    last = (valid & ((tile_id != next_tile) | (v == total - 1))).astype(jnp.int32)
    meta = jnp.stack([tile_id, e, lo, hi, first, last]).astype(jnp.int32)
    return rank.astype(jnp.int32), meta


def _pack_bf16_pairs(h):
    half = h.shape[1] // 2
    return _pack_halves(h[:, :half], h[:, half:])


def _pack_halves(lo, hi):
    bits = lambda a: lax.bitcast_convert_type(a.astype(BF16).astype(F32), jnp.uint32)
    return (bits(hi) & jnp.uint32(0xFFFF0000)) | (bits(lo) >> 16)


def _unpack_pairs_f32(u):
    lo = lax.bitcast_convert_type(u << 16, F32)
    hi = lax.bitcast_convert_type(u & jnp.uint32(0xFFFF0000), F32)
    return jnp.concatenate([lo, hi], axis=1)


def _unpack_bf16_pairs(u):
    return _unpack_pairs_f32(u).astype(BF16)


SC_CORES = 2
SC_SUBCORES = 16
SC_INDEX_WINDOW = 128


def _sc_mesh():
    return plsc.VectorSubcoreMesh(core_axis_name="c", subcore_axis_name="s")


def _sc_worker_id():
    return lax.axis_index("c") * SC_SUBCORES + lax.axis_index("s")


def _sc_scatter_rows(src, rank, nrows):
    ntok, width = src.shape
    win = SC_INDEX_WINDOW
    per = ntok // (SC_CORES * SC_SUBCORES)

    half = win // 2

    @pl.kernel(out_type=jax.ShapeDtypeStruct((nrows, width), src.dtype), mesh=_sc_mesh(),
               scratch_types=[pltpu.VMEM((1, win), jnp.int32)] * TOP_K + [pltpu.VMEM((half, width), src.dtype)] * 2
               + [pltpu.SemaphoreType.DMA] * 4,
               name="moe_dispatch_sc")
    def scatter(src_hbm, rank_hbm, o_hbm, *scratch):
        idx_vmem, bufs = scratch[:TOP_K], scratch[TOP_K:TOP_K + 2]
        load_sem, store_sem = scratch[TOP_K + 2:TOP_K + 4], scratch[TOP_K + 4:TOP_K + 6]
        wid = _sc_worker_id()

        @pl.loop(0, per // win)
        def _(blk):
            base = wid * per + blk * win
            for k in range(TOP_K):
                pltpu.sync_copy(rank_hbm.at[pl.ds(k, 1), pl.ds(base, win)], idx_vmem[k])
            loads = [pltpu.make_async_copy(src_hbm.at[pl.ds(base + j * half, half)], bufs[j], load_sem[j])
                     for j in range(2)]
            stores = [[pltpu.make_async_copy(bufs[j], o_hbm.at[idx_vmem[k].at[0, pl.ds(j * half, half)]], store_sem[j])
                       for k in range(TOP_K)] for j in range(2)]
            for ld in loads:
                ld.start()
            for j in range(2):
                loads[j].wait()
                for st in stores[j]:
                    st.start()
            for j in range(2):
                for st in stores[j]:
                    st.wait()

    return scatter(src, rank)


def _sc_gather_rows(src, rank, *, sub=32):
    nslot, ntok = rank.shape
    n = nslot * ntok
    width = src.shape[1]
    win = SC_INDEX_WINDOW
    per = n // (SC_CORES * SC_SUBCORES)
    nsub = win // sub

    @pl.kernel(out_type=jax.ShapeDtypeStruct((n, width), src.dtype), mesh=_sc_mesh(),
               scratch_types=[pltpu.VMEM((1, win), jnp.int32)] + [pltpu.VMEM((sub, width), src.dtype)] * 2
               + [pltpu.SemaphoreType.DMA] * 4,
               name="moe_gather_sc")
    def gather(src_hbm, idx_hbm, o_hbm, i_vmem, buf0, buf1, g0, g1, w0, w1):
        bufs, gsem, wsem = (buf0, buf1), (g0, g1), (w0, w1)
        wid = _sc_worker_id()

        @pl.loop(0, per // win)
        def _(blk):
            base = wid * per + blk * win
            pltpu.sync_copy(idx_hbm.at[pl.ds(base // ntok, 1), pl.ds(base % ntok, win)], i_vmem)
            gathers = [pltpu.make_async_copy(src_hbm.at[i_vmem.at[0, pl.ds(sub * j, sub)]], bufs[j % 2], gsem[j % 2])
                       for j in range(nsub)]
            writes = [pltpu.make_async_copy(bufs[j % 2], o_hbm.at[pl.ds(base + sub * j, sub)], wsem[j % 2])
                      for j in range(nsub)]
            gathers[0].start()
            for j in range(nsub):
                if j + 1 < nsub:
                    if j >= 1:
                        writes[j - 1].wait()
                    gathers[j + 1].start()
                gathers[j].wait()
                writes[j].start()
            writes[nsub - 2].wait()
            writes[nsub - 1].wait()

    return gather(src, rank)


MXU_N = 256


def _expert_kernel(meta_ref, x_ref, wg_ref, wu_ref, wd_ref, o_ref, acc_ref, xb_ref, act_ref, wgb_ref, wub_ref,
                   wdb_ref, *, ts):
    v = pl.program_id(0)
    hc = pl.program_id(1)
    lo, hi, first, last = meta_ref[2, v], meta_ref[3, v], meta_ref[4, v], meta_ref[5, v]
    tile, d = acc_ref.shape
    nsub = tile // ts
    th = wgb_ref.shape[1]
    wide = (hi - lo) * 2 > tile

    @pl.when(hc == 0)
    def _():
        for sub in range(nsub):
            xb_ref[sub * ts:(sub + 1) * ts, :] = _unpack_bf16_pairs(x_ref[sub * ts:(sub + 1) * ts, :])

    fresh = (first == 1) & (hc == 0)

    @pl.when(fresh & jnp.logical_not(wide))
    def _():
        acc_ref[...] = jnp.zeros_like(acc_ref)

    finishing = (last == 1) & (hc == pl.num_programs(1) - 1)
    half_chunks = d // 2 // MXU_N

    def whole_tile(assign, finish):
        rows = lax.broadcasted_iota(jnp.int32, (tile, 1), 0)
        mine = (rows >= lo) & (rows < hi)
        for n in range(th // MXU_N):
            cols = slice(n * MXU_N, (n + 1) * MXU_N)
            gate = _bdot(xb_ref[...], wg_ref[0, :, cols].astype(BF16))
            up = _bdot(xb_ref[...], wu_ref[0, :, cols].astype(BF16))
            act_ref[:, cols] = (_silu(gate) * up).astype(BF16)
        for n in range(d // MXU_N):
            cols = slice(n * MXU_N, (n + 1) * MXU_N)
            part = jnp.where(mine, _bdot(act_ref[...], wd_ref[0, :, cols].astype(BF16)), 0.0)
            total = part if assign else acc_ref[:, cols] + part
            if finish and n >= half_chunks:
                low = slice((n - half_chunks) * MXU_N, (n - half_chunks + 1) * MXU_N)
                o_ref[:, low] = _pack_halves(acc_ref[:, low], total)
            else:
                acc_ref[:, cols] = total

    pl.when(wide & fresh)(functools.partial(whole_tile, True, False))
    pl.when(wide & jnp.logical_not(fresh) & jnp.logical_not(finishing))(functools.partial(whole_tile, False, False))
    pl.when(wide & jnp.logical_not(fresh) & finishing)(functools.partial(whole_tile, False, True))

    @pl.when(jnp.logical_not(wide) & (hi > lo))
    def _():
        wgb_ref[...] = wg_ref[0].astype(BF16)
        wub_ref[...] = wu_ref[0].astype(BF16)
        wdb_ref[...] = wd_ref[0].astype(BF16)
        for sub in range(nsub):
            r0 = sub * ts

            @pl.when((lo < r0 + ts) & (hi > r0))
            def _():
                xs = xb_ref[r0:r0 + ts, :]
                act = (_silu(_bdot(xs, wgb_ref[...])) * _bdot(xs, wub_ref[...])).astype(BF16)
                y = _bdot(act, wdb_ref[...])
                rows = r0 + lax.broadcasted_iota(jnp.int32, (ts, 1), 0)
                acc_ref[r0:r0 + ts, :] += jnp.where((rows >= lo) & (rows < hi), y, 0.0)

    @pl.when(finishing & jnp.logical_not(wide))
    def _():
        for sub in range(nsub):
            o_ref[sub * ts:(sub + 1) * ts, :] = _pack_bf16_pairs(acc_ref[sub * ts:(sub + 1) * ts, :])


def _experts(xg, meta, w_gate_up, w_down, *, tile, th=512, ts=256):
    nrows = xg.shape[0]
    n_exp, hidden, d = w_down.shape
    n_hc = hidden // th
    ts = min(ts, tile)
    wgu = w_gate_up
    return pl.pallas_call(
        functools.partial(_expert_kernel, ts=ts),
        out_shape=jax.ShapeDtypeStruct((nrows, d // 2), jnp.uint32),
        grid_spec=pltpu.PrefetchScalarGridSpec(
            num_scalar_prefetch=1,
            grid=(meta.shape[1], n_hc),
            in_specs=[pl.BlockSpec((tile, d // 2), lambda v, c, m: (m[0, v], 0)),
                      pl.BlockSpec((1, d, th), lambda v, c, m: (m[1, v], 0, c)),
                      pl.BlockSpec((1, d, th), lambda v, c, m: (m[1, v], 0, c + n_hc)),
                      pl.BlockSpec((1, th, d), lambda v, c, m: (m[1, v], c, 0))],
            out_specs=pl.BlockSpec((tile, d // 2), lambda v, c, m: (m[0, v], 0)),
            scratch_shapes=[pltpu.VMEM((tile, d), F32), pltpu.VMEM((tile, d), BF16), pltpu.VMEM((tile, th), BF16),
                            pltpu.VMEM((d, th), BF16), pltpu.VMEM((d, th), BF16), pltpu.VMEM((th, d), BF16)]),
        compiler_params=_params(("arbitrary", "arbitrary"), 56),
        name="moe_experts",
    )(meta, xg, wgu, wgu, w_down)


def _combine_kernel(x_ref, gate_ref, fg_ref, y0_ref, y1_ref, o_ref, *, final_norm):
    g = gate_ref[...]
    out = x_ref[...] + g[:, 0:1] * _unpack_pairs_f32(y0_ref[0]) + g[:, 1:2] * _unpack_pairs_f32(y1_ref[0])
    if final_norm:
        out = _rms(out, fg_ref[...])
    o_ref[...] = out


def _combine(x2, gates_t, yk, final_gain, *, tm=512):
    ntok, d = x2.shape
    tm = min(tm, ntok)
    final_norm = final_gain is not None
    fg = (final_gain if final_norm else jnp.ones((d,), F32)).reshape(1, d)
    return pl.pallas_call(
        functools.partial(_combine_kernel, final_norm=final_norm),
        out_shape=jax.ShapeDtypeStruct((ntok, d), F32),
        grid=(ntok // tm,),
        in_specs=[pl.BlockSpec((tm, d), lambda i: (i, 0)),
                  pl.BlockSpec((tm, TOP_K), lambda i: (i, 0)),
                  pl.BlockSpec((1, d), lambda i: (0, 0)),
                  pl.BlockSpec((1, tm, d // 2), lambda i: (0, i, 0)),
                  pl.BlockSpec((1, tm, d // 2), lambda i: (1, i, 0))],
        out_specs=pl.BlockSpec((tm, d), lambda i: (i, 0)),
        compiler_params=_params(("parallel",), 40),
        name="moe_combine",
    )(x2, gates_t, fg, yk, yk)


def _moe_routed(x, ln, w_router, w_gate_up, w_down, *, tile=2048):
    bsz, seqlen, d = x.shape
    ntok = bsz * seqlen
    tile = min(tile, TOP_K * ntok)
    x2 = x.reshape(ntok, d)
    idx, gates, hp, pos, counts = _router(x2, ln, w_router)
    rank, meta = _moe_plan(idx, pos, counts, tile)
    xg = _sc_scatter_rows(hp, rank, TOP_K * ntok)
    y = _experts(xg, meta, w_gate_up, w_down, tile=tile)
    yk = _sc_gather_rows(y, rank, sub=64).reshape(TOP_K, ntok, d // 2)
    return gates.T, yk


def _moe_layer(x, ln, w_router, w_gate_up, w_down, *, final_gain=None, tile=2048):
    bsz, seqlen, d = x.shape
    gates_t, yk = _moe_routed(x, ln, w_router, w_gate_up, w_down, tile=tile)
    out = _combine(x.reshape(bsz * seqlen, d), gates_t, yk, final_gain)
    return out.reshape(bsz, seqlen, d)


def kernel(x, l0_ln1, l0_s5_lam_re, l0_s5_lam_im, l0_s5_log_dt, l0_s5_b_re, l0_s5_b_im, l0_s5_c_re, l0_s5_c_im, l0_s5_d, l0_s5_w_glu, l0_s5_b_glu, l0_ln2, l0_ffn_w_gate_up, l0_ffn_w_down, l1_ln1, l1_gla_w_in, l1_gla_w_g2, l1_gla_b_g2, l1_gla_norm, l1_gla_w_out, l1_ln2, l1_moe_router, l1_moe_w_gate_up, l1_moe_w_down, l2_ln1, l2_swa_w_qkv, l2_swa_b_qkv, l2_swa_sinks, l2_swa_w_out, l2_swa_b_out, l2_ln2, l2_ffn_w_gate_up, l2_ffn_w_down, l3_ln1, l3_s5_lam_re, l3_s5_lam_im, l3_s5_log_dt, l3_s5_b_re, l3_s5_b_im, l3_s5_c_re, l3_s5_c_im, l3_s5_d, l3_s5_w_glu, l3_s5_b_glu, l3_ln2, l3_moe_router, l3_moe_w_gate_up, l3_moe_w_down, ln_f):
    s5_params = ((l0_s5_lam_re, l0_s5_lam_im, l0_s5_log_dt, l0_s5_b_re, l0_s5_b_im, l0_s5_c_re, l0_s5_c_im),
                 (l3_s5_lam_re, l3_s5_lam_im, l3_s5_log_dt, l3_s5_b_re, l3_s5_b_im, l3_s5_c_re, l3_s5_c_im))
    s5_ops = jax.vmap(_s5_operators)(*(jnp.stack(pair) for pair in zip(*s5_params)))
    x = _s5_layer(x, l0_ln1, s5_ops, 0, l0_s5_d, l0_s5_w_glu, l0_s5_b_glu)
    x = _dense_ffn_layer(x, l0_ln2, l0_ffn_w_gate_up, l0_ffn_w_down)
    x = _gla_layer(x, l1_ln1, l1_gla_w_in, l1_gla_w_g2, l1_gla_b_g2, l1_gla_norm, l1_gla_w_out)
    gates_t, yk = _moe_routed(x, l1_ln2, l1_moe_router, l1_moe_w_gate_up, l1_moe_w_down)
    x = _swa_layer(x, gates_t, yk, l2_ln1, l2_swa_w_qkv, l2_swa_b_qkv, l2_swa_sinks, l2_swa_w_out, l2_swa_b_out)
    x = _dense_ffn_layer(x, l2_ln2, l2_ffn_w_gate_up, l2_ffn_w_down)
    x = _s5_layer(x, l3_ln1, s5_ops, 1, l3_s5_d, l3_s5_w_glu, l3_s5_b_glu)
    return _moe_layer(x, l3_ln2, l3_moe_router, l3_moe_w_gate_up, l3_moe_w_down, final_gain=ln_f)
```
